```python
import jax, jax.numpy as jnp
from jax import lax
import numpy as np

D_MODEL = 2048
BATCH = 8
SEQ = 8192
DEPTH = 4

MEM_LEN = 256
HEAD_DIM = 128
N_MIX_HEADS = 12
N_MEM_HEADS = 4
MIX_WIDTH = N_MIX_HEADS * HEAD_DIM
MEM_WIDTH = N_MEM_HEADS * HEAD_DIM
CAT_WIDTH = MIX_WIDTH + MEM_WIDTH

Q_LORA = 512
KV_LORA = 512
QK_NOPE = 128
QK_ROPE = 64
V_HEAD = HEAD_DIM
ROPE_THETA = 10000.0
Q_BLOCK = 128
MLA_MIX_IN = Q_LORA + KV_LORA + QK_ROPE
MLA_IN = MLA_MIX_IN + MEM_WIDTH

CONV_WIDTH = 4
CHUNK = 64
GDN_MIX_IN = 4 * MIX_WIDTH + 2 * N_MIX_HEADS
GDN_IN = GDN_MIX_IN + MEM_WIDTH

D_FF = 4 * D_MODEL
ALPHA = (2 * DEPTH) ** 0.25
BETA_INIT = (8 * DEPTH) ** -0.25
N_MLA = (DEPTH + 1) // 2
N_GDN = DEPTH // 2
LN_EPS = 1e-5
RMS_EPS = 1e-6

kernel_name = "hybrid_mla_gdn_deepnorm_memory"


def layer_norm(x, g, b):
    xf = x.astype(jnp.float32)
    mu = jnp.mean(xf, -1, keepdims=True)
    var = jnp.mean(jnp.square(xf - mu), -1, keepdims=True)
    return ((xf - mu) * lax.rsqrt(var + LN_EPS) * g + b).astype(x.dtype)


def rms_norm(x, g):
    xf = x.astype(jnp.float32)
    return (xf * lax.rsqrt(jnp.mean(xf * xf, -1, keepdims=True) + RMS_EPS) * g).astype(x.dtype)


def l2_normalize(x):
    return x * lax.rsqrt(jnp.sum(x * x, -1, keepdims=True) + 1e-6)


def rope(x, cos, sin):
    half = x.shape[-1] // 2
    x1, x2 = x[..., :half], x[..., half:]
    return jnp.concatenate([x1 * cos - x2 * sin, x2 * cos + x1 * sin], -1)


def causal_block_attention(q, k, v):
    B_, S_, H, Dq = q.shape
    nb = S_ // Q_BLOCK
    qb = jnp.moveaxis(q.reshape(B_, nb, Q_BLOCK, H, Dq), 1, 0)
    kpos = jnp.arange(S_)
    scale = Dq ** -0.5

    def one_block(args):
        qi, bi = args
        s = jnp.einsum('bqhd,bkhd->bhqk', qi, k).astype(jnp.float32) * scale
        qpos = bi * Q_BLOCK + jnp.arange(Q_BLOCK)
        s = jnp.where(kpos[None, :] <= qpos[:, None], s, -jnp.inf)
        p = jax.nn.softmax(s, axis=-1).astype(v.dtype)
        return jnp.einsum('bhqk,bkhd->bqhd', p, v)

    o = lax.map(one_block, (qb, jnp.arange(nb)))
    return jnp.moveaxis(o, 0, 1).reshape(B_, S_, H, v.shape[-1])


def memory_attention(q, mem_kv):
    B_, M_, _ = mem_kv.shape
    k = mem_kv[..., :MEM_WIDTH].reshape(B_, M_, N_MEM_HEADS, HEAD_DIM)
    v = mem_kv[..., MEM_WIDTH:].reshape(B_, M_, N_MEM_HEADS, HEAD_DIM)
    s = jnp.einsum('bshd,bmhd->bhsm', q, k).astype(jnp.float32) * HEAD_DIM ** -0.5
    p = jax.nn.softmax(s, axis=-1).astype(v.dtype)
    o = jnp.einsum('bhsm,bmhd->bshd', p, v)
    return o.reshape(q.shape[0], q.shape[1], MEM_WIDTH)


def mla_mixer(h, cos, sin, q_norm, w_uq, kv_norm, w_ukv):
    B_, S_, _ = h.shape
    c_q = h[..., :Q_LORA]
    c_kv = h[..., Q_LORA:Q_LORA + KV_LORA]
    k_r = h[..., Q_LORA + KV_LORA:]
    q = (rms_norm(c_q, q_norm) @ w_uq).reshape(B_, S_, N_MIX_HEADS, QK_NOPE + QK_ROPE)
    q = jnp.concatenate([q[..., :QK_NOPE], rope(q[..., QK_NOPE:], cos, sin)], -1)
    kv = (rms_norm(c_kv, kv_norm) @ w_ukv).reshape(B_, S_, N_MIX_HEADS, QK_NOPE + V_HEAD)
    k_r = rope(k_r[:, :, None, :], cos, sin)
    k = jnp.concatenate([kv[..., :QK_NOPE], jnp.broadcast_to(k_r, (B_, S_, N_MIX_HEADS, QK_ROPE))], -1)
    v = kv[..., QK_NOPE:]
    return causal_block_attention(q, k, v).reshape(B_, S_, MIX_WIDTH)


def causal_depthwise_conv(x, w):
    C = x.shape[-1]
    return lax.conv_general_dilated(
        x, w[:, None, :].astype(x.dtype), window_strides=(1,),
        padding=[(CONV_WIDTH - 1, 0)], dimension_numbers=('NWC', 'WIO', 'NWC'),
        feature_group_count=C)


def gated_delta_chunked(q, k, v, g, beta):
    B_, S_, H, Dk = q.shape
    Dv = v.shape[-1]
    N = S_ // CHUNK

    def to_chunks(t):
        return jnp.moveaxis(t.reshape((B_, N, CHUNK) + t.shape[2:]), 2, 3)

    q = to_chunks(q * Dk ** -0.5)
    k = to_chunks(k)
    v = to_chunks(v)
    beta = to_chunks(beta)
    gc = jnp.cumsum(to_chunks(g), axis=-1)

    tril = jnp.tril(jnp.ones((CHUNK, CHUNK), bool))
    strict = jnp.tril(jnp.ones((CHUNK, CHUNK), bool), k=-1)
    diff = gc[..., :, None] - gc[..., None, :]
    decay = jnp.where(tril, jnp.exp(jnp.where(tril, diff, 0.0)), 0.0)

    kb = k * beta[..., None]
    L = jnp.where(strict, jnp.einsum('bnhcd,bnhed->bnhce', kb, k) * decay, 0.0)
    a = L + jnp.eye(CHUNK, dtype=jnp.float32)
    rhs = jnp.concatenate([v * beta[..., None], kb * jnp.exp(gc)[..., None]], -1)
    sol = lax.linalg.triangular_solve(a, rhs, left_side=True, lower=True, unit_diagonal=True)
    u, w = sol[..., :Dv], sol[..., Dv:]

    a_qk = jnp.where(tril, jnp.einsum('bnhcd,bnhed->bnhce', q, k) * decay, 0.0)
    q_dec = q * jnp.exp(gc)[..., None]
    k_dec = k * jnp.exp(gc[..., -1:] - gc)[..., None]
    g_last = jnp.exp(gc[..., -1])

    def step(state, inp):
        qd, kd, uu, ww, aqk, gl = inp
        v_new = uu - jnp.einsum('bhcd,bhde->bhce', ww, state)
        o = jnp.einsum('bhcd,bhde->bhce', qd, state) + jnp.einsum('bhce,bhef->bhcf', aqk, v_new)
        state = state * gl[..., None, None] + jnp.einsum('bhcd,bhce->bhde', kd, v_new)
        return state, o

    xs = tuple(jnp.moveaxis(t, 1, 0) for t in (q_dec, k_dec, u, w, a_qk, g_last))
    state0 = jnp.zeros((B_, H, Dk, Dv), jnp.float32)
    _, o = lax.scan(step, state0, xs)
    return jnp.transpose(o, (1, 0, 3, 2, 4)).reshape(B_, S_, H, Dv)


def gdn_mixer(h, conv_w, a_log, dt_bias, o_norm):
    B_, S_, _ = h.shape
    qkv = jax.nn.silu(causal_depthwise_conv(h[..., :3 * MIX_WIDTH], conv_w)).astype(jnp.float32)
    z = h[..., 3 * MIX_WIDTH:4 * MIX_WIDTH].astype(jnp.float32)
    a_in = h[..., 4 * MIX_WIDTH:4 * MIX_WIDTH + N_MIX_HEADS].astype(jnp.float32)
    b_in = h[..., 4 * MIX_WIDTH + N_MIX_HEADS:].astype(jnp.float32)
    shp = (B_, S_, N_MIX_HEADS, HEAD_DIM)
    q = l2_normalize(qkv[..., :MIX_WIDTH].reshape(shp))
    k = l2_normalize(qkv[..., MIX_WIDTH:2 * MIX_WIDTH].reshape(shp))
    v = qkv[..., 2 * MIX_WIDTH:].reshape(shp)
    g = -jnp.exp(a_log.astype(jnp.float32)) * jax.nn.softplus(a_in + dt_bias.astype(jnp.float32))
    beta = jax.nn.sigmoid(b_in)
    o = gated_delta_chunked(q, k, v, g, beta)
    o = rms_norm(o, o_norm.astype(jnp.float32)) * jax.nn.silu(z.reshape(shp))
    return o.reshape(B_, S_, MIX_WIDTH).astype(h.dtype)


def _fwd_setup_inputs(seed: int = 0) -> dict:
    key = jax.random.key(seed)
    ks = jax.random.split(key, 24)
    f32 = jnp.float32
    nrm = lambda k, shape, scale: jax.random.normal(k, shape, f32) * scale
    gain = lambda k, shape: 1.0 + 0.02 * jax.random.normal(k, shape, f32)
    offset = jax.random.randint(ks[2], (BATCH, 1), 0, 1024, dtype=jnp.int32)
    positions = (offset + jnp.arange(SEQ, dtype=jnp.int32)[None, :]).astype(jnp.int32)
    dt = jnp.exp(jax.random.uniform(ks[11], (N_GDN, N_MIX_HEADS), f32, np.log(1e-3), np.log(1e-1)))
    return {
        "x": jax.random.normal(ks[0], (BATCH, SEQ, D_MODEL), f32),
        "mem": jax.random.normal(ks[1], (BATCH, MEM_LEN, D_MODEL), f32),
        "positions": positions,
        "mla_w_in": nrm(ks[3], (N_MLA, D_MODEL, MLA_IN), D_MODEL ** -0.5),
        "mla_q_norm": gain(ks[4], (N_MLA, Q_LORA)),
        "mla_w_uq": nrm(ks[5], (N_MLA, Q_LORA, N_MIX_HEADS * (QK_NOPE + QK_ROPE)), Q_LORA ** -0.5),
        "mla_kv_norm": gain(ks[6], (N_MLA, KV_LORA)),
        "mla_w_ukv": nrm(ks[7], (N_MLA, KV_LORA, N_MIX_HEADS * (QK_NOPE + V_HEAD)), KV_LORA ** -0.5),
        "gdn_w_in": nrm(ks[8], (N_GDN, D_MODEL, GDN_IN), D_MODEL ** -0.5),
        "gdn_conv": nrm(ks[9], (N_GDN, CONV_WIDTH, 3 * MIX_WIDTH), CONV_WIDTH ** -0.5),
        "gdn_a_log": jnp.log(jax.random.uniform(ks[10], (N_GDN, N_MIX_HEADS), f32, 1.0, 16.0)),
        "gdn_dt_bias": dt + jnp.log(-jnp.expm1(-dt)),
        "gdn_o_norm": gain(ks[12], (N_GDN, HEAD_DIM)),
        "mem_w_kv": nrm(ks[13], (DEPTH, D_MODEL, 2 * MEM_WIDTH), D_MODEL ** -0.5),
        "w_out": nrm(ks[14], (DEPTH, CAT_WIDTH, D_MODEL), CAT_WIDTH ** -0.5 * BETA_INIT),
        "ln1_g": gain(ks[15], (DEPTH, D_MODEL)),
        "ln1_b": nrm(ks[16], (DEPTH, D_MODEL), 0.02),
        "mlp_w1": nrm(ks[17], (DEPTH, D_MODEL, D_FF), D_MODEL ** -0.5),
        "mlp_w2": nrm(ks[18], (DEPTH, D_FF, D_MODEL), D_FF ** -0.5 * BETA_INIT),
        "ln2_g": gain(ks[19], (DEPTH, D_MODEL)),
        "ln2_b": nrm(ks[20], (DEPTH, D_MODEL), 0.02),
    }


def _fwd_reference(x, mem, positions, mla_w_in, mla_q_norm, mla_w_uq, mla_kv_norm, mla_w_ukv,
              gdn_w_in, gdn_conv, gdn_a_log, gdn_dt_bias, gdn_o_norm, mem_w_kv, w_out,
              ln1_g, ln1_b, mlp_w1, mlp_w2, ln2_g, ln2_b):
    B_, S_, _ = x.shape
    inv_freq = 1.0 / (ROPE_THETA ** (jnp.arange(0, QK_ROPE, 2, dtype=jnp.float32) / QK_ROPE))
    ang = positions.astype(jnp.float32)[:, :, None, None] * inv_freq
    cos = jnp.cos(ang).astype(x.dtype)
    sin = jnp.sin(ang).astype(x.dtype)
    for i in range(DEPTH):
        j = i // 2
        mem_kv = mem @ mem_w_kv[i]
        if i % 2 == 0:
            h = x @ mla_w_in[j]
            mix = mla_mixer(h[..., :MLA_MIX_IN], cos, sin, mla_q_norm[j], mla_w_uq[j],
                            mla_kv_norm[j], mla_w_ukv[j])
        else:
            h = x @ gdn_w_in[j]
            mix = gdn_mixer(h[..., :GDN_MIX_IN], gdn_conv[j], gdn_a_log[j], gdn_dt_bias[j], gdn_o_norm[j])
        q_mem = h[..., -MEM_WIDTH:].reshape(B_, S_, N_MEM_HEADS, HEAD_DIM)
        mem_o = memory_attention(q_mem, mem_kv)
        y = jnp.concatenate([mix, mem_o], -1) @ w_out[i]
        x = layer_norm(ALPHA * x + y, ln1_g[i], ln1_b[i])
        ff = jnp.square(jax.nn.relu(x @ mlp_w1[i])) @ mlp_w2[i]
        x = layer_norm(ALPHA * x + ff, ln2_g[i], ln2_b[i])
    return x


import jax as _jax
import jax.numpy as _jnp

TWIN_FORMAT = 'train_step'
FWD_PARAMS = ['x', 'mem', 'positions', 'mla_w_in', 'mla_q_norm', 'mla_w_uq', 'mla_kv_norm', 'mla_w_ukv', 'gdn_w_in', 'gdn_conv', 'gdn_a_log', 'gdn_dt_bias', 'gdn_o_norm', 'mem_w_kv', 'w_out', 'ln1_g', 'ln1_b', 'mlp_w1', 'mlp_w2', 'ln2_g', 'ln2_b']
TWIN_WEIGHTS = ['mla_w_in', 'mla_q_norm', 'mla_w_uq', 'mla_kv_norm', 'mla_w_ukv', 'gdn_w_in', 'gdn_conv', 'gdn_a_log', 'gdn_dt_bias', 'gdn_o_norm', 'mem_w_kv', 'w_out', 'ln1_g', 'ln1_b', 'mlp_w1', 'mlp_w2', 'ln2_g', 'ln2_b']
TWIN_DIFF_INPUT = 'x'
TWIN_INPUTS = ['x', 'mem', 'positions', 'mla_w_in', 'mla_q_norm', 'mla_w_uq', 'mla_kv_norm', 'mla_w_ukv', 'gdn_w_in', 'gdn_conv', 'gdn_a_log', 'gdn_dt_bias', 'gdn_o_norm', 'mem_w_kv', 'w_out', 'ln1_g', 'ln1_b', 'mlp_w1', 'mlp_w2', 'ln2_g', 'ln2_b', 'loss_target', 'm_mla_w_in', 'm_mla_q_norm', 'm_mla_w_uq', 'm_mla_kv_norm', 'm_mla_w_ukv', 'm_gdn_w_in', 'm_gdn_conv', 'm_gdn_a_log', 'm_gdn_dt_bias', 'm_gdn_o_norm', 'm_mem_w_kv', 'm_w_out', 'm_ln1_g', 'm_ln1_b', 'm_mlp_w1', 'm_mlp_w2', 'm_ln2_g', 'm_ln2_b', 'v_mla_w_in', 'v_mla_q_norm', 'v_mla_w_uq', 'v_mla_kv_norm', 'v_mla_w_ukv', 'v_gdn_w_in', 'v_gdn_conv', 'v_gdn_a_log', 'v_gdn_dt_bias', 'v_gdn_o_norm', 'v_mem_w_kv', 'v_w_out', 'v_ln1_g', 'v_ln1_b', 'v_mlp_w1', 'v_mlp_w2', 'v_ln2_g', 'v_ln2_b']
TWIN_OUTPUTS = ['loss', 'grad_x', 'grad_mla_w_in', 'grad_mla_q_norm', 'grad_mla_w_uq', 'grad_mla_kv_norm', 'grad_mla_w_ukv', 'grad_gdn_w_in', 'grad_gdn_conv', 'grad_gdn_a_log', 'grad_gdn_dt_bias', 'grad_gdn_o_norm', 'grad_mem_w_kv', 'grad_w_out', 'grad_ln1_g', 'grad_ln1_b', 'grad_mlp_w1', 'grad_mlp_w2', 'grad_ln2_g', 'grad_ln2_b', 'delta_mla_w_in', 'delta_mla_q_norm', 'delta_mla_w_uq', 'delta_mla_kv_norm', 'delta_mla_w_ukv', 'delta_gdn_w_in', 'delta_gdn_conv', 'delta_gdn_a_log', 'delta_gdn_dt_bias', 'delta_gdn_o_norm', 'delta_mem_w_kv', 'delta_w_out', 'delta_ln1_g', 'delta_ln1_b', 'delta_mlp_w1', 'delta_mlp_w2', 'delta_ln2_g', 'delta_ln2_b', 'new_m_mla_w_in', 'new_m_mla_q_norm', 'new_m_mla_w_uq', 'new_m_mla_kv_norm', 'new_m_mla_w_ukv', 'new_m_gdn_w_in', 'new_m_gdn_conv', 'new_m_gdn_a_log', 'new_m_gdn_dt_bias', 'new_m_gdn_o_norm', 'new_m_mem_w_kv', 'new_m_w_out', 'new_m_ln1_g', 'new_m_ln1_b', 'new_m_mlp_w1', 'new_m_mlp_w2', 'new_m_ln2_g', 'new_m_ln2_b', 'new_v_mla_w_in', 'new_v_mla_q_norm', 'new_v_mla_w_uq', 'new_v_mla_kv_norm', 'new_v_mla_w_ukv', 'new_v_gdn_w_in', 'new_v_gdn_conv', 'new_v_gdn_a_log', 'new_v_gdn_dt_bias', 'new_v_gdn_o_norm', 'new_v_mem_w_kv', 'new_v_w_out', 'new_v_ln1_g', 'new_v_ln1_b', 'new_v_mlp_w1', 'new_v_mlp_w2', 'new_v_ln2_g', 'new_v_ln2_b']
TWIN_LEAF_KINDS = {'loss': 'loss', 'grad_x': 'grad_x', 'grad_mla_w_in': 'grad_w', 'grad_mla_q_norm': 'grad_w', 'grad_mla_w_uq': 'grad_w', 'grad_mla_kv_norm': 'grad_w', 'grad_mla_w_ukv': 'grad_w', 'grad_gdn_w_in': 'grad_w', 'grad_gdn_conv': 'grad_w', 'grad_gdn_a_log': 'grad_w', 'grad_gdn_dt_bias': 'grad_w', 'grad_gdn_o_norm': 'grad_w', 'grad_mem_w_kv': 'grad_w', 'grad_w_out': 'grad_w', 'grad_ln1_g': 'grad_w', 'grad_ln1_b': 'grad_w', 'grad_mlp_w1': 'grad_w', 'grad_mlp_w2': 'grad_w', 'grad_ln2_g': 'grad_w', 'grad_ln2_b': 'grad_w', 'delta_mla_w_in': 'delta_w', 'delta_mla_q_norm': 'delta_w', 'delta_mla_w_uq': 'delta_w', 'delta_mla_kv_norm': 'delta_w', 'delta_mla_w_ukv': 'delta_w', 'delta_gdn_w_in': 'delta_w', 'delta_gdn_conv': 'delta_w', 'delta_gdn_a_log': 'delta_w', 'delta_gdn_dt_bias': 'delta_w', 'delta_gdn_o_norm': 'delta_w', 'delta_mem_w_kv': 'delta_w', 'delta_w_out': 'delta_w', 'delta_ln1_g': 'delta_w', 'delta_ln1_b': 'delta_w', 'delta_mlp_w1': 'delta_w', 'delta_mlp_w2': 'delta_w', 'delta_ln2_g': 'delta_w', 'delta_ln2_b': 'delta_w', 'new_m_mla_w_in': 'new_m', 'new_m_mla_q_norm': 'new_m', 'new_m_mla_w_uq': 'new_m', 'new_m_mla_kv_norm': 'new_m', 'new_m_mla_w_ukv': 'new_m', 'new_m_gdn_w_in': 'new_m', 'new_m_gdn_conv': 'new_m', 'new_m_gdn_a_log': 'new_m', 'new_m_gdn_dt_bias': 'new_m', 'new_m_gdn_o_norm': 'new_m', 'new_m_mem_w_kv': 'new_m', 'new_m_w_out': 'new_m', 'new_m_ln1_g': 'new_m', 'new_m_ln1_b': 'new_m', 'new_m_mlp_w1': 'new_m', 'new_m_mlp_w2': 'new_m', 'new_m_ln2_g': 'new_m', 'new_m_ln2_b': 'new_m', 'new_v_mla_w_in': 'new_v', 'new_v_mla_q_norm': 'new_v', 'new_v_mla_w_uq': 'new_v', 'new_v_mla_kv_norm': 'new_v', 'new_v_mla_w_ukv': 'new_v', 'new_v_gdn_w_in': 'new_v', 'new_v_gdn_conv': 'new_v', 'new_v_gdn_a_log': 'new_v', 'new_v_gdn_dt_bias': 'new_v', 'new_v_gdn_o_norm': 'new_v', 'new_v_mem_w_kv': 'new_v', 'new_v_w_out': 'new_v', 'new_v_ln1_g': 'new_v', 'new_v_ln1_b': 'new_v', 'new_v_mlp_w1': 'new_v', 'new_v_mlp_w2': 'new_v', 'new_v_ln2_g': 'new_v', 'new_v_ln2_b': 'new_v'}


def _forward(args):
    return _fwd_reference(*[args[k] for k in FWD_PARAMS])


def _output_shape():
    def fwd():
        inp = _fwd_setup_inputs(0)
        return _fwd_reference(*[inp[k] for k in FWD_PARAMS])
    out = _jax.eval_shape(fwd)
    return out.shape, out.dtype

N_MICROBATCH = 1
ADAM_LR = 0.001
ADAM_B1 = 0.9
ADAM_B2 = 0.999
ADAM_EPS = 1e-08
ADAM_WD = 0.01
ADAM_STEP = 10
PER_EXAMPLE_BATCH_AXIS = {'x': 0, 'mem': 0, 'positions': 0, 'loss_target': 0}
SHARED_INPUTS = []
_WEIGHT_DTYPES = {'mla_w_in': _jnp.float32, 'mla_q_norm': _jnp.float32, 'mla_w_uq': _jnp.float32, 'mla_kv_norm': _jnp.float32, 'mla_w_ukv': _jnp.float32, 'gdn_w_in': _jnp.float32, 'gdn_conv': _jnp.float32, 'gdn_a_log': _jnp.float32, 'gdn_dt_bias': _jnp.float32, 'gdn_o_norm': _jnp.float32, 'mem_w_kv': _jnp.float32, 'w_out': _jnp.float32, 'ln1_g': _jnp.float32, 'ln1_b': _jnp.float32, 'mlp_w1': _jnp.float32, 'mlp_w2': _jnp.float32, 'ln2_g': _jnp.float32, 'ln2_b': _jnp.float32}
MOMENT_SCALE = {'mla_w_in': 1.221126e-02, 'mla_q_norm': 1.005951e-02, 'mla_w_uq': 4.791708e-03, 'mla_kv_norm': 1.725422e-02, 'mla_w_ukv': 7.295876e-03, 'gdn_w_in': 1.582525e-02, 'gdn_conv': 1.623698e-02, 'gdn_a_log': 7.413662e-02, 'gdn_dt_bias': 7.276536e-02, 'gdn_o_norm': 6.861966e-02, 'mem_w_kv': 3.709719e-03, 'w_out': 3.651874e-02, 'ln1_g': 8.771254e-01, 'ln1_b': 5.322863e-01, 'mlp_w1': 2.263899e-02, 'mlp_w2': 1.517112e-01, 'ln2_g': 1.609435e+01, 'ln2_b': 3.866475e+00}


def _to_microbatches(a, axis):
    t = _jnp.moveaxis(a, axis, 0)
    t = t.reshape((N_MICROBATCH, t.shape[0] // N_MICROBATCH) + t.shape[1:])
    return _jnp.moveaxis(t, 1, axis + 1)


def setup_inputs(seed: int = 0) -> dict:
    inp = _fwd_setup_inputs(seed)
    key = _jax.random.fold_in(_jax.random.key(seed), 7919)
    shape, _ = _output_shape()
    out = dict(inp)
    out["loss_target"] = _jax.random.normal(_jax.random.fold_in(key, 0), shape, _jnp.float32)
    for i, name in enumerate(TWIN_WEIGHTS):
        w = inp[name].astype(_jnp.float32)
        if MOMENT_SCALE is None:
            s = _jnp.sqrt(_jnp.mean(_jnp.square(w)) + 1e-30)
        else:
            s = MOMENT_SCALE[name]
        km, kv = _jax.random.split(_jax.random.fold_in(key, i + 1))
        out[name] = w
        out["m_" + name] = s * _jax.random.normal(km, w.shape, _jnp.float32)
        out["v_" + name] = (s * s) * _jax.random.uniform(kv, w.shape, _jnp.float32, 0.5, 1.5)
    if N_MICROBATCH > 1:
        for name, axis in PER_EXAMPLE_BATCH_AXIS.items():
            out[name] = _to_microbatches(out[name], axis)
    return {'x': out['x'], 'mem': out['mem'], 'positions': out['positions'], 'mla_w_in': out['mla_w_in'], 'mla_q_norm': out['mla_q_norm'], 'mla_w_uq': out['mla_w_uq'], 'mla_kv_norm': out['mla_kv_norm'], 'mla_w_ukv': out['mla_w_ukv'], 'gdn_w_in': out['gdn_w_in'], 'gdn_conv': out['gdn_conv'], 'gdn_a_log': out['gdn_a_log'], 'gdn_dt_bias': out['gdn_dt_bias'], 'gdn_o_norm': out['gdn_o_norm'], 'mem_w_kv': out['mem_w_kv'], 'w_out': out['w_out'], 'ln1_g': out['ln1_g'], 'ln1_b': out['ln1_b'], 'mlp_w1': out['mlp_w1'], 'mlp_w2': out['mlp_w2'], 'ln2_g': out['ln2_g'], 'ln2_b': out['ln2_b'], 'loss_target': out['loss_target'], 'm_mla_w_in': out['m_mla_w_in'], 'm_mla_q_norm': out['m_mla_q_norm'], 'm_mla_w_uq': out['m_mla_w_uq'], 'm_mla_kv_norm': out['m_mla_kv_norm'], 'm_mla_w_ukv': out['m_mla_w_ukv'], 'm_gdn_w_in': out['m_gdn_w_in'], 'm_gdn_conv': out['m_gdn_conv'], 'm_gdn_a_log': out['m_gdn_a_log'], 'm_gdn_dt_bias': out['m_gdn_dt_bias'], 'm_gdn_o_norm': out['m_gdn_o_norm'], 'm_mem_w_kv': out['m_mem_w_kv'], 'm_w_out': out['m_w_out'], 'm_ln1_g': out['m_ln1_g'], 'm_ln1_b': out['m_ln1_b'], 'm_mlp_w1': out['m_mlp_w1'], 'm_mlp_w2': out['m_mlp_w2'], 'm_ln2_g': out['m_ln2_g'], 'm_ln2_b': out['m_ln2_b'], 'v_mla_w_in': out['v_mla_w_in'], 'v_mla_q_norm': out['v_mla_q_norm'], 'v_mla_w_uq': out['v_mla_w_uq'], 'v_mla_kv_norm': out['v_mla_kv_norm'], 'v_mla_w_ukv': out['v_mla_w_ukv'], 'v_gdn_w_in': out['v_gdn_w_in'], 'v_gdn_conv': out['v_gdn_conv'], 'v_gdn_a_log': out['v_gdn_a_log'], 'v_gdn_dt_bias': out['v_gdn_dt_bias'], 'v_gdn_o_norm': out['v_gdn_o_norm'], 'v_mem_w_kv': out['v_mem_w_kv'], 'v_w_out': out['v_w_out'], 'v_ln1_g': out['v_ln1_g'], 'v_ln1_b': out['v_ln1_b'], 'v_mlp_w1': out['v_mlp_w1'], 'v_mlp_w2': out['v_mlp_w2'], 'v_ln2_g': out['v_ln2_g'], 'v_ln2_b': out['v_ln2_b']}


def _loss(weights, diff, rest, loss_target):
    with _jax.named_scope("forward"):
        args = {**rest, TWIN_DIFF_INPUT: diff, **{k: w.astype(_WEIGHT_DTYPES[k]) for k, w in weights.items()}}
        y = _forward(args)
    with _jax.named_scope("loss_head"):
        err = _jnp.square(y.astype(_jnp.float32) - loss_target)
        return 0.5 * _jnp.sum(_jnp.mean(err, axis=-1)) if err.ndim else 0.5 * err


def _adamw(w, g, m, v):
    m = ADAM_B1 * m + (1.0 - ADAM_B1) * g
    v = ADAM_B2 * v + (1.0 - ADAM_B2) * _jnp.square(g)
    m_hat = m / (1.0 - ADAM_B1 ** ADAM_STEP)
    v_hat = v / (1.0 - ADAM_B2 ** ADAM_STEP)
    delta = -ADAM_LR * (m_hat / (_jnp.sqrt(v_hat) + ADAM_EPS) + ADAM_WD * w)
    return delta, m, v


def reference(x, mem, positions, mla_w_in, mla_q_norm, mla_w_uq, mla_kv_norm, mla_w_ukv, gdn_w_in, gdn_conv, gdn_a_log, gdn_dt_bias, gdn_o_norm, mem_w_kv, w_out, ln1_g, ln1_b, mlp_w1, mlp_w2, ln2_g, ln2_b, loss_target, m_mla_w_in, m_mla_q_norm, m_mla_w_uq, m_mla_kv_norm, m_mla_w_ukv, m_gdn_w_in, m_gdn_conv, m_gdn_a_log, m_gdn_dt_bias, m_gdn_o_norm, m_mem_w_kv, m_w_out, m_ln1_g, m_ln1_b, m_mlp_w1, m_mlp_w2, m_ln2_g, m_ln2_b, v_mla_w_in, v_mla_q_norm, v_mla_w_uq, v_mla_kv_norm, v_mla_w_ukv, v_gdn_w_in, v_gdn_conv, v_gdn_a_log, v_gdn_dt_bias, v_gdn_o_norm, v_mem_w_kv, v_w_out, v_ln1_g, v_ln1_b, v_mlp_w1, v_mlp_w2, v_ln2_g, v_ln2_b):
    given = dict(x=x, mem=mem, positions=positions, mla_w_in=mla_w_in, mla_q_norm=mla_q_norm, mla_w_uq=mla_w_uq, mla_kv_norm=mla_kv_norm, mla_w_ukv=mla_w_ukv, gdn_w_in=gdn_w_in, gdn_conv=gdn_conv, gdn_a_log=gdn_a_log, gdn_dt_bias=gdn_dt_bias, gdn_o_norm=gdn_o_norm, mem_w_kv=mem_w_kv, w_out=w_out, ln1_g=ln1_g, ln1_b=ln1_b, mlp_w1=mlp_w1, mlp_w2=mlp_w2, ln2_g=ln2_g, ln2_b=ln2_b, loss_target=loss_target, m_mla_w_in=m_mla_w_in, m_mla_q_norm=m_mla_q_norm, m_mla_w_uq=m_mla_w_uq, m_mla_kv_norm=m_mla_kv_norm, m_mla_w_ukv=m_mla_w_ukv, m_gdn_w_in=m_gdn_w_in, m_gdn_conv=m_gdn_conv, m_gdn_a_log=m_gdn_a_log, m_gdn_dt_bias=m_gdn_dt_bias, m_gdn_o_norm=m_gdn_o_norm, m_mem_w_kv=m_mem_w_kv, m_w_out=m_w_out, m_ln1_g=m_ln1_g, m_ln1_b=m_ln1_b, m_mlp_w1=m_mlp_w1, m_mlp_w2=m_mlp_w2, m_ln2_g=m_ln2_g, m_ln2_b=m_ln2_b, v_mla_w_in=v_mla_w_in, v_mla_q_norm=v_mla_q_norm, v_mla_w_uq=v_mla_w_uq, v_mla_kv_norm=v_mla_kv_norm, v_mla_w_ukv=v_mla_w_ukv, v_gdn_w_in=v_gdn_w_in, v_gdn_conv=v_gdn_conv, v_gdn_a_log=v_gdn_a_log, v_gdn_dt_bias=v_gdn_dt_bias, v_gdn_o_norm=v_gdn_o_norm, v_mem_w_kv=v_mem_w_kv, v_w_out=v_w_out, v_ln1_g=v_ln1_g, v_ln1_b=v_ln1_b, v_mlp_w1=v_mlp_w1, v_mlp_w2=v_mlp_w2, v_ln2_g=v_ln2_g, v_ln2_b=v_ln2_b)
    weights = {n: given[n] for n in TWIN_WEIGHTS}
    shared = {n: given[n] for n in SHARED_INPUTS}
    per_example = {n: given[n] for n in ['x', 'mem', 'positions']}
    grad_fn = _jax.value_and_grad(_loss, argnums=(0, 1))

    def one_microbatch(ex, loss_target):
        ex = dict(ex)
        diff = ex.pop(TWIN_DIFF_INPUT)
        return grad_fn(weights, diff, {**shared, **ex}, loss_target)

    if N_MICROBATCH == 1:
        loss, (grad_w, grad_x) = one_microbatch(per_example, given["loss_target"])
    else:
        def body(carry, xs):
            loss_sum, grad_sum = carry
            l_k, (gw_k, gx_k) = one_microbatch(xs[0], xs[1])
            with _jax.named_scope("update"):
                return (loss_sum + l_k, _jax.tree.map(_jnp.add, grad_sum, gw_k)), gx_k

        init = (_jnp.zeros((), _jnp.float32), _jax.tree.map(_jnp.zeros_like, weights))
        (loss, grad_w), grad_x = _jax.lax.scan(body, init, (per_example, given["loss_target"]))
    with _jax.named_scope("update"):
        delta_w, new_m, new_v = {}, {}, {}
        for n in TWIN_WEIGHTS:
            delta_w[n], new_m[n], new_v[n] = _adamw(weights[n], grad_w[n], given["m_" + n], given["v_" + n])
    return (loss, grad_x, *[grad_w[n] for n in TWIN_WEIGHTS], *[delta_w[n] for n in TWIN_WEIGHTS],
            *[new_m[n] for n in TWIN_WEIGHTS], *[new_v[n] for n in TWIN_WEIGHTS])
```

```python
import functools
import math

import jax
import jax.numpy as jnp
from jax import lax
from jax.experimental import pallas as pl
from jax.experimental.pallas import tpu as pltpu

F32 = jnp.float32
BF16 = jnp.bfloat16
MESH = pl.DeviceIdType.MESH

LANES = 128
SUBLANES = 8
VMEM_LIMIT = 56 * 1024 * 1024

HEAD_DIM = 128
QK_NOPE = 128
QK_ROPE = 64
QK_PAD = 256
ROPE_THETA = 10000.0
CONV_WIDTH = 4
CHUNK = 64
LN_EPS = 1e-5
RMS_EPS = 1e-6
ADAM_LR = 0.001
ADAM_B1 = 0.9
ADAM_B2 = 0.999
ADAM_EPS = 1e-08
ADAM_WD = 0.01
ADAM_STEP = 10
HI = lax.Precision.HIGHEST


def _cparams(sem=None):
    return pltpu.CompilerParams(dimension_semantics=sem, vmem_limit_bytes=VMEM_LIMIT)


def _tile(n, cap, unit):
    best = None
    t = unit
    while t <= min(n, cap):
        if n % t == 0:
            best = t
        t += unit
    return best if best is not None else n


def _mm(a, b, mode, out_dtypes, *, name, epilogue=None, extras=(), tm_cap=1024, tn_cap=1024, tk_cap=512):
    if mode == "nn":
        (M, K), (K2, N) = a.shape, b.shape
    elif mode == "nt":
        (M, K), (N, K2) = a.shape, b.shape
    else:
        (K, M), (K2, N) = a.shape, b.shape
    assert K == K2, (a.shape, b.shape, mode)
    tm = _tile(M, tm_cap, LANES if mode == "tn" else 16)
    tn = _tile(N, tn_cap, LANES)
    tk = _tile(K, tk_cap, 16 if mode == "tn" else LANES)
    nk = K // tk
    if mode == "nn":
        a_spec = pl.BlockSpec((tm, tk), lambda i, j, k: (i, k))
        b_spec = pl.BlockSpec((tk, tn), lambda i, j, k: (k, j))
        dims = (((1,), (0,)), ((), ()))
    elif mode == "nt":
        a_spec = pl.BlockSpec((tm, tk), lambda i, j, k: (i, k))
        b_spec = pl.BlockSpec((tn, tk), lambda i, j, k: (j, k))
        dims = (((1,), (1,)), ((), ()))
    else:
        a_spec = pl.BlockSpec((tk, tm), lambda i, j, k: (k, i))
        b_spec = pl.BlockSpec((tk, tn), lambda i, j, k: (k, j))
        dims = (((0,), (0,)), ((), ()))
    mn_spec = pl.BlockSpec((tm, tn), lambda i, j, k: (i, j))
    n_ex, n_out = len(extras), len(out_dtypes)
    for e in extras:
        assert e.shape == (M, N), (e.shape, M, N)

    def body(a_ref, b_ref, *rest):
        ex_refs, out_refs, acc = rest[:n_ex], rest[n_ex:n_ex + n_out], rest[-1]
        k = pl.program_id(2)

        @pl.when(k == 0)
        def _():
            acc[...] = jnp.zeros_like(acc)

        acc[...] += lax.dot_general(a_ref[...].astype(BF16), b_ref[...].astype(BF16), dims,
                                    preferred_element_type=F32)

        @pl.when(k == nk - 1)
        def _():
            res = (acc[...],) if epilogue is None else epilogue(acc[...], *[e[...] for e in ex_refs])
            for o_ref, r in zip(out_refs, res):
                o_ref[...] = r.astype(o_ref.dtype)

    outs = pl.pallas_call(
        body, name=name, grid=(M // tm, N // tn, nk),
        in_specs=[a_spec, b_spec] + [mn_spec] * n_ex,
        out_specs=[mn_spec] * n_out,
        out_shape=[jax.ShapeDtypeStruct((M, N), d) for d in out_dtypes],
        scratch_shapes=[pltpu.VMEM((tm, tn), F32)],
        compiler_params=_cparams(("parallel", "parallel", "arbitrary")),
    )(a, b, *extras)
    return outs


def _spec(block, imap):
    return pl.BlockSpec(block, imap)


def _ew(fn, ins, outs, grid, *, name, acc_out=()):
    n_in = len(ins)
    ng = len(grid)

    def body(*refs):
        in_refs, out_refs = refs[:n_in], refs[n_in:]
        res = fn(*[r[...] for r in in_refs])
        first = functools.reduce(jnp.logical_and, [pl.program_id(d) == 0 for d in range(ng)])
        for i, (o_ref, r) in enumerate(zip(out_refs, res)):
            if i in acc_out:
                @pl.when(first)
                def _(o_ref=o_ref):
                    o_ref[...] = jnp.zeros_like(o_ref)
                o_ref[...] += r.astype(o_ref.dtype)
            else:
                o_ref[...] = r.astype(o_ref.dtype)

    return pl.pallas_call(
        body, name=name, grid=grid,
        in_specs=[_spec(b, m) for _, b, m in ins],
        out_specs=[_spec(b, m) for _, b, m in outs],
        out_shape=[s for s, _, _ in outs],
        compiler_params=_cparams(("arbitrary",) * ng),
    )(*[a for a, _, _ in ins])


def _ew_vjp(fn, ins, cts, gouts, grid, *, name):
    n_in, n_ct = len(ins), len(cts)
    ng = len(grid)
    want = [i for i, g in enumerate(gouts) if g is not None]

    def body(*refs):
        in_refs, ct_refs, out_refs = refs[:n_in], refs[n_in:n_in + n_ct], refs[n_in + n_ct:]
        prim = [r[...] for r in in_refs]
        outs, pull = jax.vjp(fn, *prim)
        grads = pull(tuple(r[...].astype(o.dtype) for r, o in zip(ct_refs, outs)))
        first_all = functools.reduce(jnp.logical_and, [pl.program_id(d) == 0 for d in range(ng)])
        for o_ref, i in zip(out_refs, want):
            mode = gouts[i][3]
            g = grads[i]
            if mode == "set":
                o_ref[...] = g.astype(o_ref.dtype)
            elif mode == "acc":
                @pl.when(pl.program_id(ng - 1) == 0)
                def _(o_ref=o_ref):
                    o_ref[...] = jnp.zeros_like(o_ref)
                o_ref[...] += g.astype(o_ref.dtype)
            elif mode == "acc_all":
                @pl.when(first_all)
                def _(o_ref=o_ref):
                    o_ref[...] = jnp.zeros_like(o_ref)
                o_ref[...] += g.astype(o_ref.dtype)
            else:
                @pl.when(first_all)
                def _(o_ref=o_ref):
                    o_ref[...] = jnp.zeros_like(o_ref)
                idx = pl.program_id(mode[1])
                o_ref[idx] += g.astype(o_ref.dtype)

    return pl.pallas_call(
        body, name=name, grid=grid,
        in_specs=[_spec(b, m) for _, b, m in ins] + [_spec(b, m) for _, b, m in cts],
        out_specs=[_spec(gouts[i][1], gouts[i][2]) for i in want],
        out_shape=[gouts[i][0] for i in want],
        compiler_params=_cparams(("arbitrary",) * ng),
    )(*[a for a, _, _ in ins], *[a for a, _, _ in cts])


def _sds(shape, dtype=F32):
    return jax.ShapeDtypeStruct(tuple(shape), dtype)


def _ln_fn(z, g, b):
    mu = jnp.mean(z, -1, keepdims=True)
    d = z - mu
    var = jnp.mean(d * d, -1, keepdims=True)
    y = d * lax.rsqrt(var + LN_EPS) * g + b
    return y, y


def _rms_fn(x, g):
    return (x * lax.rsqrt(jnp.mean(x * x, -1, keepdims=True) + RMS_EPS) * g,)


@jax.custom_vjp
def _rot_half(x):
    lane = lax.broadcasted_iota(jnp.int32, x.shape, x.ndim - 1)
    up = pltpu.roll(x, LANES - QK_ROPE // 2, x.ndim - 1)
    dn = pltpu.roll(x, QK_ROPE // 2, x.ndim - 1)
    return jnp.where(lane < QK_ROPE // 2, -up, jnp.where(lane < QK_ROPE, dn, 0.0))


def _rot_half_fwd(x):
    return _rot_half(x), None


def _rot_half_bwd(_, ct):
    return (-_rot_half(ct),)


_rot_half.defvjp(_rot_half_fwd, _rot_half_bwd)


def _rope_blk(x, cos, sin):
    return x * cos + _rot_half(x) * sin


def _mla_prep_fn(qraw, knope, kr, cos, sin):
    qn, qr = qraw[:, :QK_NOPE], qraw[:, QK_NOPE:]
    q = jnp.concatenate([qn, _rope_blk(qr, cos, sin)], axis=1)
    k = jnp.concatenate([knope.astype(F32), _rope_blk(kr, cos, sin)], axis=1)
    return q, k


def _l2n(x):
    return x * lax.rsqrt(jnp.sum(x * x, -1, keepdims=True) + 1e-6)


def _gdn_qk_fn(qc, kc):
    return _l2n(qc) * (HEAD_DIM ** -0.5), _l2n(kc)


def _softplus(x):
    return jnp.maximum(x, 0.0) + jnp.log(1.0 + jnp.exp(-jnp.abs(x)))


def _sigmoid(x):
    return 1.0 / (1.0 + jnp.exp(-x))


def _silu(x):
    return x * _sigmoid(x)


def _gdn_gate_fn(n_heads, head_axis):
    def fn(ab, a_log, dt_bias):
        h = pl.program_id(head_axis)
        lane = lax.broadcasted_iota(jnp.int32, ab.shape, 1)
        a_in = jnp.sum(jnp.where(lane == h, ab, 0.0), -1, keepdims=True)
        b_in = jnp.sum(jnp.where(lane == h + n_heads, ab, 0.0), -1, keepdims=True)
        g = -jnp.exp(a_log[:, :CHUNK]) * _softplus(a_in + dt_bias[:, :CHUNK])
        beta = _sigmoid(b_in) + jnp.zeros_like(g)
        return g, beta
    return fn


def _gdn_out_fn(o, z, w):
    return (o * lax.rsqrt(jnp.mean(o * o, -1, keepdims=True) + RMS_EPS) * w * _silu(z),)


def _loss_fn(y, t):
    d = y - t
    return (jnp.sum(d * d, axis=0, keepdims=True) * (0.5 / y.shape[-1]), d * (1.0 / y.shape[-1]))


def _adamw_fn(w, g, m, v):
    m = ADAM_B1 * m + (1.0 - ADAM_B1) * g
    v = ADAM_B2 * v + (1.0 - ADAM_B2) * (g * g)
    m_hat = m / (1.0 - ADAM_B1 ** ADAM_STEP)
    v_hat = v / (1.0 - ADAM_B2 ** ADAM_STEP)
    delta = -ADAM_LR * (m_hat / (jnp.sqrt(v_hat) + ADAM_EPS) + ADAM_WD * w)
    return delta, m, v


def _mask_block(s, qi, ki, tq, tk):
    row = lax.broadcasted_iota(jnp.int32, s.shape, 0) + qi * tq
    col = lax.broadcasted_iota(jnp.int32, s.shape, 1) + ki * tk
    return col <= row


def _flash_fwd(q, k, v, *, H, dq, dv, qoff, koff, voff, causal, scale, tq, tk, name):
    S, Sk = q.shape[0], k.shape[0]
    nq, nk = S // tq, Sk // tk
    if causal:
        assert tq == tk and S == Sk
    kmap = (lambda h, qi, ki: (jnp.minimum(ki, qi), koff + h)) if causal else (lambda h, qi, ki: (ki, koff + h))
    vmap_ = (lambda h, qi, ki: (jnp.minimum(ki, qi), voff + h)) if causal else (lambda h, qi, ki: (ki, voff + h))

    def body(q_ref, k_ref, v_ref, o_ref, lse_ref, m_s, l_s, acc):
        qi, ki = pl.program_id(1), pl.program_id(2)

        @pl.when(ki == 0)
        def _():
            m_s[...] = jnp.full_like(m_s, -jnp.inf)
            l_s[...] = jnp.zeros_like(l_s)
            acc[...] = jnp.zeros_like(acc)

        def step(masked):
            s = lax.dot_general(q_ref[...].astype(BF16), k_ref[...].astype(BF16), (((1,), (1,)), ((), ())),
                                preferred_element_type=F32) * scale
            if masked:
                s = jnp.where(_mask_block(s, qi, ki, tq, tk), s, -jnp.inf)
            m_prev = m_s[...]
            m_new = jnp.maximum(m_prev, jnp.max(s, axis=1, keepdims=True))
            alpha = jnp.exp(m_prev - m_new)
            p = jnp.exp(s - m_new[:, :1])
            l_s[...] = alpha * l_s[...] + jnp.sum(p, axis=1, keepdims=True)
            acc[...] = acc[...] * alpha[:, :1] + lax.dot_general(
                p.astype(BF16), v_ref[...].astype(BF16), (((1,), (0,)), ((), ())), preferred_element_type=F32)
            m_s[...] = m_new

        if causal:
            pl.when(ki < qi)(lambda: step(False))
            pl.when(ki == qi)(lambda: step(True))
        else:
            step(False)

        @pl.when(ki == nk - 1)
        def _():
            o_ref[...] = (acc[...] / l_s[...][:, :1]).astype(o_ref.dtype)
            lse_ref[...] = m_s[...] + jnp.log(l_s[...])

    return pl.pallas_call(
        body, name=name, grid=(H, nq, nk),
        in_specs=[pl.BlockSpec((tq, dq), lambda h, qi, ki: (qi, qoff + h)),
                  pl.BlockSpec((tk, dq), kmap), pl.BlockSpec((tk, dv), vmap_)],
        out_specs=[pl.BlockSpec((tq, dv), lambda h, qi, ki: (qi, h)),
                   pl.BlockSpec((tq, LANES), lambda h, qi, ki: (qi, h))],
        out_shape=[_sds((S, H * dv)), _sds((S, H * LANES))],
        scratch_shapes=[pltpu.VMEM((tq, LANES), F32), pltpu.VMEM((tq, LANES), F32), pltpu.VMEM((tq, dv), F32)],
        compiler_params=_cparams(("parallel", "parallel", "arbitrary")),
    )(q, k, v)


def _flash_p_ds(q_ref, k_ref, v_ref, o_ref, do_ref, lse_ref, qi, ki, tq, tk, scale, masked):
    s = lax.dot_general(q_ref[...].astype(BF16), k_ref[...].astype(BF16), (((1,), (1,)), ((), ())),
                        preferred_element_type=F32) * scale
    p = jnp.exp(s - lse_ref[...][:, :1])
    if masked:
        p = jnp.where(_mask_block(s, qi, ki, tq, tk), p, 0.0)
    do = do_ref[...].astype(F32)
    delta = jnp.sum(do * o_ref[...].astype(F32), axis=1, keepdims=True)
    dp = lax.dot_general(do.astype(BF16), v_ref[...].astype(BF16), (((1,), (1,)), ((), ())),
                         preferred_element_type=F32)
    ds = p * (dp - delta) * scale
    return p, ds


def _flash_bwd(q, k, v, o, lse, do, *, H, dq, dv, qoff, koff, voff, dooff, causal, scale, tq, tk, name):
    S, Sk = q.shape[0], k.shape[0]
    nq, nk = S // tq, Sk // tk

    qm = (lambda off: (lambda h, ki, qi: (jnp.maximum(qi, ki), off + h))) if causal else \
         (lambda off: (lambda h, ki, qi: (qi, off + h)))

    def body_kv(q_ref, k_ref, v_ref, o_ref, do_ref, lse_ref, dk_ref, dv_ref, dk_acc, dv_acc):
        ki, qi = pl.program_id(1), pl.program_id(2)

        @pl.when(qi == 0)
        def _():
            dk_acc[...] = jnp.zeros_like(dk_acc)
            dv_acc[...] = jnp.zeros_like(dv_acc)

        def step(masked):
            p, ds = _flash_p_ds(q_ref, k_ref, v_ref, o_ref, do_ref, lse_ref, qi, ki, tq, tk, scale, masked)
            dv_acc[...] += lax.dot_general(p.astype(BF16), do_ref[...].astype(BF16), (((0,), (0,)), ((), ())),
                                           preferred_element_type=F32)
            dk_acc[...] += lax.dot_general(ds.astype(BF16), q_ref[...].astype(BF16), (((0,), (0,)), ((), ())),
                                           preferred_element_type=F32)

        if causal:
            pl.when(qi > ki)(lambda: step(False))
            pl.when(qi == ki)(lambda: step(True))
        else:
            step(False)

        @pl.when(qi == nq - 1)
        def _():
            dk_ref[...] = dk_acc[...]
            dv_ref[...] = dv_acc[...]

    dk, dvv = pl.pallas_call(
        body_kv, name=name + "_dkv", grid=(H, nk, nq),
        in_specs=[pl.BlockSpec((tq, dq), qm(qoff)),
                  pl.BlockSpec((tk, dq), lambda h, ki, qi: (ki, koff + h)),
                  pl.BlockSpec((tk, dv), lambda h, ki, qi: (ki, voff + h)),
                  pl.BlockSpec((tq, dv), qm(0)), pl.BlockSpec((tq, dv), qm(dooff)),
                  pl.BlockSpec((tq, LANES), qm(0))],
        out_specs=[pl.BlockSpec((tk, dq), lambda h, ki, qi: (ki, h)),
                   pl.BlockSpec((tk, dv), lambda h, ki, qi: (ki, h))],
        out_shape=[_sds((Sk, H * dq)), _sds((Sk, H * dv))],
        scratch_shapes=[pltpu.VMEM((tk, dq), F32), pltpu.VMEM((tk, dv), F32)],
        compiler_params=_cparams(("parallel", "parallel", "arbitrary")),
    )(q, k, v, o, do, lse)

    km = (lambda off: (lambda h, qi, ki: (jnp.minimum(ki, qi), off + h))) if causal else \
         (lambda off: (lambda h, qi, ki: (ki, off + h)))

    def body_q(q_ref, k_ref, v_ref, o_ref, do_ref, lse_ref, dq_ref, dq_acc):
        qi, ki = pl.program_id(1), pl.program_id(2)

        @pl.when(ki == 0)
        def _():
            dq_acc[...] = jnp.zeros_like(dq_acc)

        def step(masked):
            _, ds = _flash_p_ds(q_ref, k_ref, v_ref, o_ref, do_ref, lse_ref, qi, ki, tq, tk, scale, masked)
            dq_acc[...] += lax.dot_general(ds.astype(BF16), k_ref[...].astype(BF16), (((1,), (0,)), ((), ())),
                                           preferred_element_type=F32)

        if causal:
            pl.when(ki < qi)(lambda: step(False))
            pl.when(ki == qi)(lambda: step(True))
        else:
            step(False)

        @pl.when(ki == nk - 1)
        def _():
            dq_ref[...] = dq_acc[...]

    dqq, = pl.pallas_call(
        body_q, name=name + "_dq", grid=(H, nq, nk),
        in_specs=[pl.BlockSpec((tq, dq), lambda h, qi, ki: (qi, qoff + h)),
                  pl.BlockSpec((tk, dq), km(koff)), pl.BlockSpec((tk, dv), km(voff)),
                  pl.BlockSpec((tq, dv), lambda h, qi, ki: (qi, h)),
                  pl.BlockSpec((tq, dv), lambda h, qi, ki: (qi, dooff + h)),
                  pl.BlockSpec((tq, LANES), lambda h, qi, ki: (qi, h))],
        out_specs=[pl.BlockSpec((tq, dq), lambda h, qi, ki: (qi, h))],
        out_shape=[_sds((S, H * dq))],
        scratch_shapes=[pltpu.VMEM((tq, dq), F32)],
        compiler_params=_cparams(("parallel", "parallel", "arbitrary")),
    )(q, k, v, o, do, lse)
    return dqq, dk, dvv


def _shift_down(x, prev8, j):
    if j == 0:
        return x
    y = pltpu.roll(x, j, 0)
    head = pltpu.roll(prev8, j, 0)
    row = lax.broadcasted_iota(jnp.int32, x.shape, 0)
    reps = x.shape[0] // SUBLANES
    return jnp.where(row < j, jnp.tile(head, (reps, 1)), y)


def _shift_up(x, next8, j):
    if j == 0:
        return x
    n = x.shape[0]
    y = pltpu.roll(x, n - j, 0)
    tail = pltpu.roll(next8, SUBLANES - j, 0)
    row = lax.broadcasted_iota(jnp.int32, x.shape, 0)
    reps = n // SUBLANES
    return jnp.where(row >= n - j, jnp.tile(tail, (reps, 1)), y)


def _conv_pre(x_ref, p_ref, w_ref, first):
    x = x_ref[...]
    prev8 = jnp.where(first, 0.0, p_ref[...])
    w = w_ref[...]
    xs = [_shift_down(x, prev8, CONV_WIDTH - 1 - j) for j in range(CONV_WIDTH)]
    c = sum(xs[j] * w[j:j + 1, :] for j in range(CONV_WIDTH))
    return c, xs


def _conv_specs(ts, tc, C_total_blocks_off):
    rb = ts // SUBLANES
    off = C_total_blocks_off
    x_spec = pl.BlockSpec((ts, tc), lambda ci, i: (i, off + ci))
    p_spec = pl.BlockSpec((SUBLANES, tc), lambda ci, i: (jnp.maximum(i * rb - 1, 0), off + ci))
    return x_spec, p_spec


def _conv_fwd(h, w, *, C, ts, tc, name):
    S = h.shape[0]
    x_spec, p_spec = _conv_specs(ts, tc, 0)

    def body(x_ref, p_ref, w_ref, y_ref):
        c, _ = _conv_pre(x_ref, p_ref, w_ref, pl.program_id(1) == 0)
        y_ref[...] = _silu(c)

    return pl.pallas_call(
        body, name=name, grid=(C // tc, S // ts),
        in_specs=[x_spec, p_spec, pl.BlockSpec((CONV_WIDTH, tc), lambda ci, i: (0, ci))],
        out_specs=pl.BlockSpec((ts, tc), lambda ci, i: (i, ci)),
        out_shape=_sds((S, C)),
        compiler_params=_cparams(("parallel", "arbitrary")),
    )(h, h, w)


def _conv_bwd(h, w, dy, *, C, ts, tc, name):
    S = h.shape[0]
    ns = S // ts
    rb = ts // SUBLANES
    x_spec, p_spec = _conv_specs(ts, tc, 0)

    def body_a(x_ref, p_ref, w_ref, dy_ref, dc_ref, dw_ref):
        i = pl.program_id(1)
        c, xs = _conv_pre(x_ref, p_ref, w_ref, i == 0)
        sg = _sigmoid(c)
        dc = dy_ref[...] * (sg * (1.0 + c * (1.0 - sg)))
        dc_ref[...] = dc

        @pl.when(i == 0)
        def _():
            dw_ref[...] = jnp.zeros_like(dw_ref)

        dw_ref[...] += jnp.concatenate([jnp.sum(dc * xs[j], axis=0, keepdims=True) for j in range(CONV_WIDTH)], axis=0)

    dc, dw = pl.pallas_call(
        body_a, name=name + "_a", grid=(C // tc, ns),
        in_specs=[x_spec, p_spec, pl.BlockSpec((CONV_WIDTH, tc), lambda ci, i: (0, ci)),
                  pl.BlockSpec((ts, tc), lambda ci, i: (i, ci))],
        out_specs=[pl.BlockSpec((ts, tc), lambda ci, i: (i, ci)),
                   pl.BlockSpec((CONV_WIDTH, tc), lambda ci, i: (0, ci))],
        out_shape=[_sds((S, C)), _sds((CONV_WIDTH, C))],
        compiler_params=_cparams(("parallel", "arbitrary")),
    )(h, h, w, dy)

    def body_b(dc_ref, n_ref, w_ref, dx_ref):
        i = pl.program_id(1)
        dcv = dc_ref[...]
        next8 = jnp.where(i == ns - 1, 0.0, n_ref[...])
        w_ = w_ref[...]
        dx_ref[...] = sum(_shift_up(dcv, next8, CONV_WIDTH - 1 - j) * w_[j:j + 1, :] for j in range(CONV_WIDTH))

    dx = pl.pallas_call(
        body_b, name=name + "_b", grid=(C // tc, ns),
        in_specs=[pl.BlockSpec((ts, tc), lambda ci, i: (i, ci)),
                  pl.BlockSpec((SUBLANES, tc), lambda ci, i: (jnp.minimum((i + 1) * rb, ns * rb - 1), ci)),
                  pl.BlockSpec((CONV_WIDTH, tc), lambda ci, i: (0, ci))],
        out_specs=pl.BlockSpec((ts, tc), lambda ci, i: (i, ci)),
        out_shape=_sds((S, C)),
        compiler_params=_cparams(("parallel", "arbitrary")),
    )(dc, dc, w)
    return dx, dw


@jax.custom_vjp
def _nn(a, b):
    return lax.dot_general(a.astype(BF16), b.astype(BF16), (((1,), (0,)), ((), ())), preferred_element_type=F32)


@jax.custom_vjp
def _nt(a, b):
    return lax.dot_general(a.astype(BF16), b.astype(BF16), (((1,), (1,)), ((), ())), preferred_element_type=F32)


@jax.custom_vjp
def _tn(a, b):
    return lax.dot_general(a.astype(BF16), b.astype(BF16), (((0,), (0,)), ((), ())), preferred_element_type=F32)


_nn.defvjp(lambda a, b: (_nn(a, b), (a, b)), lambda r, g: (_nt(g, r[1]), _tn(r[0], g)))
_nt.defvjp(lambda a, b: (_nt(a, b), (a, b)), lambda r, g: (_nn(g, r[1]), _tn(g, r[0])))
_tn.defvjp(lambda a, b: (_tn(a, b), (a, b)), lambda r, g: (_nt(r[1], g), _nn(r[0], g)))


def _hdot(a, b, dims):
    return lax.dot_general(a, b, (dims, ((), ())), precision=HI, preferred_element_type=F32)


@jax.custom_vjp
def _nn_hi(a, b):
    return _hdot(a, b, ((1,), (0,)))


@jax.custom_vjp
def _nt_hi(a, b):
    return _hdot(a, b, ((1,), (1,)))


@jax.custom_vjp
def _tn_hi(a, b):
    return _hdot(a, b, ((0,), (0,)))


_nn_hi.defvjp(lambda a, b: (_nn_hi(a, b), (a, b)), lambda r, g: (_nt_hi(g, r[1]), _tn_hi(r[0], g)))
_nt_hi.defvjp(lambda a, b: (_nt_hi(a, b), (a, b)), lambda r, g: (_nn_hi(g, r[1]), _tn_hi(g, r[0])))
_tn_hi.defvjp(lambda a, b: (_tn_hi(a, b), (a, b)), lambda r, g: (_nt_hi(r[1], g), _nn_hi(r[0], g)))


def _gdn_chunk_fn(q, k, v, g, beta, state):
    C = CHUNK
    row = lax.broadcasted_iota(jnp.int32, (C, C), 0)
    col = lax.broadcasted_iota(jnp.int32, (C, C), 1)
    tril, strict = row >= col, row > col
    ones_tril = tril.astype(F32)
    gc = _nn_hi(ones_tril, g)
    gr = _nt_hi(jnp.full((C, C), 1.0 / C, F32), gc)
    decay = jnp.where(tril, jnp.exp(jnp.where(tril, gc - gr, 0.0)), 0.0)
    b1 = beta[:, :1]
    e_gc = jnp.exp(gc[:, :1])
    kb = k * b1
    lmat = jnp.where(strict, _nt(kb, k) * decay, 0.0)
    a = -lmat
    t = jnp.where(row == col, 1.0, 0.0) + a
    p = a
    for _ in range(5):
        p = _nn_hi(p, p)
        t = t + _nn_hi(t, p)
    rhs = jnp.concatenate([v * b1, kb * e_gc], axis=1)
    sol = _nn_hi(t, rhs)
    u, w = sol[:, :HEAD_DIM], sol[:, HEAD_DIM:]
    a_qk = jnp.where(tril, _nt(q, k) * decay, 0.0)
    gl = gc[C - 1:C, :]
    q_dec = q * e_gc
    k_dec = k * jnp.exp(gl[:, :1] - gc[:, :1])
    v_new = u - _nn(w, state)
    o = _nn(q_dec, state) + _nn(a_qk, v_new)
    new_state = state * jnp.exp(gl[:, :1]) + _tn(k_dec, v_new)
    return o, new_state


def _gdn_specs(H):
    qs = lambda off: pl.BlockSpec((CHUNK, HEAD_DIM), lambda h, n: (n, off + h))
    gs = pl.BlockSpec((None, CHUNK, CHUNK), lambda h, n: (h, n, 0))
    return qs, gs


def _gdn_fwd(q, k, v, g, beta, *, H, voff, name):
    S = q.shape[0]
    N = S // CHUNK
    qs, gs = _gdn_specs(H)

    def body(q_ref, k_ref, v_ref, g_ref, b_ref, o_ref, st_ref, state):
        @pl.when(pl.program_id(1) == 0)
        def _():
            state[...] = jnp.zeros_like(state)

        s0 = state[...]
        st_ref[...] = s0
        o, s1 = _gdn_chunk_fn(q_ref[...], k_ref[...], v_ref[...], g_ref[...], b_ref[...], s0)
        o_ref[...] = o
        state[...] = s1

    return pl.pallas_call(
        body, name=name, grid=(H, N),
        in_specs=[qs(0), qs(0), qs(voff), gs, gs],
        out_specs=[qs(0), pl.BlockSpec((None, None, HEAD_DIM, HEAD_DIM), lambda h, n: (h, n, 0, 0))],
        out_shape=[_sds((S, H * HEAD_DIM)), _sds((H, N, HEAD_DIM, HEAD_DIM))],
        scratch_shapes=[pltpu.VMEM((HEAD_DIM, HEAD_DIM), F32)],
        compiler_params=_cparams(("parallel", "arbitrary")),
    )(q, k, v, g, beta)


def _gdn_bwd(q, k, v, g, beta, states, do, *, H, voff, name):
    S = q.shape[0]
    N = S // CHUNK
    rs = lambda off: pl.BlockSpec((CHUNK, HEAD_DIM), lambda h, n: (N - 1 - n, off + h))
    gs = pl.BlockSpec((None, CHUNK, CHUNK), lambda h, n: (h, N - 1 - n, 0))

    def body(q_ref, k_ref, v_ref, g_ref, b_ref, st_ref, do_ref, dq_ref, dk_ref, dv_ref, dg_ref, db_ref, dstate):
        @pl.when(pl.program_id(1) == 0)
        def _():
            dstate[...] = jnp.zeros_like(dstate)

        _, pull = jax.vjp(_gdn_chunk_fn, q_ref[...], k_ref[...], v_ref[...], g_ref[...], b_ref[...], st_ref[...])
        dq, dk, dv, dg, db, ds = pull((do_ref[...], dstate[...]))
        dq_ref[...] = dq
        dk_ref[...] = dk
        dv_ref[...] = dv
        dg_ref[...] = dg
        db_ref[...] = db
        dstate[...] = ds

    return pl.pallas_call(
        body, name=name, grid=(H, N),
        in_specs=[rs(0), rs(0), rs(voff), gs, gs,
                  pl.BlockSpec((None, None, HEAD_DIM, HEAD_DIM), lambda h, n: (h, N - 1 - n, 0, 0)), rs(0)],
        out_specs=[rs(0), rs(0), rs(0), gs, gs],
        out_shape=[_sds((S, H * HEAD_DIM))] * 3 + [_sds((H, S, CHUNK))] * 2,
        scratch_shapes=[pltpu.VMEM((HEAD_DIM, HEAD_DIM), F32)],
        compiler_params=_cparams(("parallel", "arbitrary")),
    )(q, k, v, g, beta, states, do)


def _round_up(n, m):
    return (n + m - 1) // m * m


def _dims(S, D, M, shapes):
    c = dict(S=S, D=D, M=M)
    c["H"] = shapes["gdn_a_log"][-1]
    c["QL"] = shapes["mla_q_norm"][-1]
    c["KVL"] = shapes["mla_kv_norm"][-1]
    assert c["QL"] == c["KVL"]
    c["MEMW"] = shapes["mem_w_kv"][-1] // 2
    c["HM"] = c["MEMW"] // HEAD_DIM
    c["MW"] = c["H"] * HEAD_DIM
    c["F"] = shapes["mlp_w1"][-1]
    c["DEPTH"] = shapes["ln1_g"][0]
    c["ALPHA"] = (2 * c["DEPTH"]) ** 0.25
    c["MLA_IN"] = _round_up(c["QL"] + c["KVL"] + c["MEMW"] + LANES, 2 * LANES)
    c["GDN_IN"] = _round_up(4 * c["MW"] + c["MEMW"] + LANES, 2 * LANES)
    c["t_row"] = min(256, S)
    c["t_head"] = min(512, S)
    c["t_att"] = min(512, S)
    return c


def _pad_cols(w, n):
    return jnp.pad(w, ((0, 0), (0, n - w.shape[1])))


def _prep_mla_w_in(w, c):
    a = c["QL"] + c["KVL"]
    w = jnp.concatenate([w[:, :a], w[:, a + QK_ROPE:a + QK_ROPE + c["MEMW"]], w[:, a:a + QK_ROPE]], axis=1)
    return _pad_cols(w, c["MLA_IN"]).astype(BF16)


def _unprep_mla_w_in(dw, c):
    a, m = c["QL"] + c["KVL"], c["MEMW"]
    return jnp.concatenate([dw[:, :a], dw[:, a + m:a + m + QK_ROPE], dw[:, a:a + m]], axis=1)


def _prep_w_uq(w, c):
    w = w.reshape(c["QL"], c["H"], QK_NOPE + QK_ROPE)
    w = jnp.pad(w, ((0, 0), (0, 0), (0, QK_PAD - QK_NOPE - QK_ROPE)))
    return w.reshape(c["QL"], c["H"] * QK_PAD).astype(BF16)


def _unprep_w_uq(dw, c):
    return dw.reshape(c["QL"], c["H"], QK_PAD)[:, :, :QK_NOPE + QK_ROPE].reshape(c["QL"], c["H"] * (QK_NOPE + QK_ROPE))


def _prep_w_ukv(w, c):
    return w.reshape(c["KVL"], c["H"], 2, HEAD_DIM).transpose(0, 2, 1, 3).reshape(c["KVL"], 2 * c["MW"]).astype(BF16)


def _unprep_w_ukv(dw, c):
    return dw.reshape(c["KVL"], 2, c["H"], HEAD_DIM).transpose(0, 2, 1, 3).reshape(c["KVL"], 2 * c["MW"])


def _prep_gdn_w_in(w, c):
    a, h2 = 4 * c["MW"], 2 * c["H"]
    w = jnp.concatenate([w[:, :a], w[:, a + h2:], w[:, a:a + h2]], axis=1)
    return _pad_cols(w, c["GDN_IN"]).astype(BF16)


def _unprep_gdn_w_in(dw, c):
    a, h2, m = 4 * c["MW"], 2 * c["H"], c["MEMW"]
    return jnp.concatenate([dw[:, :a], dw[:, a + m:a + m + h2], dw[:, a:a + m]], axis=1)


def _lane_bcast(v):
    return jnp.broadcast_to(v.astype(F32)[:, None, None], (v.shape[0], 1, LANES))


def _row(i):
    return (i, 0)


def _par(i):
    return (0, 0)


def _layer_norm(z, g, b, c, name):
    S, D, ts = c["S"], c["D"], c["t_row"]
    return _ew(_ln_fn, [(z, (ts, D), _row), (g, (1, D), _par), (b, (1, D), _par)],
               [(_sds((S, D)), (ts, D), _row), (_sds((S, D), BF16), (ts, D), _row)], (S // ts,), name=name)


def _layer_norm_bwd(z, g, b, dy, c, name):
    S, D, ts = c["S"], c["D"], c["t_row"]
    fn = lambda z, g, b: _ln_fn(z, g, b)[:1]

    def both(z, g, b):
        return fn(z, g, b)

    dz, dg, db = _ew_vjp(both, [(z, (ts, D), _row), (g, (1, D), _par), (b, (1, D), _par)], [(dy, (ts, D), _row)],
                         [(_sds((S, D)), (ts, D), _row, "set"), (_sds((1, D)), (1, D), _par, "acc_all"),
                          (_sds((1, D)), (1, D), _par, "acc_all")], (S // ts,), name=name)
    return dz, dg, db


def _mem_attn_fwd(h, qoff, memkv, c, name):
    return _flash_fwd(h, memkv, memkv, H=c["HM"], dq=HEAD_DIM, dv=HEAD_DIM, qoff=qoff, koff=0, voff=c["HM"],
                      causal=False, scale=HEAD_DIM ** -0.5, tq=c["t_att"], tk=c["M"], name=name)


def _mem_attn_bwd(h, qoff, memkv, om, lsem, dcat, c, name):
    return _flash_bwd(h, memkv, memkv, om, lsem, dcat, H=c["HM"], dq=HEAD_DIM, dv=HEAD_DIM, qoff=qoff, koff=0,
                      voff=c["HM"], dooff=c["H"], causal=False, scale=HEAD_DIM ** -0.5, tq=c["t_att"], tk=c["M"],
                      name=name)


def _mla_specs(c):
    H, ts = c["H"], c["t_head"]
    kr_blk = (c["QL"] + c["KVL"] + c["MEMW"]) // LANES
    hd = lambda i, h: (i, h)
    return [((ts, QK_PAD), hd), ((ts, HEAD_DIM), hd), ((ts, LANES), lambda i, h: (i, kr_blk)),
            ((ts, LANES), lambda i, h: (i, 0)), ((ts, LANES), lambda i, h: (i, 0))]


def _mla_fwd(xb, p, cosp, sinp, c):
    S, H, QL, ts, tr = c["S"], c["H"], c["QL"], c["t_head"], c["t_row"]
    h, = _mm(xb, p["w_in"], "nn", [F32], name="mla_in")
    nq, = _ew(_rms_fn, [(h, (tr, QL), lambda i: (i, 0)), (p["q_norm"], (1, QL), _par)],
              [(_sds((S, QL), BF16), (tr, QL), _row)], (S // tr,), name="mla_qnorm")
    nkv, = _ew(_rms_fn, [(h, (tr, QL), lambda i: (i, 1)), (p["kv_norm"], (1, QL), _par)],
               [(_sds((S, QL), BF16), (tr, QL), _row)], (S // tr,), name="mla_kvnorm")
    qraw, = _mm(nq, p["w_uq"], "nn", [F32], name="mla_uq")
    kvraw, = _mm(nkv, p["w_ukv"], "nn", [BF16], name="mla_ukv")
    sp = _mla_specs(c)
    ins = [(a, b, m) for a, (b, m) in zip([qraw, kvraw, h, cosp, sinp], sp)]
    qp, kp = _ew(_mla_prep_fn, ins, [(_sds((S, H * QK_PAD), BF16), (ts, QK_PAD), lambda i, h: (i, h))] * 2,
                 (S // ts, H), name="mla_rope")
    o, lse = _flash_fwd(qp, kp, kvraw, H=H, dq=QK_PAD, dv=HEAD_DIM, qoff=0, koff=0, voff=H, causal=True,
                        scale=(QK_NOPE + QK_ROPE) ** -0.5, tq=c["t_att"], tk=c["t_att"], name="mla_attn")
    return o, dict(h=h, nq=nq, nkv=nkv, qraw=qraw, kvraw=kvraw, qp=qp, kp=kp, o=o, lse=lse)


def _mla_bwd(sv, p, cosp, sinp, dcat, dqm, c):
    S, H, QL, ts, tr = c["S"], c["H"], c["QL"], c["t_head"], c["t_row"]
    dqp, dkp, dv = _flash_bwd(sv["qp"], sv["kp"], sv["kvraw"], sv["o"], sv["lse"], dcat, H=H, dq=QK_PAD, dv=HEAD_DIM,
                              qoff=0, koff=0, voff=H, dooff=0, causal=True, scale=(QK_NOPE + QK_ROPE) ** -0.5,
                              tq=c["t_att"], tk=c["t_att"], name="mla_attn_bwd")
    sp = _mla_specs(c)
    ins = [(a, b, m) for a, (b, m) in zip([sv["qraw"], sv["kvraw"], sv["h"], cosp, sinp], sp)]
    hd = lambda i, h: (i, h)
    dqraw, dknope, dkr = _ew_vjp(
        _mla_prep_fn, ins, [(dqp, (ts, QK_PAD), hd), (dkp, (ts, QK_PAD), hd)],
        [(_sds((S, H * QK_PAD), BF16), (ts, QK_PAD), hd, "set"), (_sds((S, H * HEAD_DIM), BF16), (ts, HEAD_DIM), hd, "set"),
         (_sds((S, LANES)), (ts, LANES), lambda i, h: (i, 0), "acc"), None, None], (S // ts, H), name="mla_rope_bwd")
    dkvraw = jnp.concatenate([dknope, dv.astype(BF16)], axis=1)
    dnq, = _mm(dqraw, p["w_uq"], "nt", [F32], name="mla_uq_dx")
    dw_uq, = _mm(sv["nq"], dqraw, "tn", [F32], name="mla_uq_dw")
    dnkv, = _mm(dkvraw, p["w_ukv"], "nt", [F32], name="mla_ukv_dx")
    dw_ukv, = _mm(sv["nkv"], dkvraw, "tn", [F32], name="mla_ukv_dw")
    dcq, dgq = _ew_vjp(_rms_fn, [(sv["h"], (tr, QL), lambda i: (i, 0)), (p["q_norm"], (1, QL), _par)],
                       [(dnq, (tr, QL), _row)],
                       [(_sds((S, QL), BF16), (tr, QL), _row, "set"), (_sds((1, QL)), (1, QL), _par, "acc_all")],
                       (S // tr,), name="mla_qnorm_bwd")
    dckv, dgkv = _ew_vjp(_rms_fn, [(sv["h"], (tr, QL), lambda i: (i, 1)), (p["kv_norm"], (1, QL), _par)],
                         [(dnkv, (tr, QL), _row)],
                         [(_sds((S, QL), BF16), (tr, QL), _row, "set"), (_sds((1, QL)), (1, QL), _par, "acc_all")],
                         (S // tr,), name="mla_kvnorm_bwd")
    pad = c["MLA_IN"] - (2 * QL + c["MEMW"] + LANES)
    dh = jnp.concatenate([dcq, dckv, dqm.astype(BF16), dkr.astype(BF16)] + ([jnp.zeros((S, pad), BF16)] if pad else []),
                         axis=1)
    grads = dict(mla_q_norm=dgq[0], mla_kv_norm=dgkv[0], mla_w_uq=_unprep_w_uq(dw_uq, c),
                 mla_w_ukv=_unprep_w_ukv(dw_ukv, c))
    return dh, grads


def _gdn_ins(h, qkvc, p, c):
    H, ts = c["H"], c["t_head"]
    ab_blk = (4 * c["MW"] + c["MEMW"]) // LANES
    qk_ins = [(qkvc, (ts, HEAD_DIM), lambda i, h: (i, h)), (qkvc, (ts, HEAD_DIM), lambda i, h: (i, H + h))]
    gate_ins = [(h, (ts, LANES), lambda i, h: (i, ab_blk)), (p["a_log"], (None, 1, LANES), lambda i, h: (h, 0, 0)),
                (p["dt_bias"], (None, 1, LANES), lambda i, h: (h, 0, 0))]
    return qk_ins, gate_ins


def _gdn_out_ins(o, h, p, c):
    H, ts = c["H"], c["t_head"]
    return [(o, (ts, HEAD_DIM), lambda i, h: (i, h)), (h, (ts, HEAD_DIM), lambda i, h: (i, 3 * H + h)),
            (p["o_norm"], (1, HEAD_DIM), lambda i, h: (0, 0))]


def _gdn_layer_fwd(xb, p, c):
    S, H, MW, ts = c["S"], c["H"], c["MW"], c["t_head"]
    h, = _mm(xb, p["w_in"], "nn", [F32], name="gdn_in")
    tc = _tile(3 * MW, 512, LANES)
    qkvc = _conv_fwd(h, p["conv"], C=3 * MW, ts=ts, tc=tc, name="gdn_conv")
    qk_ins, gate_ins = _gdn_ins(h, qkvc, p, c)
    hd = lambda i, h: (i, h)
    qn, kn = _ew(_gdn_qk_fn, qk_ins, [(_sds((S, MW)), (ts, HEAD_DIM), hd)] * 2, (S // ts, H), name="gdn_qknorm")
    g3 = lambda i, h: (h, i, 0)
    g, beta = _ew(_gdn_gate_fn(H, 1), gate_ins, [(_sds((H, S, CHUNK)), (None, ts, CHUNK), g3)] * 2, (S // ts, H),
                  name="gdn_gate")
    o, states = _gdn_fwd(qn, kn, qkvc, g, beta, H=H, voff=2 * H, name="gdn_delta")
    mix, = _ew(_gdn_out_fn, _gdn_out_ins(o, h, p, c), [(_sds((S, MW)), (ts, HEAD_DIM), hd)], (S // ts, H),
               name="gdn_outnorm")
    return mix, dict(h=h, qkvc=qkvc, qn=qn, kn=kn, g=g, beta=beta, o=o, states=states)


def _gdn_layer_bwd(sv, p, dcat, dqm, c):
    S, H, MW, ts = c["S"], c["H"], c["MW"], c["t_head"]
    hd = lambda i, h: (i, h)
    g3 = lambda i, h: (h, i, 0)
    h, qkvc = sv["h"], sv["qkvc"]
    do, dz, d_onorm = _ew_vjp(_gdn_out_fn, _gdn_out_ins(sv["o"], h, p, c), [(dcat, (ts, HEAD_DIM), hd)],
                              [(_sds((S, MW)), (ts, HEAD_DIM), hd, "set"), (_sds((S, MW), BF16), (ts, HEAD_DIM), hd, "set"),
                               (_sds((1, HEAD_DIM)), (1, HEAD_DIM), lambda i, h: (0, 0), "acc_all")],
                              (S // ts, H), name="gdn_outnorm_bwd")
    dqn, dkn, dv, dg, db = _gdn_bwd(sv["qn"], sv["kn"], qkvc, sv["g"], sv["beta"], sv["states"], do, H=H, voff=2 * H,
                                    name="gdn_delta_bwd")
    qk_ins, gate_ins = _gdn_ins(h, qkvc, p, c)
    dqc, dkc = _ew_vjp(_gdn_qk_fn, qk_ins, [(dqn, (ts, HEAD_DIM), hd), (dkn, (ts, HEAD_DIM), hd)],
                       [(_sds((S, MW)), (ts, HEAD_DIM), hd, "set")] * 2, (S // ts, H), name="gdn_qknorm_bwd")
    full3 = lambda i, h: (0, 0, 0)
    dab, dalog, ddt = _ew_vjp(
        _gdn_gate_fn(H, 1), gate_ins, [(dg, (None, ts, CHUNK), g3), (db, (None, ts, CHUNK), g3)],
        [(_sds((S, LANES), BF16), (ts, LANES), lambda i, h: (i, 0), "acc"),
         (_sds((H, 1, LANES)), (H, 1, LANES), full3, ("acc_at", 1)),
         (_sds((H, 1, LANES)), (H, 1, LANES), full3, ("acc_at", 1))], (S // ts, H), name="gdn_gate_bwd")
    dqkvc = jnp.concatenate([dqc, dkc, dv], axis=1)
    tc = _tile(3 * MW, 512, LANES)
    dxc, dconv = _conv_bwd(h, p["conv"], dqkvc, C=3 * MW, ts=ts, tc=tc, name="gdn_conv_bwd")
    pad = c["GDN_IN"] - (4 * MW + c["MEMW"] + LANES)
    dh = jnp.concatenate([dxc.astype(BF16), dz, dqm.astype(BF16), dab] + ([jnp.zeros((S, pad), BF16)] if pad else []),
                         axis=1)
    grads = dict(gdn_conv=dconv, gdn_a_log=jnp.sum(dalog[:, 0, :], axis=-1), gdn_dt_bias=jnp.sum(ddt[:, 0, :], axis=-1),
                 gdn_o_norm=d_onorm[0])
    return dh, grads


def _prep_layer(W, i, c):
    j = i // 2
    p = dict(mem_w_kv=W["mem_w_kv"][i].astype(BF16), w_out=W["w_out"][i].astype(BF16),
             w1=W["mlp_w1"][i].astype(BF16), w2=W["mlp_w2"][i].astype(BF16),
             ln1_g=W["ln1_g"][i][None].astype(F32), ln1_b=W["ln1_b"][i][None].astype(F32),
             ln2_g=W["ln2_g"][i][None].astype(F32), ln2_b=W["ln2_b"][i][None].astype(F32))
    if i % 2 == 0:
        p.update(w_in=_prep_mla_w_in(W["mla_w_in"][j], c), q_norm=W["mla_q_norm"][j][None].astype(F32),
                 w_uq=_prep_w_uq(W["mla_w_uq"][j], c), kv_norm=W["mla_kv_norm"][j][None].astype(F32),
                 w_ukv=_prep_w_ukv(W["mla_w_ukv"][j], c))
    else:
        p.update(w_in=_prep_gdn_w_in(W["gdn_w_in"][j], c), conv=W["gdn_conv"][j].astype(F32),
                 a_log=_lane_bcast(W["gdn_a_log"][j]), dt_bias=_lane_bcast(W["gdn_dt_bias"][j]),
                 o_norm=W["gdn_o_norm"][j][None].astype(F32))
    return p


def _local_step(x, mem, positions, W, loss_target, c):
    S, D, H, MW, ALPHA = c["S"], c["D"], c["H"], c["MW"], c["ALPHA"]
    inv_freq = 1.0 / (ROPE_THETA ** (jnp.arange(0, QK_ROPE, 2, dtype=F32) / QK_ROPE))
    ang = positions.astype(F32)[:, None] * inv_freq
    cos, sin = jnp.cos(ang), jnp.sin(ang)
    cosp = jnp.concatenate([cos, cos, jnp.ones((S, LANES - QK_ROPE), F32)], axis=1)
    sinp = jnp.concatenate([sin, sin, jnp.zeros((S, LANES - QK_ROPE), F32)], axis=1)
    memb = mem.astype(BF16)
    xf, xb = x, x.astype(BF16)
    saved, params = [], []
    for i in range(c["DEPTH"]):
        p = _prep_layer(W, i, c)
        mla = i % 2 == 0
        memkv, = _mm(memb, p["mem_w_kv"], "nn", [BF16], name="mem_kv")
        if mla:
            mix, sv = _mla_fwd(xb, p, cosp, sinp, c)
            qoff = (c["QL"] + c["KVL"]) // LANES
        else:
            mix, sv = _gdn_layer_fwd(xb, p, c)
            qoff = 4 * MW // LANES
        om, lsem = _mem_attn_fwd(sv["h"], qoff, memkv, c, "mem_attn")
        cat = jnp.concatenate([mix, om], axis=1).astype(BF16)
        z1, = _mm(cat, p["w_out"], "nn", [F32], name="w_out", extras=(xf,), epilogue=lambda acc, r: (ALPHA * r + acc,))
        x1, x1b = _layer_norm(z1, p["ln1_g"], p["ln1_b"], c, "ln1")
        u, a = _mm(x1b, p["w1"], "nn", [F32, BF16], name="mlp_up",
                   epilogue=lambda acc: (acc, jnp.square(jnp.maximum(acc, 0.0))))
        z2, = _mm(a, p["w2"], "nn", [F32], name="mlp_down", extras=(x1,), epilogue=lambda acc, r: (ALPHA * r + acc,))
        x2, x2b = _layer_norm(z2, p["ln2_g"], p["ln2_b"], c, "ln2")
        sv.update(xb=xb, memkv=memkv, om=om, lsem=lsem, cat=cat, z1=z1, x1b=x1b, u=u, a=a, z2=z2, qoff=qoff)
        saved.append(sv)
        params.append(p)
        xf, xb = x2, x2b

    ts = c["t_row"]
    lsum, dy = _ew(_loss_fn, [(xf, (ts, D), _row), (loss_target, (ts, D), _row)],
                   [(_sds((1, D)), (1, D), _par), (_sds((S, D)), (ts, D), _row)], (S // ts,), name="loss", acc_out=(0,))
    loss = jnp.sum(lsum)

    G = {k: [None] * W[k].shape[0] for k in W}
    dx = dy
    for i in reversed(range(c["DEPTH"])):
        p, sv = params[i], saved[i]
        j = i // 2
        mla = i % 2 == 0
        dz2, dg, db = _layer_norm_bwd(sv["z2"], p["ln2_g"], p["ln2_b"], dx, c, "ln2_bwd")
        G["ln2_g"][i], G["ln2_b"][i] = dg[0], db[0]
        dz2b = dz2.astype(BF16)
        du, = _mm(dz2b, p["w2"], "nt", [BF16], name="mlp_down_dx", extras=(sv["u"],),
                  epilogue=lambda acc, u: (acc * (2.0 * jnp.maximum(u, 0.0)),))
        G["mlp_w2"][i], = _mm(sv["a"], dz2b, "tn", [F32], name="mlp_down_dw")
        G["mlp_w1"][i], = _mm(sv["x1b"], du, "tn", [F32], name="mlp_up_dw")
        dx1, = _mm(du, p["w1"], "nt", [F32], name="mlp_up_dx", extras=(dz2,), epilogue=lambda acc, r: (ALPHA * r + acc,))
        dz1, dg, db = _layer_norm_bwd(sv["z1"], p["ln1_g"], p["ln1_b"], dx1, c, "ln1_bwd")
        G["ln1_g"][i], G["ln1_b"][i] = dg[0], db[0]
        dz1b = dz1.astype(BF16)
        dcat, = _mm(dz1b, p["w_out"], "nt", [BF16], name="w_out_dx")
        G["w_out"][i], = _mm(sv["cat"], dz1b, "tn", [F32], name="w_out_dw")
        dqm, dkm, dvm = _mem_attn_bwd(sv["h"], sv["qoff"], sv["memkv"], sv["om"], sv["lsem"], dcat, c, "mem_attn_bwd")
        dmemkv = jnp.concatenate([dkm, dvm], axis=1).astype(BF16)
        G["mem_w_kv"][i], = _mm(memb, dmemkv, "tn", [F32], name="mem_kv_dw")
        if mla:
            dh, g = _mla_bwd(sv, p, cosp, sinp, dcat, dqm, c)
            for k_, v_ in g.items():
                G[k_][j] = v_
            dw_in, = _mm(sv["xb"], dh, "tn", [F32], name="mla_in_dw")
            G["mla_w_in"][j] = _unprep_mla_w_in(dw_in, c)
            dx, = _mm(dh, p["w_in"], "nt", [F32], name="mla_in_dx", extras=(dz1,),
                      epilogue=lambda acc, r: (ALPHA * r + acc,))
        else:
            dh, g = _gdn_layer_bwd(sv, p, dcat, dqm, c)
            for k_, v_ in g.items():
                G[k_][j] = v_
            dw_in, = _mm(sv["xb"], dh, "tn", [F32], name="gdn_in_dw")
            G["gdn_w_in"][j] = _unprep_gdn_w_in(dw_in, c)
            dx, = _mm(dh, p["w_in"], "nt", [F32], name="gdn_in_dx", extras=(dz1,),
                      epilogue=lambda acc, r: (ALPHA * r + acc,))
    grads = {k: jnp.stack(v, axis=0) for k, v in G.items()}
    return loss, dx, grads


N_CHIPS = 4
N_PIECES = 8
PIECE_ROWS = 1024
_HBM = pl.BlockSpec(memory_space=pltpu.HBM)
_VMEM = pl.BlockSpec(memory_space=pltpu.VMEM)


def _my_place():
    return lax.axis_index("x"), lax.axis_index("y"), lax.axis_index("c")


def _other_chips(x, y):
    return [(1 - x, y), (x, 1 - y), (1 - x, 1 - y)]


def _gather_chips(piece, name):
    R = piece.shape[0]

    def body(x_ref, out_ref, send_sems, recv_sems, local_sem):
        x, y, c = _my_place()
        chips = _other_chips(x, y)
        mine = pltpu.make_async_copy(x_ref, out_ref.at[2 * x + y], local_sem)
        mine.start()

        def copy(k, slab, to):
            return pltpu.make_async_remote_copy(src_ref=x_ref, dst_ref=out_ref.at[slab], send_sem=send_sems.at[k],
                                                recv_sem=recv_sems.at[k], device_id=to, device_id_type=MESH)

        sends = [copy(k, 2 * x + y, (cx, cy, c)) for k, (cx, cy) in enumerate(chips)]
        for cp in sends:
            cp.start()
        for k, (cx, cy) in enumerate(chips):
            copy(k, 2 * cx + cy, (cx, cy, c)).wait_recv()
        for cp in sends:
            cp.wait_send()
        mine.wait()

    return pl.pallas_call(
        body, name=name, in_specs=[_HBM], out_specs=_HBM,
        out_shape=jax.ShapeDtypeStruct((N_CHIPS, R, LANES), piece.dtype),
        scratch_shapes=[pltpu.SemaphoreType.DMA((3,)), pltpu.SemaphoreType.DMA((3,)), pltpu.SemaphoreType.DMA],
    )(piece)


def _swap_halves(g, name):
    _, n, R, _ = g.shape

    def body(g_ref, out_ref, send_sem, recv_sem):
        x, y, c = _my_place()
        cp = pltpu.make_async_remote_copy(src_ref=g_ref.at[1 - c], dst_ref=out_ref, send_sem=send_sem,
                                          recv_sem=recv_sem, device_id=(x, y, 1 - c), device_id_type=MESH)
        cp.start()
        cp.wait()

    return pl.pallas_call(
        body, name=name, in_specs=[_HBM], out_specs=_HBM, out_shape=jax.ShapeDtypeStruct((n, R, LANES), g.dtype),
        scratch_shapes=[pltpu.SemaphoreType.DMA, pltpu.SemaphoreType.DMA],
    )(g)


def _scatter_chips(p, name):
    _, R, _ = p.shape

    def body(p_ref, out_ref, send_sems, recv_sems):
        x, y, c = _my_place()
        cps = [pltpu.make_async_remote_copy(src_ref=p_ref.at[2 * cx + cy], dst_ref=out_ref.at[k],
                                            send_sem=send_sems.at[k], recv_sem=recv_sems.at[k],
                                            device_id=(cx, cy, c), device_id_type=MESH)
               for k, (cx, cy) in enumerate(_other_chips(x, y))]
        for cp in cps:
            cp.start()
        for cp in cps:
            cp.wait()

    return pl.pallas_call(
        body, name=name, in_specs=[_HBM], out_specs=_HBM, out_shape=jax.ShapeDtypeStruct((3, R, LANES), p.dtype),
        scratch_shapes=[pltpu.SemaphoreType.DMA((3,)), pltpu.SemaphoreType.DMA((3,))],
    )(p)


def _join_halves(f, name):
    R = f.shape[0]

    def body(f_ref, out_ref, send_sem, recv_sem, local_sem):
        x, y, c = _my_place()
        mine = pltpu.make_async_copy(f_ref, out_ref.at[c], local_sem)
        mine.start()
        send = pltpu.make_async_remote_copy(src_ref=f_ref, dst_ref=out_ref.at[c], send_sem=send_sem, recv_sem=recv_sem,
                                            device_id=(x, y, 1 - c), device_id_type=MESH)
        send.start()
        pltpu.make_async_remote_copy(src_ref=f_ref, dst_ref=out_ref.at[1 - c], send_sem=send_sem, recv_sem=recv_sem,
                                     device_id=(x, y, 1 - c), device_id_type=MESH).wait_recv()
        send.wait_send()
        mine.wait()

    return pl.pallas_call(
        body, name=name, in_specs=[_HBM], out_specs=_HBM, out_shape=jax.ShapeDtypeStruct((2, R, LANES), f.dtype),
        scratch_shapes=[pltpu.SemaphoreType.DMA, pltpu.SemaphoreType.DMA, pltpu.SemaphoreType.DMA],
    )(f)


def _all_reduce_small(v, name):
    r = v.shape[0]
    masks = [(mx, my, mc) for mx in (0, 1) for my in (0, 1) for mc in (0, 1)][1:]

    def body(v_ref, out_ref, gath, send_sems, recv_sems):
        x, y, c = _my_place()
        me = 4 * x + 2 * y + c
        gath[me] = v_ref[...]

        def peer(m):
            return (x + m[0] - 2 * x * m[0], y + m[1] - 2 * y * m[1], c + m[2] - 2 * c * m[2])

        def copy(k, slab, to):
            return pltpu.make_async_remote_copy(src_ref=v_ref, dst_ref=gath.at[slab], send_sem=send_sems.at[k],
                                                recv_sem=recv_sems.at[k], device_id=to, device_id_type=MESH)

        sends = [copy(k, me, peer(m)) for k, m in enumerate(masks)]
        for cp in sends:
            cp.start()
        for k, m in enumerate(masks):
            px, py, pc = peer(m)
            copy(k, 4 * px + 2 * py + pc, (px, py, pc)).wait_recv()
        for cp in sends:
            cp.wait_send()
        total = gath[0]
        for d in range(1, 8):
            total = total + gath[d]
        out_ref[...] = total

    return pl.pallas_call(
        body, name=name, in_specs=[_VMEM], out_specs=_VMEM, out_shape=jax.ShapeDtypeStruct((r, LANES), F32),
        scratch_shapes=[pltpu.VMEM((8, r, LANES), F32), pltpu.SemaphoreType.DMA((7,)), pltpu.SemaphoreType.DMA((7,))],
    )(v)


def _add_rows(arrs, name):
    R = arrs[0].shape[0]
    tr = _tile(R, 4096, SUBLANES)
    spec = ((tr, LANES), _row)
    return _ew(lambda *a: (functools.reduce(lambda p, q: p + q, a),), [(a, *spec) for a in arrs],
               [(_sds((R, LANES)), *spec)], (R // tr,), name=name)[0]


def _pack_rows(arrs, dtype, row_mult):
    flat = jnp.concatenate([a.astype(dtype).reshape(-1) for a in arrs])
    n = flat.shape[0]
    rows = _round_up(-(-n // LANES), row_mult)
    return jnp.pad(flat, (0, rows * LANES - n)).reshape(rows, LANES)


def _unpack_rows(buf, shapes):
    lead = buf.shape[:-2]
    flat = buf.reshape(lead + (-1,))
    out, o = [], 0
    for s in shapes:
        n = math.prod(s)
        out.append(lax.slice_in_dim(flat, o, o + n, axis=len(lead)).reshape(lead + tuple(s)))
        o += n
    return out


WEIGHTS = ["mla_w_in", "mla_q_norm", "mla_w_uq", "mla_kv_norm", "mla_w_ukv", "gdn_w_in", "gdn_conv", "gdn_a_log",
           "gdn_dt_bias", "gdn_o_norm", "mem_w_kv", "w_out", "ln1_g", "ln1_b", "mlp_w1", "mlp_w2", "ln2_g", "ln2_b"]
SHARD_AXIS = {"mla_w_in": 1, "mla_w_uq": 2, "mla_w_ukv": 2, "gdn_w_in": 2, "gdn_conv": 2, "mem_w_kv": 1, "w_out": 1,
              "mlp_w1": 2, "mlp_w2": 1}
SHARDED = [k for k in WEIGHTS if k in SHARD_AXIS]
REPLICATED = [k for k in WEIGHTS if k not in SHARD_AXIS]


def _gather_weights(w):
    shapes = [w[k].shape for k in SHARDED]
    buf = _pack_rows([w[k] for k in SHARDED], BF16, N_PIECES * PIECE_ROWS)
    rp = buf.shape[0] // N_PIECES
    pieces = [_gather_chips(buf[i * rp:(i + 1) * rp], "gather_weights") for i in range(N_PIECES)]
    allb = jnp.concatenate(pieces, axis=1)
    parts = _unpack_rows(allb, shapes)
    return {k: jnp.concatenate([p[d] for d in range(N_CHIPS)], axis=SHARD_AXIS[k]) for k, p in zip(SHARDED, parts)}


def _reduce_grads(G, shard_shapes):
    x, y, c = _my_place()
    flats = []
    for d in range(N_CHIPS):
        parts = []
        for k in SHARDED:
            ax = SHARD_AXIS[k]
            n = shard_shapes[k][ax]
            parts.append(lax.slice_in_dim(G[k], d * n, (d + 1) * n, axis=ax))
        flats.append(_pack_rows(parts, F32, 2 * N_PIECES * PIECE_ROWS))
    rows = flats[0].shape[0]
    rp = rows // (2 * N_PIECES)
    g = jnp.stack([f.reshape(2, N_PIECES, rp, LANES) for f in flats], axis=1)
    done = []
    for i in range(N_PIECES):
        gi = g[:, :, i]
        theirs = _swap_halves(gi, "grad_swap_halves")
        mine = lax.dynamic_index_in_dim(gi, c, 0, keepdims=False)
        chip_sum = _add_rows([mine.reshape(-1, LANES), theirs.reshape(-1, LANES)], "grad_add_core")
        chip_sum = chip_sum.reshape(N_CHIPS, rp, LANES)
        got = _scatter_chips(chip_sum, "grad_scatter_chips")
        own = lax.dynamic_index_in_dim(chip_sum, 2 * x + y, 0, keepdims=False)
        half = _add_rows([own, got[0], got[1], got[2]], "grad_add_chips")
        done.append(_join_halves(half, "grad_join_halves"))
    red = jnp.stack(done, axis=1).reshape(rows, LANES)
    return dict(zip(SHARDED, _unpack_rows(red, [shard_shapes[k] for k in SHARDED])))


def _adamw(w, g, m, v, name):
    shape = w.shape
    cols = shape[-1]
    rows = math.prod(shape[:-1])
    tr = _tile(rows, max(SUBLANES, (1 << 19) // cols // SUBLANES * SUBLANES), SUBLANES)
    spec = ((tr, cols), _row)
    outs = _ew(_adamw_fn, [(a.reshape(rows, cols), *spec) for a in (w, g, m, v)], [(_sds((rows, cols)), *spec)] * 3,
               (rows // tr,), name=name)
    return [o.reshape(shape) for o in outs]


def kernel(x, mem, positions, mla_w_in, mla_q_norm, mla_w_uq, mla_kv_norm, mla_w_ukv, gdn_w_in, gdn_conv, gdn_a_log, gdn_dt_bias, gdn_o_norm, mem_w_kv, w_out, ln1_g, ln1_b, mlp_w1, mlp_w2, ln2_g, ln2_b, loss_target, m_mla_w_in, m_mla_q_norm, m_mla_w_uq, m_mla_kv_norm, m_mla_w_ukv, m_gdn_w_in, m_gdn_conv, m_gdn_a_log, m_gdn_dt_bias, m_gdn_o_norm, m_mem_w_kv, m_w_out, m_ln1_g, m_ln1_b, m_mlp_w1, m_mlp_w2, m_ln2_g, m_ln2_b, v_mla_w_in, v_mla_q_norm, v_mla_w_uq, v_mla_kv_norm, v_mla_w_ukv, v_gdn_w_in, v_gdn_conv, v_gdn_a_log, v_gdn_dt_bias, v_gdn_o_norm, v_mem_w_kv, v_w_out, v_ln1_g, v_ln1_b, v_mlp_w1, v_mlp_w2, v_ln2_g, v_ln2_b):
    w = dict(zip(WEIGHTS, (mla_w_in, mla_q_norm, mla_w_uq, mla_kv_norm, mla_w_ukv, gdn_w_in, gdn_conv, gdn_a_log,
                           gdn_dt_bias, gdn_o_norm, mem_w_kv, w_out, ln1_g, ln1_b, mlp_w1, mlp_w2, ln2_g, ln2_b)))
    m = dict(zip(WEIGHTS, (m_mla_w_in, m_mla_q_norm, m_mla_w_uq, m_mla_kv_norm, m_mla_w_ukv, m_gdn_w_in, m_gdn_conv,
                           m_gdn_a_log, m_gdn_dt_bias, m_gdn_o_norm, m_mem_w_kv, m_w_out, m_ln1_g, m_ln1_b, m_mlp_w1,
                           m_mlp_w2, m_ln2_g, m_ln2_b)))
    v = dict(zip(WEIGHTS, (v_mla_w_in, v_mla_q_norm, v_mla_w_uq, v_mla_kv_norm, v_mla_w_ukv, v_gdn_w_in, v_gdn_conv,
                           v_gdn_a_log, v_gdn_dt_bias, v_gdn_o_norm, v_mem_w_kv, v_w_out, v_ln1_g, v_ln1_b, v_mlp_w1,
                           v_mlp_w2, v_ln2_g, v_ln2_b)))
    assert x.shape[0] == 1, "one sequence per device"
    shard_shapes = {k: w[k].shape for k in SHARDED}
    full_shapes = {k: w[k].shape for k in WEIGHTS}
    for k, ax in SHARD_AXIS.items():
        s = list(w[k].shape)
        s[ax] *= N_CHIPS
        full_shapes[k] = tuple(s)
    c = _dims(x.shape[1], x.shape[2], mem.shape[1], full_shapes)

    W = dict(w)
    W.update(_gather_weights(w))
    loss_local, grad_x, G = _local_step(x[0], mem[0], positions[0], W, loss_target[0], c)
    loss = lax.psum(loss_local, ("x", "y", "c"))

    grads = _reduce_grads(G, shard_shapes)
    small_shapes = [w[k].shape for k in REPLICATED]
    gsmall = _all_reduce_small(_pack_rows([G[k] for k in REPLICATED], F32, SUBLANES), "grad_all_reduce_small")
    grads.update(dict(zip(REPLICATED, _unpack_rows(gsmall, small_shapes))))

    delta, new_m, new_v = {}, {}, {}
    for k in SHARDED:
        delta[k], new_m[k], new_v[k] = _adamw(w[k], grads[k], m[k], v[k], "adamw")
    packed = [_pack_rows([d[k] for k in REPLICATED], F32, SUBLANES) for d in (w, m, v)]
    ds, ms, vs = _adamw(packed[0], gsmall, packed[1], packed[2], "adamw_small")
    for d, buf in ((delta, ds), (new_m, ms), (new_v, vs)):
        d.update(dict(zip(REPLICATED, _unpack_rows(buf, small_shapes))))

    return (loss, grad_x[None], *[grads[k] for k in WEIGHTS], *[delta[k] for k in WEIGHTS],
            *[new_m[k] for k in WEIGHTS], *[new_v[k] for k in WEIGHTS])
```

```python
import functools
import math

import jax
import jax.numpy as jnp
from jax import lax
from jax.experimental import pallas as pl
from jax.experimental.pallas import tpu as pltpu

F32 = jnp.float32
BF16 = jnp.bfloat16
MESH = pl.DeviceIdType.MESH

LANES = 128
SUBLANES = 8
VMEM_LIMIT = 56 * 1024 * 1024
N_CHIPS = 4

HEAD_DIM = 128
QK_NOPE = 128
QK_ROPE = 64
QK_PAD = 256
ROPE_THETA = 10000.0
CONV_WIDTH = 4
CHUNK = 64
LN_EPS = 1e-5
RMS_EPS = 1e-6
ADAM_LR = 0.001
ADAM_B1 = 0.9
ADAM_B2 = 0.999
ADAM_EPS = 1e-08
ADAM_WD = 0.01
ADAM_STEP = 10
HI = lax.Precision.HIGHEST


def _cparams(sem=None):
    return pltpu.CompilerParams(dimension_semantics=sem, vmem_limit_bytes=VMEM_LIMIT)


def _tile(n, cap, unit):
    best = None
    t = unit
    while t <= min(n, cap):
        if n % t == 0:
            best = t
        t += unit
    return best if best is not None else n


def _mm(a, b, mode, out_dtypes, *, name, epilogue=None, extras=(), tm_cap=1024, tn_cap=1024, tk_cap=512,
        b_major=False, out_major=False):
    if b_major:
        b_shape = (b.shape[1], N_CHIPS * b.shape[2])
    else:
        b_shape = b.shape
    if mode == "nn":
        (M, K), (K2, N) = a.shape, b_shape
    elif mode == "nt":
        (M, K), (N, K2) = a.shape, b_shape
    else:
        (K, M), (K2, N) = a.shape, b_shape
    assert K == K2, (a.shape, b.shape, mode)
    tm = _tile(M, tm_cap, LANES if mode == "tn" else 16)
    tn = _tile(N // N_CHIPS if (out_major or (b_major and mode == "nn")) else N, tn_cap, LANES)
    tk = _tile(K // N_CHIPS if (b_major and mode == "nt") else K, tk_cap, 16 if mode == "tn" else LANES)
    nk = K // tk
    nj4, nk4 = max(N // N_CHIPS // tn, 1), max(K // N_CHIPS // tk, 1)
    if mode == "nn":
        a_spec = pl.BlockSpec((tm, tk), lambda i, j, k: (i, k))
        b_spec = pl.BlockSpec((tk, tn), lambda i, j, k: (k, j))
        if b_major:
            b_spec = pl.BlockSpec((None, tk, tn), lambda i, j, k: (j // nj4, k, j % nj4))
        dims = (((1,), (0,)), ((), ()))
    elif mode == "nt":
        a_spec = pl.BlockSpec((tm, tk), lambda i, j, k: (i, k))
        b_spec = pl.BlockSpec((tn, tk), lambda i, j, k: (j, k))
        if b_major:
            b_spec = pl.BlockSpec((None, tn, tk), lambda i, j, k: (k // nk4, j, k % nk4))
        dims = (((1,), (1,)), ((), ()))
    else:
        assert not b_major
        a_spec = pl.BlockSpec((tk, tm), lambda i, j, k: (k, i))
        b_spec = pl.BlockSpec((tk, tn), lambda i, j, k: (k, j))
        dims = (((0,), (0,)), ((), ()))
    mn_spec = pl.BlockSpec((tm, tn), lambda i, j, k: (i, j))
    o_spec, o_shape = mn_spec, (M, N)
    if out_major:
        o_spec = pl.BlockSpec((None, tm, tn), lambda i, j, k: (j // nj4, i, j % nj4))
        o_shape = (N_CHIPS, M, N // N_CHIPS)
    n_ex, n_out = len(extras), len(out_dtypes)
    for e in extras:
        assert e.shape == (M, N), (e.shape, M, N)

    def body(a_ref, b_ref, *rest):
        ex_refs, out_refs, acc = rest[:n_ex], rest[n_ex:n_ex + n_out], rest[-1]
        k = pl.program_id(2)

        @pl.when(k == 0)
        def _():
            acc[...] = jnp.zeros_like(acc)

        acc[...] += lax.dot_general(a_ref[...].astype(BF16), b_ref[...].astype(BF16), dims,
                                    preferred_element_type=F32)

        @pl.when(k == nk - 1)
        def _():
            res = (acc[...],) if epilogue is None else epilogue(acc[...], *[e[...] for e in ex_refs])
            for o_ref, r in zip(out_refs, res):
                o_ref[...] = r.astype(o_ref.dtype)

    outs = pl.pallas_call(
        body, name=name, grid=(M // tm, N // tn, nk),
        in_specs=[a_spec, b_spec] + [mn_spec] * n_ex,
        out_specs=[o_spec] * n_out,
        out_shape=[jax.ShapeDtypeStruct(o_shape, d) for d in out_dtypes],
        scratch_shapes=[pltpu.VMEM((tm, tn), F32)],
        compiler_params=_cparams(("parallel", "parallel", "arbitrary")),
    )(a, b, *extras)
    return outs


def _spec(block, imap):
    return pl.BlockSpec(block, imap)


def _ew(fn, ins, outs, grid, *, name, acc_out=()):
    n_in = len(ins)
    ng = len(grid)

    def body(*refs):
        in_refs, out_refs = refs[:n_in], refs[n_in:]
        res = fn(*[r[...] for r in in_refs])
        first = functools.reduce(jnp.logical_and, [pl.program_id(d) == 0 for d in range(ng)])
        for i, (o_ref, r) in enumerate(zip(out_refs, res)):
            if i in acc_out:
                @pl.when(first)
                def _(o_ref=o_ref):
                    o_ref[...] = jnp.zeros_like(o_ref)
                o_ref[...] += r.astype(o_ref.dtype)
            else:
                o_ref[...] = r.astype(o_ref.dtype)

    return pl.pallas_call(
        body, name=name, grid=grid,
        in_specs=[_spec(b, m) for _, b, m in ins],
        out_specs=[_spec(b, m) for _, b, m in outs],
        out_shape=[s for s, _, _ in outs],
        compiler_params=_cparams(("arbitrary",) * ng),
    )(*[a for a, _, _ in ins])


def _ew_vjp(fn, ins, cts, gouts, grid, *, name):
    n_in, n_ct = len(ins), len(cts)
    ng = len(grid)
    want = [i for i, g in enumerate(gouts) if g is not None]

    def body(*refs):
        in_refs, ct_refs, out_refs = refs[:n_in], refs[n_in:n_in + n_ct], refs[n_in + n_ct:]
        prim = [r[...] for r in in_refs]
        outs, pull = jax.vjp(fn, *prim)
        grads = pull(tuple(r[...].astype(o.dtype) for r, o in zip(ct_refs, outs)))
        first_all = functools.reduce(jnp.logical_and, [pl.program_id(d) == 0 for d in range(ng)])
        for o_ref, i in zip(out_refs, want):
            mode = gouts[i][3]
            g = grads[i]
            if mode == "set":
                o_ref[...] = g.astype(o_ref.dtype)
            elif mode == "acc":
                @pl.when(pl.program_id(ng - 1) == 0)
                def _(o_ref=o_ref):
                    o_ref[...] = jnp.zeros_like(o_ref)
                o_ref[...] += g.astype(o_ref.dtype)
            elif mode == "acc_all":
                @pl.when(first_all)
                def _(o_ref=o_ref):
                    o_ref[...] = jnp.zeros_like(o_ref)
                o_ref[...] += g.astype(o_ref.dtype)
            else:
                @pl.when(first_all)
                def _(o_ref=o_ref):
                    o_ref[...] = jnp.zeros_like(o_ref)
                idx = pl.program_id(mode[1])
                o_ref[idx] += g.astype(o_ref.dtype)

    return pl.pallas_call(
        body, name=name, grid=grid,
        in_specs=[_spec(b, m) for _, b, m in ins] + [_spec(b, m) for _, b, m in cts],
        out_specs=[_spec(gouts[i][1], gouts[i][2]) for i in want],
        out_shape=[gouts[i][0] for i in want],
        compiler_params=_cparams(("arbitrary",) * ng),
    )(*[a for a, _, _ in ins], *[a for a, _, _ in cts])


def _sds(shape, dtype=F32):
    return jax.ShapeDtypeStruct(tuple(shape), dtype)


def _ln_fn(z, g, b):
    mu = jnp.mean(z, -1, keepdims=True)
    d = z - mu
    var = jnp.mean(d * d, -1, keepdims=True)
    y = d * lax.rsqrt(var + LN_EPS) * g + b
    return y, y


def _rms_fn(x, g):
    return (x * lax.rsqrt(jnp.mean(x * x, -1, keepdims=True) + RMS_EPS) * g,)


@jax.custom_vjp
def _rot_half(x):
    lane = lax.broadcasted_iota(jnp.int32, x.shape, x.ndim - 1)
    up = pltpu.roll(x, LANES - QK_ROPE // 2, x.ndim - 1)
    dn = pltpu.roll(x, QK_ROPE // 2, x.ndim - 1)
    return jnp.where(lane < QK_ROPE // 2, -up, jnp.where(lane < QK_ROPE, dn, 0.0))


def _rot_half_fwd(x):
    return _rot_half(x), None


def _rot_half_bwd(_, ct):
    return (-_rot_half(ct),)


_rot_half.defvjp(_rot_half_fwd, _rot_half_bwd)


def _rope_blk(x, cos, sin):
    return x * cos + _rot_half(x) * sin


def _mla_prep_fn(qraw, knope, kr, cos, sin):
    qn, qr = qraw[:, :QK_NOPE], qraw[:, QK_NOPE:]
    q = jnp.concatenate([qn, _rope_blk(qr, cos, sin)], axis=1)
    k = jnp.concatenate([knope.astype(F32), _rope_blk(kr, cos, sin)], axis=1)
    return q, k


def _l2n(x):
    return x * lax.rsqrt(jnp.sum(x * x, -1, keepdims=True) + 1e-6)


def _gdn_qk_fn(qc, kc):
    return _l2n(qc) * (HEAD_DIM ** -0.5), _l2n(kc)


def _softplus(x):
    return jnp.maximum(x, 0.0) + jnp.log(1.0 + jnp.exp(-jnp.abs(x)))


def _sigmoid(x):
    return 1.0 / (1.0 + jnp.exp(-x))


def _silu(x):
    return x * _sigmoid(x)


def _gdn_gate_fn(n_heads, head_axis):
    def fn(ab, a_log, dt_bias):
        h = pl.program_id(head_axis)
        lane = lax.broadcasted_iota(jnp.int32, ab.shape, 1)
        a_in = jnp.sum(jnp.where(lane == h, ab, 0.0), -1, keepdims=True)
        b_in = jnp.sum(jnp.where(lane == h + n_heads, ab, 0.0), -1, keepdims=True)
        g = -jnp.exp(a_log[:, :CHUNK]) * _softplus(a_in + dt_bias[:, :CHUNK])
        beta = _sigmoid(b_in) + jnp.zeros_like(g)
        return g, beta
    return fn


def _gdn_out_fn(o, z, w):
    return (o * lax.rsqrt(jnp.mean(o * o, -1, keepdims=True) + RMS_EPS) * w * _silu(z),)


def _loss_fn(y, t):
    d = y - t
    return (jnp.sum(d * d, axis=0, keepdims=True) * (0.5 / y.shape[-1]), d * (1.0 / y.shape[-1]))


def _adamw_fn(w, g, m, v):
    m = ADAM_B1 * m + (1.0 - ADAM_B1) * g
    v = ADAM_B2 * v + (1.0 - ADAM_B2) * (g * g)
    m_hat = m / (1.0 - ADAM_B1 ** ADAM_STEP)
    v_hat = v / (1.0 - ADAM_B2 ** ADAM_STEP)
    delta = -ADAM_LR * (m_hat / (jnp.sqrt(v_hat) + ADAM_EPS) + ADAM_WD * w)
    return delta, m, v


def _mask_block(s, qi, ki, tq, tk):
    row = lax.broadcasted_iota(jnp.int32, s.shape, 0) + qi * tq
    col = lax.broadcasted_iota(jnp.int32, s.shape, 1) + ki * tk
    return col <= row


def _flash_fwd(q, k, v, *, H, dq, dv, qoff, koff, voff, causal, scale, tq, tk, name):
    S, Sk = q.shape[0], k.shape[0]
    nq, nk = S // tq, Sk // tk
    if causal:
        assert tq == tk and S == Sk
    kmap = (lambda h, qi, ki: (jnp.minimum(ki, qi), koff + h)) if causal else (lambda h, qi, ki: (ki, koff + h))
    vmap_ = (lambda h, qi, ki: (jnp.minimum(ki, qi), voff + h)) if causal else (lambda h, qi, ki: (ki, voff + h))

    def body(q_ref, k_ref, v_ref, o_ref, lse_ref, m_s, l_s, acc):
        qi, ki = pl.program_id(1), pl.program_id(2)

        @pl.when(ki == 0)
        def _():
            m_s[...] = jnp.full_like(m_s, -jnp.inf)
            l_s[...] = jnp.zeros_like(l_s)
            acc[...] = jnp.zeros_like(acc)

        def step(masked):
            s = lax.dot_general(q_ref[...].astype(BF16), k_ref[...].astype(BF16), (((1,), (1,)), ((), ())),
                                preferred_element_type=F32) * scale
            if masked:
                s = jnp.where(_mask_block(s, qi, ki, tq, tk), s, -jnp.inf)
            m_prev = m_s[...]
            m_new = jnp.maximum(m_prev, jnp.max(s, axis=1, keepdims=True))
            alpha = jnp.exp(m_prev - m_new)
            p = jnp.exp(s - m_new[:, :1])
            l_s[...] = alpha * l_s[...] + jnp.sum(p, axis=1, keepdims=True)
            acc[...] = acc[...] * alpha[:, :1] + lax.dot_general(
                p.astype(BF16), v_ref[...].astype(BF16), (((1,), (0,)), ((), ())), preferred_element_type=F32)
            m_s[...] = m_new

        if causal:
            pl.when(ki < qi)(lambda: step(False))
            pl.when(ki == qi)(lambda: step(True))
        else:
            step(False)

        @pl.when(ki == nk - 1)
        def _():
            o_ref[...] = (acc[...] / l_s[...][:, :1]).astype(o_ref.dtype)
            lse_ref[...] = m_s[...] + jnp.log(l_s[...])

    return pl.pallas_call(
        body, name=name, grid=(H, nq, nk),
        in_specs=[pl.BlockSpec((tq, dq), lambda h, qi, ki: (qi, qoff + h)),
                  pl.BlockSpec((tk, dq), kmap), pl.BlockSpec((tk, dv), vmap_)],
        out_specs=[pl.BlockSpec((tq, dv), lambda h, qi, ki: (qi, h)),
                   pl.BlockSpec((tq, LANES), lambda h, qi, ki: (qi, h))],
        out_shape=[_sds((S, H * dv)), _sds((S, H * LANES))],
        scratch_shapes=[pltpu.VMEM((tq, LANES), F32), pltpu.VMEM((tq, LANES), F32), pltpu.VMEM((tq, dv), F32)],
        compiler_params=_cparams(("parallel", "parallel", "arbitrary")),
    )(q, k, v)


def _flash_p_ds(q_ref, k_ref, v_ref, o_ref, do_ref, lse_ref, qi, ki, tq, tk, scale, masked):
    s = lax.dot_general(q_ref[...].astype(BF16), k_ref[...].astype(BF16), (((1,), (1,)), ((), ())),
                        preferred_element_type=F32) * scale
    p = jnp.exp(s - lse_ref[...][:, :1])
    if masked:
        p = jnp.where(_mask_block(s, qi, ki, tq, tk), p, 0.0)
    do = do_ref[...].astype(F32)
    delta = jnp.sum(do * o_ref[...].astype(F32), axis=1, keepdims=True)
    dp = lax.dot_general(do.astype(BF16), v_ref[...].astype(BF16), (((1,), (1,)), ((), ())),
                         preferred_element_type=F32)
    ds = p * (dp - delta) * scale
    return p, ds


def _flash_bwd(q, k, v, o, lse, do, *, H, dq, dv, qoff, koff, voff, dooff, causal, scale, tq, tk, name):
    S, Sk = q.shape[0], k.shape[0]
    nq, nk = S // tq, Sk // tk

    qm = (lambda off: (lambda h, ki, qi: (jnp.maximum(qi, ki), off + h))) if causal else \
         (lambda off: (lambda h, ki, qi: (qi, off + h)))

    def body_kv(q_ref, k_ref, v_ref, o_ref, do_ref, lse_ref, dk_ref, dv_ref, dk_acc, dv_acc):
        ki, qi = pl.program_id(1), pl.program_id(2)

        @pl.when(qi == 0)
        def _():
            dk_acc[...] = jnp.zeros_like(dk_acc)
            dv_acc[...] = jnp.zeros_like(dv_acc)

        def step(masked):
            p, ds = _flash_p_ds(q_ref, k_ref, v_ref, o_ref, do_ref, lse_ref, qi, ki, tq, tk, scale, masked)
            dv_acc[...] += lax.dot_general(p.astype(BF16), do_ref[...].astype(BF16), (((0,), (0,)), ((), ())),
                                           preferred_element_type=F32)
            dk_acc[...] += lax.dot_general(ds.astype(BF16), q_ref[...].astype(BF16), (((0,), (0,)), ((), ())),
                                           preferred_element_type=F32)

        if causal:
            pl.when(qi > ki)(lambda: step(False))
            pl.when(qi == ki)(lambda: step(True))
        else:
            step(False)

        @pl.when(qi == nq - 1)
        def _():
            dk_ref[...] = dk_acc[...]
            dv_ref[...] = dv_acc[...]

    dk, dvv = pl.pallas_call(
        body_kv, name=name + "_dkv", grid=(H, nk, nq),
        in_specs=[pl.BlockSpec((tq, dq), qm(qoff)),
                  pl.BlockSpec((tk, dq), lambda h, ki, qi: (ki, koff + h)),
                  pl.BlockSpec((tk, dv), lambda h, ki, qi: (ki, voff + h)),
                  pl.BlockSpec((tq, dv), qm(0)), pl.BlockSpec((tq, dv), qm(dooff)),
                  pl.BlockSpec((tq, LANES), qm(0))],
        out_specs=[pl.BlockSpec((tk, dq), lambda h, ki, qi: (ki, h)),
                   pl.BlockSpec((tk, dv), lambda h, ki, qi: (ki, h))],
        out_shape=[_sds((Sk, H * dq)), _sds((Sk, H * dv))],
        scratch_shapes=[pltpu.VMEM((tk, dq), F32), pltpu.VMEM((tk, dv), F32)],
        compiler_params=_cparams(("parallel", "parallel", "arbitrary")),
    )(q, k, v, o, do, lse)

    km = (lambda off: (lambda h, qi, ki: (jnp.minimum(ki, qi), off + h))) if causal else \
         (lambda off: (lambda h, qi, ki: (ki, off + h)))

    def body_q(q_ref, k_ref, v_ref, o_ref, do_ref, lse_ref, dq_ref, dq_acc):
        qi, ki = pl.program_id(1), pl.program_id(2)

        @pl.when(ki == 0)
        def _():
            dq_acc[...] = jnp.zeros_like(dq_acc)

        def step(masked):
            _, ds = _flash_p_ds(q_ref, k_ref, v_ref, o_ref, do_ref, lse_ref, qi, ki, tq, tk, scale, masked)
            dq_acc[...] += lax.dot_general(ds.astype(BF16), k_ref[...].astype(BF16), (((1,), (0,)), ((), ())),
                                           preferred_element_type=F32)

        if causal:
            pl.when(ki < qi)(lambda: step(False))
            pl.when(ki == qi)(lambda: step(True))
        else:
            step(False)

        @pl.when(ki == nk - 1)
        def _():
            dq_ref[...] = dq_acc[...]

    dqq, = pl.pallas_call(
        body_q, name=name + "_dq", grid=(H, nq, nk),
        in_specs=[pl.BlockSpec((tq, dq), lambda h, qi, ki: (qi, qoff + h)),
                  pl.BlockSpec((tk, dq), km(koff)), pl.BlockSpec((tk, dv), km(voff)),
                  pl.BlockSpec((tq, dv), lambda h, qi, ki: (qi, h)),
                  pl.BlockSpec((tq, dv), lambda h, qi, ki: (qi, dooff + h)),
                  pl.BlockSpec((tq, LANES), lambda h, qi, ki: (qi, h))],
        out_specs=[pl.BlockSpec((tq, dq), lambda h, qi, ki: (qi, h))],
        out_shape=[_sds((S, H * dq))],
        scratch_shapes=[pltpu.VMEM((tq, dq), F32)],
        compiler_params=_cparams(("parallel", "parallel", "arbitrary")),
    )(q, k, v, o, do, lse)
    return dqq, dk, dvv


def _shift_down(x, prev8, j):
    if j == 0:
        return x
    y = pltpu.roll(x, j, 0)
    head = pltpu.roll(prev8, j, 0)
    row = lax.broadcasted_iota(jnp.int32, x.shape, 0)
    reps = x.shape[0] // SUBLANES
    return jnp.where(row < j, jnp.tile(head, (reps, 1)), y)


def _shift_up(x, next8, j):
    if j == 0:
        return x
    n = x.shape[0]
    y = pltpu.roll(x, n - j, 0)
    tail = pltpu.roll(next8, SUBLANES - j, 0)
    row = lax.broadcasted_iota(jnp.int32, x.shape, 0)
    reps = n // SUBLANES
    return jnp.where(row >= n - j, jnp.tile(tail, (reps, 1)), y)


def _conv_pre(x_ref, p_ref, w_ref, first):
    x = x_ref[...]
    prev8 = jnp.where(first, 0.0, p_ref[...])
    w = w_ref[...]
    xs = [_shift_down(x, prev8, CONV_WIDTH - 1 - j) for j in range(CONV_WIDTH)]
    c = sum(xs[j] * w[j:j + 1, :] for j in range(CONV_WIDTH))
    return c, xs


def _conv_specs(ts, tc, C_total_blocks_off):
    rb = ts // SUBLANES
    off = C_total_blocks_off
    x_spec = pl.BlockSpec((ts, tc), lambda ci, i: (i, off + ci))
    p_spec = pl.BlockSpec((SUBLANES, tc), lambda ci, i: (jnp.maximum(i * rb - 1, 0), off + ci))
    return x_spec, p_spec


def _conv_fwd(h, w, *, C, ts, tc, name):
    S = h.shape[0]
    x_spec, p_spec = _conv_specs(ts, tc, 0)

    def body(x_ref, p_ref, w_ref, y_ref):
        c, _ = _conv_pre(x_ref, p_ref, w_ref, pl.program_id(1) == 0)
        y_ref[...] = _silu(c)

    return pl.pallas_call(
        body, name=name, grid=(C // tc, S // ts),
        in_specs=[x_spec, p_spec, pl.BlockSpec((CONV_WIDTH, tc), lambda ci, i: (0, ci))],
        out_specs=pl.BlockSpec((ts, tc), lambda ci, i: (i, ci)),
        out_shape=_sds((S, C)),
        compiler_params=_cparams(("parallel", "arbitrary")),
    )(h, h, w)


def _conv_bwd(h, w, dy, *, C, ts, tc, name):
    S = h.shape[0]
    ns = S // ts
    rb = ts // SUBLANES
    x_spec, p_spec = _conv_specs(ts, tc, 0)

    def body_a(x_ref, p_ref, w_ref, dy_ref, dc_ref, dw_ref):
        i = pl.program_id(1)
        c, xs = _conv_pre(x_ref, p_ref, w_ref, i == 0)
        sg = _sigmoid(c)
        dc = dy_ref[...] * (sg * (1.0 + c * (1.0 - sg)))
        dc_ref[...] = dc

        @pl.when(i == 0)
        def _():
            dw_ref[...] = jnp.zeros_like(dw_ref)

        dw_ref[...] += jnp.concatenate([jnp.sum(dc * xs[j], axis=0, keepdims=True) for j in range(CONV_WIDTH)], axis=0)

    dc, dw = pl.pallas_call(
        body_a, name=name + "_a", grid=(C // tc, ns),
        in_specs=[x_spec, p_spec, pl.BlockSpec((CONV_WIDTH, tc), lambda ci, i: (0, ci)),
                  pl.BlockSpec((ts, tc), lambda ci, i: (i, ci))],
        out_specs=[pl.BlockSpec((ts, tc), lambda ci, i: (i, ci)),
                   pl.BlockSpec((CONV_WIDTH, tc), lambda ci, i: (0, ci))],
        out_shape=[_sds((S, C)), _sds((CONV_WIDTH, C))],
        compiler_params=_cparams(("parallel", "arbitrary")),
    )(h, h, w, dy)

    def body_b(dc_ref, n_ref, w_ref, dx_ref):
        i = pl.program_id(1)
        dcv = dc_ref[...]
        next8 = jnp.where(i == ns - 1, 0.0, n_ref[...])
        w_ = w_ref[...]
        dx_ref[...] = sum(_shift_up(dcv, next8, CONV_WIDTH - 1 - j) * w_[j:j + 1, :] for j in range(CONV_WIDTH))

    dx = pl.pallas_call(
        body_b, name=name + "_b", grid=(C // tc, ns),
        in_specs=[pl.BlockSpec((ts, tc), lambda ci, i: (i, ci)),
                  pl.BlockSpec((SUBLANES, tc), lambda ci, i: (jnp.minimum((i + 1) * rb, ns * rb - 1), ci)),
                  pl.BlockSpec((CONV_WIDTH, tc), lambda ci, i: (0, ci))],
        out_specs=pl.BlockSpec((ts, tc), lambda ci, i: (i, ci)),
        out_shape=_sds((S, C)),
        compiler_params=_cparams(("parallel", "arbitrary")),
    )(dc, dc, w)
    return dx, dw


def _bdot(a, b, ca, cb, precision=None):
    nb = a.ndim - 2
    batch = tuple(range(nb))
    return lax.dot_general(a, b, (((nb + ca,), (nb + cb,)), (batch, batch)), precision=precision,
                           preferred_element_type=F32)


@jax.custom_vjp
def _nn(a, b):
    return _bdot(a.astype(BF16), b.astype(BF16), 1, 0)


@jax.custom_vjp
def _nt(a, b):
    return _bdot(a.astype(BF16), b.astype(BF16), 1, 1)


@jax.custom_vjp
def _tn(a, b):
    return _bdot(a.astype(BF16), b.astype(BF16), 0, 0)


_nn.defvjp(lambda a, b: (_nn(a, b), (a, b)), lambda r, g: (_nt(g, r[1]), _tn(r[0], g)))
_nt.defvjp(lambda a, b: (_nt(a, b), (a, b)), lambda r, g: (_nn(g, r[1]), _tn(g, r[0])))
_tn.defvjp(lambda a, b: (_tn(a, b), (a, b)), lambda r, g: (_nt(r[1], g), _nn(r[0], g)))


@jax.custom_vjp
def _nn_hi(a, b):
    return _bdot(a, b, 1, 0, HI)


@jax.custom_vjp
def _nt_hi(a, b):
    return _bdot(a, b, 1, 1, HI)


@jax.custom_vjp
def _tn_hi(a, b):
    return _bdot(a, b, 0, 0, HI)


_nn_hi.defvjp(lambda a, b: (_nn_hi(a, b), (a, b)), lambda r, g: (_nt_hi(g, r[1]), _tn_hi(r[0], g)))
_nt_hi.defvjp(lambda a, b: (_nt_hi(a, b), (a, b)), lambda r, g: (_nn_hi(g, r[1]), _tn_hi(g, r[0])))
_tn_hi.defvjp(lambda a, b: (_tn_hi(a, b), (a, b)), lambda r, g: (_nt_hi(r[1], g), _nn_hi(r[0], g)))


def _gdn_chunk_fn(q, k, v, g, beta, state):
    C = CHUNK
    B = q.shape[0]
    row = lax.broadcasted_iota(jnp.int32, (B, C, C), 1)
    col = lax.broadcasted_iota(jnp.int32, (B, C, C), 2)
    tril, strict = row >= col, row > col
    ones_tril = tril.astype(F32)
    gc = _nn_hi(ones_tril, g)
    gr = _nt_hi(jnp.full((B, C, C), 1.0 / C, F32), gc)
    decay = jnp.where(tril, jnp.exp(jnp.where(tril, gc - gr, 0.0)), 0.0)
    b1 = beta[:, :, :1]
    e_gc = jnp.exp(gc[:, :, :1])
    kb = k * b1
    lmat = jnp.where(strict, _nt(kb, k) * decay, 0.0)
    a = -lmat
    t = jnp.where(row == col, 1.0, 0.0) + a
    p = a
    for _ in range(5):
        p = _nn_hi(p, p)
        t = t + _nn_hi(t, p)
    rhs = jnp.concatenate([v * b1, kb * e_gc], axis=2)
    sol = _nn_hi(t, rhs)
    u, w = sol[:, :, :HEAD_DIM], sol[:, :, HEAD_DIM:]
    a_qk = jnp.where(tril, _nt(q, k) * decay, 0.0)
    gl = gc[:, C - 1:C, :1]
    q_dec = q * e_gc
    k_dec = k * jnp.exp(gl - gc[:, :, :1])
    v_new = u - _nn(w, state)
    o = _nn(q_dec, state) + _nn(a_qk, v_new)
    new_state = state * jnp.exp(gl) + _tn(k_dec, v_new)
    return o, new_state


def _split_heads(x, B):
    return jnp.stack([x[:, j * HEAD_DIM:(j + 1) * HEAD_DIM] for j in range(B)], axis=0)


def _merge_heads(x):
    return jnp.concatenate([x[j] for j in range(x.shape[0])], axis=1)


def _gdn_group(H, voff):
    return next(b for b in (12, 6, 4, 3, 2, 1) if H % b == 0 and voff % b == 0)


def _gdn_fwd(q, k, v, g, beta, *, H, voff, name):
    S = q.shape[0]
    N = S // CHUNK
    B = _gdn_group(H, voff)
    W = B * HEAD_DIM
    qs = lambda off: pl.BlockSpec((CHUNK, W), lambda h, n: (n, off // B + h))
    gs = pl.BlockSpec((B, CHUNK, CHUNK), lambda h, n: (h, n, 0))

    def body(q_ref, k_ref, v_ref, g_ref, b_ref, o_ref, st_ref, state):
        @pl.when(pl.program_id(1) == 0)
        def _():
            state[...] = jnp.zeros_like(state)

        s0 = state[...]
        st_ref[...] = s0
        o, s1 = _gdn_chunk_fn(_split_heads(q_ref[...], B), _split_heads(k_ref[...], B), _split_heads(v_ref[...], B),
                              g_ref[...], b_ref[...], s0)
        o_ref[...] = _merge_heads(o)
        state[...] = s1

    return pl.pallas_call(
        body, name=name, grid=(H // B, N),
        in_specs=[qs(0), qs(0), qs(voff), gs, gs],
        out_specs=[qs(0), pl.BlockSpec((B, None, HEAD_DIM, HEAD_DIM), lambda h, n: (h, n, 0, 0))],
        out_shape=[_sds((S, H * HEAD_DIM)), _sds((H, N, HEAD_DIM, HEAD_DIM))],
        scratch_shapes=[pltpu.VMEM((B, HEAD_DIM, HEAD_DIM), F32)],
        compiler_params=_cparams(("parallel", "arbitrary")),
    )(q, k, v, g, beta)


def _gdn_bwd(q, k, v, g, beta, states, do, *, H, voff, name):
    S = q.shape[0]
    N = S // CHUNK
    B = _gdn_group(H, voff)
    W = B * HEAD_DIM
    rs = lambda off: pl.BlockSpec((CHUNK, W), lambda h, n: (N - 1 - n, off // B + h))
    gs = pl.BlockSpec((B, CHUNK, CHUNK), lambda h, n: (h, N - 1 - n, 0))

    def body(q_ref, k_ref, v_ref, g_ref, b_ref, st_ref, do_ref, dq_ref, dk_ref, dv_ref, dg_ref, db_ref, dstate):
        @pl.when(pl.program_id(1) == 0)
        def _():
            dstate[...] = jnp.zeros_like(dstate)

        _, pull = jax.vjp(_gdn_chunk_fn, _split_heads(q_ref[...], B), _split_heads(k_ref[...], B),
                          _split_heads(v_ref[...], B), g_ref[...], b_ref[...], st_ref[...])
        dq, dk, dv, dg, db, ds = pull((_split_heads(do_ref[...], B), dstate[...]))
        dq_ref[...] = _merge_heads(dq)
        dk_ref[...] = _merge_heads(dk)
        dv_ref[...] = _merge_heads(dv)
        dg_ref[...] = dg
        db_ref[...] = db
        dstate[...] = ds

    return pl.pallas_call(
        body, name=name, grid=(H // B, N),
        in_specs=[rs(0), rs(0), rs(voff), gs, gs,
                  pl.BlockSpec((B, None, HEAD_DIM, HEAD_DIM), lambda h, n: (h, N - 1 - n, 0, 0)), rs(0)],
        out_specs=[rs(0), rs(0), rs(0), gs, gs],
        out_shape=[_sds((S, H * HEAD_DIM))] * 3 + [_sds((H, S, CHUNK))] * 2,
        scratch_shapes=[pltpu.VMEM((B, HEAD_DIM, HEAD_DIM), F32)],
        compiler_params=_cparams(("parallel", "arbitrary")),
    )(q, k, v, g, beta, states, do)


def _round_up(n, m):
    return (n + m - 1) // m * m


def _dims(S, D, M, shapes):
    c = dict(S=S, D=D, M=M)
    c["H"] = shapes["gdn_a_log"][-1]
    c["QL"] = shapes["mla_q_norm"][-1]
    c["KVL"] = shapes["mla_kv_norm"][-1]
    assert c["QL"] == c["KVL"]
    c["MEMW"] = shapes["mem_w_kv"][-1] // 2
    c["HM"] = c["MEMW"] // HEAD_DIM
    c["MW"] = c["H"] * HEAD_DIM
    c["F"] = shapes["mlp_w1"][-1]
    c["DEPTH"] = shapes["ln1_g"][0]
    c["ALPHA"] = (2 * c["DEPTH"]) ** 0.25
    c["MLA_IN"] = _round_up(c["QL"] + c["KVL"] + c["MEMW"] + LANES, 2 * LANES)
    c["GDN_IN"] = _round_up(4 * c["MW"] + c["MEMW"] + LANES, 2 * LANES)
    c["t_row"] = min(256, S)
    c["t_head"] = min(512, S)
    c["t_att"] = min(512, S)
    return c


def _pad_cols(w, n):
    return jnp.pad(w, ((0, 0), (0, n - w.shape[1])))


def _prep_mla_w_in(w, c):
    a = c["QL"] + c["KVL"]
    w = jnp.concatenate([w[:, :a], w[:, a + QK_ROPE:a + QK_ROPE + c["MEMW"]], w[:, a:a + QK_ROPE]], axis=1)
    return _pad_cols(w, c["MLA_IN"]).astype(BF16)


def _unprep_mla_w_in(dw, c):
    a, m = c["QL"] + c["KVL"], c["MEMW"]
    return jnp.concatenate([dw[:, :a], dw[:, a + m:a + m + QK_ROPE], dw[:, a:a + m]], axis=1)


def _prep_w_uq(w, c):
    w = w.reshape(c["QL"], c["H"], QK_NOPE + QK_ROPE)
    w = jnp.pad(w, ((0, 0), (0, 0), (0, QK_PAD - QK_NOPE - QK_ROPE)))
    return w.reshape(c["QL"], c["H"] * QK_PAD).astype(BF16)


def _unprep_w_uq(dw, c):
    return dw.reshape(c["QL"], c["H"], QK_PAD)[:, :, :QK_NOPE + QK_ROPE].reshape(c["QL"], c["H"] * (QK_NOPE + QK_ROPE))


def _prep_w_ukv(w, c):
    return w.reshape(c["KVL"], c["H"], 2, HEAD_DIM).transpose(0, 2, 1, 3).reshape(c["KVL"], 2 * c["MW"]).astype(BF16)


def _unprep_w_ukv(dw, c):
    return dw.reshape(c["KVL"], 2, c["H"], HEAD_DIM).transpose(0, 2, 1, 3).reshape(c["KVL"], 2 * c["MW"])


def _prep_gdn_w_in(w, c):
    a, h2 = 4 * c["MW"], 2 * c["H"]
    w = jnp.concatenate([w[:, :a], w[:, a + h2:], w[:, a:a + h2]], axis=1)
    return _pad_cols(w, c["GDN_IN"]).astype(BF16)


def _unprep_gdn_w_in(dw, c):
    a, h2, m = 4 * c["MW"], 2 * c["H"], c["MEMW"]
    return jnp.concatenate([dw[:, :a], dw[:, a + m:a + m + h2], dw[:, a:a + m]], axis=1)


def _lane_bcast(v):
    return jnp.broadcast_to(v.astype(F32)[:, None, None], (v.shape[0], 1, LANES))


def _row(i):
    return (i, 0)


def _par(i):
    return (0, 0)


def _layer_norm(z, g, b, c, name):
    S, D, ts = c["S"], c["D"], c["t_row"]
    return _ew(_ln_fn, [(z, (ts, D), _row), (g, (1, D), _par), (b, (1, D), _par)],
               [(_sds((S, D)), (ts, D), _row), (_sds((S, D), BF16), (ts, D), _row)], (S // ts,), name=name)


def _layer_norm_bwd(z, g, b, dy, c, name):
    S, D, ts = c["S"], c["D"], c["t_row"]
    fn = lambda z, g, b: _ln_fn(z, g, b)[:1]

    def both(z, g, b):
        return fn(z, g, b)

    dz, dg, db = _ew_vjp(both, [(z, (ts, D), _row), (g, (1, D), _par), (b, (1, D), _par)], [(dy, (ts, D), _row)],
                         [(_sds((S, D)), (ts, D), _row, "set"), (_sds((1, D)), (1, D), _par, "acc_all"),
                          (_sds((1, D)), (1, D), _par, "acc_all")], (S // ts,), name=name)
    return dz, dg, db


def _mem_attn_fwd(h, qoff, memkv, c, name):
    return _flash_fwd(h, memkv, memkv, H=c["HM"], dq=HEAD_DIM, dv=HEAD_DIM, qoff=qoff, koff=0, voff=c["HM"],
                      causal=False, scale=HEAD_DIM ** -0.5, tq=c["t_att"], tk=c["M"], name=name)


def _mem_attn_bwd(h, qoff, memkv, om, lsem, dcat, c, name):
    return _flash_bwd(h, memkv, memkv, om, lsem, dcat, H=c["HM"], dq=HEAD_DIM, dv=HEAD_DIM, qoff=qoff, koff=0,
                      voff=c["HM"], dooff=c["H"], causal=False, scale=HEAD_DIM ** -0.5, tq=c["t_att"], tk=c["M"],
                      name=name)


def _mla_specs(c):
    H, ts = c["H"], c["t_head"]
    kr_blk = (c["QL"] + c["KVL"] + c["MEMW"]) // LANES
    hd = lambda i, h: (i, h)
    return [((ts, QK_PAD), hd), ((ts, HEAD_DIM), hd), ((ts, LANES), lambda i, h: (i, kr_blk)),
            ((ts, LANES), lambda i, h: (i, 0)), ((ts, LANES), lambda i, h: (i, 0))]


def _mla_fwd(xb, p, cosp, sinp, c):
    S, H, QL, ts, tr = c["S"], c["H"], c["QL"], c["t_head"], c["t_row"]
    h, = _mm(xb, p["w_in"], "nn", [F32], name="mla_in")
    nq, = _ew(_rms_fn, [(h, (tr, QL), lambda i: (i, 0)), (p["q_norm"], (1, QL), _par)],
              [(_sds((S, QL), BF16), (tr, QL), _row)], (S // tr,), name="mla_qnorm")
    nkv, = _ew(_rms_fn, [(h, (tr, QL), lambda i: (i, 1)), (p["kv_norm"], (1, QL), _par)],
               [(_sds((S, QL), BF16), (tr, QL), _row)], (S // tr,), name="mla_kvnorm")
    qraw, = _mm(nq, p["w_uq"], "nn", [F32], name="mla_uq")
    kvraw, = _mm(nkv, p["w_ukv"], "nn", [BF16], name="mla_ukv")
    sp = _mla_specs(c)
    ins = [(a, b, m) for a, (b, m) in zip([qraw, kvraw, h, cosp, sinp], sp)]
    qp, kp = _ew(_mla_prep_fn, ins, [(_sds((S, H * QK_PAD), BF16), (ts, QK_PAD), lambda i, h: (i, h))] * 2,
                 (S // ts, H), name="mla_rope")
    o, lse = _flash_fwd(qp, kp, kvraw, H=H, dq=QK_PAD, dv=HEAD_DIM, qoff=0, koff=0, voff=H, causal=True,
                        scale=(QK_NOPE + QK_ROPE) ** -0.5, tq=c["t_att"], tk=c["t_att"], name="mla_attn")
    return o, dict(h=h, nq=nq, nkv=nkv, qraw=qraw, kvraw=kvraw, qp=qp, kp=kp, o=o, lse=lse)


def _mla_bwd(sv, p, cosp, sinp, dcat, dqm, c):
    S, H, QL, ts, tr = c["S"], c["H"], c["QL"], c["t_head"], c["t_row"]
    dqp, dkp, dv = _flash_bwd(sv["qp"], sv["kp"], sv["kvraw"], sv["o"], sv["lse"], dcat, H=H, dq=QK_PAD, dv=HEAD_DIM,
                              qoff=0, koff=0, voff=H, dooff=0, causal=True, scale=(QK_NOPE + QK_ROPE) ** -0.5,
                              tq=c["t_att"], tk=c["t_att"], name="mla_attn_bwd")
    sp = _mla_specs(c)
    ins = [(a, b, m) for a, (b, m) in zip([sv["qraw"], sv["kvraw"], sv["h"], cosp, sinp], sp)]
    hd = lambda i, h: (i, h)
    dqraw, dknope, dkr = _ew_vjp(
        _mla_prep_fn, ins, [(dqp, (ts, QK_PAD), hd), (dkp, (ts, QK_PAD), hd)],
        [(_sds((S, H * QK_PAD), BF16), (ts, QK_PAD), hd, "set"), (_sds((S, H * HEAD_DIM), BF16), (ts, HEAD_DIM), hd, "set"),
         (_sds((S, LANES)), (ts, LANES), lambda i, h: (i, 0), "acc"), None, None], (S // ts, H), name="mla_rope_bwd")
    dkvraw = jnp.concatenate([dknope, dv.astype(BF16)], axis=1)
    dnq, = _mm(dqraw, p["w_uq"], "nt", [F32], name="mla_uq_dx")
    dw_uq, = _mm(sv["nq"], dqraw, "tn", [F32], name="mla_uq_dw")
    dnkv, = _mm(dkvraw, p["w_ukv"], "nt", [F32], name="mla_ukv_dx")
    dw_ukv, = _mm(sv["nkv"], dkvraw, "tn", [F32], name="mla_ukv_dw")
    dcq, dgq = _ew_vjp(_rms_fn, [(sv["h"], (tr, QL), lambda i: (i, 0)), (p["q_norm"], (1, QL), _par)],
                       [(dnq, (tr, QL), _row)],
                       [(_sds((S, QL), BF16), (tr, QL), _row, "set"), (_sds((1, QL)), (1, QL), _par, "acc_all")],
                       (S // tr,), name="mla_qnorm_bwd")
    dckv, dgkv = _ew_vjp(_rms_fn, [(sv["h"], (tr, QL), lambda i: (i, 1)), (p["kv_norm"], (1, QL), _par)],
                         [(dnkv, (tr, QL), _row)],
                         [(_sds((S, QL), BF16), (tr, QL), _row, "set"), (_sds((1, QL)), (1, QL), _par, "acc_all")],
                         (S // tr,), name="mla_kvnorm_bwd")
    pad = c["MLA_IN"] - (2 * QL + c["MEMW"] + LANES)
    dh = jnp.concatenate([dcq, dckv, dqm.astype(BF16), dkr.astype(BF16)] + ([jnp.zeros((S, pad), BF16)] if pad else []),
                         axis=1)
    grads = dict(mla_q_norm=dgq[0], mla_kv_norm=dgkv[0], mla_w_uq=_unprep_w_uq(dw_uq, c),
                 mla_w_ukv=_unprep_w_ukv(dw_ukv, c))
    return dh, grads


def _gdn_ins(h, qkvc, p, c):
    H, ts = c["H"], c["t_head"]
    ab_blk = (4 * c["MW"] + c["MEMW"]) // LANES
    qk_ins = [(qkvc, (ts, HEAD_DIM), lambda i, h: (i, h)), (qkvc, (ts, HEAD_DIM), lambda i, h: (i, H + h))]
    gate_ins = [(h, (ts, LANES), lambda i, h: (i, ab_blk)), (p["a_log"], (None, 1, LANES), lambda i, h: (h, 0, 0)),
                (p["dt_bias"], (None, 1, LANES), lambda i, h: (h, 0, 0))]
    return qk_ins, gate_ins


def _gdn_out_ins(o, h, p, c):
    H, ts = c["H"], c["t_head"]
    return [(o, (ts, HEAD_DIM), lambda i, h: (i, h)), (h, (ts, HEAD_DIM), lambda i, h: (i, 3 * H + h)),
            (p["o_norm"], (1, HEAD_DIM), lambda i, h: (0, 0))]


def _gdn_layer_fwd(xb, p, c):
    S, H, MW, ts = c["S"], c["H"], c["MW"], c["t_head"]
    h, = _mm(xb, p["w_in"], "nn", [F32], name="gdn_in")
    tc = _tile(3 * MW, 512, LANES)
    qkvc = _conv_fwd(h, p["conv"], C=3 * MW, ts=ts, tc=tc, name="gdn_conv")
    qk_ins, gate_ins = _gdn_ins(h, qkvc, p, c)
    hd = lambda i, h: (i, h)
    qn, kn = _ew(_gdn_qk_fn, qk_ins, [(_sds((S, MW)), (ts, HEAD_DIM), hd)] * 2, (S // ts, H), name="gdn_qknorm")
    g3 = lambda i, h: (h, i, 0)
    g, beta = _ew(_gdn_gate_fn(H, 1), gate_ins, [(_sds((H, S, CHUNK)), (None, ts, CHUNK), g3)] * 2, (S // ts, H),
                  name="gdn_gate")
    o, states = _gdn_fwd(qn, kn, qkvc, g, beta, H=H, voff=2 * H, name="gdn_delta")
    mix, = _ew(_gdn_out_fn, _gdn_out_ins(o, h, p, c), [(_sds((S, MW)), (ts, HEAD_DIM), hd)], (S // ts, H),
               name="gdn_outnorm")
    return mix, dict(h=h, qkvc=qkvc, qn=qn, kn=kn, g=g, beta=beta, o=o, states=states)


def _gdn_layer_bwd(sv, p, dcat, dqm, c):
    S, H, MW, ts = c["S"], c["H"], c["MW"], c["t_head"]
    hd = lambda i, h: (i, h)
    g3 = lambda i, h: (h, i, 0)
    h, qkvc = sv["h"], sv["qkvc"]
    do, dz, d_onorm = _ew_vjp(_gdn_out_fn, _gdn_out_ins(sv["o"], h, p, c), [(dcat, (ts, HEAD_DIM), hd)],
                              [(_sds((S, MW)), (ts, HEAD_DIM), hd, "set"), (_sds((S, MW), BF16), (ts, HEAD_DIM), hd, "set"),
                               (_sds((1, HEAD_DIM)), (1, HEAD_DIM), lambda i, h: (0, 0), "acc_all")],
                              (S // ts, H), name="gdn_outnorm_bwd")
    dqn, dkn, dv, dg, db = _gdn_bwd(sv["qn"], sv["kn"], qkvc, sv["g"], sv["beta"], sv["states"], do, H=H, voff=2 * H,
                                    name="gdn_delta_bwd")
    qk_ins, gate_ins = _gdn_ins(h, qkvc, p, c)
    dqc, dkc = _ew_vjp(_gdn_qk_fn, qk_ins, [(dqn, (ts, HEAD_DIM), hd), (dkn, (ts, HEAD_DIM), hd)],
                       [(_sds((S, MW)), (ts, HEAD_DIM), hd, "set")] * 2, (S // ts, H), name="gdn_qknorm_bwd")
    full3 = lambda i, h: (0, 0, 0)
    dab, dalog, ddt = _ew_vjp(
        _gdn_gate_fn(H, 1), gate_ins, [(dg, (None, ts, CHUNK), g3), (db, (None, ts, CHUNK), g3)],
        [(_sds((S, LANES), BF16), (ts, LANES), lambda i, h: (i, 0), "acc"),
         (_sds((H, 1, LANES)), (H, 1, LANES), full3, ("acc_at", 1)),
         (_sds((H, 1, LANES)), (H, 1, LANES), full3, ("acc_at", 1))], (S // ts, H), name="gdn_gate_bwd")
    dqkvc = jnp.concatenate([dqc, dkc, dv], axis=1)
    tc = _tile(3 * MW, 512, LANES)
    dxc, dconv = _conv_bwd(h, p["conv"], dqkvc, C=3 * MW, ts=ts, tc=tc, name="gdn_conv_bwd")
    pad = c["GDN_IN"] - (4 * MW + c["MEMW"] + LANES)
    dh = jnp.concatenate([dxc.astype(BF16), dz, dqm.astype(BF16), dab] + ([jnp.zeros((S, pad), BF16)] if pad else []),
                         axis=1)
    grads = dict(gdn_conv=dconv, gdn_a_log=jnp.sum(dalog[:, 0, :], axis=-1), gdn_dt_bias=jnp.sum(ddt[:, 0, :], axis=-1),
                 gdn_o_norm=d_onorm[0])
    return dh, grads


def _prep_layer(W, i, c):
    p = dict(mem_w_kv=W["mem_w_kv"].astype(BF16), w_out=W["w_out"].astype(BF16),
             w1=W["mlp_w1"].astype(BF16), w2=W["mlp_w2"].astype(BF16),
             ln1_g=W["ln1_g"][None].astype(F32), ln1_b=W["ln1_b"][None].astype(F32),
             ln2_g=W["ln2_g"][None].astype(F32), ln2_b=W["ln2_b"][None].astype(F32))
    if i % 2 == 0:
        p.update(w_in=_prep_mla_w_in(W["mla_w_in"], c), q_norm=W["mla_q_norm"][None].astype(F32),
                 w_uq=_prep_w_uq(W["mla_w_uq"], c), kv_norm=W["mla_kv_norm"][None].astype(F32),
                 w_ukv=_prep_w_ukv(W["mla_w_ukv"], c))
    else:
        p.update(w_in=_prep_gdn_w_in(W["gdn_w_in"], c), conv=W["gdn_conv"].astype(F32),
                 a_log=_lane_bcast(W["gdn_a_log"]), dt_bias=_lane_bcast(W["gdn_dt_bias"]),
                 o_norm=W["gdn_o_norm"][None].astype(F32))
    return p


def _local_step(x, mem, positions, W, loss_target, c, grad_sink=None):
    S, D, H, MW, ALPHA = c["S"], c["D"], c["H"], c["MW"], c["ALPHA"]
    inv_freq = 1.0 / (ROPE_THETA ** (jnp.arange(0, QK_ROPE, 2, dtype=F32) / QK_ROPE))
    ang = positions.astype(F32)[:, None] * inv_freq
    cos, sin = jnp.cos(ang), jnp.sin(ang)
    cosp = jnp.concatenate([cos, cos, jnp.ones((S, LANES - QK_ROPE), F32)], axis=1)
    sinp = jnp.concatenate([sin, sin, jnp.zeros((S, LANES - QK_ROPE), F32)], axis=1)
    memb = mem.astype(BF16)
    xf, xb = x, x.astype(BF16)
    saved, params = [], []
    for i in range(c["DEPTH"]):
        p = _prep_layer(W[i], i, c)
        mla = i % 2 == 0
        memkv, = _mm(memb, p["mem_w_kv"], "nn", [BF16], name="mem_kv")
        if mla:
            mix, sv = _mla_fwd(xb, p, cosp, sinp, c)
            qoff = (c["QL"] + c["KVL"]) // LANES
        else:
            mix, sv = _gdn_layer_fwd(xb, p, c)
            qoff = 4 * MW // LANES
        om, lsem = _mem_attn_fwd(sv["h"], qoff, memkv, c, "mem_attn")
        cat = jnp.concatenate([mix, om], axis=1).astype(BF16)
        z1, = _mm(cat, p["w_out"], "nn", [F32], name="w_out", extras=(xf,), epilogue=lambda acc, r: (ALPHA * r + acc,))
        x1, x1b = _layer_norm(z1, p["ln1_g"], p["ln1_b"], c, "ln1")
        u, a = _mm(x1b, p["w1"], "nn", [F32, BF16], name="mlp_up", b_major=True,
                   epilogue=lambda acc: (acc, jnp.square(jnp.maximum(acc, 0.0))))
        z2, = _mm(a, p["w2"], "nn", [F32], name="mlp_down", extras=(x1,), epilogue=lambda acc, r: (ALPHA * r + acc,))
        x2, x2b = _layer_norm(z2, p["ln2_g"], p["ln2_b"], c, "ln2")
        sv.update(xb=xb, memkv=memkv, om=om, lsem=lsem, cat=cat, z1=z1, x1b=x1b, u=u, a=a, z2=z2, qoff=qoff)
        saved.append(sv)
        params.append(p)
        xf, xb = x2, x2b

    ts = c["t_row"]
    lsum, dy = _ew(_loss_fn, [(xf, (ts, D), _row), (loss_target, (ts, D), _row)],
                   [(_sds((1, D)), (1, D), _par), (_sds((S, D)), (ts, D), _row)], (S // ts,), name="loss", acc_out=(0,))
    loss = jnp.sum(lsum)

    grads = [None] * c["DEPTH"]
    dx = dy
    for i in reversed(range(c["DEPTH"])):
        p, sv = params[i], saved[i]
        mla = i % 2 == 0
        G = {}
        dz2, dg, db = _layer_norm_bwd(sv["z2"], p["ln2_g"], p["ln2_b"], dx, c, "ln2_bwd")
        G["ln2_g"], G["ln2_b"] = dg[0], db[0]
        dz2b = dz2.astype(BF16)
        du, = _mm(dz2b, p["w2"], "nt", [BF16], name="mlp_down_dx", extras=(sv["u"],),
                  epilogue=lambda acc, u: (acc * (2.0 * jnp.maximum(u, 0.0)),))
        G["mlp_w2"], = _mm(sv["a"], dz2b, "tn", [F32], name="mlp_down_dw")
        G["mlp_w1"], = _mm(sv["x1b"], du, "tn", [F32], name="mlp_up_dw", out_major=True)
        dx1, = _mm(du, p["w1"], "nt", [F32], name="mlp_up_dx", extras=(dz2,), b_major=True,
                   epilogue=lambda acc, r: (ALPHA * r + acc,))
        dz1, dg, db = _layer_norm_bwd(sv["z1"], p["ln1_g"], p["ln1_b"], dx1, c, "ln1_bwd")
        G["ln1_g"], G["ln1_b"] = dg[0], db[0]
        dz1b = dz1.astype(BF16)
        dcat, = _mm(dz1b, p["w_out"], "nt", [BF16], name="w_out_dx")
        G["w_out"], = _mm(sv["cat"], dz1b, "tn", [F32], name="w_out_dw")
        dqm, dkm, dvm = _mem_attn_bwd(sv["h"], sv["qoff"], sv["memkv"], sv["om"], sv["lsem"], dcat, c, "mem_attn_bwd")
        dmemkv = jnp.concatenate([dkm, dvm], axis=1).astype(BF16)
        G["mem_w_kv"], = _mm(memb, dmemkv, "tn", [F32], name="mem_kv_dw")
        if mla:
            dh, g = _mla_bwd(sv, p, cosp, sinp, dcat, dqm, c)
            G.update(g)
            dw_in, = _mm(sv["xb"], dh, "tn", [F32], name="mla_in_dw")
            G["mla_w_in"] = _unprep_mla_w_in(dw_in, c)
            dx, = _mm(dh, p["w_in"], "nt", [F32], name="mla_in_dx", extras=(dz1,),
                      epilogue=lambda acc, r: (ALPHA * r + acc,))
        else:
            dh, g = _gdn_layer_bwd(sv, p, dcat, dqm, c)
            G.update(g)
            dw_in, = _mm(sv["xb"], dh, "tn", [F32], name="gdn_in_dw")
            G["gdn_w_in"] = _unprep_gdn_w_in(dw_in, c)
            dx, = _mm(dh, p["w_in"], "nt", [F32], name="gdn_in_dx", extras=(dz1,),
                      epilogue=lambda acc, r: (ALPHA * r + acc,))
        grads[i] = G if grad_sink is None else grad_sink(i, G)
    return loss, dx, grads


_HBM = pl.BlockSpec(memory_space=pltpu.HBM)
_VMEM = pl.BlockSpec(memory_space=pltpu.VMEM)


def _my_place():
    return lax.axis_index("x"), lax.axis_index("y"), lax.axis_index("c")


def _my_chip():
    return 2 * lax.axis_index("x") + lax.axis_index("y")


def _other_chips(x, y):
    return [(1 - x, y), (x, 1 - y), (1 - x, 1 - y)]


def _gather_chips(arrs, by_rows, name):
    n = len(arrs)

    def body(*refs):
        ins, outs = refs[:n], refs[n:2 * n]
        send_sems, recv_sems, local_sems = refs[2 * n:]
        x, y, c = _my_place()
        chips = _other_chips(x, y)

        def slab(i, chip):
            r = arrs[i].shape[0]
            return outs[i].at[pl.ds(chip * r, r)] if by_rows[i] else outs[i].at[chip]

        def copy(i, k, chip, to):
            return pltpu.make_async_remote_copy(src_ref=ins[i], dst_ref=slab(i, chip), send_sem=send_sems.at[3 * i + k],
                                                recv_sem=recv_sems.at[3 * i + k], device_id=to, device_id_type=MESH)

        mine = [pltpu.make_async_copy(ins[i], slab(i, 2 * x + y), local_sems.at[i]) for i in range(n)]
        sends = [copy(i, k, 2 * x + y, (cx, cy, c)) for i in range(n) for k, (cx, cy) in enumerate(chips)]
        for cp in mine + sends:
            cp.start()
        for i in range(n):
            for k, (cx, cy) in enumerate(chips):
                copy(i, k, 2 * cx + cy, (cx, cy, c)).wait_recv()
        for cp in sends:
            cp.wait_send()
        for cp in mine:
            cp.wait()

    shapes = [jax.ShapeDtypeStruct((N_CHIPS * a.shape[0], a.shape[1]) if r else (N_CHIPS,) + a.shape, a.dtype)
              for a, r in zip(arrs, by_rows)]
    return pl.pallas_call(
        body, name=name, in_specs=[_HBM] * n, out_specs=[_HBM] * n, out_shape=shapes,
        scratch_shapes=[pltpu.SemaphoreType.DMA((3 * n,)), pltpu.SemaphoreType.DMA((3 * n,)),
                        pltpu.SemaphoreType.DMA((n,))],
    )(*arrs)


def _swap_halves(gs, name):
    n = len(gs)

    def body(*refs):
        ins, outs = refs[:n], refs[n:2 * n]
        send_sems, recv_sems = refs[2 * n:]
        x, y, c = _my_place()
        cps = [pltpu.make_async_remote_copy(src_ref=ins[i].at[:, 1 - c], dst_ref=outs[i], send_sem=send_sems.at[i],
                                            recv_sem=recv_sems.at[i], device_id=(x, y, 1 - c), device_id_type=MESH)
               for i in range(n)]
        for cp in cps:
            cp.start()
        for cp in cps:
            cp.wait()

    return pl.pallas_call(
        body, name=name, in_specs=[_HBM] * n, out_specs=[_HBM] * n,
        out_shape=[jax.ShapeDtypeStruct((g.shape[0],) + g.shape[2:], g.dtype) for g in gs],
        scratch_shapes=[pltpu.SemaphoreType.DMA((n,)), pltpu.SemaphoreType.DMA((n,))],
    )(*gs)


def _scatter_chips(ps, name):
    n = len(ps)

    def body(*refs):
        ins, outs = refs[:n], refs[n:2 * n]
        send_sems, recv_sems = refs[2 * n:]
        x, y, c = _my_place()
        cps = [pltpu.make_async_remote_copy(src_ref=ins[i].at[2 * cx + cy], dst_ref=outs[i].at[k],
                                            send_sem=send_sems.at[3 * i + k], recv_sem=recv_sems.at[3 * i + k],
                                            device_id=(cx, cy, c), device_id_type=MESH)
               for i in range(n) for k, (cx, cy) in enumerate(_other_chips(x, y))]
        for cp in cps:
            cp.start()
        for cp in cps:
            cp.wait()

    return pl.pallas_call(
        body, name=name, in_specs=[_HBM] * n, out_specs=[_HBM] * n,
        out_shape=[jax.ShapeDtypeStruct((3,) + p.shape[1:], p.dtype) for p in ps],
        scratch_shapes=[pltpu.SemaphoreType.DMA((3 * n,)), pltpu.SemaphoreType.DMA((3 * n,))],
    )(*ps)


def _join_halves(fs, name):
    n = len(fs)

    def body(*refs):
        ins, outs = refs[:n], refs[n:2 * n]
        send_sems, recv_sems, local_sems = refs[2 * n:]
        x, y, c = _my_place()

        def copy(i, half):
            return pltpu.make_async_remote_copy(src_ref=ins[i], dst_ref=outs[i].at[half], send_sem=send_sems.at[i],
                                                recv_sem=recv_sems.at[i], device_id=(x, y, 1 - c), device_id_type=MESH)

        mine = [pltpu.make_async_copy(ins[i], outs[i].at[c], local_sems.at[i]) for i in range(n)]
        sends = [copy(i, c) for i in range(n)]
        for cp in mine + sends:
            cp.start()
        for i in range(n):
            copy(i, 1 - c).wait_recv()
        for cp in sends:
            cp.wait_send()
        for cp in mine:
            cp.wait()

    return pl.pallas_call(
        body, name=name, in_specs=[_HBM] * n, out_specs=[_HBM] * n,
        out_shape=[jax.ShapeDtypeStruct((2,) + f.shape, f.dtype) for f in fs],
        scratch_shapes=[pltpu.SemaphoreType.DMA((n,)), pltpu.SemaphoreType.DMA((n,)), pltpu.SemaphoreType.DMA((n,))],
    )(*fs)


def _row_tile(a, b):
    return _tile(a, max(SUBLANES, (1 << 19) // b // SUBLANES * SUBLANES), SUBLANES)


def _add_core(g, got, name):
    _, _, A, B = g.shape
    ta = _row_tile(A, B)
    return _ew(lambda p, q: (p + q,),
               [(g, (None, None, ta, B), lambda s, i: (s, lax.axis_index("c"), i, 0)),
                (got, (None, ta, B), lambda s, i: (s, i, 0))],
               [(_sds((N_CHIPS, A, B)), (None, ta, B), lambda s, i: (s, i, 0))], (N_CHIPS, A // ta), name=name)[0]


def _add_chips(p, got, name):
    _, A, B = p.shape
    ta = _row_tile(A, B)
    blk = (None, ta, B)
    return _ew(lambda a, b, c_, d: (((a + b) + c_) + d,),
               [(p, blk, lambda i: (_my_chip(), i, 0)), (got, blk, lambda i: (0, i, 0)),
                (got, blk, lambda i: (1, i, 0)), (got, blk, lambda i: (2, i, 0))],
               [(_sds((A, B)), (ta, B), _row)], (A // ta,), name=name)[0]


def _all_reduce_small(v, name):
    r = v.shape[0]
    masks = [(mx, my, mc) for mx in (0, 1) for my in (0, 1) for mc in (0, 1)][1:]

    def body(v_ref, out_ref, gath, send_sems, recv_sems):
        x, y, c = _my_place()
        me = 4 * x + 2 * y + c
        gath[me] = v_ref[...]

        def peer(m):
            return (x + m[0] - 2 * x * m[0], y + m[1] - 2 * y * m[1], c + m[2] - 2 * c * m[2])

        def copy(k, slab, to):
            return pltpu.make_async_remote_copy(src_ref=v_ref, dst_ref=gath.at[slab], send_sem=send_sems.at[k],
                                                recv_sem=recv_sems.at[k], device_id=to, device_id_type=MESH)

        sends = [copy(k, me, peer(m)) for k, m in enumerate(masks)]
        for cp in sends:
            cp.start()
        for k, m in enumerate(masks):
            px, py, pc = peer(m)
            copy(k, 4 * px + 2 * py + pc, (px, py, pc)).wait_recv()
        for cp in sends:
            cp.wait_send()
        total = gath[0]
        for d in range(1, 8):
            total = total + gath[d]
        out_ref[...] = total

    return pl.pallas_call(
        body, name=name, in_specs=[_VMEM], out_specs=_VMEM, out_shape=jax.ShapeDtypeStruct((r, LANES), F32),
        scratch_shapes=[pltpu.VMEM((8, r, LANES), F32), pltpu.SemaphoreType.DMA((7,)), pltpu.SemaphoreType.DMA((7,))],
    )(v)


def _pack_rows(arrs, dtype, row_mult):
    flat = jnp.concatenate([a.astype(dtype).reshape(-1) for a in arrs])
    n = flat.shape[0]
    rows = _round_up(-(-n // LANES), row_mult)
    return jnp.pad(flat, (0, rows * LANES - n)).reshape(rows, LANES)


def _unpack_rows(buf, shapes):
    lead = buf.shape[:-2]
    flat = buf.reshape(lead + (-1,))
    out, o = [], 0
    for s in shapes:
        n = math.prod(s)
        out.append(lax.slice_in_dim(flat, o, o + n, axis=len(lead)).reshape(lead + tuple(s)))
        o += n
    return out


WEIGHTS = ["mla_w_in", "mla_q_norm", "mla_w_uq", "mla_kv_norm", "mla_w_ukv", "gdn_w_in", "gdn_conv", "gdn_a_log",
           "gdn_dt_bias", "gdn_o_norm", "mem_w_kv", "w_out", "ln1_g", "ln1_b", "mlp_w1", "mlp_w2", "ln2_g", "ln2_b"]
SHARD_AXIS = {"mla_w_in": 1, "mla_w_uq": 2, "mla_w_ukv": 2, "gdn_w_in": 2, "gdn_conv": 2, "mem_w_kv": 1, "w_out": 1,
              "mlp_w1": 2, "mlp_w2": 1}
SMALL = [k for k in WEIGHTS if k not in SHARD_AXIS] + ["gdn_conv"]
BIG = [k for k in WEIGHTS if k not in SMALL]
MLA_KEYS = ["mla_w_in", "mla_q_norm", "mla_w_uq", "mla_kv_norm", "mla_w_ukv"]
GDN_KEYS = ["gdn_w_in", "gdn_conv", "gdn_a_log", "gdn_dt_bias", "gdn_o_norm"]
ALL_KEYS = ["mem_w_kv", "w_out", "ln1_g", "ln1_b", "mlp_w1", "mlp_w2", "ln2_g", "ln2_b"]


def _layer_keys(i):
    return (MLA_KEYS if i % 2 == 0 else GDN_KEYS) + ALL_KEYS


def _layer_slot(k, i):
    return i // 2 if k in MLA_KEYS or k in GDN_KEYS else i


def _gather_layer(w, i):
    kind = "mla" if i % 2 == 0 else "gdn"
    keys = [k for k in _layer_keys(i) if k in SHARD_AXIS]
    arrs = [w[k][_layer_slot(k, i)].astype(F32 if k == "gdn_conv" else BF16) for k in keys]
    outs = _gather_chips(arrs, [SHARD_AXIS[k] == 1 for k in keys], "gather_" + kind)
    full = {k: w[k][_layer_slot(k, i)] for k in _layer_keys(i) if k not in SHARD_AXIS}
    for k, o in zip(keys, outs):
        by_cols = SHARD_AXIS[k] == 2 and k != "mlp_w1"
        full[k] = jnp.concatenate([o[d] for d in range(N_CHIPS)], axis=1) if by_cols else o
    return full


def _reduce_layer(i, G):
    kind = "mla" if i % 2 == 0 else "gdn"
    keys = [k for k in _layer_keys(i) if k in BIG]
    canon = []
    for k in keys:
        g = G[k]
        if k == "mlp_w1":
            g = g.reshape(N_CHIPS, 2, g.shape[1] // 2, g.shape[2])
        elif SHARD_AXIS[k] == 1:
            g = g.reshape(N_CHIPS, 2, g.shape[0] // (2 * N_CHIPS), g.shape[1])
        else:
            rows, cw = g.shape[0], g.shape[1] // N_CHIPS
            g = g.reshape(rows, N_CHIPS, cw).transpose(1, 0, 2).reshape(N_CHIPS, 2, rows // 2, cw)
        canon.append(g)
    theirs = _swap_halves(canon, "grad_swap_" + kind)
    chip_sums = [_add_core(g, t, "grad_add_core") for g, t in zip(canon, theirs)]
    got = _scatter_chips(chip_sums, "grad_scatter_" + kind)
    halves = [_add_chips(p, s, "grad_add_chips") for p, s in zip(chip_sums, got)]
    joined = _join_halves(halves, "grad_join_" + kind)
    out = {k: v for k, v in G.items() if k in SMALL}
    out.update({k: j.reshape(2 * j.shape[1], j.shape[2]) for k, j in zip(keys, joined)})
    return out


def _adamw(w, g, m, v, name):
    shape = w.shape
    cols = shape[-1]
    rows = math.prod(shape[:-1])
    tr = _tile(rows, max(SUBLANES, (1 << 19) // cols // SUBLANES * SUBLANES), SUBLANES)
    spec = ((tr, cols), _row)
    outs = _ew(_adamw_fn, [(a.reshape(rows, cols), *spec) for a in (w, g, m, v)], [(_sds((rows, cols)), *spec)] * 3,
               (rows // tr,), name=name)
    return [o.reshape(shape) for o in outs]


def kernel(x, mem, positions, mla_w_in, mla_q_norm, mla_w_uq, mla_kv_norm, mla_w_ukv, gdn_w_in, gdn_conv, gdn_a_log, gdn_dt_bias, gdn_o_norm, mem_w_kv, w_out, ln1_g, ln1_b, mlp_w1, mlp_w2, ln2_g, ln2_b, loss_target, m_mla_w_in, m_mla_q_norm, m_mla_w_uq, m_mla_kv_norm, m_mla_w_ukv, m_gdn_w_in, m_gdn_conv, m_gdn_a_log, m_gdn_dt_bias, m_gdn_o_norm, m_mem_w_kv, m_w_out, m_ln1_g, m_ln1_b, m_mlp_w1, m_mlp_w2, m_ln2_g, m_ln2_b, v_mla_w_in, v_mla_q_norm, v_mla_w_uq, v_mla_kv_norm, v_mla_w_ukv, v_gdn_w_in, v_gdn_conv, v_gdn_a_log, v_gdn_dt_bias, v_gdn_o_norm, v_mem_w_kv, v_w_out, v_ln1_g, v_ln1_b, v_mlp_w1, v_mlp_w2, v_ln2_g, v_ln2_b):
    w = dict(zip(WEIGHTS, (mla_w_in, mla_q_norm, mla_w_uq, mla_kv_norm, mla_w_ukv, gdn_w_in, gdn_conv, gdn_a_log,
                           gdn_dt_bias, gdn_o_norm, mem_w_kv, w_out, ln1_g, ln1_b, mlp_w1, mlp_w2, ln2_g, ln2_b)))
    m = dict(zip(WEIGHTS, (m_mla_w_in, m_mla_q_norm, m_mla_w_uq, m_mla_kv_norm, m_mla_w_ukv, m_gdn_w_in, m_gdn_conv,
                           m_gdn_a_log, m_gdn_dt_bias, m_gdn_o_norm, m_mem_w_kv, m_w_out, m_ln1_g, m_ln1_b, m_mlp_w1,
                           m_mlp_w2, m_ln2_g, m_ln2_b)))
    v = dict(zip(WEIGHTS, (v_mla_w_in, v_mla_q_norm, v_mla_w_uq, v_mla_kv_norm, v_mla_w_ukv, v_gdn_w_in, v_gdn_conv,
                           v_gdn_a_log, v_gdn_dt_bias, v_gdn_o_norm, v_mem_w_kv, v_w_out, v_ln1_g, v_ln1_b, v_mlp_w1,
                           v_mlp_w2, v_ln2_g, v_ln2_b)))
    assert x.shape[0] == 1, "one sequence per device"
    full_shapes = {k: w[k].shape for k in WEIGHTS}
    for k, ax in SHARD_AXIS.items():
        s = list(w[k].shape)
        s[ax] *= N_CHIPS
        full_shapes[k] = tuple(s)
    c = _dims(x.shape[1], x.shape[2], mem.shape[1], full_shapes)
    depth = c["DEPTH"]

    W = [_gather_layer(w, i) for i in range(depth)]
    loss_local, grad_x, G = _local_step(x[0], mem[0], positions[0], W, loss_target[0], c, grad_sink=_reduce_layer)
    loss = lax.psum(loss_local, ("x", "y", "c"))

    def stacked(k):
        return jnp.stack([G[i][k] for i in range(depth) if k in G[i]], axis=0)

    grads = {k: stacked(k) for k in BIG}
    small_shapes = [full_shapes[k] for k in SMALL]
    gsmall = _all_reduce_small(_pack_rows([stacked(k) for k in SMALL], F32, SUBLANES), "grad_all_reduce_small")
    grads.update(dict(zip(SMALL, _unpack_rows(gsmall, small_shapes))))
    conv_cols = w["gdn_conv"].shape[2]
    grads["gdn_conv"] = lax.dynamic_slice_in_dim(grads["gdn_conv"], _my_chip() * conv_cols, conv_cols, axis=2)

    delta, new_m, new_v = {}, {}, {}
    for k in BIG + ["gdn_conv"]:
        delta[k], new_m[k], new_v[k] = _adamw(w[k], grads[k], m[k], v[k], "adamw")
    small = [k for k in SMALL if k != "gdn_conv"]
    packed = [_pack_rows([d[k] for k in small], F32, SUBLANES) for d in (w, grads, m, v)]
    ds, ms, vs = _adamw(*packed, "adamw_small")
    for d, buf in ((delta, ds), (new_m, ms), (new_v, vs)):
        d.update(dict(zip(small, _unpack_rows(buf, [w[k].shape for k in small]))))

    return (loss, grad_x[None], *[grads[k] for k in WEIGHTS], *[delta[k] for k in WEIGHTS],
            *[new_m[k] for k in WEIGHTS], *[new_v[k] for k in WEIGHTS])
```

```python
import functools
import math

import jax
import jax.numpy as jnp
from jax import lax
from jax.experimental import pallas as pl
from jax.experimental.pallas import tpu as pltpu

F32 = jnp.float32
BF16 = jnp.bfloat16
MESH = pl.DeviceIdType.MESH

LANES = 128
SUBLANES = 8
VMEM_LIMIT = 56 * 1024 * 1024
N_CHIPS = 4

HEAD_DIM = 128
QK_NOPE = 128
QK_ROPE = 64
QK_PAD = 256
ROPE_THETA = 10000.0
CONV_WIDTH = 4
CHUNK = 64
LN_EPS = 1e-5
RMS_EPS = 1e-6
ADAM_LR = 0.001
ADAM_B1 = 0.9
ADAM_B2 = 0.999
ADAM_EPS = 1e-08
ADAM_WD = 0.01
ADAM_STEP = 10
HI = lax.Precision.HIGHEST


def _cparams(sem=None):
    return pltpu.CompilerParams(dimension_semantics=sem, vmem_limit_bytes=VMEM_LIMIT)


def _tile(n, cap, unit):
    best = None
    t = unit
    while t <= min(n, cap):
        if n % t == 0:
            best = t
        t += unit
    return best if best is not None else n


def _mm(a, b, mode, out_dtypes, *, name, epilogue=None, extras=(), tm_cap=1024, tn_cap=1024, tk_cap=512,
        b_major=False, out_major=False):
    if b_major:
        b_shape = (b.shape[1], N_CHIPS * b.shape[2])
    else:
        b_shape = b.shape
    if mode == "nn":
        (M, K), (K2, N) = a.shape, b_shape
    elif mode == "nt":
        (M, K), (N, K2) = a.shape, b_shape
    else:
        (K, M), (K2, N) = a.shape, b_shape
    assert K == K2, (a.shape, b.shape, mode)
    tm = _tile(M, tm_cap, LANES if mode == "tn" else 16)
    tn = _tile(N // N_CHIPS if (out_major or (b_major and mode == "nn")) else N, tn_cap, LANES)
    tk = _tile(K // N_CHIPS if (b_major and mode == "nt") else K, tk_cap, 16 if mode == "tn" else LANES)
    nk = K // tk
    nj4, nk4 = max(N // N_CHIPS // tn, 1), max(K // N_CHIPS // tk, 1)
    if mode == "nn":
        a_spec = pl.BlockSpec((tm, tk), lambda i, j, k: (i, k))
        b_spec = pl.BlockSpec((tk, tn), lambda i, j, k: (k, j))
        if b_major:
            b_spec = pl.BlockSpec((None, tk, tn), lambda i, j, k: (j // nj4, k, j % nj4))
        dims = (((1,), (0,)), ((), ()))
    elif mode == "nt":
        a_spec = pl.BlockSpec((tm, tk), lambda i, j, k: (i, k))
        b_spec = pl.BlockSpec((tn, tk), lambda i, j, k: (j, k))
        if b_major:
            b_spec = pl.BlockSpec((None, tn, tk), lambda i, j, k: (k // nk4, j, k % nk4))
        dims = (((1,), (1,)), ((), ()))
    else:
        assert not b_major
        a_spec = pl.BlockSpec((tk, tm), lambda i, j, k: (k, i))
        b_spec = pl.BlockSpec((tk, tn), lambda i, j, k: (k, j))
        dims = (((0,), (0,)), ((), ()))
    mn_spec = pl.BlockSpec((tm, tn), lambda i, j, k: (i, j))
    o_spec, o_shape = mn_spec, (M, N)
    if out_major:
        o_spec = pl.BlockSpec((None, tm, tn), lambda i, j, k: (j // nj4, i, j % nj4))
        o_shape = (N_CHIPS, M, N // N_CHIPS)
    n_ex, n_out = len(extras), len(out_dtypes)
    for e in extras:
        assert e.shape == (M, N), (e.shape, M, N)

    def body(a_ref, b_ref, *rest):
        ex_refs, out_refs, acc = rest[:n_ex], rest[n_ex:n_ex + n_out], rest[-1]
        k = pl.program_id(2)

        @pl.when(k == 0)
        def _():
            acc[...] = jnp.zeros_like(acc)

        acc[...] += lax.dot_general(a_ref[...].astype(BF16), b_ref[...].astype(BF16), dims,
                                    preferred_element_type=F32)

        @pl.when(k == nk - 1)
        def _():
            res = (acc[...],) if epilogue is None else epilogue(acc[...], *[e[...] for e in ex_refs])
            for o_ref, r in zip(out_refs, res):
                o_ref[...] = r.astype(o_ref.dtype)

    outs = pl.pallas_call(
        body, name=name, grid=(M // tm, N // tn, nk),
        in_specs=[a_spec, b_spec] + [mn_spec] * n_ex,
        out_specs=[o_spec] * n_out,
        out_shape=[jax.ShapeDtypeStruct(o_shape, d) for d in out_dtypes],
        scratch_shapes=[pltpu.VMEM((tm, tn), F32)],
        compiler_params=_cparams(("parallel", "parallel", "arbitrary")),
    )(a, b, *extras)
    return outs


def _spec(block, imap):
    return pl.BlockSpec(block, imap)


def _ew(fn, ins, outs, grid, *, name, acc_out=()):
    n_in = len(ins)
    ng = len(grid)

    def body(*refs):
        in_refs, out_refs = refs[:n_in], refs[n_in:]
        res = fn(*[r[...] for r in in_refs])
        first = functools.reduce(jnp.logical_and, [pl.program_id(d) == 0 for d in range(ng)])
        for i, (o_ref, r) in enumerate(zip(out_refs, res)):
            if i in acc_out:
                @pl.when(first)
                def _(o_ref=o_ref):
                    o_ref[...] = jnp.zeros_like(o_ref)
                o_ref[...] += r.astype(o_ref.dtype)
            else:
                o_ref[...] = r.astype(o_ref.dtype)

    return pl.pallas_call(
        body, name=name, grid=grid,
        in_specs=[_spec(b, m) for _, b, m in ins],
        out_specs=[_spec(b, m) for _, b, m in outs],
        out_shape=[s for s, _, _ in outs],
        compiler_params=_cparams(("arbitrary",) * ng),
    )(*[a for a, _, _ in ins])


def _ew_vjp(fn, ins, cts, gouts, grid, *, name):
    n_in, n_ct = len(ins), len(cts)
    ng = len(grid)
    want = [i for i, g in enumerate(gouts) if g is not None]

    def body(*refs):
        in_refs, ct_refs, out_refs = refs[:n_in], refs[n_in:n_in + n_ct], refs[n_in + n_ct:]
        prim = [r[...] for r in in_refs]
        outs, pull = jax.vjp(fn, *prim)
        grads = pull(tuple(r[...].astype(o.dtype) for r, o in zip(ct_refs, outs)))
        first_all = functools.reduce(jnp.logical_and, [pl.program_id(d) == 0 for d in range(ng)])
        for o_ref, i in zip(out_refs, want):
            mode = gouts[i][3]
            g = grads[i]
            if mode == "set":
                o_ref[...] = g.astype(o_ref.dtype)
            elif mode == "acc":
                @pl.when(pl.program_id(ng - 1) == 0)
                def _(o_ref=o_ref):
                    o_ref[...] = jnp.zeros_like(o_ref)
                o_ref[...] += g.astype(o_ref.dtype)
            elif mode == "acc_all":
                @pl.when(first_all)
                def _(o_ref=o_ref):
                    o_ref[...] = jnp.zeros_like(o_ref)
                o_ref[...] += g.astype(o_ref.dtype)
            else:
                @pl.when(first_all)
                def _(o_ref=o_ref):
                    o_ref[...] = jnp.zeros_like(o_ref)
                idx = pl.program_id(mode[1])
                o_ref[idx] += g.astype(o_ref.dtype)

    return pl.pallas_call(
        body, name=name, grid=grid,
        in_specs=[_spec(b, m) for _, b, m in ins] + [_spec(b, m) for _, b, m in cts],
        out_specs=[_spec(gouts[i][1], gouts[i][2]) for i in want],
        out_shape=[gouts[i][0] for i in want],
        compiler_params=_cparams(("arbitrary",) * ng),
    )(*[a for a, _, _ in ins], *[a for a, _, _ in cts])


def _sds(shape, dtype=F32):
    return jax.ShapeDtypeStruct(tuple(shape), dtype)


def _ln_fn(z, g, b):
    mu = jnp.mean(z, -1, keepdims=True)
    d = z - mu
    var = jnp.mean(d * d, -1, keepdims=True)
    y = d * lax.rsqrt(var + LN_EPS) * g + b
    return y, y


def _rms_fn(x, g):
    return (x * lax.rsqrt(jnp.mean(x * x, -1, keepdims=True) + RMS_EPS) * g,)


@jax.custom_vjp
def _rot_half(x):
    lane = lax.broadcasted_iota(jnp.int32, x.shape, x.ndim - 1)
    up = pltpu.roll(x, LANES - QK_ROPE // 2, x.ndim - 1)
    dn = pltpu.roll(x, QK_ROPE // 2, x.ndim - 1)
    return jnp.where(lane < QK_ROPE // 2, -up, jnp.where(lane < QK_ROPE, dn, 0.0))


def _rot_half_fwd(x):
    return _rot_half(x), None


def _rot_half_bwd(_, ct):
    return (-_rot_half(ct),)


_rot_half.defvjp(_rot_half_fwd, _rot_half_bwd)


def _rope_blk(x, cos, sin):
    return x * cos + _rot_half(x) * sin


def _mla_prep_fn(qraw, knope, kr, cos, sin):
    qn, qr = qraw[:, :QK_NOPE], qraw[:, QK_NOPE:]
    q = jnp.concatenate([qn, _rope_blk(qr, cos, sin)], axis=1)
    k = jnp.concatenate([knope.astype(F32), _rope_blk(kr, cos, sin)], axis=1)
    return q, k


def _l2n(x):
    return x * lax.rsqrt(jnp.sum(x * x, -1, keepdims=True) + 1e-6)


def _gdn_qk_fn(qc, kc):
    return _l2n(qc) * (HEAD_DIM ** -0.5), _l2n(kc)


def _softplus(x):
    return jnp.maximum(x, 0.0) + jnp.log(1.0 + jnp.exp(-jnp.abs(x)))


def _sigmoid(x):
    return 1.0 / (1.0 + jnp.exp(-x))


def _silu(x):
    return x * _sigmoid(x)


def _gdn_gate_fn(n_heads, head_axis):
    def fn(ab, a_log, dt_bias):
        h = pl.program_id(head_axis)
        lane = lax.broadcasted_iota(jnp.int32, ab.shape, 1)
        a_in = jnp.sum(jnp.where(lane == h, ab, 0.0), -1, keepdims=True)
        b_in = jnp.sum(jnp.where(lane == h + n_heads, ab, 0.0), -1, keepdims=True)
        g = -jnp.exp(a_log[:, :CHUNK]) * _softplus(a_in + dt_bias[:, :CHUNK])
        beta = _sigmoid(b_in) + jnp.zeros_like(g)
        return g, beta
    return fn


def _gdn_out_fn(o, z, w):
    return (o * lax.rsqrt(jnp.mean(o * o, -1, keepdims=True) + RMS_EPS) * w * _silu(z),)


def _loss_fn(y, t):
    d = y - t
    return (jnp.sum(d * d, axis=0, keepdims=True) * (0.5 / y.shape[-1]), d * (1.0 / y.shape[-1]))


def _adamw_fn(w, g, m, v):
    m = ADAM_B1 * m + (1.0 - ADAM_B1) * g
    v = ADAM_B2 * v + (1.0 - ADAM_B2) * (g * g)
    m_hat = m / (1.0 - ADAM_B1 ** ADAM_STEP)
    v_hat = v / (1.0 - ADAM_B2 ** ADAM_STEP)
    delta = -ADAM_LR * (m_hat / (jnp.sqrt(v_hat) + ADAM_EPS) + ADAM_WD * w)
    return delta, m, v


def _causal_mask(shape, row0, col0):
    row = lax.broadcasted_iota(jnp.int32, shape, 0) + row0
    col = lax.broadcasted_iota(jnp.int32, shape, 1) + col0
    return col <= row


def _rows(ref, i, t):
    return ref[pl.ds(pl.multiple_of(i * t, t), t), :]


def _walk(first, n_loop, tail, products, update):
    stop = first + n_loop
    t0 = tail[0][0]

    def step(j, carry):
        nxt = products(jnp.where(j + 1 < stop, j + 1, t0))
        update(carry, j, False)
        return nxt

    carry = lax.fori_loop(first, stop, step, products(jnp.where(n_loop > 0, first, t0)))
    for n, (j, masked) in enumerate(tail):
        nxt = products(tail[n + 1][0]) if n + 1 < len(tail) else None
        update(carry, j, masked)
        carry = nxt


def _flash_fwd(q, k, v, *, H, dq, dv, qoff, koff, voff, causal, scale, tq, tk, name, side=None):
    S, Sk = q.shape[0], k.shape[0]
    nq = S // tq
    assert (tq == tk and S == Sk) or not causal

    def body(q_ref, k_ref, v_ref, o_ref, lse_ref, m_s, l_s, acc):
        qi = pl.program_id(1)
        m_s[...] = jnp.full_like(m_s, -jnp.inf)
        l_s[...] = jnp.zeros_like(l_s)
        acc[...] = jnp.zeros_like(acc)
        qb = q_ref[...].astype(BF16)

        def products(j):
            return lax.dot_general(qb, _rows(k_ref, j, tk).astype(BF16), (((1,), (1,)), ((), ())),
                                   preferred_element_type=F32)

        def update(s, j, masked):
            s = s * scale
            if masked:
                s = jnp.where(_causal_mask(s.shape, qi * tq, j * tk), s, -jnp.inf)
            m_prev = m_s[...]
            m_new = jnp.maximum(m_prev, jnp.max(s, axis=1, keepdims=True))
            alpha = jnp.exp(m_prev - m_new)
            p = jnp.exp(s - m_new[:, :1])
            l_s[...] = alpha * l_s[...] + jnp.sum(p, axis=1, keepdims=True)
            acc[...] = acc[...] * alpha[:, :1] + lax.dot_general(
                p.astype(BF16), _rows(v_ref, j, tk).astype(BF16), (((1,), (0,)), ((), ())), preferred_element_type=F32)
            m_s[...] = m_new

        if causal:
            _walk(0, qi, [(qi, True)], products, update)
        else:
            _walk(0, Sk // tk - 1, [(Sk // tk - 1, False)], products, update)
        o_ref[...] = (acc[...] / l_s[...][:, :1]).astype(o_ref.dtype)
        lse_ref[...] = m_s[...] + jnp.log(l_s[...])

    body, s_in, s_out, s_shape, s_sems, s_args = _with_side(body, 3, 2, (H, nq), side)
    o, lse, *side_outs = pl.pallas_call(
        body, name=name, grid=(H, nq),
        in_specs=[pl.BlockSpec((tq, dq), lambda h, qi: (qi, qoff + h)),
                  pl.BlockSpec((Sk, dq), lambda h, qi: (0, koff + h)),
                  pl.BlockSpec((Sk, dv), lambda h, qi: (0, voff + h))] + s_in,
        out_specs=[pl.BlockSpec((tq, dv), lambda h, qi: (qi, h)),
                   pl.BlockSpec((tq, LANES), lambda h, qi: (qi, h))] + s_out,
        out_shape=[_sds((S, H * dv)), _sds((S, H * LANES))] + s_shape,
        scratch_shapes=[pltpu.VMEM((tq, LANES), F32), pltpu.VMEM((tq, LANES), F32), pltpu.VMEM((tq, dv), F32)] + s_sems,
        compiler_params=_cparams(("arbitrary", "arbitrary")),
    )(q, k, v, *s_args)
    return o, lse, side_outs


def _flash_bwd(q, k, v, o, lse, do, *, H, dq, dv, qoff, koff, voff, dooff, causal, scale, tq, tk, name, side=None):
    S, Sk = q.shape[0], k.shape[0]
    nq, nk = S // tq, Sk // tk
    assert (tq == tk and S == Sk) or not causal
    nt = (((1,), (1,)), ((), ()))
    tn = (((0,), (0,)), ((), ()))

    def body_q(q_ref, k_ref, v_ref, o_ref, do_ref, lse_ref, dq_ref, delta_ref, dq_acc):
        qi = pl.program_id(1)
        qb, dob = q_ref[...].astype(BF16), do_ref[...].astype(BF16)
        delta = jnp.sum(do_ref[...].astype(F32) * o_ref[...].astype(F32), axis=1, keepdims=True)
        delta_ref[...] = delta + jnp.zeros_like(delta_ref)
        lse1 = lse_ref[...][:, :1]
        dq_acc[...] = jnp.zeros_like(dq_acc)

        def products(j):
            return (lax.dot_general(qb, _rows(k_ref, j, tk).astype(BF16), nt, preferred_element_type=F32),
                    lax.dot_general(dob, _rows(v_ref, j, tk).astype(BF16), nt, preferred_element_type=F32))

        def update(sp, j, masked):
            s, dp = sp
            p = jnp.exp(s * scale - lse1)
            if masked:
                p = jnp.where(_causal_mask(s.shape, qi * tq, j * tk), p, 0.0)
            ds = p * (dp - delta) * scale
            dq_acc[...] += lax.dot_general(ds.astype(BF16), _rows(k_ref, j, tk).astype(BF16), (((1,), (0,)), ((), ())),
                                           preferred_element_type=F32)

        if causal:
            _walk(0, qi, [(qi, True)], products, update)
        else:
            _walk(0, nk - 1, [(nk - 1, False)], products, update)
        dq_ref[...] = dq_acc[...]

    body_q, s_in, s_out, s_shape, s_sems, s_args = _with_side(body_q, 6, 2, (H, nq), side)
    dqq, delta, *side_outs = pl.pallas_call(
        body_q, name=name + "_dq", grid=(H, nq),
        in_specs=[pl.BlockSpec((tq, dq), lambda h, qi: (qi, qoff + h)),
                  pl.BlockSpec((Sk, dq), lambda h, qi: (0, koff + h)),
                  pl.BlockSpec((Sk, dv), lambda h, qi: (0, voff + h)),
                  pl.BlockSpec((tq, dv), lambda h, qi: (qi, h)),
                  pl.BlockSpec((tq, dv), lambda h, qi: (qi, dooff + h)),
                  pl.BlockSpec((tq, LANES), lambda h, qi: (qi, h))] + s_in,
        out_specs=[pl.BlockSpec((tq, dq), lambda h, qi: (qi, h)),
                   pl.BlockSpec((tq, LANES), lambda h, qi: (qi, h))] + s_out,
        out_shape=[_sds((S, H * dq)), _sds((S, H * LANES))] + s_shape,
        scratch_shapes=[pltpu.VMEM((tq, dq), F32)] + s_sems,
        compiler_params=_cparams(("arbitrary", "arbitrary")),
    )(q, k, v, o, do, lse, *s_args)

    def body_kv(q_ref, k_ref, v_ref, do_ref, lse_ref, delta_ref, dk_ref, dv_ref, dk_acc, dv_acc):
        kj = pl.program_id(1)
        kb, vb = k_ref[...].astype(BF16), v_ref[...].astype(BF16)
        dk_acc[...] = jnp.zeros_like(dk_acc)
        dv_acc[...] = jnp.zeros_like(dv_acc)

        def products(i):
            return (lax.dot_general(_rows(q_ref, i, tq).astype(BF16), kb, nt, preferred_element_type=F32),
                    lax.dot_general(_rows(do_ref, i, tq).astype(BF16), vb, nt, preferred_element_type=F32))

        def update(sp, i, masked):
            s, dp = sp
            p = jnp.exp(s * scale - _rows(lse_ref, i, tq)[:, :1])
            if masked:
                p = jnp.where(_causal_mask(s.shape, i * tq, kj * tk), p, 0.0)
            ds = p * (dp - _rows(delta_ref, i, tq)[:, :1]) * scale
            dv_acc[...] += lax.dot_general(p.astype(BF16), _rows(do_ref, i, tq).astype(BF16), tn,
                                           preferred_element_type=F32)
            dk_acc[...] += lax.dot_general(ds.astype(BF16), _rows(q_ref, i, tq).astype(BF16), tn,
                                           preferred_element_type=F32)

        if causal:
            _walk(kj + 1, nq - 1 - kj, [(kj, True)], products, update)
        else:
            _walk(0, nq - 1, [(nq - 1, False)], products, update)
        dk_ref[...] = dk_acc[...]
        dv_ref[...] = dv_acc[...]

    dk, dvv = pl.pallas_call(
        body_kv, name=name + "_dkv", grid=(H, nk),
        in_specs=[pl.BlockSpec((S, dq), lambda h, kj: (0, qoff + h)),
                  pl.BlockSpec((tk, dq), lambda h, kj: (kj, koff + h)),
                  pl.BlockSpec((tk, dv), lambda h, kj: (kj, voff + h)),
                  pl.BlockSpec((S, dv), lambda h, kj: (0, dooff + h)),
                  pl.BlockSpec((S, LANES), lambda h, kj: (0, h)),
                  pl.BlockSpec((S, LANES), lambda h, kj: (0, h))],
        out_specs=[pl.BlockSpec((tk, dq), lambda h, kj: (kj, h)), pl.BlockSpec((tk, dv), lambda h, kj: (kj, h))],
        out_shape=[_sds((Sk, H * dq)), _sds((Sk, H * dv))],
        scratch_shapes=[pltpu.VMEM((tk, dq), F32), pltpu.VMEM((tk, dv), F32)],
        compiler_params=_cparams(("parallel", "arbitrary")),
    )(q, k, v, do, lse, delta)
    return dqq, dk, dvv, side_outs


def _shift_down(x, prev8, j):
    if j == 0:
        return x
    y = pltpu.roll(x, j, 0)
    head = pltpu.roll(prev8, j, 0)
    row = lax.broadcasted_iota(jnp.int32, x.shape, 0)
    reps = x.shape[0] // SUBLANES
    return jnp.where(row < j, jnp.tile(head, (reps, 1)), y)


def _shift_up(x, next8, j):
    if j == 0:
        return x
    n = x.shape[0]
    y = pltpu.roll(x, n - j, 0)
    tail = pltpu.roll(next8, SUBLANES - j, 0)
    row = lax.broadcasted_iota(jnp.int32, x.shape, 0)
    reps = n // SUBLANES
    return jnp.where(row >= n - j, jnp.tile(tail, (reps, 1)), y)


def _conv_pre(x_ref, p_ref, w_ref, first):
    x = x_ref[...]
    prev8 = jnp.where(first, 0.0, p_ref[...])
    w = w_ref[...]
    xs = [_shift_down(x, prev8, CONV_WIDTH - 1 - j) for j in range(CONV_WIDTH)]
    c = sum(xs[j] * w[j:j + 1, :] for j in range(CONV_WIDTH))
    return c, xs


def _conv_specs(ts, tc, C_total_blocks_off):
    rb = ts // SUBLANES
    off = C_total_blocks_off
    x_spec = pl.BlockSpec((ts, tc), lambda ci, i: (i, off + ci))
    p_spec = pl.BlockSpec((SUBLANES, tc), lambda ci, i: (jnp.maximum(i * rb - 1, 0), off + ci))
    return x_spec, p_spec


def _conv_fwd(h, w, *, C, ts, tc, name):
    S = h.shape[0]
    x_spec, p_spec = _conv_specs(ts, tc, 0)

    def body(x_ref, p_ref, w_ref, y_ref):
        c, _ = _conv_pre(x_ref, p_ref, w_ref, pl.program_id(1) == 0)
        y_ref[...] = _silu(c)

    return pl.pallas_call(
        body, name=name, grid=(C // tc, S // ts),
        in_specs=[x_spec, p_spec, pl.BlockSpec((CONV_WIDTH, tc), lambda ci, i: (0, ci))],
        out_specs=pl.BlockSpec((ts, tc), lambda ci, i: (i, ci)),
        out_shape=_sds((S, C)),
        compiler_params=_cparams(("parallel", "arbitrary")),
    )(h, h, w)


def _conv_bwd(h, w, dy, *, C, ts, tc, name):
    S = h.shape[0]
    ns = S // ts
    rb = ts // SUBLANES
    x_spec, p_spec = _conv_specs(ts, tc, 0)

    def body_a(x_ref, p_ref, w_ref, dy_ref, dc_ref, dw_ref):
        i = pl.program_id(1)
        c, xs = _conv_pre(x_ref, p_ref, w_ref, i == 0)
        sg = _sigmoid(c)
        dc = dy_ref[...] * (sg * (1.0 + c * (1.0 - sg)))
        dc_ref[...] = dc

        @pl.when(i == 0)
        def _():
            dw_ref[...] = jnp.zeros_like(dw_ref)

        dw_ref[...] += jnp.concatenate([jnp.sum(dc * xs[j], axis=0, keepdims=True) for j in range(CONV_WIDTH)], axis=0)

    dc, dw = pl.pallas_call(
        body_a, name=name + "_a", grid=(C // tc, ns),
        in_specs=[x_spec, p_spec, pl.BlockSpec((CONV_WIDTH, tc), lambda ci, i: (0, ci)),
                  pl.BlockSpec((ts, tc), lambda ci, i: (i, ci))],
        out_specs=[pl.BlockSpec((ts, tc), lambda ci, i: (i, ci)),
                   pl.BlockSpec((CONV_WIDTH, tc), lambda ci, i: (0, ci))],
        out_shape=[_sds((S, C)), _sds((CONV_WIDTH, C))],
        compiler_params=_cparams(("parallel", "arbitrary")),
    )(h, h, w, dy)

    def body_b(dc_ref, n_ref, w_ref, dx_ref):
        i = pl.program_id(1)
        dcv = dc_ref[...]
        next8 = jnp.where(i == ns - 1, 0.0, n_ref[...])
        w_ = w_ref[...]
        dx_ref[...] = sum(_shift_up(dcv, next8, CONV_WIDTH - 1 - j) * w_[j:j + 1, :] for j in range(CONV_WIDTH))

    dx = pl.pallas_call(
        body_b, name=name + "_b", grid=(C // tc, ns),
        in_specs=[pl.BlockSpec((ts, tc), lambda ci, i: (i, ci)),
                  pl.BlockSpec((SUBLANES, tc), lambda ci, i: (jnp.minimum((i + 1) * rb, ns * rb - 1), ci)),
                  pl.BlockSpec((CONV_WIDTH, tc), lambda ci, i: (0, ci))],
        out_specs=pl.BlockSpec((ts, tc), lambda ci, i: (i, ci)),
        out_shape=_sds((S, C)),
        compiler_params=_cparams(("parallel", "arbitrary")),
    )(dc, dc, w)
    return dx, dw


def _bdot(a, b, ca, cb, precision=None):
    nb = a.ndim - 2
    batch = tuple(range(nb))
    return lax.dot_general(a, b, (((nb + ca,), (nb + cb,)), (batch, batch)), precision=precision,
                           preferred_element_type=F32)


@jax.custom_vjp
def _nn(a, b):
    return _bdot(a.astype(BF16), b.astype(BF16), 1, 0)


@jax.custom_vjp
def _nt(a, b):
    return _bdot(a.astype(BF16), b.astype(BF16), 1, 1)


@jax.custom_vjp
def _tn(a, b):
    return _bdot(a.astype(BF16), b.astype(BF16), 0, 0)


_nn.defvjp(lambda a, b: (_nn(a, b), (a, b)), lambda r, g: (_nt(g, r[1]), _tn(r[0], g)))
_nt.defvjp(lambda a, b: (_nt(a, b), (a, b)), lambda r, g: (_nn(g, r[1]), _tn(g, r[0])))
_tn.defvjp(lambda a, b: (_tn(a, b), (a, b)), lambda r, g: (_nt(r[1], g), _nn(r[0], g)))


@jax.custom_vjp
def _nn_hi(a, b):
    return _bdot(a, b, 1, 0, HI)


@jax.custom_vjp
def _nt_hi(a, b):
    return _bdot(a, b, 1, 1, HI)


@jax.custom_vjp
def _tn_hi(a, b):
    return _bdot(a, b, 0, 0, HI)


_nn_hi.defvjp(lambda a, b: (_nn_hi(a, b), (a, b)), lambda r, g: (_nt_hi(g, r[1]), _tn_hi(r[0], g)))
_nt_hi.defvjp(lambda a, b: (_nt_hi(a, b), (a, b)), lambda r, g: (_nn_hi(g, r[1]), _tn_hi(g, r[0])))
_tn_hi.defvjp(lambda a, b: (_tn_hi(a, b), (a, b)), lambda r, g: (_nt_hi(r[1], g), _nn_hi(r[0], g)))


def _gdn_chunk_fn(q, k, v, g, beta, state):
    C = CHUNK
    B = q.shape[0]
    row = lax.broadcasted_iota(jnp.int32, (B, C, C), 1)
    col = lax.broadcasted_iota(jnp.int32, (B, C, C), 2)
    tril, strict = row >= col, row > col
    ones_tril = tril.astype(F32)
    gc = _nn_hi(ones_tril, g)
    gr = _nt_hi(jnp.full((B, C, C), 1.0 / C, F32), gc)
    decay = jnp.where(tril, jnp.exp(jnp.where(tril, gc - gr, 0.0)), 0.0)
    b1 = beta[:, :, :1]
    e_gc = jnp.exp(gc[:, :, :1])
    kb = k * b1
    lmat = jnp.where(strict, _nt(kb, k) * decay, 0.0)
    a = -lmat
    t = jnp.where(row == col, 1.0, 0.0) + a
    p = a
    for _ in range(5):
        p = _nn_hi(p, p)
        t = t + _nn_hi(t, p)
    rhs = jnp.concatenate([v * b1, kb * e_gc], axis=2)
    sol = _nn_hi(t, rhs)
    u, w = sol[:, :, :HEAD_DIM], sol[:, :, HEAD_DIM:]
    a_qk = jnp.where(tril, _nt(q, k) * decay, 0.0)
    gl = gc[:, C - 1:C, :1]
    q_dec = q * e_gc
    k_dec = k * jnp.exp(gl - gc[:, :, :1])
    v_new = u - _nn(w, state)
    o = _nn(q_dec, state) + _nn(a_qk, v_new)
    new_state = state * jnp.exp(gl) + _tn(k_dec, v_new)
    return o, new_state


def _split_heads(x, B):
    return jnp.stack([x[:, j * HEAD_DIM:(j + 1) * HEAD_DIM] for j in range(B)], axis=0)


def _merge_heads(x):
    return jnp.concatenate([x[j] for j in range(x.shape[0])], axis=1)


def _gdn_group(H, voff):
    return next(b for b in (12, 6, 4, 3, 2, 1) if H % b == 0 and voff % b == 0)


def _gdn_fwd(q, k, v, g, beta, *, H, voff, name, side=None):
    S = q.shape[0]
    N = S // CHUNK
    B = _gdn_group(H, voff)
    W = B * HEAD_DIM
    qs = lambda off: pl.BlockSpec((CHUNK, W), lambda h, n: (n, off // B + h))
    gs = pl.BlockSpec((B, CHUNK, CHUNK), lambda h, n: (h, n, 0))

    def body(q_ref, k_ref, v_ref, g_ref, b_ref, o_ref, st_ref, state):
        @pl.when(pl.program_id(1) == 0)
        def _():
            state[...] = jnp.zeros_like(state)

        s0 = state[...]
        st_ref[...] = s0
        o, s1 = _gdn_chunk_fn(_split_heads(q_ref[...], B), _split_heads(k_ref[...], B), _split_heads(v_ref[...], B),
                              g_ref[...], b_ref[...], s0)
        o_ref[...] = _merge_heads(o)
        state[...] = s1

    body, s_in, s_out, s_shape, s_sems, s_args = _with_side(body, 5, 2, (H // B, N), side)
    o, states, *side_outs = pl.pallas_call(
        body, name=name, grid=(H // B, N),
        in_specs=[qs(0), qs(0), qs(voff), gs, gs] + s_in,
        out_specs=[qs(0), pl.BlockSpec((B, None, HEAD_DIM, HEAD_DIM), lambda h, n: (h, n, 0, 0))] + s_out,
        out_shape=[_sds((S, H * HEAD_DIM)), _sds((H, N, HEAD_DIM, HEAD_DIM))] + s_shape,
        scratch_shapes=[pltpu.VMEM((B, HEAD_DIM, HEAD_DIM), F32)] + s_sems,
        compiler_params=_cparams(("arbitrary", "arbitrary")),
    )(q, k, v, g, beta, *s_args)
    return o, states, side_outs


def _gdn_bwd(q, k, v, g, beta, states, do, *, H, voff, name, side=None):
    S = q.shape[0]
    N = S // CHUNK
    B = _gdn_group(H, voff)
    W = B * HEAD_DIM
    rs = lambda off: pl.BlockSpec((CHUNK, W), lambda h, n: (N - 1 - n, off // B + h))
    gs = pl.BlockSpec((B, CHUNK, CHUNK), lambda h, n: (h, N - 1 - n, 0))

    def body(q_ref, k_ref, v_ref, g_ref, b_ref, st_ref, do_ref, dq_ref, dk_ref, dv_ref, dg_ref, db_ref, dstate):
        @pl.when(pl.program_id(1) == 0)
        def _():
            dstate[...] = jnp.zeros_like(dstate)

        _, pull = jax.vjp(_gdn_chunk_fn, _split_heads(q_ref[...], B), _split_heads(k_ref[...], B),
                          _split_heads(v_ref[...], B), g_ref[...], b_ref[...], st_ref[...])
        dq, dk, dv, dg, db, ds = pull((_split_heads(do_ref[...], B), dstate[...]))
        dq_ref[...] = _merge_heads(dq)
        dk_ref[...] = _merge_heads(dk)
        dv_ref[...] = _merge_heads(dv)
        dg_ref[...] = dg
        db_ref[...] = db
        dstate[...] = ds

    body, s_in, s_out, s_shape, s_sems, s_args = _with_side(body, 7, 5, (H // B, N), side)
    dq, dk, dv, dg, db, *side_outs = pl.pallas_call(
        body, name=name, grid=(H // B, N),
        in_specs=[rs(0), rs(0), rs(voff), gs, gs,
                  pl.BlockSpec((B, None, HEAD_DIM, HEAD_DIM), lambda h, n: (h, N - 1 - n, 0, 0)), rs(0)] + s_in,
        out_specs=[rs(0), rs(0), rs(0), gs, gs] + s_out,
        out_shape=[_sds((S, H * HEAD_DIM))] * 3 + [_sds((H, S, CHUNK))] * 2 + s_shape,
        scratch_shapes=[pltpu.VMEM((B, HEAD_DIM, HEAD_DIM), F32)] + s_sems,
        compiler_params=_cparams(("arbitrary", "arbitrary")),
    )(q, k, v, g, beta, states, do, *s_args)
    return dq, dk, dv, dg, db, side_outs


def _round_up(n, m):
    return (n + m - 1) // m * m


def _dims(S, D, M, shapes):
    c = dict(S=S, D=D, M=M)
    c["H"] = shapes["gdn_a_log"][-1]
    c["QL"] = shapes["mla_q_norm"][-1]
    c["KVL"] = shapes["mla_kv_norm"][-1]
    assert c["QL"] == c["KVL"]
    c["MEMW"] = shapes["mem_w_kv"][-1] // 2
    c["HM"] = c["MEMW"] // HEAD_DIM
    c["MW"] = c["H"] * HEAD_DIM
    c["F"] = shapes["mlp_w1"][-1]
    c["DEPTH"] = shapes["ln1_g"][0]
    c["ALPHA"] = (2 * c["DEPTH"]) ** 0.25
    c["MLA_IN"] = _round_up(c["QL"] + c["KVL"] + c["MEMW"] + LANES, 2 * LANES)
    c["GDN_IN"] = _round_up(4 * c["MW"] + c["MEMW"] + LANES, 2 * LANES)
    c["t_row"] = min(256, S)
    c["t_head"] = min(512, S)
    c["t_att"] = min(512, S)
    return c


def _pad_cols(w, n):
    return jnp.pad(w, ((0, 0), (0, n - w.shape[1])))


def _prep_mla_w_in(w, c):
    a = c["QL"] + c["KVL"]
    w = jnp.concatenate([w[:, :a], w[:, a + QK_ROPE:a + QK_ROPE + c["MEMW"]], w[:, a:a + QK_ROPE]], axis=1)
    return _pad_cols(w, c["MLA_IN"]).astype(BF16)


def _unprep_mla_w_in(dw, c):
    a, m = c["QL"] + c["KVL"], c["MEMW"]
    return jnp.concatenate([dw[:, :a], dw[:, a + m:a + m + QK_ROPE], dw[:, a:a + m]], axis=1)


def _prep_w_uq(w, c):
    w = w.reshape(c["QL"], c["H"], QK_NOPE + QK_ROPE)
    w = jnp.pad(w, ((0, 0), (0, 0), (0, QK_PAD - QK_NOPE - QK_ROPE)))
    return w.reshape(c["QL"], c["H"] * QK_PAD).astype(BF16)


def _unprep_w_uq(dw, c):
    return dw.reshape(c["QL"], c["H"], QK_PAD)[:, :, :QK_NOPE + QK_ROPE].reshape(c["QL"], c["H"] * (QK_NOPE + QK_ROPE))


def _prep_w_ukv(w, c):
    return w.reshape(c["KVL"], c["H"], 2, HEAD_DIM).transpose(0, 2, 1, 3).reshape(c["KVL"], 2 * c["MW"]).astype(BF16)


def _unprep_w_ukv(dw, c):
    return dw.reshape(c["KVL"], 2, c["H"], HEAD_DIM).transpose(0, 2, 1, 3).reshape(c["KVL"], 2 * c["MW"])


def _prep_gdn_w_in(w, c):
    a, h2 = 4 * c["MW"], 2 * c["H"]
    w = jnp.concatenate([w[:, :a], w[:, a + h2:], w[:, a:a + h2]], axis=1)
    return _pad_cols(w, c["GDN_IN"]).astype(BF16)


def _unprep_gdn_w_in(dw, c):
    a, h2, m = 4 * c["MW"], 2 * c["H"], c["MEMW"]
    return jnp.concatenate([dw[:, :a], dw[:, a + m:a + m + h2], dw[:, a:a + m]], axis=1)


def _lane_bcast(v):
    return jnp.broadcast_to(v.astype(F32)[:, None, None], (v.shape[0], 1, LANES))


def _row(i):
    return (i, 0)


def _par(i):
    return (0, 0)


def _layer_norm(z, g, b, c, name):
    S, D, ts = c["S"], c["D"], c["t_row"]
    return _ew(_ln_fn, [(z, (ts, D), _row), (g, (1, D), _par), (b, (1, D), _par)],
               [(_sds((S, D)), (ts, D), _row), (_sds((S, D), BF16), (ts, D), _row)], (S // ts,), name=name)


def _layer_norm_bwd(z, g, b, dy, c, name):
    S, D, ts = c["S"], c["D"], c["t_row"]
    fn = lambda z, g, b: _ln_fn(z, g, b)[:1]

    def both(z, g, b):
        return fn(z, g, b)

    dz, dg, db = _ew_vjp(both, [(z, (ts, D), _row), (g, (1, D), _par), (b, (1, D), _par)], [(dy, (ts, D), _row)],
                         [(_sds((S, D)), (ts, D), _row, "set"), (_sds((1, D)), (1, D), _par, "acc_all"),
                          (_sds((1, D)), (1, D), _par, "acc_all")], (S // ts,), name=name)
    return dz, dg, db


def _mem_attn_fwd(h, qoff, memkv, c, name):
    return _flash_fwd(h, memkv, memkv, H=c["HM"], dq=HEAD_DIM, dv=HEAD_DIM, qoff=qoff, koff=0, voff=c["HM"],
                      causal=False, scale=HEAD_DIM ** -0.5, tq=c["t_att"], tk=c["M"], name=name)[:2]


def _mem_attn_bwd(h, qoff, memkv, om, lsem, dcat, c, name):
    return _flash_bwd(h, memkv, memkv, om, lsem, dcat, H=c["HM"], dq=HEAD_DIM, dv=HEAD_DIM, qoff=qoff, koff=0,
                      voff=c["HM"], dooff=c["H"], causal=False, scale=HEAD_DIM ** -0.5, tq=c["t_att"], tk=c["M"],
                      name=name)[:3]


def _mla_specs(c):
    H, ts = c["H"], c["t_head"]
    kr_blk = (c["QL"] + c["KVL"] + c["MEMW"]) // LANES
    hd = lambda i, h: (i, h)
    return [((ts, QK_PAD), hd), ((ts, HEAD_DIM), hd), ((ts, LANES), lambda i, h: (i, kr_blk)),
            ((ts, LANES), lambda i, h: (i, 0)), ((ts, LANES), lambda i, h: (i, 0))]


def _mla_fwd(xb, p, cosp, sinp, c, side=None):
    S, H, QL, ts, tr = c["S"], c["H"], c["QL"], c["t_head"], c["t_row"]
    h, = _mm(xb, p["w_in"], "nn", [F32], name="mla_in")
    nq, = _ew(_rms_fn, [(h, (tr, QL), lambda i: (i, 0)), (p["q_norm"], (1, QL), _par)],
              [(_sds((S, QL), BF16), (tr, QL), _row)], (S // tr,), name="mla_qnorm")
    nkv, = _ew(_rms_fn, [(h, (tr, QL), lambda i: (i, 1)), (p["kv_norm"], (1, QL), _par)],
               [(_sds((S, QL), BF16), (tr, QL), _row)], (S // tr,), name="mla_kvnorm")
    qraw, = _mm(nq, p["w_uq"], "nn", [F32], name="mla_uq")
    kvraw, = _mm(nkv, p["w_ukv"], "nn", [BF16], name="mla_ukv")
    sp = _mla_specs(c)
    ins = [(a, b, m) for a, (b, m) in zip([qraw, kvraw, h, cosp, sinp], sp)]
    qp, kp = _ew(_mla_prep_fn, ins, [(_sds((S, H * QK_PAD), BF16), (ts, QK_PAD), lambda i, h: (i, h))] * 2,
                 (S // ts, H), name="mla_rope")
    o, lse, got = _flash_fwd(qp, kp, kvraw, H=H, dq=QK_PAD, dv=HEAD_DIM, qoff=0, koff=0, voff=H, causal=True,
                             scale=(QK_NOPE + QK_ROPE) ** -0.5, tq=c["t_att"], tk=c["t_att"], name="mla_attn", side=side)
    return o, dict(h=h, nq=nq, nkv=nkv, qraw=qraw, kvraw=kvraw, qp=qp, kp=kp, o=o, lse=lse), got


def _mla_bwd(sv, p, cosp, sinp, dcat, dqm, c, side=None):
    S, H, QL, ts, tr = c["S"], c["H"], c["QL"], c["t_head"], c["t_row"]
    dqp, dkp, dv, got = _flash_bwd(sv["qp"], sv["kp"], sv["kvraw"], sv["o"], sv["lse"], dcat, H=H, dq=QK_PAD,
                                   dv=HEAD_DIM, qoff=0, koff=0, voff=H, dooff=0, causal=True,
                                   scale=(QK_NOPE + QK_ROPE) ** -0.5, tq=c["t_att"], tk=c["t_att"],
                                   name="mla_attn_bwd", side=side)
    sp = _mla_specs(c)
    ins = [(a, b, m) for a, (b, m) in zip([sv["qraw"], sv["kvraw"], sv["h"], cosp, sinp], sp)]
    hd = lambda i, h: (i, h)
    dqraw, dknope, dkr = _ew_vjp(
        _mla_prep_fn, ins, [(dqp, (ts, QK_PAD), hd), (dkp, (ts, QK_PAD), hd)],
        [(_sds((S, H * QK_PAD), BF16), (ts, QK_PAD), hd, "set"), (_sds((S, H * HEAD_DIM), BF16), (ts, HEAD_DIM), hd, "set"),
         (_sds((S, LANES)), (ts, LANES), lambda i, h: (i, 0), "acc"), None, None], (S // ts, H), name="mla_rope_bwd")
    dkvraw = jnp.concatenate([dknope, dv.astype(BF16)], axis=1)
    dnq, = _mm(dqraw, p["w_uq"], "nt", [F32], name="mla_uq_dx")
    dw_uq, = _mm(sv["nq"], dqraw, "tn", [F32], name="mla_uq_dw")
    dnkv, = _mm(dkvraw, p["w_ukv"], "nt", [F32], name="mla_ukv_dx")
    dw_ukv, = _mm(sv["nkv"], dkvraw, "tn", [F32], name="mla_ukv_dw")
    dcq, dgq = _ew_vjp(_rms_fn, [(sv["h"], (tr, QL), lambda i: (i, 0)), (p["q_norm"], (1, QL), _par)],
                       [(dnq, (tr, QL), _row)],
                       [(_sds((S, QL), BF16), (tr, QL), _row, "set"), (_sds((1, QL)), (1, QL), _par, "acc_all")],
                       (S // tr,), name="mla_qnorm_bwd")
    dckv, dgkv = _ew_vjp(_rms_fn, [(sv["h"], (tr, QL), lambda i: (i, 1)), (p["kv_norm"], (1, QL), _par)],
                         [(dnkv, (tr, QL), _row)],
                         [(_sds((S, QL), BF16), (tr, QL), _row, "set"), (_sds((1, QL)), (1, QL), _par, "acc_all")],
                         (S // tr,), name="mla_kvnorm_bwd")
    pad = c["MLA_IN"] - (2 * QL + c["MEMW"] + LANES)
    dh = jnp.concatenate([dcq, dckv, dqm.astype(BF16), dkr.astype(BF16)] + ([jnp.zeros((S, pad), BF16)] if pad else []),
                         axis=1)
    grads = dict(mla_q_norm=dgq[0], mla_kv_norm=dgkv[0], mla_w_uq=_unprep_w_uq(dw_uq, c),
                 mla_w_ukv=_unprep_w_ukv(dw_ukv, c))
    return dh, grads, got


def _gdn_ins(h, qkvc, p, c):
    H, ts = c["H"], c["t_head"]
    ab_blk = (4 * c["MW"] + c["MEMW"]) // LANES
    qk_ins = [(qkvc, (ts, HEAD_DIM), lambda i, h: (i, h)), (qkvc, (ts, HEAD_DIM), lambda i, h: (i, H + h))]
    gate_ins = [(h, (ts, LANES), lambda i, h: (i, ab_blk)), (p["a_log"], (None, 1, LANES), lambda i, h: (h, 0, 0)),
                (p["dt_bias"], (None, 1, LANES), lambda i, h: (h, 0, 0))]
    return qk_ins, gate_ins


def _gdn_out_ins(o, h, p, c):
    H, ts = c["H"], c["t_head"]
    return [(o, (ts, HEAD_DIM), lambda i, h: (i, h)), (h, (ts, HEAD_DIM), lambda i, h: (i, 3 * H + h)),
            (p["o_norm"], (1, HEAD_DIM), lambda i, h: (0, 0))]


def _gdn_layer_fwd(xb, p, c, side=None):
    S, H, MW, ts = c["S"], c["H"], c["MW"], c["t_head"]
    h, = _mm(xb, p["w_in"], "nn", [F32], name="gdn_in")
    tc = _tile(3 * MW, 512, LANES)
    qkvc = _conv_fwd(h, p["conv"], C=3 * MW, ts=ts, tc=tc, name="gdn_conv")
    qk_ins, gate_ins = _gdn_ins(h, qkvc, p, c)
    hd = lambda i, h: (i, h)
    qn, kn = _ew(_gdn_qk_fn, qk_ins, [(_sds((S, MW)), (ts, HEAD_DIM), hd)] * 2, (S // ts, H), name="gdn_qknorm")
    g3 = lambda i, h: (h, i, 0)
    g, beta = _ew(_gdn_gate_fn(H, 1), gate_ins, [(_sds((H, S, CHUNK)), (None, ts, CHUNK), g3)] * 2, (S // ts, H),
                  name="gdn_gate")
    o, states, got = _gdn_fwd(qn, kn, qkvc, g, beta, H=H, voff=2 * H, name="gdn_delta", side=side)
    mix, = _ew(_gdn_out_fn, _gdn_out_ins(o, h, p, c), [(_sds((S, MW)), (ts, HEAD_DIM), hd)], (S // ts, H),
               name="gdn_outnorm")
    return mix, dict(h=h, qkvc=qkvc, qn=qn, kn=kn, g=g, beta=beta, o=o, states=states), got


def _gdn_layer_bwd(sv, p, dcat, dqm, c, side=None):
    S, H, MW, ts = c["S"], c["H"], c["MW"], c["t_head"]
    hd = lambda i, h: (i, h)
    g3 = lambda i, h: (h, i, 0)
    h, qkvc = sv["h"], sv["qkvc"]
    do, dz, d_onorm = _ew_vjp(_gdn_out_fn, _gdn_out_ins(sv["o"], h, p, c), [(dcat, (ts, HEAD_DIM), hd)],
                              [(_sds((S, MW)), (ts, HEAD_DIM), hd, "set"), (_sds((S, MW), BF16), (ts, HEAD_DIM), hd, "set"),
                               (_sds((1, HEAD_DIM)), (1, HEAD_DIM), lambda i, h: (0, 0), "acc_all")],
                              (S // ts, H), name="gdn_outnorm_bwd")
    dqn, dkn, dv, dg, db, got = _gdn_bwd(sv["qn"], sv["kn"], qkvc, sv["g"], sv["beta"], sv["states"], do, H=H,
                                         voff=2 * H, name="gdn_delta_bwd", side=side)
    qk_ins, gate_ins = _gdn_ins(h, qkvc, p, c)
    dqc, dkc = _ew_vjp(_gdn_qk_fn, qk_ins, [(dqn, (ts, HEAD_DIM), hd), (dkn, (ts, HEAD_DIM), hd)],
                       [(_sds((S, MW)), (ts, HEAD_DIM), hd, "set")] * 2, (S // ts, H), name="gdn_qknorm_bwd")
    full3 = lambda i, h: (0, 0, 0)
    dab, dalog, ddt = _ew_vjp(
        _gdn_gate_fn(H, 1), gate_ins, [(dg, (None, ts, CHUNK), g3), (db, (None, ts, CHUNK), g3)],
        [(_sds((S, LANES), BF16), (ts, LANES), lambda i, h: (i, 0), "acc"),
         (_sds((H, 1, LANES)), (H, 1, LANES), full3, ("acc_at", 1)),
         (_sds((H, 1, LANES)), (H, 1, LANES), full3, ("acc_at", 1))], (S // ts, H), name="gdn_gate_bwd")
    dqkvc = jnp.concatenate([dqc, dkc, dv], axis=1)
    tc = _tile(3 * MW, 512, LANES)
    dxc, dconv = _conv_bwd(h, p["conv"], dqkvc, C=3 * MW, ts=ts, tc=tc, name="gdn_conv_bwd")
    pad = c["GDN_IN"] - (4 * MW + c["MEMW"] + LANES)
    dh = jnp.concatenate([dxc.astype(BF16), dz, dqm.astype(BF16), dab] + ([jnp.zeros((S, pad), BF16)] if pad else []),
                         axis=1)
    grads = dict(gdn_conv=dconv, gdn_a_log=jnp.sum(dalog[:, 0, :], axis=-1), gdn_dt_bias=jnp.sum(ddt[:, 0, :], axis=-1),
                 gdn_o_norm=d_onorm[0])
    return dh, grads, got


def _prep_layer(W, i, c):
    p = dict(mem_w_kv=W["mem_w_kv"].astype(BF16), w_out=W["w_out"].astype(BF16),
             w1=W["mlp_w1"].astype(BF16), w2=W["mlp_w2"].astype(BF16),
             ln1_g=W["ln1_g"][None].astype(F32), ln1_b=W["ln1_b"][None].astype(F32),
             ln2_g=W["ln2_g"][None].astype(F32), ln2_b=W["ln2_b"][None].astype(F32))
    if i % 2 == 0:
        p.update(w_in=_prep_mla_w_in(W["mla_w_in"], c), q_norm=W["mla_q_norm"][None].astype(F32),
                 w_uq=_prep_w_uq(W["mla_w_uq"], c), kv_norm=W["mla_kv_norm"][None].astype(F32),
                 w_ukv=_prep_w_ukv(W["mla_w_ukv"], c))
    else:
        p.update(w_in=_prep_gdn_w_in(W["gdn_w_in"], c), conv=W["gdn_conv"].astype(F32),
                 a_log=_lane_bcast(W["gdn_a_log"]), dt_bias=_lane_bcast(W["gdn_dt_bias"]),
                 o_norm=W["gdn_o_norm"][None].astype(F32))
    return p


def _local_step(x, mem, positions, W, loss_target, c, next_weights=None, grad_sink=None):
    S, D, H, MW, ALPHA = c["S"], c["D"], c["H"], c["MW"], c["ALPHA"]
    inv_freq = 1.0 / (ROPE_THETA ** (jnp.arange(0, QK_ROPE, 2, dtype=F32) / QK_ROPE))
    ang = positions.astype(F32)[:, None] * inv_freq
    cos, sin = jnp.cos(ang), jnp.sin(ang)
    cosp = jnp.concatenate([cos, cos, jnp.ones((S, LANES - QK_ROPE), F32)], axis=1)
    sinp = jnp.concatenate([sin, sin, jnp.zeros((S, LANES - QK_ROPE), F32)], axis=1)
    memb = mem.astype(BF16)
    xf, xb = x, x.astype(BF16)
    saved, params = [], []
    w_next = W[0]
    for i in range(c["DEPTH"]):
        p = _prep_layer(w_next if next_weights is not None else W[i], i, c)
        mla = i % 2 == 0
        memkv, = _mm(memb, p["mem_w_kv"], "nn", [BF16], name="mem_kv")
        side, arrived = next_weights(i + 1) if next_weights is not None and i + 1 < c["DEPTH"] else (None, None)
        if mla:
            mix, sv, got = _mla_fwd(xb, p, cosp, sinp, c, side)
            qoff = (c["QL"] + c["KVL"]) // LANES
        else:
            mix, sv, got = _gdn_layer_fwd(xb, p, c, side)
            qoff = 4 * MW // LANES
        if side is not None:
            w_next = arrived(got)
        om, lsem = _mem_attn_fwd(sv["h"], qoff, memkv, c, "mem_attn")
        cat = jnp.concatenate([mix, om], axis=1).astype(BF16)
        z1, = _mm(cat, p["w_out"], "nn", [F32], name="w_out", extras=(xf,), epilogue=lambda acc, r: (ALPHA * r + acc,))
        x1, x1b = _layer_norm(z1, p["ln1_g"], p["ln1_b"], c, "ln1")
        u, a = _mm(x1b, p["w1"], "nn", [F32, BF16], name="mlp_up", b_major=True,
                   epilogue=lambda acc: (acc, jnp.square(jnp.maximum(acc, 0.0))))
        z2, = _mm(a, p["w2"], "nn", [F32], name="mlp_down", extras=(x1,), epilogue=lambda acc, r: (ALPHA * r + acc,))
        x2, x2b = _layer_norm(z2, p["ln2_g"], p["ln2_b"], c, "ln2")
        sv.update(xb=xb, memkv=memkv, om=om, lsem=lsem, cat=cat, z1=z1, x1b=x1b, u=u, a=a, z2=z2, qoff=qoff)
        saved.append(sv)
        params.append(p)
        xf, xb = x2, x2b

    ts = c["t_row"]
    lsum, dy = _ew(_loss_fn, [(xf, (ts, D), _row), (loss_target, (ts, D), _row)],
                   [(_sds((1, D)), (1, D), _par), (_sds((S, D)), (ts, D), _row)], (S // ts,), name="loss", acc_out=(0,))
    loss = jnp.sum(lsum)

    grads = [None] * c["DEPTH"]
    pending = None
    dx = dy
    for i in reversed(range(c["DEPTH"])):
        p, sv = params[i], saved[i]
        mla = i % 2 == 0
        side = pending[0] if pending is not None else None
        G = {}
        dz2, dg, db = _layer_norm_bwd(sv["z2"], p["ln2_g"], p["ln2_b"], dx, c, "ln2_bwd")
        G["ln2_g"], G["ln2_b"] = dg[0], db[0]
        dz2b = dz2.astype(BF16)
        du, = _mm(dz2b, p["w2"], "nt", [BF16], name="mlp_down_dx", extras=(sv["u"],),
                  epilogue=lambda acc, u: (acc * (2.0 * jnp.maximum(u, 0.0)),))
        G["mlp_w2"], = _mm(sv["a"], dz2b, "tn", [F32], name="mlp_down_dw")
        G["mlp_w1"], = _mm(sv["x1b"], du, "tn", [F32], name="mlp_up_dw", out_major=True)
        dx1, = _mm(du, p["w1"], "nt", [F32], name="mlp_up_dx", extras=(dz2,), b_major=True,
                   epilogue=lambda acc, r: (ALPHA * r + acc,))
        dz1, dg, db = _layer_norm_bwd(sv["z1"], p["ln1_g"], p["ln1_b"], dx1, c, "ln1_bwd")
        G["ln1_g"], G["ln1_b"] = dg[0], db[0]
        dz1b = dz1.astype(BF16)
        dcat, = _mm(dz1b, p["w_out"], "nt", [BF16], name="w_out_dx")
        G["w_out"], = _mm(sv["cat"], dz1b, "tn", [F32], name="w_out_dw")
        dqm, dkm, dvm = _mem_attn_bwd(sv["h"], sv["qoff"], sv["memkv"], sv["om"], sv["lsem"], dcat, c, "mem_attn_bwd")
        dmemkv = jnp.concatenate([dkm, dvm], axis=1).astype(BF16)
        G["mem_w_kv"], = _mm(memb, dmemkv, "tn", [F32], name="mem_kv_dw")
        if mla:
            dh, g, got = _mla_bwd(sv, p, cosp, sinp, dcat, dqm, c, side)
            G.update(g)
            dw_in, = _mm(sv["xb"], dh, "tn", [F32], name="mla_in_dw")
            G["mla_w_in"] = _unprep_mla_w_in(dw_in, c)
            dx, = _mm(dh, p["w_in"], "nt", [F32], name="mla_in_dx", extras=(dz1,),
                      epilogue=lambda acc, r: (ALPHA * r + acc,))
        else:
            dh, g, got = _gdn_layer_bwd(sv, p, dcat, dqm, c, side)
            G.update(g)
            dw_in, = _mm(sv["xb"], dh, "tn", [F32], name="gdn_in_dw")
            G["gdn_w_in"] = _unprep_gdn_w_in(dw_in, c)
            dx, = _mm(dh, p["w_in"], "nt", [F32], name="gdn_in_dx", extras=(dz1,),
                      epilogue=lambda acc, r: (ALPHA * r + acc,))
        if pending is not None:
            grads[pending[2]] = pending[1](got)
            pending = None
        if grad_sink is None:
            grads[i] = G
        else:
            pending = grad_sink(i, G) + (i,)
    if pending is not None:
        grads[pending[2]] = pending[1](_run_plan(pending[0], "grad_scatter_last"))
    return loss, dx, grads


_HBM = pl.BlockSpec(memory_space=pltpu.HBM)
_VMEM = pl.BlockSpec(memory_space=pltpu.VMEM)


def _my_place():
    return lax.axis_index("x"), lax.axis_index("y"), lax.axis_index("c")


def _my_chip():
    return 2 * lax.axis_index("x") + lax.axis_index("y")


def _other_chips(x, y):
    return [(1 - x, y), (x, 1 - y), (1 - x, 1 - y)]


def _gather_plan(arrs, by_rows):
    n = len(arrs)

    def run(ins, outs, sems, start, wait):
        send_sems, recv_sems, local_sems = sems
        x, y, c = _my_place()
        chips = _other_chips(x, y)

        def slab(i, chip):
            r = arrs[i].shape[0]
            return outs[i].at[pl.ds(chip * r, r)] if by_rows[i] else outs[i].at[chip]

        def copy(i, k, chip, to):
            return pltpu.make_async_remote_copy(src_ref=ins[i], dst_ref=slab(i, chip), send_sem=send_sems.at[3 * i + k],
                                                recv_sem=recv_sems.at[3 * i + k], device_id=to, device_id_type=MESH)

        mine = [pltpu.make_async_copy(ins[i], slab(i, 2 * x + y), local_sems.at[i]) for i in range(n)]
        sends = [copy(i, k, 2 * x + y, (cx, cy, c)) for i in range(n) for k, (cx, cy) in enumerate(chips)]
        if start:
            for cp in mine + sends:
                cp.start()
        if wait:
            for i in range(n):
                for k, (cx, cy) in enumerate(chips):
                    copy(i, k, 2 * cx + cy, (cx, cy, c)).wait_recv()
            for cp in sends:
                cp.wait_send()
            for cp in mine:
                cp.wait()

    shapes = [jax.ShapeDtypeStruct((N_CHIPS * a.shape[0], a.shape[1]) if r else (N_CHIPS,) + a.shape, a.dtype)
              for a, r in zip(arrs, by_rows)]
    sems = [pltpu.SemaphoreType.DMA((3 * n,)), pltpu.SemaphoreType.DMA((3 * n,)), pltpu.SemaphoreType.DMA((n,))]
    return dict(arrs=list(arrs), out_shape=shapes, sems=sems, run=run)


def _scatter_plan(ps):
    n = len(ps)

    def run(ins, outs, sems, start, wait):
        send_sems, recv_sems = sems
        x, y, c = _my_place()
        cps = [pltpu.make_async_remote_copy(src_ref=ins[i].at[2 * cx + cy], dst_ref=outs[i].at[k],
                                            send_sem=send_sems.at[3 * i + k], recv_sem=recv_sems.at[3 * i + k],
                                            device_id=(cx, cy, c), device_id_type=MESH)
               for i in range(n) for k, (cx, cy) in enumerate(_other_chips(x, y))]
        if start:
            for cp in cps:
                cp.start()
        if wait:
            for cp in cps:
                cp.wait()

    shapes = [jax.ShapeDtypeStruct((3,) + p.shape[1:], p.dtype) for p in ps]
    sems = [pltpu.SemaphoreType.DMA((3 * n,)), pltpu.SemaphoreType.DMA((3 * n,))]
    return dict(arrs=list(ps), out_shape=shapes, sems=sems, run=run)


def _run_plan(plan, name):
    n_in, n_out = len(plan["arrs"]), len(plan["out_shape"])

    def body(*refs):
        plan["run"](refs[:n_in], refs[n_in:n_in + n_out], refs[n_in + n_out:], True, True)

    return pl.pallas_call(body, name=name, in_specs=[_HBM] * n_in, out_specs=[_HBM] * n_out,
                          out_shape=plan["out_shape"], scratch_shapes=plan["sems"])(*plan["arrs"])


def _with_side(body, n_in, n_out, grid, side):
    if side is None:
        return body, [], [], [], [], []
    s_in, s_out, s_sem = len(side["arrs"]), len(side["out_shape"]), len(side["sems"])

    def wrapped(*refs):
        ins, s_ins = refs[:n_in], refs[n_in:n_in + s_in]
        o0 = n_in + s_in
        outs, s_outs = refs[o0:o0 + n_out], refs[o0 + n_out:o0 + n_out + s_out]
        rest = refs[o0 + n_out + s_out:]
        scratch, s_sems = rest[:len(rest) - s_sem], rest[len(rest) - s_sem:]
        ids = [pl.program_id(d) for d in range(len(grid))]
        first = functools.reduce(jnp.logical_and, [i == 0 for i in ids])
        last = functools.reduce(jnp.logical_and, [i == g - 1 for i, g in zip(ids, grid)])
        pl.when(first)(lambda: side["run"](s_ins, s_outs, s_sems, True, False))
        body(*ins, *outs, *scratch)
        pl.when(last)(lambda: side["run"](s_ins, s_outs, s_sems, False, True))

    return wrapped, [_HBM] * s_in, [_HBM] * s_out, side["out_shape"], side["sems"], side["arrs"]


def _swap_halves(gs, name):
    n = len(gs)

    def body(*refs):
        ins, outs = refs[:n], refs[n:2 * n]
        send_sems, recv_sems = refs[2 * n:]
        x, y, c = _my_place()
        cps = [pltpu.make_async_remote_copy(src_ref=ins[i].at[:, 1 - c], dst_ref=outs[i], send_sem=send_sems.at[i],
                                            recv_sem=recv_sems.at[i], device_id=(x, y, 1 - c), device_id_type=MESH)
               for i in range(n)]
        for cp in cps:
            cp.start()
        for cp in cps:
            cp.wait()

    return pl.pallas_call(
        body, name=name, in_specs=[_HBM] * n, out_specs=[_HBM] * n,
        out_shape=[jax.ShapeDtypeStruct((g.shape[0],) + g.shape[2:], g.dtype) for g in gs],
        scratch_shapes=[pltpu.SemaphoreType.DMA((n,)), pltpu.SemaphoreType.DMA((n,))],
    )(*gs)


def _join_halves(fs, name):
    n = len(fs)

    def body(*refs):
        ins, outs = refs[:n], refs[n:2 * n]
        send_sems, recv_sems, local_sems = refs[2 * n:]
        x, y, c = _my_place()

        def copy(i, half):
            return pltpu.make_async_remote_copy(src_ref=ins[i], dst_ref=outs[i].at[half], send_sem=send_sems.at[i],
                                                recv_sem=recv_sems.at[i], device_id=(x, y, 1 - c), device_id_type=MESH)

        mine = [pltpu.make_async_copy(ins[i], outs[i].at[c], local_sems.at[i]) for i in range(n)]
        sends = [copy(i, c) for i in range(n)]
        for cp in mine + sends:
            cp.start()
        for i in range(n):
            copy(i, 1 - c).wait_recv()
        for cp in sends:
            cp.wait_send()
        for cp in mine:
            cp.wait()

    return pl.pallas_call(
        body, name=name, in_specs=[_HBM] * n, out_specs=[_HBM] * n,
        out_shape=[jax.ShapeDtypeStruct((2,) + f.shape, f.dtype) for f in fs],
        scratch_shapes=[pltpu.SemaphoreType.DMA((n,)), pltpu.SemaphoreType.DMA((n,)), pltpu.SemaphoreType.DMA((n,))],
    )(*fs)


def _row_tile(a, b):
    return _tile(a, max(SUBLANES, (1 << 19) // b // SUBLANES * SUBLANES), SUBLANES)


def _add_core(g, got, name):
    _, _, A, B = g.shape
    ta = _row_tile(A, B)
    return _ew(lambda p, q: (p + q,),
               [(g, (None, None, ta, B), lambda s, i: (s, lax.axis_index("c"), i, 0)),
                (got, (None, ta, B), lambda s, i: (s, i, 0))],
               [(_sds((N_CHIPS, A, B)), (None, ta, B), lambda s, i: (s, i, 0))], (N_CHIPS, A // ta), name=name)[0]


def _add_chips(p, got, name):
    _, A, B = p.shape
    ta = _row_tile(A, B)
    blk = (None, ta, B)
    return _ew(lambda a, b, c_, d: (((a + b) + c_) + d,),
               [(p, blk, lambda i: (_my_chip(), i, 0)), (got, blk, lambda i: (0, i, 0)),
                (got, blk, lambda i: (1, i, 0)), (got, blk, lambda i: (2, i, 0))],
               [(_sds((A, B)), (ta, B), _row)], (A // ta,), name=name)[0]


def _all_reduce_small(v, name):
    r = v.shape[0]
    masks = [(mx, my, mc) for mx in (0, 1) for my in (0, 1) for mc in (0, 1)][1:]

    def body(v_ref, out_ref, gath, send_sems, recv_sems):
        x, y, c = _my_place()
        me = 4 * x + 2 * y + c
        gath[me] = v_ref[...]

        def peer(m):
            return (x + m[0] - 2 * x * m[0], y + m[1] - 2 * y * m[1], c + m[2] - 2 * c * m[2])

        def copy(k, slab, to):
            return pltpu.make_async_remote_copy(src_ref=v_ref, dst_ref=gath.at[slab], send_sem=send_sems.at[k],
                                                recv_sem=recv_sems.at[k], device_id=to, device_id_type=MESH)

        sends = [copy(k, me, peer(m)) for k, m in enumerate(masks)]
        for cp in sends:
            cp.start()
        for k, m in enumerate(masks):
            px, py, pc = peer(m)
            copy(k, 4 * px + 2 * py + pc, (px, py, pc)).wait_recv()
        for cp in sends:
            cp.wait_send()
        total = gath[0]
        for d in range(1, 8):
            total = total + gath[d]
        out_ref[...] = total

    return pl.pallas_call(
        body, name=name, in_specs=[_VMEM], out_specs=_VMEM, out_shape=jax.ShapeDtypeStruct((r, LANES), F32),
        scratch_shapes=[pltpu.VMEM((8, r, LANES), F32), pltpu.SemaphoreType.DMA((7,)), pltpu.SemaphoreType.DMA((7,))],
    )(v)


def _pack_rows(arrs, dtype, row_mult):
    flat = jnp.concatenate([a.astype(dtype).reshape(-1) for a in arrs])
    n = flat.shape[0]
    rows = _round_up(-(-n // LANES), row_mult)
    return jnp.pad(flat, (0, rows * LANES - n)).reshape(rows, LANES)


def _unpack_rows(buf, shapes):
    lead = buf.shape[:-2]
    flat = buf.reshape(lead + (-1,))
    out, o = [], 0
    for s in shapes:
        n = math.prod(s)
        out.append(lax.slice_in_dim(flat, o, o + n, axis=len(lead)).reshape(lead + tuple(s)))
        o += n
    return out


WEIGHTS = ["mla_w_in", "mla_q_norm", "mla_w_uq", "mla_kv_norm", "mla_w_ukv", "gdn_w_in", "gdn_conv", "gdn_a_log",
           "gdn_dt_bias", "gdn_o_norm", "mem_w_kv", "w_out", "ln1_g", "ln1_b", "mlp_w1", "mlp_w2", "ln2_g", "ln2_b"]
SHARD_AXIS = {"mla_w_in": 1, "mla_w_uq": 2, "mla_w_ukv": 2, "gdn_w_in": 2, "gdn_conv": 2, "mem_w_kv": 1, "w_out": 1,
              "mlp_w1": 2, "mlp_w2": 1}
SMALL = [k for k in WEIGHTS if k not in SHARD_AXIS] + ["gdn_conv"]
BIG = [k for k in WEIGHTS if k not in SMALL]
MLA_KEYS = ["mla_w_in", "mla_q_norm", "mla_w_uq", "mla_kv_norm", "mla_w_ukv"]
GDN_KEYS = ["gdn_w_in", "gdn_conv", "gdn_a_log", "gdn_dt_bias", "gdn_o_norm"]
ALL_KEYS = ["mem_w_kv", "w_out", "ln1_g", "ln1_b", "mlp_w1", "mlp_w2", "ln2_g", "ln2_b"]


def _layer_keys(i):
    return (MLA_KEYS if i % 2 == 0 else GDN_KEYS) + ALL_KEYS


def _layer_slot(k, i):
    return i // 2 if k in MLA_KEYS or k in GDN_KEYS else i


def _gather_layer(w, i):
    keys = [k for k in _layer_keys(i) if k in SHARD_AXIS]
    arrs = [w[k][_layer_slot(k, i)].astype(F32 if k == "gdn_conv" else BF16) for k in keys]

    def arrived(outs):
        full = {k: w[k][_layer_slot(k, i)] for k in _layer_keys(i) if k not in SHARD_AXIS}
        for k, o in zip(keys, outs):
            by_cols = SHARD_AXIS[k] == 2 and k != "mlp_w1"
            full[k] = jnp.concatenate([o[d] for d in range(N_CHIPS)], axis=1) if by_cols else o
        return full

    return _gather_plan(arrs, [SHARD_AXIS[k] == 1 for k in keys]), arrived


def _reduce_layer(i, G):
    kind = "mla" if i % 2 == 0 else "gdn"
    keys = [k for k in _layer_keys(i) if k in BIG]
    canon = []
    for k in keys:
        g = G[k]
        if k == "mlp_w1":
            g = g.reshape(N_CHIPS, 2, g.shape[1] // 2, g.shape[2])
        elif SHARD_AXIS[k] == 1:
            g = g.reshape(N_CHIPS, 2, g.shape[0] // (2 * N_CHIPS), g.shape[1])
        else:
            rows, cw = g.shape[0], g.shape[1] // N_CHIPS
            g = g.reshape(rows, N_CHIPS, cw).transpose(1, 0, 2).reshape(N_CHIPS, 2, rows // 2, cw)
        canon.append(g)
    theirs = _swap_halves(canon, "grad_swap_" + kind)
    chip_sums = [_add_core(g, t, "grad_add_core") for g, t in zip(canon, theirs)]

    def done(got):
        halves = [_add_chips(p, s, "grad_add_chips") for p, s in zip(chip_sums, got)]
        joined = _join_halves(halves, "grad_join_" + kind)
        out = {k: v for k, v in G.items() if k in SMALL}
        out.update({k: j.reshape(2 * j.shape[1], j.shape[2]) for k, j in zip(keys, joined)})
        return out

    return _scatter_plan(chip_sums), done


def _adamw(w, g, m, v, name):
    shape = w.shape
    cols = shape[-1]
    rows = math.prod(shape[:-1])
    tr = _tile(rows, max(SUBLANES, (1 << 19) // cols // SUBLANES * SUBLANES), SUBLANES)
    spec = ((tr, cols), _row)
    outs = _ew(_adamw_fn, [(a.reshape(rows, cols), *spec) for a in (w, g, m, v)], [(_sds((rows, cols)), *spec)] * 3,
               (rows // tr,), name=name)
    return [o.reshape(shape) for o in outs]


def kernel(x, mem, positions, mla_w_in, mla_q_norm, mla_w_uq, mla_kv_norm, mla_w_ukv, gdn_w_in, gdn_conv, gdn_a_log, gdn_dt_bias, gdn_o_norm, mem_w_kv, w_out, ln1_g, ln1_b, mlp_w1, mlp_w2, ln2_g, ln2_b, loss_target, m_mla_w_in, m_mla_q_norm, m_mla_w_uq, m_mla_kv_norm, m_mla_w_ukv, m_gdn_w_in, m_gdn_conv, m_gdn_a_log, m_gdn_dt_bias, m_gdn_o_norm, m_mem_w_kv, m_w_out, m_ln1_g, m_ln1_b, m_mlp_w1, m_mlp_w2, m_ln2_g, m_ln2_b, v_mla_w_in, v_mla_q_norm, v_mla_w_uq, v_mla_kv_norm, v_mla_w_ukv, v_gdn_w_in, v_gdn_conv, v_gdn_a_log, v_gdn_dt_bias, v_gdn_o_norm, v_mem_w_kv, v_w_out, v_ln1_g, v_ln1_b, v_mlp_w1, v_mlp_w2, v_ln2_g, v_ln2_b):
    w = dict(zip(WEIGHTS, (mla_w_in, mla_q_norm, mla_w_uq, mla_kv_norm, mla_w_ukv, gdn_w_in, gdn_conv, gdn_a_log,
                           gdn_dt_bias, gdn_o_norm, mem_w_kv, w_out, ln1_g, ln1_b, mlp_w1, mlp_w2, ln2_g, ln2_b)))
    m = dict(zip(WEIGHTS, (m_mla_w_in, m_mla_q_norm, m_mla_w_uq, m_mla_kv_norm, m_mla_w_ukv, m_gdn_w_in, m_gdn_conv,
                           m_gdn_a_log, m_gdn_dt_bias, m_gdn_o_norm, m_mem_w_kv, m_w_out, m_ln1_g, m_ln1_b, m_mlp_w1,
                           m_mlp_w2, m_ln2_g, m_ln2_b)))
    v = dict(zip(WEIGHTS, (v_mla_w_in, v_mla_q_norm, v_mla_w_uq, v_mla_kv_norm, v_mla_w_ukv, v_gdn_w_in, v_gdn_conv,
                           v_gdn_a_log, v_gdn_dt_bias, v_gdn_o_norm, v_mem_w_kv, v_w_out, v_ln1_g, v_ln1_b, v_mlp_w1,
                           v_mlp_w2, v_ln2_g, v_ln2_b)))
    assert x.shape[0] == 1, "one sequence per device"
    full_shapes = {k: w[k].shape for k in WEIGHTS}
    for k, ax in SHARD_AXIS.items():
        s = list(w[k].shape)
        s[ax] *= N_CHIPS
        full_shapes[k] = tuple(s)
    c = _dims(x.shape[1], x.shape[2], mem.shape[1], full_shapes)
    depth = c["DEPTH"]

    first, arrived = _gather_layer(w, 0)
    W = [arrived(_run_plan(first, "gather_first"))]
    loss_local, grad_x, G = _local_step(x[0], mem[0], positions[0], W, loss_target[0], c,
                                        next_weights=lambda i: _gather_layer(w, i), grad_sink=_reduce_layer)
    loss = lax.psum(loss_local, ("x", "y", "c"))

    def stacked(k):
        return jnp.stack([G[i][k] for i in range(depth) if k in G[i]], axis=0)

    grads = {k: stacked(k) for k in BIG}
    small_shapes = [full_shapes[k] for k in SMALL]
    gsmall = _all_reduce_small(_pack_rows([stacked(k) for k in SMALL], F32, SUBLANES), "grad_all_reduce_small")
    grads.update(dict(zip(SMALL, _unpack_rows(gsmall, small_shapes))))
    conv_cols = w["gdn_conv"].shape[2]
    grads["gdn_conv"] = lax.dynamic_slice_in_dim(grads["gdn_conv"], _my_chip() * conv_cols, conv_cols, axis=2)

    delta, new_m, new_v = {}, {}, {}
    for k in BIG + ["gdn_conv"]:
        delta[k], new_m[k], new_v[k] = _adamw(w[k], grads[k], m[k], v[k], "adamw")
    small = [k for k in SMALL if k != "gdn_conv"]
    packed = [_pack_rows([d[k] for k in small], F32, SUBLANES) for d in (w, grads, m, v)]
    ds, ms, vs = _adamw(*packed, "adamw_small")
    for d, buf in ((delta, ds), (new_m, ms), (new_v, vs)):
        d.update(dict(zip(small, _unpack_rows(buf, [w[k].shape for k in small]))))

    return (loss, grad_x[None], *[grads[k] for k in WEIGHTS], *[delta[k] for k in WEIGHTS],
            *[new_m[k] for k in WEIGHTS], *[new_v[k] for k in WEIGHTS])
```

```python
import functools
import math

import jax
import jax.numpy as jnp
from jax import lax
from jax.experimental import pallas as pl
from jax.experimental.pallas import tpu as pltpu

F32 = jnp.float32
BF16 = jnp.bfloat16
MESH = pl.DeviceIdType.MESH

LANES = 128
SUBLANES = 8
VMEM_LIMIT = 56 * 1024 * 1024
N_CHIPS = 4

HEAD_DIM = 128
QK_NOPE = 128
QK_ROPE = 64
QK_PAD = 256
ROPE_THETA = 10000.0
CONV_WIDTH = 4
CHUNK = 64
LN_EPS = 1e-5
RMS_EPS = 1e-6
ADAM_LR = 0.001
ADAM_B1 = 0.9
ADAM_B2 = 0.999
ADAM_EPS = 1e-08
ADAM_WD = 0.01
ADAM_STEP = 10
HI = lax.Precision.HIGHEST


def _cparams(sem=None):
    return pltpu.CompilerParams(dimension_semantics=sem, vmem_limit_bytes=VMEM_LIMIT)


def _tile(n, cap, unit):
    best = None
    t = unit
    while t <= min(n, cap):
        if n % t == 0:
            best = t
        t += unit
    return best if best is not None else n


def _mm(a, b, mode, out_dtypes, *, name, epilogue=None, extras=(), tm_cap=1024, tn_cap=1024, tk_cap=2048,
        b_major=False, out_major=False):
    if b_major:
        b_shape = (b.shape[1], N_CHIPS * b.shape[2])
    else:
        b_shape = b.shape
    if mode == "nn":
        (M, K), (K2, N) = a.shape, b_shape
    elif mode == "nt":
        (M, K), (N, K2) = a.shape, b_shape
    else:
        (K, M), (K2, N) = a.shape, b_shape
    assert K == K2, (a.shape, b.shape, mode)
    tm = _tile(M, tm_cap, LANES if mode == "tn" else 16)
    tn = _tile(N // N_CHIPS if (out_major or (b_major and mode == "nn")) else N, tn_cap, LANES)
    tk = _tile(K // N_CHIPS if (b_major and mode == "nt") else K, tk_cap, 16 if mode == "tn" else LANES)
    nk = K // tk
    nj4, nk4 = max(N // N_CHIPS // tn, 1), max(K // N_CHIPS // tk, 1)
    if mode == "nn":
        a_spec = pl.BlockSpec((tm, tk), lambda i, j, k: (i, k))
        b_spec = pl.BlockSpec((tk, tn), lambda i, j, k: (k, j))
        if b_major:
            b_spec = pl.BlockSpec((None, tk, tn), lambda i, j, k: (j // nj4, k, j % nj4))
        dims = (((1,), (0,)), ((), ()))
    elif mode == "nt":
        a_spec = pl.BlockSpec((tm, tk), lambda i, j, k: (i, k))
        b_spec = pl.BlockSpec((tn, tk), lambda i, j, k: (j, k))
        if b_major:
            b_spec = pl.BlockSpec((None, tn, tk), lambda i, j, k: (k // nk4, j, k % nk4))
        dims = (((1,), (1,)), ((), ()))
    else:
        assert not b_major
        a_spec = pl.BlockSpec((tk, tm), lambda i, j, k: (k, i))
        b_spec = pl.BlockSpec((tk, tn), lambda i, j, k: (k, j))
        dims = (((0,), (0,)), ((), ()))
    mn_spec = pl.BlockSpec((tm, tn), lambda i, j, k: (i, j))
    o_spec, o_shape = mn_spec, (M, N)
    if out_major:
        o_spec = pl.BlockSpec((None, tm, tn), lambda i, j, k: (j // nj4, i, j % nj4))
        o_shape = (N_CHIPS, M, N // N_CHIPS)
    n_ex, n_out = len(extras), len(out_dtypes)
    for e in extras:
        assert e.shape == (M, N), (e.shape, M, N)

    def body(a_ref, b_ref, *rest):
        ex_refs, out_refs, acc = rest[:n_ex], rest[n_ex:n_ex + n_out], rest[-1]
        k = pl.program_id(2)

        @pl.when(k == 0)
        def _():
            acc[...] = jnp.zeros_like(acc)

        acc[...] += lax.dot_general(a_ref[...].astype(BF16), b_ref[...].astype(BF16), dims,
                                    preferred_element_type=F32)

        @pl.when(k == nk - 1)
        def _():
            res = (acc[...],) if epilogue is None else epilogue(acc[...], *[e[...] for e in ex_refs])
            for o_ref, r in zip(out_refs, res):
                o_ref[...] = r.astype(o_ref.dtype)

    outs = pl.pallas_call(
        body, name=name, grid=(M // tm, N // tn, nk),
        in_specs=[a_spec, b_spec] + [mn_spec] * n_ex,
        out_specs=[o_spec] * n_out,
        out_shape=[jax.ShapeDtypeStruct(o_shape, d) for d in out_dtypes],
        scratch_shapes=[pltpu.VMEM((tm, tn), F32)],
        compiler_params=_cparams(("parallel", "parallel", "arbitrary")),
    )(a, b, *extras)
    return outs


def _spec(block, imap):
    return pl.BlockSpec(block, imap)


def _ew(fn, ins, outs, grid, *, name, acc_out=()):
    n_in = len(ins)
    ng = len(grid)

    def body(*refs):
        in_refs, out_refs = refs[:n_in], refs[n_in:]
        res = fn(*[r[...] for r in in_refs])
        first = functools.reduce(jnp.logical_and, [pl.program_id(d) == 0 for d in range(ng)])
        for i, (o_ref, r) in enumerate(zip(out_refs, res)):
            if i in acc_out:
                @pl.when(first)
                def _(o_ref=o_ref):
                    o_ref[...] = jnp.zeros_like(o_ref)
                o_ref[...] += r.astype(o_ref.dtype)
            else:
                o_ref[...] = r.astype(o_ref.dtype)

    return pl.pallas_call(
        body, name=name, grid=grid,
        in_specs=[_spec(b, m) for _, b, m in ins],
        out_specs=[_spec(b, m) for _, b, m in outs],
        out_shape=[s for s, _, _ in outs],
        compiler_params=_cparams(("arbitrary",) * ng),
    )(*[a for a, _, _ in ins])


def _ew_vjp(fn, ins, cts, gouts, grid, *, name):
    n_in, n_ct = len(ins), len(cts)
    ng = len(grid)
    want = [i for i, g in enumerate(gouts) if g is not None]

    def body(*refs):
        in_refs, ct_refs, out_refs = refs[:n_in], refs[n_in:n_in + n_ct], refs[n_in + n_ct:]
        prim = [r[...] for r in in_refs]
        outs, pull = jax.vjp(fn, *prim)
        grads = pull(tuple(r[...].astype(o.dtype) for r, o in zip(ct_refs, outs)))
        first_all = functools.reduce(jnp.logical_and, [pl.program_id(d) == 0 for d in range(ng)])
        for o_ref, i in zip(out_refs, want):
            mode = gouts[i][3]
            g = grads[i]
            if mode == "set":
                o_ref[...] = g.astype(o_ref.dtype)
            elif mode == "acc":
                @pl.when(pl.program_id(ng - 1) == 0)
                def _(o_ref=o_ref):
                    o_ref[...] = jnp.zeros_like(o_ref)
                o_ref[...] += g.astype(o_ref.dtype)
            elif mode == "acc_all":
                @pl.when(first_all)
                def _(o_ref=o_ref):
                    o_ref[...] = jnp.zeros_like(o_ref)
                o_ref[...] += g.astype(o_ref.dtype)
            else:
                @pl.when(first_all)
                def _(o_ref=o_ref):
                    o_ref[...] = jnp.zeros_like(o_ref)
                idx = pl.program_id(mode[1])
                o_ref[idx] += g.astype(o_ref.dtype)

    return pl.pallas_call(
        body, name=name, grid=grid,
        in_specs=[_spec(b, m) for _, b, m in ins] + [_spec(b, m) for _, b, m in cts],
        out_specs=[_spec(gouts[i][1], gouts[i][2]) for i in want],
        out_shape=[gouts[i][0] for i in want],
        compiler_params=_cparams(("arbitrary",) * ng),
    )(*[a for a, _, _ in ins], *[a for a, _, _ in cts])


def _sds(shape, dtype=F32):
    return jax.ShapeDtypeStruct(tuple(shape), dtype)


def _ln_fn(z, g, b):
    mu = jnp.mean(z, -1, keepdims=True)
    d = z - mu
    var = jnp.mean(d * d, -1, keepdims=True)
    y = d * lax.rsqrt(var + LN_EPS) * g + b
    return y, y


def _rms_fn(x, g):
    return (x * lax.rsqrt(jnp.mean(x * x, -1, keepdims=True) + RMS_EPS) * g,)


@jax.custom_vjp
def _rot_half(x):
    lane = lax.broadcasted_iota(jnp.int32, x.shape, x.ndim - 1)
    up = pltpu.roll(x, LANES - QK_ROPE // 2, x.ndim - 1)
    dn = pltpu.roll(x, QK_ROPE // 2, x.ndim - 1)
    return jnp.where(lane < QK_ROPE // 2, -up, jnp.where(lane < QK_ROPE, dn, 0.0))


def _rot_half_fwd(x):
    return _rot_half(x), None


def _rot_half_bwd(_, ct):
    return (-_rot_half(ct),)


_rot_half.defvjp(_rot_half_fwd, _rot_half_bwd)


def _rope_blk(x, cos, sin):
    return x * cos + _rot_half(x) * sin


def _mla_prep_fn(qraw, knope, kr, cos, sin):
    qn, qr = qraw[:, :QK_NOPE], qraw[:, QK_NOPE:]
    q = jnp.concatenate([qn, _rope_blk(qr, cos, sin)], axis=1)
    k = jnp.concatenate([knope.astype(F32), _rope_blk(kr, cos, sin)], axis=1)
    return q, k


def _l2n(x):
    return x * lax.rsqrt(jnp.sum(x * x, -1, keepdims=True) + 1e-6)


def _gdn_qk_fn(qc, kc):
    return _l2n(qc) * (HEAD_DIM ** -0.5), _l2n(kc)


def _softplus(x):
    return jnp.maximum(x, 0.0) + jnp.log(1.0 + jnp.exp(-jnp.abs(x)))


def _sigmoid(x):
    return 1.0 / (1.0 + jnp.exp(-x))


def _silu(x):
    return x * _sigmoid(x)


def _gdn_gate_fn(n_heads, head_axis):
    def fn(ab, a_log, dt_bias):
        h = pl.program_id(head_axis)
        lane = lax.broadcasted_iota(jnp.int32, ab.shape, 1)
        a_in = jnp.sum(jnp.where(lane == h, ab, 0.0), -1, keepdims=True)
        b_in = jnp.sum(jnp.where(lane == h + n_heads, ab, 0.0), -1, keepdims=True)
        g = -jnp.exp(a_log[:, :CHUNK]) * _softplus(a_in + dt_bias[:, :CHUNK])
        beta = _sigmoid(b_in) + jnp.zeros_like(g)
        return g, beta
    return fn


def _gdn_out_fn(o, z, w):
    return (o * lax.rsqrt(jnp.mean(o * o, -1, keepdims=True) + RMS_EPS) * w * _silu(z),)


def _loss_fn(y, t):
    d = y - t
    return (jnp.sum(d * d, axis=0, keepdims=True) * (0.5 / y.shape[-1]), d * (1.0 / y.shape[-1]))


def _adamw_fn(w, g, m, v):
    m = ADAM_B1 * m + (1.0 - ADAM_B1) * g
    v = ADAM_B2 * v + (1.0 - ADAM_B2) * (g * g)
    m_hat = m / (1.0 - ADAM_B1 ** ADAM_STEP)
    v_hat = v / (1.0 - ADAM_B2 ** ADAM_STEP)
    delta = -ADAM_LR * (m_hat / (jnp.sqrt(v_hat) + ADAM_EPS) + ADAM_WD * w)
    return delta, m, v


def _causal_mask(shape, row0, col0):
    row = lax.broadcasted_iota(jnp.int32, shape, 0) + row0
    col = lax.broadcasted_iota(jnp.int32, shape, 1) + col0
    return col <= row


def _rows(ref, i, t):
    return ref[pl.ds(pl.multiple_of(i * t, t), t), :]


def _walk(first, n_loop, tail, products, update):
    stop = first + n_loop
    t0 = tail[0][0]

    def step(j, carry):
        nxt = products(jnp.where(j + 1 < stop, j + 1, t0))
        update(carry, j, False)
        return nxt

    carry = lax.fori_loop(first, stop, step, products(jnp.where(n_loop > 0, first, t0)))
    for n, (j, masked) in enumerate(tail):
        nxt = products(tail[n + 1][0]) if n + 1 < len(tail) else None
        update(carry, j, masked)
        carry = nxt


def _flash_fwd(q, k, v, *, H, dq, dv, qoff, koff, voff, causal, scale, tq, tk, name, side=None):
    S, Sk = q.shape[0], k.shape[0]
    nq = S // tq
    assert (tq == tk and S == Sk) or not causal

    def body(q_ref, k_ref, v_ref, o_ref, lse_ref, m_s, l_s, acc):
        qi = pl.program_id(1)
        m_s[...] = jnp.full_like(m_s, -jnp.inf)
        l_s[...] = jnp.zeros_like(l_s)
        acc[...] = jnp.zeros_like(acc)
        qb = q_ref[...].astype(BF16)

        def products(j):
            return lax.dot_general(qb, _rows(k_ref, j, tk).astype(BF16), (((1,), (1,)), ((), ())),
                                   preferred_element_type=F32)

        def update(s, j, masked):
            s = s * scale
            if masked:
                s = jnp.where(_causal_mask(s.shape, qi * tq, j * tk), s, -jnp.inf)
            m_prev = m_s[...]
            m_new = jnp.maximum(m_prev, jnp.max(s, axis=1, keepdims=True))
            alpha = jnp.exp(m_prev - m_new)
            p = jnp.exp(s - m_new[:, :1])
            l_s[...] = alpha * l_s[...] + jnp.sum(p, axis=1, keepdims=True)
            acc[...] = acc[...] * alpha[:, :1] + lax.dot_general(
                p.astype(BF16), _rows(v_ref, j, tk).astype(BF16), (((1,), (0,)), ((), ())), preferred_element_type=F32)
            m_s[...] = m_new

        if causal:
            _walk(0, qi, [(qi, True)], products, update)
        else:
            _walk(0, Sk // tk - 1, [(Sk // tk - 1, False)], products, update)
        o_ref[...] = (acc[...] / l_s[...][:, :1]).astype(o_ref.dtype)
        lse_ref[...] = m_s[...] + jnp.log(l_s[...])

    body, s_in, s_out, s_shape, s_sems, s_args = _with_side(body, 3, 2, (H, nq), side)
    o, lse, *side_outs = pl.pallas_call(
        body, name=name, grid=(H, nq),
        in_specs=[pl.BlockSpec((tq, dq), lambda h, qi: (qi, qoff + h)),
                  pl.BlockSpec((Sk, dq), lambda h, qi: (0, koff + h)),
                  pl.BlockSpec((Sk, dv), lambda h, qi: (0, voff + h))] + s_in,
        out_specs=[pl.BlockSpec((tq, dv), lambda h, qi: (qi, h)),
                   pl.BlockSpec((tq, LANES), lambda h, qi: (qi, h))] + s_out,
        out_shape=[_sds((S, H * dv)), _sds((S, H * LANES))] + s_shape,
        scratch_shapes=[pltpu.VMEM((tq, LANES), F32), pltpu.VMEM((tq, LANES), F32), pltpu.VMEM((tq, dv), F32)] + s_sems,
        compiler_params=_cparams(("arbitrary", "arbitrary")),
    )(q, k, v, *s_args)
    return o, lse, side_outs


def _flash_bwd(q, k, v, o, lse, do, *, H, dq, dv, qoff, koff, voff, dooff, causal, scale, tq, tk, name, side=None):
    S, Sk = q.shape[0], k.shape[0]
    nq, nk = S // tq, Sk // tk
    assert (tq == tk and S == Sk) or not causal
    nt = (((1,), (1,)), ((), ()))
    tn = (((0,), (0,)), ((), ()))

    def body_q(q_ref, k_ref, v_ref, o_ref, do_ref, lse_ref, dq_ref, delta_ref, dq_acc):
        qi = pl.program_id(1)
        qb, dob = q_ref[...].astype(BF16), do_ref[...].astype(BF16)
        delta = jnp.sum(do_ref[...].astype(F32) * o_ref[...].astype(F32), axis=1, keepdims=True)
        delta_ref[...] = delta + jnp.zeros_like(delta_ref)
        lse1 = lse_ref[...][:, :1]
        dq_acc[...] = jnp.zeros_like(dq_acc)

        def products(j):
            return (lax.dot_general(qb, _rows(k_ref, j, tk).astype(BF16), nt, preferred_element_type=F32),
                    lax.dot_general(dob, _rows(v_ref, j, tk).astype(BF16), nt, preferred_element_type=F32))

        def update(sp, j, masked):
            s, dp = sp
            p = jnp.exp(s * scale - lse1)
            if masked:
                p = jnp.where(_causal_mask(s.shape, qi * tq, j * tk), p, 0.0)
            ds = p * (dp - delta) * scale
            dq_acc[...] += lax.dot_general(ds.astype(BF16), _rows(k_ref, j, tk).astype(BF16), (((1,), (0,)), ((), ())),
                                           preferred_element_type=F32)

        if causal:
            _walk(0, qi, [(qi, True)], products, update)
        else:
            _walk(0, nk - 1, [(nk - 1, False)], products, update)
        dq_ref[...] = dq_acc[...]

    body_q, s_in, s_out, s_shape, s_sems, s_args = _with_side(body_q, 6, 2, (H, nq), side)
    dqq, delta, *side_outs = pl.pallas_call(
        body_q, name=name + "_dq", grid=(H, nq),
        in_specs=[pl.BlockSpec((tq, dq), lambda h, qi: (qi, qoff + h)),
                  pl.BlockSpec((Sk, dq), lambda h, qi: (0, koff + h)),
                  pl.BlockSpec((Sk, dv), lambda h, qi: (0, voff + h)),
                  pl.BlockSpec((tq, dv), lambda h, qi: (qi, h)),
                  pl.BlockSpec((tq, dv), lambda h, qi: (qi, dooff + h)),
                  pl.BlockSpec((tq, LANES), lambda h, qi: (qi, h))] + s_in,
        out_specs=[pl.BlockSpec((tq, dq), lambda h, qi: (qi, h)),
                   pl.BlockSpec((tq, LANES), lambda h, qi: (qi, h))] + s_out,
        out_shape=[_sds((S, H * dq)), _sds((S, H * LANES))] + s_shape,
        scratch_shapes=[pltpu.VMEM((tq, dq), F32)] + s_sems,
        compiler_params=_cparams(("arbitrary", "arbitrary")),
    )(q, k, v, o, do, lse, *s_args)

    def as_rows(t):
        return t[:, ::LANES].T.reshape(H, nq, 1, tq)

    nn = (((1,), (0,)), ((), ()))

    def body_kv(q_ref, k_ref, v_ref, do_ref, lse_ref, delta_ref, dk_ref, dv_ref, dk_acc, dv_acc):
        kj = pl.program_id(1)
        kb, vb = k_ref[...].astype(BF16), v_ref[...].astype(BF16)
        dk_acc[...] = jnp.zeros_like(dk_acc)
        dv_acc[...] = jnp.zeros_like(dv_acc)

        def products(i):
            return (lax.dot_general(kb, _rows(q_ref, i, tq).astype(BF16), nt, preferred_element_type=F32),
                    lax.dot_general(vb, _rows(do_ref, i, tq).astype(BF16), nt, preferred_element_type=F32))

        def update(sp, i, masked):
            st, dpt = sp
            pt = jnp.exp(st * scale - lse_ref[i])
            if masked:
                key = lax.broadcasted_iota(jnp.int32, st.shape, 0) + kj * tk
                qry = lax.broadcasted_iota(jnp.int32, st.shape, 1) + i * tq
                pt = jnp.where(key <= qry, pt, 0.0)
            dst = pt * (dpt - delta_ref[i]) * scale
            dv_acc[...] += lax.dot_general(pt.astype(BF16), _rows(do_ref, i, tq).astype(BF16), nn,
                                           preferred_element_type=F32)
            dk_acc[...] += lax.dot_general(dst.astype(BF16), _rows(q_ref, i, tq).astype(BF16), nn,
                                           preferred_element_type=F32)

        if causal:
            _walk(kj + 1, nq - 1 - kj, [(kj, True)], products, update)
        else:
            _walk(0, nq - 1, [(nq - 1, False)], products, update)
        dk_ref[...] = dk_acc[...]
        dv_ref[...] = dv_acc[...]

    row_spec = pl.BlockSpec((None, nq, 1, tq), lambda h, kj: (h, 0, 0, 0))
    dk, dvv = pl.pallas_call(
        body_kv, name=name + "_dkv", grid=(H, nk),
        in_specs=[pl.BlockSpec((S, dq), lambda h, kj: (0, qoff + h)),
                  pl.BlockSpec((tk, dq), lambda h, kj: (kj, koff + h)),
                  pl.BlockSpec((tk, dv), lambda h, kj: (kj, voff + h)),
                  pl.BlockSpec((S, dv), lambda h, kj: (0, dooff + h)), row_spec, row_spec],
        out_specs=[pl.BlockSpec((tk, dq), lambda h, kj: (kj, h)), pl.BlockSpec((tk, dv), lambda h, kj: (kj, h))],
        out_shape=[_sds((Sk, H * dq)), _sds((Sk, H * dv))],
        scratch_shapes=[pltpu.VMEM((tk, dq), F32), pltpu.VMEM((tk, dv), F32)],
        compiler_params=_cparams(("parallel", "arbitrary")),
    )(q, k, v, do, as_rows(lse), as_rows(delta))
    return dqq, dk, dvv, side_outs


def _shift_down(x, prev8, j):
    if j == 0:
        return x
    y = pltpu.roll(x, j, 0)
    head = pltpu.roll(prev8, j, 0)
    row = lax.broadcasted_iota(jnp.int32, x.shape, 0)
    reps = x.shape[0] // SUBLANES
    return jnp.where(row < j, jnp.tile(head, (reps, 1)), y)


def _shift_up(x, next8, j):
    if j == 0:
        return x
    n = x.shape[0]
    y = pltpu.roll(x, n - j, 0)
    tail = pltpu.roll(next8, SUBLANES - j, 0)
    row = lax.broadcasted_iota(jnp.int32, x.shape, 0)
    reps = n // SUBLANES
    return jnp.where(row >= n - j, jnp.tile(tail, (reps, 1)), y)


def _conv_pre(x_ref, p_ref, w_ref, first):
    x = x_ref[...]
    prev8 = jnp.where(first, 0.0, p_ref[...])
    w = w_ref[...]
    xs = [_shift_down(x, prev8, CONV_WIDTH - 1 - j) for j in range(CONV_WIDTH)]
    c = sum(xs[j] * w[j:j + 1, :] for j in range(CONV_WIDTH))
    return c, xs


def _conv_specs(ts, tc, C_total_blocks_off):
    rb = ts // SUBLANES
    off = C_total_blocks_off
    x_spec = pl.BlockSpec((ts, tc), lambda ci, i: (i, off + ci))
    p_spec = pl.BlockSpec((SUBLANES, tc), lambda ci, i: (jnp.maximum(i * rb - 1, 0), off + ci))
    return x_spec, p_spec


def _conv_fwd(h, w, *, C, ts, tc, name):
    S = h.shape[0]
    x_spec, p_spec = _conv_specs(ts, tc, 0)

    def body(x_ref, p_ref, w_ref, y_ref):
        c, _ = _conv_pre(x_ref, p_ref, w_ref, pl.program_id(1) == 0)
        y_ref[...] = _silu(c)

    return pl.pallas_call(
        body, name=name, grid=(C // tc, S // ts),
        in_specs=[x_spec, p_spec, pl.BlockSpec((CONV_WIDTH, tc), lambda ci, i: (0, ci))],
        out_specs=pl.BlockSpec((ts, tc), lambda ci, i: (i, ci)),
        out_shape=_sds((S, C)),
        compiler_params=_cparams(("parallel", "arbitrary")),
    )(h, h, w)


def _conv_bwd(h, w, dy, *, C, ts, tc, name):
    S = h.shape[0]
    ns = S // ts
    rb = ts // SUBLANES
    x_spec, p_spec = _conv_specs(ts, tc, 0)

    def body_a(x_ref, p_ref, w_ref, dy_ref, dc_ref, dw_ref):
        i = pl.program_id(1)
        c, xs = _conv_pre(x_ref, p_ref, w_ref, i == 0)
        sg = _sigmoid(c)
        dc = dy_ref[...] * (sg * (1.0 + c * (1.0 - sg)))
        dc_ref[...] = dc

        @pl.when(i == 0)
        def _():
            dw_ref[...] = jnp.zeros_like(dw_ref)

        dw_ref[...] += jnp.concatenate([jnp.sum(dc * xs[j], axis=0, keepdims=True) for j in range(CONV_WIDTH)], axis=0)

    dc, dw = pl.pallas_call(
        body_a, name=name + "_a", grid=(C // tc, ns),
        in_specs=[x_spec, p_spec, pl.BlockSpec((CONV_WIDTH, tc), lambda ci, i: (0, ci)),
                  pl.BlockSpec((ts, tc), lambda ci, i: (i, ci))],
        out_specs=[pl.BlockSpec((ts, tc), lambda ci, i: (i, ci)),
                   pl.BlockSpec((CONV_WIDTH, tc), lambda ci, i: (0, ci))],
        out_shape=[_sds((S, C)), _sds((CONV_WIDTH, C))],
        compiler_params=_cparams(("parallel", "arbitrary")),
    )(h, h, w, dy)

    def body_b(dc_ref, n_ref, w_ref, dx_ref):
        i = pl.program_id(1)
        dcv = dc_ref[...]
        next8 = jnp.where(i == ns - 1, 0.0, n_ref[...])
        w_ = w_ref[...]
        dx_ref[...] = sum(_shift_up(dcv, next8, CONV_WIDTH - 1 - j) * w_[j:j + 1, :] for j in range(CONV_WIDTH))

    dx = pl.pallas_call(
        body_b, name=name + "_b", grid=(C // tc, ns),
        in_specs=[pl.BlockSpec((ts, tc), lambda ci, i: (i, ci)),
                  pl.BlockSpec((SUBLANES, tc), lambda ci, i: (jnp.minimum((i + 1) * rb, ns * rb - 1), ci)),
                  pl.BlockSpec((CONV_WIDTH, tc), lambda ci, i: (0, ci))],
        out_specs=pl.BlockSpec((ts, tc), lambda ci, i: (i, ci)),
        out_shape=_sds((S, C)),
        compiler_params=_cparams(("parallel", "arbitrary")),
    )(dc, dc, w)
    return dx, dw


def _bdot(a, b, ca, cb, precision=None):
    nb = a.ndim - 2
    batch = tuple(range(nb))
    return lax.dot_general(a, b, (((nb + ca,), (nb + cb,)), (batch, batch)), precision=precision,
                           preferred_element_type=F32)


@jax.custom_vjp
def _nn(a, b):
    return _bdot(a.astype(BF16), b.astype(BF16), 1, 0)


@jax.custom_vjp
def _nt(a, b):
    return _bdot(a.astype(BF16), b.astype(BF16), 1, 1)


@jax.custom_vjp
def _tn(a, b):
    return _bdot(a.astype(BF16), b.astype(BF16), 0, 0)


_nn.defvjp(lambda a, b: (_nn(a, b), (a, b)), lambda r, g: (_nt(g, r[1]), _tn(r[0], g)))
_nt.defvjp(lambda a, b: (_nt(a, b), (a, b)), lambda r, g: (_nn(g, r[1]), _tn(g, r[0])))
_tn.defvjp(lambda a, b: (_tn(a, b), (a, b)), lambda r, g: (_nt(r[1], g), _nn(r[0], g)))


def _dot3(a, b, ca, cb):
    ah, bh = a.astype(BF16), b.astype(BF16)
    al, bl = (a - ah.astype(F32)).astype(BF16), (b - bh.astype(F32)).astype(BF16)
    return _bdot(ah, bh, ca, cb) + (_bdot(ah, bl, ca, cb) + _bdot(al, bh, ca, cb))


@jax.custom_vjp
def _nn_x3(a, b):
    return _dot3(a, b, 1, 0)


@jax.custom_vjp
def _nt_x3(a, b):
    return _dot3(a, b, 1, 1)


@jax.custom_vjp
def _tn_x3(a, b):
    return _dot3(a, b, 0, 0)


_nn_x3.defvjp(lambda a, b: (_nn_x3(a, b), (a, b)), lambda r, g: (_nt_x3(g, r[1]), _tn_x3(r[0], g)))
_nt_x3.defvjp(lambda a, b: (_nt_x3(a, b), (a, b)), lambda r, g: (_nn_x3(g, r[1]), _tn_x3(g, r[0])))
_tn_x3.defvjp(lambda a, b: (_tn_x3(a, b), (a, b)), lambda r, g: (_nt_x3(r[1], g), _nn_x3(r[0], g)))


@jax.custom_vjp
def _nn_hi(a, b):
    return _bdot(a, b, 1, 0, HI)


@jax.custom_vjp
def _nt_hi(a, b):
    return _bdot(a, b, 1, 1, HI)


@jax.custom_vjp
def _tn_hi(a, b):
    return _bdot(a, b, 0, 0, HI)


_nn_hi.defvjp(lambda a, b: (_nn_hi(a, b), (a, b)), lambda r, g: (_nt_hi(g, r[1]), _tn_hi(r[0], g)))
_nt_hi.defvjp(lambda a, b: (_nt_hi(a, b), (a, b)), lambda r, g: (_nn_hi(g, r[1]), _tn_hi(g, r[0])))
_tn_hi.defvjp(lambda a, b: (_tn_hi(a, b), (a, b)), lambda r, g: (_nt_hi(r[1], g), _nn_hi(r[0], g)))


def _gdn_chunk_fn(q, k, v, g, beta, state):
    C = CHUNK
    B = q.shape[0]
    row = lax.broadcasted_iota(jnp.int32, (B, C, C), 1)
    col = lax.broadcasted_iota(jnp.int32, (B, C, C), 2)
    tril, strict = row >= col, row > col
    ones_tril = tril.astype(F32)
    gc = _nn_hi(ones_tril, g)
    gr = _nt_hi(jnp.full((B, C, C), 1.0 / C, F32), gc)
    decay = jnp.where(tril, jnp.exp(jnp.where(tril, gc - gr, 0.0)), 0.0)
    b1 = beta[:, :, :1]
    e_gc = jnp.exp(gc[:, :, :1])
    kb = k * b1
    lmat = jnp.where(strict, _nt(kb, k) * decay, 0.0)
    a = -lmat
    t = jnp.where(row == col, 1.0, 0.0) + a
    p = a
    for _ in range(5):
        p = _nn_x3(p, p)
        t = t + _nn_x3(t, p)
    rhs = jnp.concatenate([v * b1, kb * e_gc], axis=2)
    sol = _nn_x3(t, rhs)
    u, w = sol[:, :, :HEAD_DIM], sol[:, :, HEAD_DIM:]
    a_qk = jnp.where(tril, _nt(q, k) * decay, 0.0)
    gl = gc[:, C - 1:C, :1]
    q_dec = q * e_gc
    k_dec = k * jnp.exp(gl - gc[:, :, :1])
    v_new = u - _nn(w, state)
    o = _nn(q_dec, state) + _nn(a_qk, v_new)
    new_state = state * jnp.exp(gl) + _tn(k_dec, v_new)
    return o, new_state


def _split_heads(x, B):
    return jnp.stack([x[:, j * HEAD_DIM:(j + 1) * HEAD_DIM] for j in range(B)], axis=0)


def _merge_heads(x):
    return jnp.concatenate([x[j] for j in range(x.shape[0])], axis=1)


def _gdn_group(H, voff):
    return next(b for b in (12, 6, 4, 3, 2, 1) if H % b == 0 and voff % b == 0)


def _gdn_fwd(q, k, v, g, beta, *, H, voff, name, side=None):
    S = q.shape[0]
    N = S // CHUNK
    B = _gdn_group(H, voff)
    W = B * HEAD_DIM
    qs = lambda off: pl.BlockSpec((CHUNK, W), lambda h, n: (n, off // B + h))
    gs = pl.BlockSpec((B, CHUNK, CHUNK), lambda h, n: (h, n, 0))

    def body(q_ref, k_ref, v_ref, g_ref, b_ref, o_ref, st_ref, state):
        @pl.when(pl.program_id(1) == 0)
        def _():
            state[...] = jnp.zeros_like(state)

        s0 = state[...]
        st_ref[...] = s0
        o, s1 = _gdn_chunk_fn(_split_heads(q_ref[...], B), _split_heads(k_ref[...], B), _split_heads(v_ref[...], B),
                              g_ref[...], b_ref[...], s0)
        o_ref[...] = _merge_heads(o)
        state[...] = s1

    body, s_in, s_out, s_shape, s_sems, s_args = _with_side(body, 5, 2, (H // B, N), side)
    o, states, *side_outs = pl.pallas_call(
        body, name=name, grid=(H // B, N),
        in_specs=[qs(0), qs(0), qs(voff), gs, gs] + s_in,
        out_specs=[qs(0), pl.BlockSpec((B, None, HEAD_DIM, HEAD_DIM), lambda h, n: (h, n, 0, 0))] + s_out,
        out_shape=[_sds((S, H * HEAD_DIM)), _sds((H, N, HEAD_DIM, HEAD_DIM))] + s_shape,
        scratch_shapes=[pltpu.VMEM((B, HEAD_DIM, HEAD_DIM), F32)] + s_sems,
        compiler_params=_cparams(("arbitrary", "arbitrary")),
    )(q, k, v, g, beta, *s_args)
    return o, states, side_outs


def _gdn_bwd(q, k, v, g, beta, states, do, *, H, voff, name, side=None):
    S = q.shape[0]
    N = S // CHUNK
    B = _gdn_group(H, voff)
    W = B * HEAD_DIM
    rs = lambda off: pl.BlockSpec((CHUNK, W), lambda h, n: (N - 1 - n, off // B + h))
    gs = pl.BlockSpec((B, CHUNK, CHUNK), lambda h, n: (h, N - 1 - n, 0))

    def body(q_ref, k_ref, v_ref, g_ref, b_ref, st_ref, do_ref, dq_ref, dk_ref, dv_ref, dg_ref, db_ref, dstate):
        @pl.when(pl.program_id(1) == 0)
        def _():
            dstate[...] = jnp.zeros_like(dstate)

        _, pull = jax.vjp(_gdn_chunk_fn, _split_heads(q_ref[...], B), _split_heads(k_ref[...], B),
                          _split_heads(v_ref[...], B), g_ref[...], b_ref[...], st_ref[...])
        dq, dk, dv, dg, db, ds = pull((_split_heads(do_ref[...], B), dstate[...]))
        dq_ref[...] = _merge_heads(dq)
        dk_ref[...] = _merge_heads(dk)
        dv_ref[...] = _merge_heads(dv)
        dg_ref[...] = dg
        db_ref[...] = db
        dstate[...] = ds

    body, s_in, s_out, s_shape, s_sems, s_args = _with_side(body, 7, 5, (H // B, N), side)
    dq, dk, dv, dg, db, *side_outs = pl.pallas_call(
        body, name=name, grid=(H // B, N),
        in_specs=[rs(0), rs(0), rs(voff), gs, gs,
                  pl.BlockSpec((B, None, HEAD_DIM, HEAD_DIM), lambda h, n: (h, N - 1 - n, 0, 0)), rs(0)] + s_in,
        out_specs=[rs(0), rs(0), rs(0), gs, gs] + s_out,
        out_shape=[_sds((S, H * HEAD_DIM))] * 3 + [_sds((H, S, CHUNK))] * 2 + s_shape,
        scratch_shapes=[pltpu.VMEM((B, HEAD_DIM, HEAD_DIM), F32)] + s_sems,
        compiler_params=_cparams(("arbitrary", "arbitrary")),
    )(q, k, v, g, beta, states, do, *s_args)
    return dq, dk, dv, dg, db, side_outs


def _round_up(n, m):
    return (n + m - 1) // m * m


def _dims(S, D, M, shapes):
    c = dict(S=S, D=D, M=M)
    c["H"] = shapes["gdn_a_log"][-1]
    c["QL"] = shapes["mla_q_norm"][-1]
    c["KVL"] = shapes["mla_kv_norm"][-1]
    assert c["QL"] == c["KVL"]
    c["MEMW"] = shapes["mem_w_kv"][-1] // 2
    c["HM"] = c["MEMW"] // HEAD_DIM
    c["MW"] = c["H"] * HEAD_DIM
    c["F"] = shapes["mlp_w1"][-1]
    c["DEPTH"] = shapes["ln1_g"][0]
    c["ALPHA"] = (2 * c["DEPTH"]) ** 0.25
    c["MLA_IN"] = _round_up(c["QL"] + c["KVL"] + c["MEMW"] + LANES, 2 * LANES)
    c["GDN_IN"] = _round_up(4 * c["MW"] + c["MEMW"] + LANES, 2 * LANES)
    c["t_row"] = min(256, S)
    c["t_head"] = min(512, S)
    c["t_att"] = min(512, S)
    return c


def _pad_cols(w, n):
    return jnp.pad(w, ((0, 0), (0, n - w.shape[1])))


def _prep_mla_w_in(w, c):
    a = c["QL"] + c["KVL"]
    w = jnp.concatenate([w[:, :a], w[:, a + QK_ROPE:a + QK_ROPE + c["MEMW"]], w[:, a:a + QK_ROPE]], axis=1)
    return _pad_cols(w, c["MLA_IN"]).astype(BF16)


def _unprep_mla_w_in(dw, c):
    a, m = c["QL"] + c["KVL"], c["MEMW"]
    return jnp.concatenate([dw[:, :a], dw[:, a + m:a + m + QK_ROPE], dw[:, a:a + m]], axis=1)


def _prep_w_uq(w, c):
    w = w.reshape(c["QL"], c["H"], QK_NOPE + QK_ROPE)
    w = jnp.pad(w, ((0, 0), (0, 0), (0, QK_PAD - QK_NOPE - QK_ROPE)))
    return w.reshape(c["QL"], c["H"] * QK_PAD).astype(BF16)


def _unprep_w_uq(dw, c):
    return dw.reshape(c["QL"], c["H"], QK_PAD)[:, :, :QK_NOPE + QK_ROPE].reshape(c["QL"], c["H"] * (QK_NOPE + QK_ROPE))


def _prep_w_ukv(w, c):
    return w.reshape(c["KVL"], c["H"], 2, HEAD_DIM).transpose(0, 2, 1, 3).reshape(c["KVL"], 2 * c["MW"]).astype(BF16)


def _unprep_w_ukv(dw, c):
    return dw.reshape(c["KVL"], 2, c["H"], HEAD_DIM).transpose(0, 2, 1, 3).reshape(c["KVL"], 2 * c["MW"])


def _prep_gdn_w_in(w, c):
    a, h2 = 4 * c["MW"], 2 * c["H"]
    w = jnp.concatenate([w[:, :a], w[:, a + h2:], w[:, a:a + h2]], axis=1)
    return _pad_cols(w, c["GDN_IN"]).astype(BF16)


def _unprep_gdn_w_in(dw, c):
    a, h2, m = 4 * c["MW"], 2 * c["H"], c["MEMW"]
    return jnp.concatenate([dw[:, :a], dw[:, a + m:a + m + h2], dw[:, a:a + m]], axis=1)


def _lane_bcast(v):
    return jnp.broadcast_to(v.astype(F32)[:, None, None], (v.shape[0], 1, LANES))


def _row(i):
    return (i, 0)


def _par(i):
    return (0, 0)


def _layer_norm(z, g, b, c, name):
    S, D, ts = c["S"], c["D"], c["t_row"]
    return _ew(_ln_fn, [(z, (ts, D), _row), (g, (1, D), _par), (b, (1, D), _par)],
               [(_sds((S, D)), (ts, D), _row), (_sds((S, D), BF16), (ts, D), _row)], (S // ts,), name=name)


def _layer_norm_bwd(z, g, b, dy, c, name):
    S, D, ts = c["S"], c["D"], c["t_row"]
    fn = lambda z, g, b: _ln_fn(z, g, b)[:1]

    def both(z, g, b):
        return fn(z, g, b)

    dz, dg, db = _ew_vjp(both, [(z, (ts, D), _row), (g, (1, D), _par), (b, (1, D), _par)], [(dy, (ts, D), _row)],
                         [(_sds((S, D)), (ts, D), _row, "set"), (_sds((1, D)), (1, D), _par, "acc_all"),
                          (_sds((1, D)), (1, D), _par, "acc_all")], (S // ts,), name=name)
    return dz, dg, db


def _mem_attn_fwd(h, qoff, memkv, c, name):
    return _flash_fwd(h, memkv, memkv, H=c["HM"], dq=HEAD_DIM, dv=HEAD_DIM, qoff=qoff, koff=0, voff=c["HM"],
                      causal=False, scale=HEAD_DIM ** -0.5, tq=c["t_att"], tk=c["M"], name=name)[:2]


def _mem_attn_bwd(h, qoff, memkv, om, lsem, dcat, c, name):
    return _flash_bwd(h, memkv, memkv, om, lsem, dcat, H=c["HM"], dq=HEAD_DIM, dv=HEAD_DIM, qoff=qoff, koff=0,
                      voff=c["HM"], dooff=c["H"], causal=False, scale=HEAD_DIM ** -0.5, tq=c["t_att"], tk=c["M"],
                      name=name)[:3]


def _mla_specs(c):
    H, ts = c["H"], c["t_head"]
    kr_blk = (c["QL"] + c["KVL"] + c["MEMW"]) // LANES
    hd = lambda i, h: (i, h)
    return [((ts, QK_PAD), hd), ((ts, HEAD_DIM), hd), ((ts, LANES), lambda i, h: (i, kr_blk)),
            ((ts, LANES), lambda i, h: (i, 0)), ((ts, LANES), lambda i, h: (i, 0))]


def _mla_fwd(xb, p, cosp, sinp, c, side=None):
    S, H, QL, ts, tr = c["S"], c["H"], c["QL"], c["t_head"], c["t_row"]
    h, = _mm(xb, p["w_in"], "nn", [F32], name="mla_in")
    nq, = _ew(_rms_fn, [(h, (tr, QL), lambda i: (i, 0)), (p["q_norm"], (1, QL), _par)],
              [(_sds((S, QL), BF16), (tr, QL), _row)], (S // tr,), name="mla_qnorm")
    nkv, = _ew(_rms_fn, [(h, (tr, QL), lambda i: (i, 1)), (p["kv_norm"], (1, QL), _par)],
               [(_sds((S, QL), BF16), (tr, QL), _row)], (S // tr,), name="mla_kvnorm")
    qraw, = _mm(nq, p["w_uq"], "nn", [F32], name="mla_uq")
    kvraw, = _mm(nkv, p["w_ukv"], "nn", [BF16], name="mla_ukv")
    sp = _mla_specs(c)
    ins = [(a, b, m) for a, (b, m) in zip([qraw, kvraw, h, cosp, sinp], sp)]
    qp, kp = _ew(_mla_prep_fn, ins, [(_sds((S, H * QK_PAD), BF16), (ts, QK_PAD), lambda i, h: (i, h))] * 2,
                 (S // ts, H), name="mla_rope")
    o, lse, got = _flash_fwd(qp, kp, kvraw, H=H, dq=QK_PAD, dv=HEAD_DIM, qoff=0, koff=0, voff=H, causal=True,
                             scale=(QK_NOPE + QK_ROPE) ** -0.5, tq=c["t_att"], tk=c["t_att"], name="mla_attn", side=side)
    return o, dict(h=h, nq=nq, nkv=nkv, qraw=qraw, kvraw=kvraw, qp=qp, kp=kp, o=o, lse=lse), got


def _mla_bwd(sv, p, cosp, sinp, dcat, dqm, c, side=None):
    S, H, QL, ts, tr = c["S"], c["H"], c["QL"], c["t_head"], c["t_row"]
    dqp, dkp, dv, got = _flash_bwd(sv["qp"], sv["kp"], sv["kvraw"], sv["o"], sv["lse"], dcat, H=H, dq=QK_PAD,
                                   dv=HEAD_DIM, qoff=0, koff=0, voff=H, dooff=0, causal=True,
                                   scale=(QK_NOPE + QK_ROPE) ** -0.5, tq=c["t_att"], tk=c["t_att"],
                                   name="mla_attn_bwd", side=side)
    sp = _mla_specs(c)
    ins = [(a, b, m) for a, (b, m) in zip([sv["qraw"], sv["kvraw"], sv["h"], cosp, sinp], sp)]
    hd = lambda i, h: (i, h)
    dqraw, dknope, dkr = _ew_vjp(
        _mla_prep_fn, ins, [(dqp, (ts, QK_PAD), hd), (dkp, (ts, QK_PAD), hd)],
        [(_sds((S, H * QK_PAD), BF16), (ts, QK_PAD), hd, "set"), (_sds((S, H * HEAD_DIM), BF16), (ts, HEAD_DIM), hd, "set"),
         (_sds((S, LANES)), (ts, LANES), lambda i, h: (i, 0), "acc"), None, None], (S // ts, H), name="mla_rope_bwd")
    dkvraw = jnp.concatenate([dknope, dv.astype(BF16)], axis=1)
    dnq, = _mm(dqraw, p["w_uq"], "nt", [F32], name="mla_uq_dx")
    dw_uq, = _mm(sv["nq"], dqraw, "tn", [F32], name="mla_uq_dw")
    dnkv, = _mm(dkvraw, p["w_ukv"], "nt", [F32], name="mla_ukv_dx")
    dw_ukv, = _mm(sv["nkv"], dkvraw, "tn", [F32], name="mla_ukv_dw")
    dcq, dgq = _ew_vjp(_rms_fn, [(sv["h"], (tr, QL), lambda i: (i, 0)), (p["q_norm"], (1, QL), _par)],
                       [(dnq, (tr, QL), _row)],
                       [(_sds((S, QL), BF16), (tr, QL), _row, "set"), (_sds((1, QL)), (1, QL), _par, "acc_all")],
                       (S // tr,), name="mla_qnorm_bwd")
    dckv, dgkv = _ew_vjp(_rms_fn, [(sv["h"], (tr, QL), lambda i: (i, 1)), (p["kv_norm"], (1, QL), _par)],
                         [(dnkv, (tr, QL), _row)],
                         [(_sds((S, QL), BF16), (tr, QL), _row, "set"), (_sds((1, QL)), (1, QL), _par, "acc_all")],
                         (S // tr,), name="mla_kvnorm_bwd")
    pad = c["MLA_IN"] - (2 * QL + c["MEMW"] + LANES)
    dh = jnp.concatenate([dcq, dckv, dqm.astype(BF16), dkr.astype(BF16)] + ([jnp.zeros((S, pad), BF16)] if pad else []),
                         axis=1)
    grads = dict(mla_q_norm=dgq[0], mla_kv_norm=dgkv[0], mla_w_uq=_unprep_w_uq(dw_uq, c),
                 mla_w_ukv=_unprep_w_ukv(dw_ukv, c))
    return dh, grads, got


def _gdn_ins(h, qkvc, p, c):
    H, ts = c["H"], c["t_head"]
    ab_blk = (4 * c["MW"] + c["MEMW"]) // LANES
    qk_ins = [(qkvc, (ts, HEAD_DIM), lambda i, h: (i, h)), (qkvc, (ts, HEAD_DIM), lambda i, h: (i, H + h))]
    gate_ins = [(h, (ts, LANES), lambda i, h: (i, ab_blk)), (p["a_log"], (None, 1, LANES), lambda i, h: (h, 0, 0)),
                (p["dt_bias"], (None, 1, LANES), lambda i, h: (h, 0, 0))]
    return qk_ins, gate_ins


def _gdn_out_ins(o, h, p, c):
    H, ts = c["H"], c["t_head"]
    return [(o, (ts, HEAD_DIM), lambda i, h: (i, h)), (h, (ts, HEAD_DIM), lambda i, h: (i, 3 * H + h)),
            (p["o_norm"], (1, HEAD_DIM), lambda i, h: (0, 0))]


def _gdn_layer_fwd(xb, p, c, side=None):
    S, H, MW, ts = c["S"], c["H"], c["MW"], c["t_head"]
    h, = _mm(xb, p["w_in"], "nn", [F32], name="gdn_in")
    tc = _tile(3 * MW, 512, LANES)
    qkvc = _conv_fwd(h, p["conv"], C=3 * MW, ts=ts, tc=tc, name="gdn_conv")
    qk_ins, gate_ins = _gdn_ins(h, qkvc, p, c)
    hd = lambda i, h: (i, h)
    qn, kn = _ew(_gdn_qk_fn, qk_ins, [(_sds((S, MW)), (ts, HEAD_DIM), hd)] * 2, (S // ts, H), name="gdn_qknorm")
    g3 = lambda i, h: (h, i, 0)
    g, beta = _ew(_gdn_gate_fn(H, 1), gate_ins, [(_sds((H, S, CHUNK)), (None, ts, CHUNK), g3)] * 2, (S // ts, H),
                  name="gdn_gate")
    o, states, got = _gdn_fwd(qn, kn, qkvc, g, beta, H=H, voff=2 * H, name="gdn_delta", side=side)
    mix, = _ew(_gdn_out_fn, _gdn_out_ins(o, h, p, c), [(_sds((S, MW)), (ts, HEAD_DIM), hd)], (S // ts, H),
               name="gdn_outnorm")
    return mix, dict(h=h, qkvc=qkvc, qn=qn, kn=kn, g=g, beta=beta, o=o, states=states), got


def _gdn_layer_bwd(sv, p, dcat, dqm, c, side=None):
    S, H, MW, ts = c["S"], c["H"], c["MW"], c["t_head"]
    hd = lambda i, h: (i, h)
    g3 = lambda i, h: (h, i, 0)
    h, qkvc = sv["h"], sv["qkvc"]
    do, dz, d_onorm = _ew_vjp(_gdn_out_fn, _gdn_out_ins(sv["o"], h, p, c), [(dcat, (ts, HEAD_DIM), hd)],
                              [(_sds((S, MW)), (ts, HEAD_DIM), hd, "set"), (_sds((S, MW), BF16), (ts, HEAD_DIM), hd, "set"),
                               (_sds((1, HEAD_DIM)), (1, HEAD_DIM), lambda i, h: (0, 0), "acc_all")],
                              (S // ts, H), name="gdn_outnorm_bwd")
    dqn, dkn, dv, dg, db, got = _gdn_bwd(sv["qn"], sv["kn"], qkvc, sv["g"], sv["beta"], sv["states"], do, H=H,
                                         voff=2 * H, name="gdn_delta_bwd", side=side)
    qk_ins, gate_ins = _gdn_ins(h, qkvc, p, c)
    dqc, dkc = _ew_vjp(_gdn_qk_fn, qk_ins, [(dqn, (ts, HEAD_DIM), hd), (dkn, (ts, HEAD_DIM), hd)],
                       [(_sds((S, MW)), (ts, HEAD_DIM), hd, "set")] * 2, (S // ts, H), name="gdn_qknorm_bwd")
    full3 = lambda i, h: (0, 0, 0)
    dab, dalog, ddt = _ew_vjp(
        _gdn_gate_fn(H, 1), gate_ins, [(dg, (None, ts, CHUNK), g3), (db, (None, ts, CHUNK), g3)],
        [(_sds((S, LANES), BF16), (ts, LANES), lambda i, h: (i, 0), "acc"),
         (_sds((H, 1, LANES)), (H, 1, LANES), full3, ("acc_at", 1)),
         (_sds((H, 1, LANES)), (H, 1, LANES), full3, ("acc_at", 1))], (S // ts, H), name="gdn_gate_bwd")
    dqkvc = jnp.concatenate([dqc, dkc, dv], axis=1)
    tc = _tile(3 * MW, 512, LANES)
    dxc, dconv = _conv_bwd(h, p["conv"], dqkvc, C=3 * MW, ts=ts, tc=tc, name="gdn_conv_bwd")
    pad = c["GDN_IN"] - (4 * MW + c["MEMW"] + LANES)
    dh = jnp.concatenate([dxc.astype(BF16), dz, dqm.astype(BF16), dab] + ([jnp.zeros((S, pad), BF16)] if pad else []),
                         axis=1)
    grads = dict(gdn_conv=dconv, gdn_a_log=jnp.sum(dalog[:, 0, :], axis=-1), gdn_dt_bias=jnp.sum(ddt[:, 0, :], axis=-1),
                 gdn_o_norm=d_onorm[0])
    return dh, grads, got


def _prep_layer(W, i, c):
    p = dict(mem_w_kv=W["mem_w_kv"].astype(BF16), w_out=W["w_out"].astype(BF16),
             w1=W["mlp_w1"].astype(BF16), w2=W["mlp_w2"].astype(BF16),
             ln1_g=W["ln1_g"][None].astype(F32), ln1_b=W["ln1_b"][None].astype(F32),
             ln2_g=W["ln2_g"][None].astype(F32), ln2_b=W["ln2_b"][None].astype(F32))
    if i % 2 == 0:
        p.update(w_in=_prep_mla_w_in(W["mla_w_in"], c), q_norm=W["mla_q_norm"][None].astype(F32),
                 w_uq=_prep_w_uq(W["mla_w_uq"], c), kv_norm=W["mla_kv_norm"][None].astype(F32),
                 w_ukv=_prep_w_ukv(W["mla_w_ukv"], c))
    else:
        p.update(w_in=_prep_gdn_w_in(W["gdn_w_in"], c), conv=W["gdn_conv"].astype(F32),
                 a_log=_lane_bcast(W["gdn_a_log"]), dt_bias=_lane_bcast(W["gdn_dt_bias"]),
                 o_norm=W["gdn_o_norm"][None].astype(F32))
    return p


def _local_step(x, mem, positions, W, loss_target, c, next_weights=None, grad_sink=None):
    S, D, H, MW, ALPHA = c["S"], c["D"], c["H"], c["MW"], c["ALPHA"]
    inv_freq = 1.0 / (ROPE_THETA ** (jnp.arange(0, QK_ROPE, 2, dtype=F32) / QK_ROPE))
    ang = positions.astype(F32)[:, None] * inv_freq
    cos, sin = jnp.cos(ang), jnp.sin(ang)
    cosp = jnp.concatenate([cos, cos, jnp.ones((S, LANES - QK_ROPE), F32)], axis=1)
    sinp = jnp.concatenate([sin, sin, jnp.zeros((S, LANES - QK_ROPE), F32)], axis=1)
    memb = mem.astype(BF16)
    xf, xb = x, x.astype(BF16)
    saved, params = [], []
    w_next = W[0]
    for i in range(c["DEPTH"]):
        p = _prep_layer(w_next if next_weights is not None else W[i], i, c)
        mla = i % 2 == 0
        memkv, = _mm(memb, p["mem_w_kv"], "nn", [BF16], name="mem_kv")
        side, arrived = next_weights(i + 1) if next_weights is not None and i + 1 < c["DEPTH"] else (None, None)
        if mla:
            mix, sv, got = _mla_fwd(xb, p, cosp, sinp, c, side)
            qoff = (c["QL"] + c["KVL"]) // LANES
        else:
            mix, sv, got = _gdn_layer_fwd(xb, p, c, side)
            qoff = 4 * MW // LANES
        if side is not None:
            w_next = arrived(got)
        om, lsem = _mem_attn_fwd(sv["h"], qoff, memkv, c, "mem_attn")
        cat = jnp.concatenate([mix, om], axis=1).astype(BF16)
        z1, = _mm(cat, p["w_out"], "nn", [F32], name="w_out", extras=(xf,), epilogue=lambda acc, r: (ALPHA * r + acc,))
        x1, x1b = _layer_norm(z1, p["ln1_g"], p["ln1_b"], c, "ln1")
        u, a = _mm(x1b, p["w1"], "nn", [F32, BF16], name="mlp_up", b_major=True,
                   epilogue=lambda acc: (acc, jnp.square(jnp.maximum(acc, 0.0))))
        z2, = _mm(a, p["w2"], "nn", [F32], name="mlp_down", extras=(x1,), epilogue=lambda acc, r: (ALPHA * r + acc,))
        x2, x2b = _layer_norm(z2, p["ln2_g"], p["ln2_b"], c, "ln2")
        sv.update(xb=xb, memkv=memkv, om=om, lsem=lsem, cat=cat, z1=z1, x1b=x1b, u=u, a=a, z2=z2, qoff=qoff)
        saved.append(sv)
        params.append(p)
        xf, xb = x2, x2b

    ts = c["t_row"]
    lsum, dy = _ew(_loss_fn, [(xf, (ts, D), _row), (loss_target, (ts, D), _row)],
                   [(_sds((1, D)), (1, D), _par), (_sds((S, D)), (ts, D), _row)], (S // ts,), name="loss", acc_out=(0,))
    loss = jnp.sum(lsum)

    grads = [None] * c["DEPTH"]
    pending = None
    dx = dy
    for i in reversed(range(c["DEPTH"])):
        p, sv = params[i], saved[i]
        mla = i % 2 == 0
        side = pending[0] if pending is not None else None
        G = {}
        dz2, dg, db = _layer_norm_bwd(sv["z2"], p["ln2_g"], p["ln2_b"], dx, c, "ln2_bwd")
        G["ln2_g"], G["ln2_b"] = dg[0], db[0]
        dz2b = dz2.astype(BF16)
        du, = _mm(dz2b, p["w2"], "nt", [BF16], name="mlp_down_dx", extras=(sv["u"],),
                  epilogue=lambda acc, u: (acc * (2.0 * jnp.maximum(u, 0.0)),))
        G["mlp_w2"], = _mm(sv["a"], dz2b, "tn", [F32], name="mlp_down_dw")
        G["mlp_w1"], = _mm(sv["x1b"], du, "tn", [F32], name="mlp_up_dw", out_major=True)
        dx1, = _mm(du, p["w1"], "nt", [F32], name="mlp_up_dx", extras=(dz2,), b_major=True,
                   epilogue=lambda acc, r: (ALPHA * r + acc,))
        dz1, dg, db = _layer_norm_bwd(sv["z1"], p["ln1_g"], p["ln1_b"], dx1, c, "ln1_bwd")
        G["ln1_g"], G["ln1_b"] = dg[0], db[0]
        dz1b = dz1.astype(BF16)
        dcat, = _mm(dz1b, p["w_out"], "nt", [BF16], name="w_out_dx")
        G["w_out"], = _mm(sv["cat"], dz1b, "tn", [F32], name="w_out_dw")
        dqm, dkm, dvm = _mem_attn_bwd(sv["h"], sv["qoff"], sv["memkv"], sv["om"], sv["lsem"], dcat, c, "mem_attn_bwd")
        dmemkv = jnp.concatenate([dkm, dvm], axis=1).astype(BF16)
        G["mem_w_kv"], = _mm(memb, dmemkv, "tn", [F32], name="mem_kv_dw")
        if mla:
            dh, g, got = _mla_bwd(sv, p, cosp, sinp, dcat, dqm, c, side)
            G.update(g)
            dw_in, = _mm(sv["xb"], dh, "tn", [F32], name="mla_in_dw")
            G["mla_w_in"] = _unprep_mla_w_in(dw_in, c)
            dx, = _mm(dh, p["w_in"], "nt", [F32], name="mla_in_dx", extras=(dz1,),
                      epilogue=lambda acc, r: (ALPHA * r + acc,))
        else:
            dh, g, got = _gdn_layer_bwd(sv, p, dcat, dqm, c, side)
            G.update(g)
            dw_in, = _mm(sv["xb"], dh, "tn", [F32], name="gdn_in_dw")
            G["gdn_w_in"] = _unprep_gdn_w_in(dw_in, c)
            dx, = _mm(dh, p["w_in"], "nt", [F32], name="gdn_in_dx", extras=(dz1,),
                      epilogue=lambda acc, r: (ALPHA * r + acc,))
        if pending is not None:
            grads[pending[2]] = pending[1](got)
            pending = None
        if grad_sink is None:
            grads[i] = G
        else:
            pending = grad_sink(i, G) + (i,)
    if pending is not None:
        grads[pending[2]] = pending[1](_run_plan(pending[0], "grad_scatter_last"))
    return loss, dx, grads


_HBM = pl.BlockSpec(memory_space=pltpu.HBM)
_VMEM = pl.BlockSpec(memory_space=pltpu.VMEM)


def _my_place():
    return lax.axis_index("x"), lax.axis_index("y"), lax.axis_index("c")


def _my_chip():
    return 2 * lax.axis_index("x") + lax.axis_index("y")


def _other_chips(x, y):
    return [(1 - x, y), (x, 1 - y), (1 - x, 1 - y)]


def _gather_plan(arrs, by_rows):
    n = len(arrs)

    def run(ins, outs, sems, start, wait):
        send_sems, recv_sems, local_sems = sems
        x, y, c = _my_place()
        chips = _other_chips(x, y)

        def slab(i, chip):
            r = arrs[i].shape[0]
            return outs[i].at[pl.ds(chip * r, r)] if by_rows[i] else outs[i].at[chip]

        def copy(i, k, chip, to):
            return pltpu.make_async_remote_copy(src_ref=ins[i], dst_ref=slab(i, chip), send_sem=send_sems.at[3 * i + k],
                                                recv_sem=recv_sems.at[3 * i + k], device_id=to, device_id_type=MESH)

        mine = [pltpu.make_async_copy(ins[i], slab(i, 2 * x + y), local_sems.at[i]) for i in range(n)]
        sends = [copy(i, k, 2 * x + y, (cx, cy, c)) for i in range(n) for k, (cx, cy) in enumerate(chips)]
        if start:
            for cp in mine + sends:
                cp.start()
        if wait:
            for i in range(n):
                for k, (cx, cy) in enumerate(chips):
                    copy(i, k, 2 * cx + cy, (cx, cy, c)).wait_recv()
            for cp in sends:
                cp.wait_send()
            for cp in mine:
                cp.wait()

    shapes = [jax.ShapeDtypeStruct((N_CHIPS * a.shape[0], a.shape[1]) if r else (N_CHIPS,) + a.shape, a.dtype)
              for a, r in zip(arrs, by_rows)]
    sems = [pltpu.SemaphoreType.DMA((3 * n,)), pltpu.SemaphoreType.DMA((3 * n,)), pltpu.SemaphoreType.DMA((n,))]
    return dict(arrs=list(arrs), out_shape=shapes, sems=sems, run=run)


def _scatter_plan(ps):
    n = len(ps)

    def run(ins, outs, sems, start, wait):
        send_sems, recv_sems = sems
        x, y, c = _my_place()
        cps = [pltpu.make_async_remote_copy(src_ref=ins[i].at[2 * cx + cy], dst_ref=outs[i].at[k],
                                            send_sem=send_sems.at[3 * i + k], recv_sem=recv_sems.at[3 * i + k],
                                            device_id=(cx, cy, c), device_id_type=MESH)
               for i in range(n) for k, (cx, cy) in enumerate(_other_chips(x, y))]
        if start:
            for cp in cps:
                cp.start()
        if wait:
            for cp in cps:
                cp.wait()

    shapes = [jax.ShapeDtypeStruct((3,) + p.shape[1:], p.dtype) for p in ps]
    sems = [pltpu.SemaphoreType.DMA((3 * n,)), pltpu.SemaphoreType.DMA((3 * n,))]
    return dict(arrs=list(ps), out_shape=shapes, sems=sems, run=run)


def _run_plan(plan, name):
    n_in, n_out = len(plan["arrs"]), len(plan["out_shape"])

    def body(*refs):
        plan["run"](refs[:n_in], refs[n_in:n_in + n_out], refs[n_in + n_out:], True, True)

    return pl.pallas_call(body, name=name, in_specs=[_HBM] * n_in, out_specs=[_HBM] * n_out,
                          out_shape=plan["out_shape"], scratch_shapes=plan["sems"])(*plan["arrs"])


def _with_side(body, n_in, n_out, grid, side):
    if side is None:
        return body, [], [], [], [], []
    s_in, s_out, s_sem = len(side["arrs"]), len(side["out_shape"]), len(side["sems"])

    def wrapped(*refs):
        ins, s_ins = refs[:n_in], refs[n_in:n_in + s_in]
        o0 = n_in + s_in
        outs, s_outs = refs[o0:o0 + n_out], refs[o0 + n_out:o0 + n_out + s_out]
        rest = refs[o0 + n_out + s_out:]
        scratch, s_sems = rest[:len(rest) - s_sem], rest[len(rest) - s_sem:]
        ids = [pl.program_id(d) for d in range(len(grid))]
        first = functools.reduce(jnp.logical_and, [i == 0 for i in ids])
        last = functools.reduce(jnp.logical_and, [i == g - 1 for i, g in zip(ids, grid)])
        pl.when(first)(lambda: side["run"](s_ins, s_outs, s_sems, True, False))
        body(*ins, *outs, *scratch)
        pl.when(last)(lambda: side["run"](s_ins, s_outs, s_sems, False, True))

    return wrapped, [_HBM] * s_in, [_HBM] * s_out, side["out_shape"], side["sems"], side["arrs"]


def _swap_halves(gs, name):
    n = len(gs)

    def body(*refs):
        ins, outs = refs[:n], refs[n:2 * n]
        send_sems, recv_sems = refs[2 * n:]
        x, y, c = _my_place()
        cps = [pltpu.make_async_remote_copy(src_ref=ins[i].at[:, 1 - c], dst_ref=outs[i], send_sem=send_sems.at[i],
                                            recv_sem=recv_sems.at[i], device_id=(x, y, 1 - c), device_id_type=MESH)
               for i in range(n)]
        for cp in cps:
            cp.start()
        for cp in cps:
            cp.wait()

    return pl.pallas_call(
        body, name=name, in_specs=[_HBM] * n, out_specs=[_HBM] * n,
        out_shape=[jax.ShapeDtypeStruct((g.shape[0],) + g.shape[2:], g.dtype) for g in gs],
        scratch_shapes=[pltpu.SemaphoreType.DMA((n,)), pltpu.SemaphoreType.DMA((n,))],
    )(*gs)


def _join_halves(fs, name):
    n = len(fs)

    def body(*refs):
        ins, outs = refs[:n], refs[n:2 * n]
        send_sems, recv_sems = refs[2 * n:]
        x, y, c = _my_place()

        def copy(i, half):
            return pltpu.make_async_remote_copy(src_ref=ins[i].at[half], dst_ref=outs[i].at[half],
                                                send_sem=send_sems.at[i], recv_sem=recv_sems.at[i],
                                                device_id=(x, y, 1 - c), device_id_type=MESH)

        sends = [copy(i, c) for i in range(n)]
        for cp in sends:
            cp.start()
        for i in range(n):
            copy(i, 1 - c).wait_recv()
        for cp in sends:
            cp.wait_send()

    return pl.pallas_call(
        body, name=name, in_specs=[_HBM] * n, out_specs=[_HBM] * n,
        out_shape=[jax.ShapeDtypeStruct(f.shape, f.dtype) for f in fs],
        input_output_aliases={i: i for i in range(n)},
        scratch_shapes=[pltpu.SemaphoreType.DMA((n,)), pltpu.SemaphoreType.DMA((n,))],
    )(*fs)


def _row_tile(a, b):
    return _tile(a, max(SUBLANES, (1 << 19) // b // SUBLANES * SUBLANES), SUBLANES)


def _add_core(g, got, name):
    _, _, A, B = g.shape
    ta = _row_tile(A, B)
    return _ew(lambda p, q: (p + q,),
               [(g, (None, None, ta, B), lambda s, i: (s, lax.axis_index("c"), i, 0)),
                (got, (None, ta, B), lambda s, i: (s, i, 0))],
               [(_sds((N_CHIPS, A, B)), (None, ta, B), lambda s, i: (s, i, 0))], (N_CHIPS, A // ta), name=name)[0]


def _add_chips(p, got, name):
    _, A, B = p.shape
    ta = _row_tile(A, B)
    blk = (None, ta, B)
    return _ew(lambda a, b, c_, d: (((a + b) + c_) + d,),
               [(p, blk, lambda i: (_my_chip(), i, 0)), (got, blk, lambda i: (0, i, 0)),
                (got, blk, lambda i: (1, i, 0)), (got, blk, lambda i: (2, i, 0))],
               [(_sds((2, A, B)), blk, lambda i: (lax.axis_index("c"), i, 0))], (A // ta,), name=name)[0]


def _all_reduce_small(v, name):
    r = v.shape[0]
    masks = [(mx, my, mc) for mx in (0, 1) for my in (0, 1) for mc in (0, 1)][1:]

    def body(v_ref, out_ref, gath, send_sems, recv_sems):
        x, y, c = _my_place()
        me = 4 * x + 2 * y + c
        gath[me] = v_ref[...]

        def peer(m):
            return (x + m[0] - 2 * x * m[0], y + m[1] - 2 * y * m[1], c + m[2] - 2 * c * m[2])

        def copy(k, slab, to):
            return pltpu.make_async_remote_copy(src_ref=v_ref, dst_ref=gath.at[slab], send_sem=send_sems.at[k],
                                                recv_sem=recv_sems.at[k], device_id=to, device_id_type=MESH)

        sends = [copy(k, me, peer(m)) for k, m in enumerate(masks)]
        for cp in sends:
            cp.start()
        for k, m in enumerate(masks):
            px, py, pc = peer(m)
            copy(k, 4 * px + 2 * py + pc, (px, py, pc)).wait_recv()
        for cp in sends:
            cp.wait_send()
        total = gath[0]
        for d in range(1, 8):
            total = total + gath[d]
        out_ref[...] = total

    return pl.pallas_call(
        body, name=name, in_specs=[_VMEM], out_specs=_VMEM, out_shape=jax.ShapeDtypeStruct((r, LANES), F32),
        scratch_shapes=[pltpu.VMEM((8, r, LANES), F32), pltpu.SemaphoreType.DMA((7,)), pltpu.SemaphoreType.DMA((7,))],
    )(v)


def _pack_rows(arrs, dtype, row_mult):
    flat = jnp.concatenate([a.astype(dtype).reshape(-1) for a in arrs])
    n = flat.shape[0]
    rows = _round_up(-(-n // LANES), row_mult)
    return jnp.pad(flat, (0, rows * LANES - n)).reshape(rows, LANES)


def _unpack_rows(buf, shapes):
    lead = buf.shape[:-2]
    flat = buf.reshape(lead + (-1,))
    out, o = [], 0
    for s in shapes:
        n = math.prod(s)
        out.append(lax.slice_in_dim(flat, o, o + n, axis=len(lead)).reshape(lead + tuple(s)))
        o += n
    return out


WEIGHTS = ["mla_w_in", "mla_q_norm", "mla_w_uq", "mla_kv_norm", "mla_w_ukv", "gdn_w_in", "gdn_conv", "gdn_a_log",
           "gdn_dt_bias", "gdn_o_norm", "mem_w_kv", "w_out", "ln1_g", "ln1_b", "mlp_w1", "mlp_w2", "ln2_g", "ln2_b"]
SHARD_AXIS = {"mla_w_in": 1, "mla_w_uq": 2, "mla_w_ukv": 2, "gdn_w_in": 2, "gdn_conv": 2, "mem_w_kv": 1, "w_out": 1,
              "mlp_w1": 2, "mlp_w2": 1}
SMALL = [k for k in WEIGHTS if k not in SHARD_AXIS] + ["gdn_conv"]
BIG = [k for k in WEIGHTS if k not in SMALL]
MLA_KEYS = ["mla_w_in", "mla_q_norm", "mla_w_uq", "mla_kv_norm", "mla_w_ukv"]
GDN_KEYS = ["gdn_w_in", "gdn_conv", "gdn_a_log", "gdn_dt_bias", "gdn_o_norm"]
ALL_KEYS = ["mem_w_kv", "w_out", "ln1_g", "ln1_b", "mlp_w1", "mlp_w2", "ln2_g", "ln2_b"]


def _layer_keys(i):
    return (MLA_KEYS if i % 2 == 0 else GDN_KEYS) + ALL_KEYS


def _layer_slot(k, i):
    return i // 2 if k in MLA_KEYS or k in GDN_KEYS else i


def _gather_layer(w, i):
    keys = [k for k in _layer_keys(i) if k in SHARD_AXIS]
    arrs = [w[k][_layer_slot(k, i)].astype(F32 if k == "gdn_conv" else BF16) for k in keys]

    def arrived(outs):
        full = {k: w[k][_layer_slot(k, i)] for k in _layer_keys(i) if k not in SHARD_AXIS}
        for k, o in zip(keys, outs):
            by_cols = SHARD_AXIS[k] == 2 and k != "mlp_w1"
            full[k] = jnp.concatenate([o[d] for d in range(N_CHIPS)], axis=1) if by_cols else o
        return full

    return _gather_plan(arrs, [SHARD_AXIS[k] == 1 for k in keys]), arrived


def _reduce_layer(i, G):
    kind = "mla" if i % 2 == 0 else "gdn"
    keys = [k for k in _layer_keys(i) if k in BIG]
    canon = []
    for k in keys:
        g = G[k]
        if k == "mlp_w1":
            g = g.reshape(N_CHIPS, 2, g.shape[1] // 2, g.shape[2])
        elif SHARD_AXIS[k] == 1:
            g = g.reshape(N_CHIPS, 2, g.shape[0] // (2 * N_CHIPS), g.shape[1])
        else:
            rows, cw = g.shape[0], g.shape[1] // N_CHIPS
            g = g.reshape(rows, N_CHIPS, cw).transpose(1, 0, 2).reshape(N_CHIPS, 2, rows // 2, cw)
        canon.append(g)
    theirs = _swap_halves(canon, "grad_swap_" + kind)
    chip_sums = [_add_core(g, t, "grad_add_core") for g, t in zip(canon, theirs)]

    def done(got):
        halves = [_add_chips(p, s, "grad_add_chips") for p, s in zip(chip_sums, got)]
        joined = _join_halves(halves, "grad_join_" + kind)
        out = {k: v for k, v in G.items() if k in SMALL}
        out.update({k: j.reshape(2 * j.shape[1], j.shape[2]) for k, j in zip(keys, joined)})
        return out

    return _scatter_plan(chip_sums), done


def _adamw(w, g, m, v, name):
    shape = w.shape
    cols = shape[-1]
    rows = math.prod(shape[:-1])
    tr = _tile(rows, max(SUBLANES, (1 << 19) // cols // SUBLANES * SUBLANES), SUBLANES)
    spec = ((tr, cols), _row)
    outs = _ew(_adamw_fn, [(a.reshape(rows, cols), *spec) for a in (w, g, m, v)], [(_sds((rows, cols)), *spec)] * 3,
               (rows // tr,), name=name)
    return [o.reshape(shape) for o in outs]


def kernel(x, mem, positions, mla_w_in, mla_q_norm, mla_w_uq, mla_kv_norm, mla_w_ukv, gdn_w_in, gdn_conv, gdn_a_log, gdn_dt_bias, gdn_o_norm, mem_w_kv, w_out, ln1_g, ln1_b, mlp_w1, mlp_w2, ln2_g, ln2_b, loss_target, m_mla_w_in, m_mla_q_norm, m_mla_w_uq, m_mla_kv_norm, m_mla_w_ukv, m_gdn_w_in, m_gdn_conv, m_gdn_a_log, m_gdn_dt_bias, m_gdn_o_norm, m_mem_w_kv, m_w_out, m_ln1_g, m_ln1_b, m_mlp_w1, m_mlp_w2, m_ln2_g, m_ln2_b, v_mla_w_in, v_mla_q_norm, v_mla_w_uq, v_mla_kv_norm, v_mla_w_ukv, v_gdn_w_in, v_gdn_conv, v_gdn_a_log, v_gdn_dt_bias, v_gdn_o_norm, v_mem_w_kv, v_w_out, v_ln1_g, v_ln1_b, v_mlp_w1, v_mlp_w2, v_ln2_g, v_ln2_b):
    w = dict(zip(WEIGHTS, (mla_w_in, mla_q_norm, mla_w_uq, mla_kv_norm, mla_w_ukv, gdn_w_in, gdn_conv, gdn_a_log,
                           gdn_dt_bias, gdn_o_norm, mem_w_kv, w_out, ln1_g, ln1_b, mlp_w1, mlp_w2, ln2_g, ln2_b)))
    m = dict(zip(WEIGHTS, (m_mla_w_in, m_mla_q_norm, m_mla_w_uq, m_mla_kv_norm, m_mla_w_ukv, m_gdn_w_in, m_gdn_conv,
                           m_gdn_a_log, m_gdn_dt_bias, m_gdn_o_norm, m_mem_w_kv, m_w_out, m_ln1_g, m_ln1_b, m_mlp_w1,
                           m_mlp_w2, m_ln2_g, m_ln2_b)))
    v = dict(zip(WEIGHTS, (v_mla_w_in, v_mla_q_norm, v_mla_w_uq, v_mla_kv_norm, v_mla_w_ukv, v_gdn_w_in, v_gdn_conv,
                           v_gdn_a_log, v_gdn_dt_bias, v_gdn_o_norm, v_mem_w_kv, v_w_out, v_ln1_g, v_ln1_b, v_mlp_w1,
                           v_mlp_w2, v_ln2_g, v_ln2_b)))
    assert x.shape[0] == 1, "one sequence per device"
    full_shapes = {k: w[k].shape for k in WEIGHTS}
    for k, ax in SHARD_AXIS.items():
        s = list(w[k].shape)
        s[ax] *= N_CHIPS
        full_shapes[k] = tuple(s)
    c = _dims(x.shape[1], x.shape[2], mem.shape[1], full_shapes)
    depth = c["DEPTH"]

    first, arrived = _gather_layer(w, 0)
    W = [arrived(_run_plan(first, "gather_first"))]
    loss_local, grad_x, G = _local_step(x[0], mem[0], positions[0], W, loss_target[0], c,
                                        next_weights=lambda i: _gather_layer(w, i), grad_sink=_reduce_layer)
    loss = lax.psum(loss_local, ("x", "y", "c"))

    def stacked(k):
        return jnp.stack([G[i][k] for i in range(depth) if k in G[i]], axis=0)

    grads = {k: stacked(k) for k in BIG}
    small_shapes = [full_shapes[k] for k in SMALL]
    gsmall = _all_reduce_small(_pack_rows([stacked(k) for k in SMALL], F32, SUBLANES), "grad_all_reduce_small")
    grads.update(dict(zip(SMALL, _unpack_rows(gsmall, small_shapes))))
    conv_cols = w["gdn_conv"].shape[2]
    grads["gdn_conv"] = lax.dynamic_slice_in_dim(grads["gdn_conv"], _my_chip() * conv_cols, conv_cols, axis=2)

    delta, new_m, new_v = {}, {}, {}
    for k in BIG + ["gdn_conv"]:
        delta[k], new_m[k], new_v[k] = _adamw(w[k], grads[k], m[k], v[k], "adamw")
    small = [k for k in SMALL if k != "gdn_conv"]
    packed = [_pack_rows([d[k] for k in small], F32, SUBLANES) for d in (w, grads, m, v)]
    ds, ms, vs = _adamw(*packed, "adamw_small")
    for d, buf in ((delta, ds), (new_m, ms), (new_v, vs)):
        d.update(dict(zip(small, _unpack_rows(buf, [w[k].shape for k in small]))))

    return (loss, grad_x[None], *[grads[k] for k in WEIGHTS], *[delta[k] for k in WEIGHTS],
            *[new_m[k] for k in WEIGHTS], *[new_v[k] for k in WEIGHTS])
```

```python
import functools
import math

import jax
import jax.numpy as jnp
from jax import lax
from jax.experimental import pallas as pl
from jax.experimental.pallas import tpu as pltpu

F32 = jnp.float32
BF16 = jnp.bfloat16
MESH = pl.DeviceIdType.MESH

LANES = 128
SUBLANES = 8
VMEM_LIMIT = 56 * 1024 * 1024
N_CHIPS = 4

HEAD_DIM = 128
QK_NOPE = 128
QK_ROPE = 64
QK_PAD = 256
ROPE_THETA = 10000.0
CONV_WIDTH = 4
CHUNK = 64
LN_EPS = 1e-5
RMS_EPS = 1e-6
ADAM_LR = 0.001
ADAM_B1 = 0.9
ADAM_B2 = 0.999
ADAM_EPS = 1e-08
ADAM_WD = 0.01
ADAM_STEP = 10
HI = lax.Precision.HIGHEST


def _cparams(sem=None):
    return pltpu.CompilerParams(dimension_semantics=sem, vmem_limit_bytes=VMEM_LIMIT)


def _tile(n, cap, unit):
    best = None
    t = unit
    while t <= min(n, cap):
        if n % t == 0:
            best = t
        t += unit
    return best if best is not None else n


def _mm(a, b, mode, out_dtypes, *, name, epilogue=None, extras=(), tm_cap=1024, tn_cap=1024, tk_cap=2048,
        b_major=False, out_major=False):
    if b_major:
        b_shape = (b.shape[1], N_CHIPS * b.shape[2])
    else:
        b_shape = b.shape
    if mode == "nn":
        (M, K), (K2, N) = a.shape, b_shape
    elif mode == "nt":
        (M, K), (N, K2) = a.shape, b_shape
    else:
        (K, M), (K2, N) = a.shape, b_shape
    assert K == K2, (a.shape, b.shape, mode)
    tm = _tile(M, tm_cap, LANES if mode == "tn" else 16)
    tn = _tile(N // N_CHIPS if (out_major or (b_major and mode == "nn")) else N, tn_cap, LANES)
    tk = _tile(K // N_CHIPS if (b_major and mode == "nt") else K, tk_cap, 16 if mode == "tn" else LANES)
    nk = K // tk
    nj4, nk4 = max(N // N_CHIPS // tn, 1), max(K // N_CHIPS // tk, 1)
    if mode == "nn":
        a_spec = pl.BlockSpec((tm, tk), lambda i, j, k: (i, k))
        b_spec = pl.BlockSpec((tk, tn), lambda i, j, k: (k, j))
        if b_major:
            b_spec = pl.BlockSpec((None, tk, tn), lambda i, j, k: (j // nj4, k, j % nj4))
        dims = (((1,), (0,)), ((), ()))
    elif mode == "nt":
        a_spec = pl.BlockSpec((tm, tk), lambda i, j, k: (i, k))
        b_spec = pl.BlockSpec((tn, tk), lambda i, j, k: (j, k))
        if b_major:
            b_spec = pl.BlockSpec((None, tn, tk), lambda i, j, k: (k // nk4, j, k % nk4))
        dims = (((1,), (1,)), ((), ()))
    else:
        assert not b_major
        a_spec = pl.BlockSpec((tk, tm), lambda i, j, k: (k, i))
        b_spec = pl.BlockSpec((tk, tn), lambda i, j, k: (k, j))
        dims = (((0,), (0,)), ((), ()))
    mn_spec = pl.BlockSpec((tm, tn), lambda i, j, k: (i, j))
    o_spec, o_shape = mn_spec, (M, N)
    if out_major:
        o_spec = pl.BlockSpec((None, tm, tn), lambda i, j, k: (j // nj4, i, j % nj4))
        o_shape = (N_CHIPS, M, N // N_CHIPS)
    n_ex, n_out = len(extras), len(out_dtypes)
    for e in extras:
        assert e.shape == (M, N), (e.shape, M, N)

    def body(a_ref, b_ref, *rest):
        ex_refs, out_refs, acc = rest[:n_ex], rest[n_ex:n_ex + n_out], rest[-1]
        k = pl.program_id(2)

        @pl.when(k == 0)
        def _():
            acc[...] = jnp.zeros_like(acc)

        acc[...] += lax.dot_general(a_ref[...].astype(BF16), b_ref[...].astype(BF16), dims,
                                    preferred_element_type=F32)

        @pl.when(k == nk - 1)
        def _():
            res = (acc[...],) if epilogue is None else epilogue(acc[...], *[e[...] for e in ex_refs])
            for o_ref, r in zip(out_refs, res):
                o_ref[...] = r.astype(o_ref.dtype)

    outs = pl.pallas_call(
        body, name=name, grid=(M // tm, N // tn, nk),
        in_specs=[a_spec, b_spec] + [mn_spec] * n_ex,
        out_specs=[o_spec] * n_out,
        out_shape=[jax.ShapeDtypeStruct(o_shape, d) for d in out_dtypes],
        scratch_shapes=[pltpu.VMEM((tm, tn), F32)],
        compiler_params=_cparams(("parallel", "parallel", "arbitrary")),
    )(a, b, *extras)
    return outs


def _spec(block, imap):
    return pl.BlockSpec(block, imap)


def _ew(fn, ins, outs, grid, *, name, acc_out=()):
    n_in = len(ins)
    ng = len(grid)

    def body(*refs):
        in_refs, out_refs = refs[:n_in], refs[n_in:]
        res = fn(*[r[...] for r in in_refs])
        first = functools.reduce(jnp.logical_and, [pl.program_id(d) == 0 for d in range(ng)])
        for i, (o_ref, r) in enumerate(zip(out_refs, res)):
            if i in acc_out:
                @pl.when(first)
                def _(o_ref=o_ref):
                    o_ref[...] = jnp.zeros_like(o_ref)
                o_ref[...] += r.astype(o_ref.dtype)
            else:
                o_ref[...] = r.astype(o_ref.dtype)

    return pl.pallas_call(
        body, name=name, grid=grid,
        in_specs=[_spec(b, m) for _, b, m in ins],
        out_specs=[_spec(b, m) for _, b, m in outs],
        out_shape=[s for s, _, _ in outs],
        compiler_params=_cparams(("arbitrary",) * ng),
    )(*[a for a, _, _ in ins])


def _ew_vjp(fn, ins, cts, gouts, grid, *, name):
    n_in, n_ct = len(ins), len(cts)
    ng = len(grid)
    want = [i for i, g in enumerate(gouts) if g is not None]

    def body(*refs):
        in_refs, ct_refs, out_refs = refs[:n_in], refs[n_in:n_in + n_ct], refs[n_in + n_ct:]
        prim = [r[...] for r in in_refs]
        outs, pull = jax.vjp(fn, *prim)
        grads = pull(tuple(r[...].astype(o.dtype) for r, o in zip(ct_refs, outs)))
        first_all = functools.reduce(jnp.logical_and, [pl.program_id(d) == 0 for d in range(ng)])
        for o_ref, i in zip(out_refs, want):
            mode = gouts[i][3]
            g = grads[i]
            if mode == "set":
                o_ref[...] = g.astype(o_ref.dtype)
            elif mode == "acc":
                @pl.when(pl.program_id(ng - 1) == 0)
                def _(o_ref=o_ref):
                    o_ref[...] = jnp.zeros_like(o_ref)
                o_ref[...] += g.astype(o_ref.dtype)
            elif mode == "acc_all":
                @pl.when(first_all)
                def _(o_ref=o_ref):
                    o_ref[...] = jnp.zeros_like(o_ref)
                o_ref[...] += g.astype(o_ref.dtype)
            else:
                @pl.when(first_all)
                def _(o_ref=o_ref):
                    o_ref[...] = jnp.zeros_like(o_ref)
                idx = pl.program_id(mode[1])
                o_ref[idx] += g.astype(o_ref.dtype)

    return pl.pallas_call(
        body, name=name, grid=grid,
        in_specs=[_spec(b, m) for _, b, m in ins] + [_spec(b, m) for _, b, m in cts],
        out_specs=[_spec(gouts[i][1], gouts[i][2]) for i in want],
        out_shape=[gouts[i][0] for i in want],
        compiler_params=_cparams(("arbitrary",) * ng),
    )(*[a for a, _, _ in ins], *[a for a, _, _ in cts])


def _sds(shape, dtype=F32):
    return jax.ShapeDtypeStruct(tuple(shape), dtype)


def _ln_fn(z, g, b):
    mu = jnp.mean(z, -1, keepdims=True)
    d = z - mu
    var = jnp.mean(d * d, -1, keepdims=True)
    y = d * lax.rsqrt(var + LN_EPS) * g + b
    return y, y


def _rms_fn(x, g):
    return (x * lax.rsqrt(jnp.mean(x * x, -1, keepdims=True) + RMS_EPS) * g,)


@jax.custom_vjp
def _rot_half(x):
    lane = lax.broadcasted_iota(jnp.int32, x.shape, x.ndim - 1)
    up = pltpu.roll(x, LANES - QK_ROPE // 2, x.ndim - 1)
    dn = pltpu.roll(x, QK_ROPE // 2, x.ndim - 1)
    return jnp.where(lane < QK_ROPE // 2, -up, jnp.where(lane < QK_ROPE, dn, 0.0))


def _rot_half_fwd(x):
    return _rot_half(x), None


def _rot_half_bwd(_, ct):
    return (-_rot_half(ct),)


_rot_half.defvjp(_rot_half_fwd, _rot_half_bwd)


def _rope_blk(x, cos, sin):
    return x * cos + _rot_half(x) * sin


def _mla_prep_fn(qraw, knope, kr, cos, sin):
    qn, qr = qraw[:, :QK_NOPE], qraw[:, QK_NOPE:]
    q = jnp.concatenate([qn, _rope_blk(qr, cos, sin)], axis=1)
    k = jnp.concatenate([knope.astype(F32), _rope_blk(kr, cos, sin)], axis=1)
    return q, k


def _l2n(x):
    return x * lax.rsqrt(jnp.sum(x * x, -1, keepdims=True) + 1e-6)


def _gdn_qk_fn(qc, kc):
    return _l2n(qc) * (HEAD_DIM ** -0.5), _l2n(kc)


def _softplus(x):
    return jnp.maximum(x, 0.0) + jnp.log(1.0 + jnp.exp(-jnp.abs(x)))


def _sigmoid(x):
    return 1.0 / (1.0 + jnp.exp(-x))


def _silu(x):
    return x * _sigmoid(x)


def _gdn_gate_fn(n_heads, head_axis):
    def fn(ab, a_log, dt_bias):
        h = pl.program_id(head_axis)
        lane = lax.broadcasted_iota(jnp.int32, ab.shape, 1)
        a_in = jnp.sum(jnp.where(lane == h, ab, 0.0), -1, keepdims=True)
        b_in = jnp.sum(jnp.where(lane == h + n_heads, ab, 0.0), -1, keepdims=True)
        g = -jnp.exp(a_log[:, :CHUNK]) * _softplus(a_in + dt_bias[:, :CHUNK])
        beta = _sigmoid(b_in) + jnp.zeros_like(g)
        return g, beta
    return fn


def _gdn_out_fn(o, z, w):
    return (o * lax.rsqrt(jnp.mean(o * o, -1, keepdims=True) + RMS_EPS) * w * _silu(z),)


def _loss_fn(y, t):
    d = y - t
    return (jnp.sum(d * d, axis=0, keepdims=True) * (0.5 / y.shape[-1]), d * (1.0 / y.shape[-1]))


def _adamw_fn(w, g, m, v):
    m = ADAM_B1 * m + (1.0 - ADAM_B1) * g
    v = ADAM_B2 * v + (1.0 - ADAM_B2) * (g * g)
    m_hat = m / (1.0 - ADAM_B1 ** ADAM_STEP)
    v_hat = v / (1.0 - ADAM_B2 ** ADAM_STEP)
    delta = -ADAM_LR * (m_hat / (jnp.sqrt(v_hat) + ADAM_EPS) + ADAM_WD * w)
    return delta, m, v


def _causal_mask(shape, row0, col0):
    row = lax.broadcasted_iota(jnp.int32, shape, 0) + row0
    col = lax.broadcasted_iota(jnp.int32, shape, 1) + col0
    return col <= row


def _rows(ref, i, t):
    return ref[pl.ds(pl.multiple_of(i * t, t), t), :]


def _walk(first, n_loop, tail, products, update):
    stop = first + n_loop
    t0 = tail[0][0]

    def step(j, carry):
        nxt = products(jnp.where(j + 1 < stop, j + 1, t0))
        update(carry, j, False)
        return nxt

    carry = lax.fori_loop(first, stop, step, products(jnp.where(n_loop > 0, first, t0)))
    for n, (j, masked) in enumerate(tail):
        nxt = products(tail[n + 1][0]) if n + 1 < len(tail) else None
        update(carry, j, masked)
        carry = nxt


def _flash_fwd(q, k, v, *, H, dq, dv, qoff, koff, voff, causal, scale, tq, tk, name, side=None):
    S, Sk = q.shape[0], k.shape[0]
    nq = S // tq
    assert (tq == tk and S == Sk) or not causal

    def body(q_ref, k_ref, v_ref, o_ref, lse_ref, m_s, l_s, acc):
        qi = pl.program_id(1)
        m_s[...] = jnp.full_like(m_s, -jnp.inf)
        l_s[...] = jnp.zeros_like(l_s)
        acc[...] = jnp.zeros_like(acc)
        qb = q_ref[...].astype(BF16)

        def products(j):
            return lax.dot_general(qb, _rows(k_ref, j, tk).astype(BF16), (((1,), (1,)), ((), ())),
                                   preferred_element_type=F32)

        def update(s, j, masked):
            s = s * scale
            if masked:
                s = jnp.where(_causal_mask(s.shape, qi * tq, j * tk), s, -jnp.inf)
            m_prev = m_s[...]
            m_new = jnp.maximum(m_prev, jnp.max(s, axis=1, keepdims=True))
            alpha = jnp.exp(m_prev - m_new)
            p = jnp.exp(s - m_new[:, :1])
            l_s[...] = alpha * l_s[...] + jnp.sum(p, axis=1, keepdims=True)
            acc[...] = acc[...] * alpha[:, :1] + lax.dot_general(
                p.astype(BF16), _rows(v_ref, j, tk).astype(BF16), (((1,), (0,)), ((), ())), preferred_element_type=F32)
            m_s[...] = m_new

        if causal:
            _walk(0, qi, [(qi, True)], products, update)
        else:
            _walk(0, Sk // tk - 1, [(Sk // tk - 1, False)], products, update)
        o_ref[...] = (acc[...] / l_s[...][:, :1]).astype(o_ref.dtype)
        lse_ref[...] = m_s[...] + jnp.log(l_s[...])

    body, s_in, s_out, s_shape, s_sems, s_args = _with_side(body, 3, 2, (H, nq), side)
    o, lse, *side_outs = pl.pallas_call(
        body, name=name, grid=(H, nq),
        in_specs=[pl.BlockSpec((tq, dq), lambda h, qi: (qi, qoff + h)),
                  pl.BlockSpec((Sk, dq), lambda h, qi: (0, koff + h)),
                  pl.BlockSpec((Sk, dv), lambda h, qi: (0, voff + h))] + s_in,
        out_specs=[pl.BlockSpec((tq, dv), lambda h, qi: (qi, h)),
                   pl.BlockSpec((tq, LANES), lambda h, qi: (qi, h))] + s_out,
        out_shape=[_sds((S, H * dv)), _sds((S, H * LANES))] + s_shape,
        scratch_shapes=[pltpu.VMEM((tq, LANES), F32), pltpu.VMEM((tq, LANES), F32), pltpu.VMEM((tq, dv), F32)] + s_sems,
        compiler_params=_cparams(("arbitrary", "arbitrary")),
    )(q, k, v, *s_args)
    return o, lse, side_outs


def _flash_bwd(q, k, v, o, lse, do, *, H, dq, dv, qoff, koff, voff, dooff, causal, scale, tq, tk, name, side=None):
    S, Sk = q.shape[0], k.shape[0]
    nq, nk = S // tq, Sk // tk
    assert (tq == tk and S == Sk) or not causal
    nt = (((1,), (1,)), ((), ()))
    tn = (((0,), (0,)), ((), ()))

    def body_q(q_ref, k_ref, v_ref, o_ref, do_ref, lse_ref, dq_ref, delta_ref, dq_acc):
        qi = pl.program_id(1)
        qb, dob = q_ref[...].astype(BF16), do_ref[...].astype(BF16)
        delta = jnp.sum(do_ref[...].astype(F32) * o_ref[...].astype(F32), axis=1, keepdims=True)
        delta_ref[...] = delta + jnp.zeros_like(delta_ref)
        lse1 = lse_ref[...][:, :1]
        dq_acc[...] = jnp.zeros_like(dq_acc)

        def products(j):
            return (lax.dot_general(qb, _rows(k_ref, j, tk).astype(BF16), nt, preferred_element_type=F32),
                    lax.dot_general(dob, _rows(v_ref, j, tk).astype(BF16), nt, preferred_element_type=F32))

        def update(sp, j, masked):
            s, dp = sp
            p = jnp.exp(s * scale - lse1)
            if masked:
                p = jnp.where(_causal_mask(s.shape, qi * tq, j * tk), p, 0.0)
            ds = p * (dp - delta) * scale
            dq_acc[...] += lax.dot_general(ds.astype(BF16), _rows(k_ref, j, tk).astype(BF16), (((1,), (0,)), ((), ())),
                                           preferred_element_type=F32)

        if causal:
            _walk(0, qi, [(qi, True)], products, update)
        else:
            _walk(0, nk - 1, [(nk - 1, False)], products, update)
        dq_ref[...] = dq_acc[...]

    body_q, s_in, s_out, s_shape, s_sems, s_args = _with_side(body_q, 6, 2, (H, nq), side)
    dqq, delta, *side_outs = pl.pallas_call(
        body_q, name=name + "_dq", grid=(H, nq),
        in_specs=[pl.BlockSpec((tq, dq), lambda h, qi: (qi, qoff + h)),
                  pl.BlockSpec((Sk, dq), lambda h, qi: (0, koff + h)),
                  pl.BlockSpec((Sk, dv), lambda h, qi: (0, voff + h)),
                  pl.BlockSpec((tq, dv), lambda h, qi: (qi, h)),
                  pl.BlockSpec((tq, dv), lambda h, qi: (qi, dooff + h)),
                  pl.BlockSpec((tq, LANES), lambda h, qi: (qi, h))] + s_in,
        out_specs=[pl.BlockSpec((tq, dq), lambda h, qi: (qi, h)),
                   pl.BlockSpec((tq, LANES), lambda h, qi: (qi, h))] + s_out,
        out_shape=[_sds((S, H * dq)), _sds((S, H * LANES))] + s_shape,
        scratch_shapes=[pltpu.VMEM((tq, dq), F32)] + s_sems,
        compiler_params=_cparams(("arbitrary", "arbitrary")),
    )(q, k, v, o, do, lse, *s_args)

    def as_rows(t):
        return t[:, ::LANES].T.reshape(H, nq, 1, tq)

    nn = (((1,), (0,)), ((), ()))

    def body_kv(q_ref, k_ref, v_ref, do_ref, lse_ref, delta_ref, dk_ref, dv_ref, dk_acc, dv_acc):
        kj = pl.program_id(1)
        kb, vb = k_ref[...].astype(BF16), v_ref[...].astype(BF16)
        dk_acc[...] = jnp.zeros_like(dk_acc)
        dv_acc[...] = jnp.zeros_like(dv_acc)

        def products(i):
            return (lax.dot_general(kb, _rows(q_ref, i, tq).astype(BF16), nt, preferred_element_type=F32),
                    lax.dot_general(vb, _rows(do_ref, i, tq).astype(BF16), nt, preferred_element_type=F32))

        def update(sp, i, masked):
            st, dpt = sp
            pt = jnp.exp(st * scale - lse_ref[i])
            if masked:
                key = lax.broadcasted_iota(jnp.int32, st.shape, 0) + kj * tk
                qry = lax.broadcasted_iota(jnp.int32, st.shape, 1) + i * tq
                pt = jnp.where(key <= qry, pt, 0.0)
            dst = pt * (dpt - delta_ref[i]) * scale
            dv_acc[...] += lax.dot_general(pt.astype(BF16), _rows(do_ref, i, tq).astype(BF16), nn,
                                           preferred_element_type=F32)
            dk_acc[...] += lax.dot_general(dst.astype(BF16), _rows(q_ref, i, tq).astype(BF16), nn,
                                           preferred_element_type=F32)

        if causal:
            _walk(kj + 1, nq - 1 - kj, [(kj, True)], products, update)
        else:
            _walk(0, nq - 1, [(nq - 1, False)], products, update)
        dk_ref[...] = dk_acc[...]
        dv_ref[...] = dv_acc[...]

    row_spec = pl.BlockSpec((None, nq, 1, tq), lambda h, kj: (h, 0, 0, 0))
    dk, dvv = pl.pallas_call(
        body_kv, name=name + "_dkv", grid=(H, nk),
        in_specs=[pl.BlockSpec((S, dq), lambda h, kj: (0, qoff + h)),
                  pl.BlockSpec((tk, dq), lambda h, kj: (kj, koff + h)),
                  pl.BlockSpec((tk, dv), lambda h, kj: (kj, voff + h)),
                  pl.BlockSpec((S, dv), lambda h, kj: (0, dooff + h)), row_spec, row_spec],
        out_specs=[pl.BlockSpec((tk, dq), lambda h, kj: (kj, h)), pl.BlockSpec((tk, dv), lambda h, kj: (kj, h))],
        out_shape=[_sds((Sk, H * dq)), _sds((Sk, H * dv))],
        scratch_shapes=[pltpu.VMEM((tk, dq), F32), pltpu.VMEM((tk, dv), F32)],
        compiler_params=_cparams(("parallel", "arbitrary")),
    )(q, k, v, do, as_rows(lse), as_rows(delta))
    return dqq, dk, dvv, side_outs


def _shift_down(x, prev8, j):
    if j == 0:
        return x
    y = pltpu.roll(x, j, 0)
    head = pltpu.roll(prev8, j, 0)
    row = lax.broadcasted_iota(jnp.int32, x.shape, 0)
    reps = x.shape[0] // SUBLANES
    return jnp.where(row < j, jnp.tile(head, (reps, 1)), y)


def _shift_up(x, next8, j):
    if j == 0:
        return x
    n = x.shape[0]
    y = pltpu.roll(x, n - j, 0)
    tail = pltpu.roll(next8, SUBLANES - j, 0)
    row = lax.broadcasted_iota(jnp.int32, x.shape, 0)
    reps = n // SUBLANES
    return jnp.where(row >= n - j, jnp.tile(tail, (reps, 1)), y)


def _conv_pre(x_ref, p_ref, w_ref, first):
    x = x_ref[...]
    prev8 = jnp.where(first, 0.0, p_ref[...])
    w = w_ref[...]
    xs = [_shift_down(x, prev8, CONV_WIDTH - 1 - j) for j in range(CONV_WIDTH)]
    c = sum(xs[j] * w[j:j + 1, :] for j in range(CONV_WIDTH))
    return c, xs


def _conv_specs(ts, tc, C_total_blocks_off):
    rb = ts // SUBLANES
    off = C_total_blocks_off
    x_spec = pl.BlockSpec((ts, tc), lambda ci, i: (i, off + ci))
    p_spec = pl.BlockSpec((SUBLANES, tc), lambda ci, i: (jnp.maximum(i * rb - 1, 0), off + ci))
    return x_spec, p_spec


def _conv_fwd(h, w, *, C, ts, tc, name):
    S = h.shape[0]
    x_spec, p_spec = _conv_specs(ts, tc, 0)

    def body(x_ref, p_ref, w_ref, y_ref):
        c, _ = _conv_pre(x_ref, p_ref, w_ref, pl.program_id(1) == 0)
        y_ref[...] = _silu(c)

    return pl.pallas_call(
        body, name=name, grid=(C // tc, S // ts),
        in_specs=[x_spec, p_spec, pl.BlockSpec((CONV_WIDTH, tc), lambda ci, i: (0, ci))],
        out_specs=pl.BlockSpec((ts, tc), lambda ci, i: (i, ci)),
        out_shape=_sds((S, C)),
        compiler_params=_cparams(("parallel", "arbitrary")),
    )(h, h, w)


def _conv_bwd(h, w, dy, *, C, ts, tc, name):
    S = h.shape[0]
    ns = S // ts
    rb = ts // SUBLANES
    x_spec, p_spec = _conv_specs(ts, tc, 0)

    def body_a(x_ref, p_ref, w_ref, dy_ref, dc_ref, dw_ref):
        i = pl.program_id(1)
        c, xs = _conv_pre(x_ref, p_ref, w_ref, i == 0)
        sg = _sigmoid(c)
        dc = dy_ref[...] * (sg * (1.0 + c * (1.0 - sg)))
        dc_ref[...] = dc

        @pl.when(i == 0)
        def _():
            dw_ref[...] = jnp.zeros_like(dw_ref)

        dw_ref[...] += jnp.concatenate([jnp.sum(dc * xs[j], axis=0, keepdims=True) for j in range(CONV_WIDTH)], axis=0)

    dc, dw = pl.pallas_call(
        body_a, name=name + "_a", grid=(C // tc, ns),
        in_specs=[x_spec, p_spec, pl.BlockSpec((CONV_WIDTH, tc), lambda ci, i: (0, ci)),
                  pl.BlockSpec((ts, tc), lambda ci, i: (i, ci))],
        out_specs=[pl.BlockSpec((ts, tc), lambda ci, i: (i, ci)),
                   pl.BlockSpec((CONV_WIDTH, tc), lambda ci, i: (0, ci))],
        out_shape=[_sds((S, C)), _sds((CONV_WIDTH, C))],
        compiler_params=_cparams(("parallel", "arbitrary")),
    )(h, h, w, dy)

    def body_b(dc_ref, n_ref, w_ref, dx_ref):
        i = pl.program_id(1)
        dcv = dc_ref[...]
        next8 = jnp.where(i == ns - 1, 0.0, n_ref[...])
        w_ = w_ref[...]
        dx_ref[...] = sum(_shift_up(dcv, next8, CONV_WIDTH - 1 - j) * w_[j:j + 1, :] for j in range(CONV_WIDTH))

    dx = pl.pallas_call(
        body_b, name=name + "_b", grid=(C // tc, ns),
        in_specs=[pl.BlockSpec((ts, tc), lambda ci, i: (i, ci)),
                  pl.BlockSpec((SUBLANES, tc), lambda ci, i: (jnp.minimum((i + 1) * rb, ns * rb - 1), ci)),
                  pl.BlockSpec((CONV_WIDTH, tc), lambda ci, i: (0, ci))],
        out_specs=pl.BlockSpec((ts, tc), lambda ci, i: (i, ci)),
        out_shape=_sds((S, C)),
        compiler_params=_cparams(("parallel", "arbitrary")),
    )(dc, dc, w)
    return dx, dw


def _bdot(a, b, ca, cb, precision=None):
    nb = a.ndim - 2
    batch = tuple(range(nb))
    return lax.dot_general(a, b, (((nb + ca,), (nb + cb,)), (batch, batch)), precision=precision,
                           preferred_element_type=F32)


@jax.custom_vjp
def _nn(a, b):
    return _bdot(a.astype(BF16), b.astype(BF16), 1, 0)


@jax.custom_vjp
def _nt(a, b):
    return _bdot(a.astype(BF16), b.astype(BF16), 1, 1)


@jax.custom_vjp
def _tn(a, b):
    return _bdot(a.astype(BF16), b.astype(BF16), 0, 0)


_nn.defvjp(lambda a, b: (_nn(a, b), (a, b)), lambda r, g: (_nt(g, r[1]), _tn(r[0], g)))
_nt.defvjp(lambda a, b: (_nt(a, b), (a, b)), lambda r, g: (_nn(g, r[1]), _tn(g, r[0])))
_tn.defvjp(lambda a, b: (_tn(a, b), (a, b)), lambda r, g: (_nt(r[1], g), _nn(r[0], g)))


def _dot3(a, b, ca, cb):
    ah, bh = a.astype(BF16), b.astype(BF16)
    al, bl = (a - ah.astype(F32)).astype(BF16), (b - bh.astype(F32)).astype(BF16)
    return _bdot(ah, bh, ca, cb) + (_bdot(ah, bl, ca, cb) + _bdot(al, bh, ca, cb))


@jax.custom_vjp
def _nn_x3(a, b):
    return _dot3(a, b, 1, 0)


@jax.custom_vjp
def _nt_x3(a, b):
    return _dot3(a, b, 1, 1)


@jax.custom_vjp
def _tn_x3(a, b):
    return _dot3(a, b, 0, 0)


_nn_x3.defvjp(lambda a, b: (_nn_x3(a, b), (a, b)), lambda r, g: (_nt_x3(g, r[1]), _tn_x3(r[0], g)))
_nt_x3.defvjp(lambda a, b: (_nt_x3(a, b), (a, b)), lambda r, g: (_nn_x3(g, r[1]), _tn_x3(g, r[0])))
_tn_x3.defvjp(lambda a, b: (_tn_x3(a, b), (a, b)), lambda r, g: (_nt_x3(r[1], g), _nn_x3(r[0], g)))


@jax.custom_vjp
def _nn_hi(a, b):
    return _bdot(a, b, 1, 0, HI)


@jax.custom_vjp
def _nt_hi(a, b):
    return _bdot(a, b, 1, 1, HI)


@jax.custom_vjp
def _tn_hi(a, b):
    return _bdot(a, b, 0, 0, HI)


_nn_hi.defvjp(lambda a, b: (_nn_hi(a, b), (a, b)), lambda r, g: (_nt_hi(g, r[1]), _tn_hi(r[0], g)))
_nt_hi.defvjp(lambda a, b: (_nt_hi(a, b), (a, b)), lambda r, g: (_nn_hi(g, r[1]), _tn_hi(g, r[0])))
_tn_hi.defvjp(lambda a, b: (_tn_hi(a, b), (a, b)), lambda r, g: (_nt_hi(r[1], g), _nn_hi(r[0], g)))


def _gdn_chunk_fn(q, k, v, g, beta, state):
    C = CHUNK
    B = q.shape[0]
    row = lax.broadcasted_iota(jnp.int32, (B, C, C), 1)
    col = lax.broadcasted_iota(jnp.int32, (B, C, C), 2)
    tril, strict = row >= col, row > col
    ones_tril = tril.astype(F32)
    gc = _nn_hi(ones_tril, g)
    gr = _nt_hi(jnp.full((B, C, C), 1.0 / C, F32), gc)
    decay = jnp.where(tril, jnp.exp(jnp.where(tril, gc - gr, 0.0)), 0.0)
    b1 = beta[:, :, :1]
    e_gc = jnp.exp(gc[:, :, :1])
    kb = k * b1
    lmat = jnp.where(strict, _nt(kb, k) * decay, 0.0)
    a = -lmat
    t = jnp.where(row == col, 1.0, 0.0) + a
    p = a
    for _ in range(5):
        p = _nn_x3(p, p)
        t = t + _nn_x3(t, p)
    rhs = jnp.concatenate([v * b1, kb * e_gc], axis=2)
    sol = _nn_x3(t, rhs)
    u, w = sol[:, :, :HEAD_DIM], sol[:, :, HEAD_DIM:]
    a_qk = jnp.where(tril, _nt(q, k) * decay, 0.0)
    gl = gc[:, C - 1:C, :1]
    q_dec = q * e_gc
    k_dec = k * jnp.exp(gl - gc[:, :, :1])
    v_new = u - _nn(w, state)
    o = _nn(q_dec, state) + _nn(a_qk, v_new)
    new_state = state * jnp.exp(gl) + _tn(k_dec, v_new)
    return o, new_state


def _split_heads(x, B):
    return jnp.stack([x[:, j * HEAD_DIM:(j + 1) * HEAD_DIM] for j in range(B)], axis=0)


def _merge_heads(x):
    return jnp.concatenate([x[j] for j in range(x.shape[0])], axis=1)


def _gdn_group(H, voff):
    return next(b for b in (12, 6, 4, 3, 2, 1) if H % b == 0 and voff % b == 0)


def _gdn_fwd(q, k, v, g, beta, *, H, voff, name, side=None):
    S = q.shape[0]
    N = S // CHUNK
    B = _gdn_group(H, voff)
    W = B * HEAD_DIM
    qs = lambda off: pl.BlockSpec((CHUNK, W), lambda h, n: (n, off // B + h))
    gs = pl.BlockSpec((B, CHUNK, CHUNK), lambda h, n: (h, n, 0))

    def body(q_ref, k_ref, v_ref, g_ref, b_ref, o_ref, st_ref, state):
        @pl.when(pl.program_id(1) == 0)
        def _():
            state[...] = jnp.zeros_like(state)

        s0 = state[...]
        st_ref[...] = s0
        o, s1 = _gdn_chunk_fn(_split_heads(q_ref[...], B), _split_heads(k_ref[...], B), _split_heads(v_ref[...], B),
                              g_ref[...], b_ref[...], s0)
        o_ref[...] = _merge_heads(o)
        state[...] = s1

    body, s_in, s_out, s_shape, s_sems, s_args = _with_side(body, 5, 2, (H // B, N), side)
    o, states, *side_outs = pl.pallas_call(
        body, name=name, grid=(H // B, N),
        in_specs=[qs(0), qs(0), qs(voff), gs, gs] + s_in,
        out_specs=[qs(0), pl.BlockSpec((B, None, HEAD_DIM, HEAD_DIM), lambda h, n: (h, n, 0, 0))] + s_out,
        out_shape=[_sds((S, H * HEAD_DIM)), _sds((H, N, HEAD_DIM, HEAD_DIM))] + s_shape,
        scratch_shapes=[pltpu.VMEM((B, HEAD_DIM, HEAD_DIM), F32)] + s_sems,
        compiler_params=_cparams(("arbitrary", "arbitrary")),
    )(q, k, v, g, beta, *s_args)
    return o, states, side_outs


def _gdn_bwd(q, k, v, g, beta, states, do, *, H, voff, name, side=None):
    S = q.shape[0]
    N = S // CHUNK
    B = _gdn_group(H, voff)
    W = B * HEAD_DIM
    rs = lambda off: pl.BlockSpec((CHUNK, W), lambda h, n: (N - 1 - n, off // B + h))
    gs = pl.BlockSpec((B, CHUNK, CHUNK), lambda h, n: (h, N - 1 - n, 0))

    def body(q_ref, k_ref, v_ref, g_ref, b_ref, st_ref, do_ref, dq_ref, dk_ref, dv_ref, dg_ref, db_ref, dstate):
        @pl.when(pl.program_id(1) == 0)
        def _():
            dstate[...] = jnp.zeros_like(dstate)

        _, pull = jax.vjp(_gdn_chunk_fn, _split_heads(q_ref[...], B), _split_heads(k_ref[...], B),
                          _split_heads(v_ref[...], B), g_ref[...], b_ref[...], st_ref[...])
        dq, dk, dv, dg, db, ds = pull((_split_heads(do_ref[...], B), dstate[...]))
        dq_ref[...] = _merge_heads(dq)
        dk_ref[...] = _merge_heads(dk)
        dv_ref[...] = _merge_heads(dv)
        dg_ref[...] = dg
        db_ref[...] = db
        dstate[...] = ds

    body, s_in, s_out, s_shape, s_sems, s_args = _with_side(body, 7, 5, (H // B, N), side)
    dq, dk, dv, dg, db, *side_outs = pl.pallas_call(
        body, name=name, grid=(H // B, N),
        in_specs=[rs(0), rs(0), rs(voff), gs, gs,
                  pl.BlockSpec((B, None, HEAD_DIM, HEAD_DIM), lambda h, n: (h, N - 1 - n, 0, 0)), rs(0)] + s_in,
        out_specs=[rs(0), rs(0), rs(0), gs, gs] + s_out,
        out_shape=[_sds((S, H * HEAD_DIM))] * 3 + [_sds((H, S, CHUNK))] * 2 + s_shape,
        scratch_shapes=[pltpu.VMEM((B, HEAD_DIM, HEAD_DIM), F32)] + s_sems,
        compiler_params=_cparams(("arbitrary", "arbitrary")),
    )(q, k, v, g, beta, states, do, *s_args)
    return dq, dk, dv, dg, db, side_outs


def _round_up(n, m):
    return (n + m - 1) // m * m


def _dims(S, D, M, shapes):
    c = dict(S=S, D=D, M=M)
    c["H"] = shapes["gdn_a_log"][-1]
    c["QL"] = shapes["mla_q_norm"][-1]
    c["KVL"] = shapes["mla_kv_norm"][-1]
    assert c["QL"] == c["KVL"]
    c["MEMW"] = shapes["mem_w_kv"][-1] // 2
    c["HM"] = c["MEMW"] // HEAD_DIM
    c["MW"] = c["H"] * HEAD_DIM
    c["F"] = shapes["mlp_w1"][-1]
    c["DEPTH"] = shapes["ln1_g"][0]
    c["ALPHA"] = (2 * c["DEPTH"]) ** 0.25
    c["MLA_IN"] = _round_up(c["QL"] + c["KVL"] + c["MEMW"] + LANES, 2 * LANES)
    c["GDN_IN"] = _round_up(4 * c["MW"] + c["MEMW"] + LANES, 2 * LANES)
    c["t_row"] = min(256, S)
    c["t_head"] = min(2048, S)
    c["t_conv"] = min(512, S)
    c["t_att"] = min(512, S)
    return c


def _pad_cols(w, n):
    return jnp.pad(w, ((0, 0), (0, n - w.shape[1])))


def _prep_mla_w_in(w, c):
    a = c["QL"] + c["KVL"]
    w = jnp.concatenate([w[:, :a], w[:, a + QK_ROPE:a + QK_ROPE + c["MEMW"]], w[:, a:a + QK_ROPE]], axis=1)
    return _pad_cols(w, c["MLA_IN"]).astype(BF16)


def _unprep_mla_w_in(dw, c):
    a, m = c["QL"] + c["KVL"], c["MEMW"]
    return jnp.concatenate([dw[:, :a], dw[:, a + m:a + m + QK_ROPE], dw[:, a:a + m]], axis=1)


def _prep_w_uq(w, c):
    w = w.reshape(c["QL"], c["H"], QK_NOPE + QK_ROPE)
    w = jnp.pad(w, ((0, 0), (0, 0), (0, QK_PAD - QK_NOPE - QK_ROPE)))
    return w.reshape(c["QL"], c["H"] * QK_PAD).astype(BF16)


def _unprep_w_uq(dw, c):
    return dw.reshape(c["QL"], c["H"], QK_PAD)[:, :, :QK_NOPE + QK_ROPE].reshape(c["QL"], c["H"] * (QK_NOPE + QK_ROPE))


def _prep_w_ukv(w, c):
    return w.reshape(c["KVL"], c["H"], 2, HEAD_DIM).transpose(0, 2, 1, 3).reshape(c["KVL"], 2 * c["MW"]).astype(BF16)


def _unprep_w_ukv(dw, c):
    return dw.reshape(c["KVL"], 2, c["H"], HEAD_DIM).transpose(0, 2, 1, 3).reshape(c["KVL"], 2 * c["MW"])


def _prep_gdn_w_in(w, c):
    a, h2 = 4 * c["MW"], 2 * c["H"]
    w = jnp.concatenate([w[:, :a], w[:, a + h2:], w[:, a:a + h2]], axis=1)
    return _pad_cols(w, c["GDN_IN"]).astype(BF16)


def _unprep_gdn_w_in(dw, c):
    a, h2, m = 4 * c["MW"], 2 * c["H"], c["MEMW"]
    return jnp.concatenate([dw[:, :a], dw[:, a + m:a + m + h2], dw[:, a:a + m]], axis=1)


def _lane_bcast(v):
    return jnp.broadcast_to(v.astype(F32)[:, None, None], (v.shape[0], 1, LANES))


def _row(i):
    return (i, 0)


def _par(i):
    return (0, 0)


def _layer_norm(z, g, b, c, name):
    S, D, ts = c["S"], c["D"], c["t_row"]
    return _ew(_ln_fn, [(z, (ts, D), _row), (g, (1, D), _par), (b, (1, D), _par)],
               [(_sds((S, D)), (ts, D), _row), (_sds((S, D), BF16), (ts, D), _row)], (S // ts,), name=name)


def _layer_norm_bwd(z, g, b, dy, c, name):
    S, D, ts = c["S"], c["D"], c["t_row"]
    fn = lambda z, g, b: _ln_fn(z, g, b)[:1]

    def both(z, g, b):
        return fn(z, g, b)

    dz, dg, db = _ew_vjp(both, [(z, (ts, D), _row), (g, (1, D), _par), (b, (1, D), _par)], [(dy, (ts, D), _row)],
                         [(_sds((S, D)), (ts, D), _row, "set"), (_sds((1, D)), (1, D), _par, "acc_all"),
                          (_sds((1, D)), (1, D), _par, "acc_all")], (S // ts,), name=name)
    return dz, dg, db


def _mem_attn_fwd(h, qoff, memkv, c, name):
    return _flash_fwd(h, memkv, memkv, H=c["HM"], dq=HEAD_DIM, dv=HEAD_DIM, qoff=qoff, koff=0, voff=c["HM"],
                      causal=False, scale=HEAD_DIM ** -0.5, tq=c["t_att"], tk=c["M"], name=name)[:2]


def _mem_attn_bwd(h, qoff, memkv, om, lsem, dcat, c, name):
    return _flash_bwd(h, memkv, memkv, om, lsem, dcat, H=c["HM"], dq=HEAD_DIM, dv=HEAD_DIM, qoff=qoff, koff=0,
                      voff=c["HM"], dooff=c["H"], causal=False, scale=HEAD_DIM ** -0.5, tq=c["t_att"], tk=c["M"],
                      name=name)[:3]


def _mla_specs(c):
    H, ts = c["H"], c["t_head"]
    kr_blk = (c["QL"] + c["KVL"] + c["MEMW"]) // LANES
    hd = lambda i, h: (i, h)
    return [((ts, QK_PAD), hd), ((ts, HEAD_DIM), hd), ((ts, LANES), lambda i, h: (i, kr_blk)),
            ((ts, LANES), lambda i, h: (i, 0)), ((ts, LANES), lambda i, h: (i, 0))]


def _mla_fwd(xb, p, cosp, sinp, c, side=None):
    S, H, QL, ts, tr = c["S"], c["H"], c["QL"], c["t_head"], c["t_row"]
    h, = _mm(xb, p["w_in"], "nn", [F32], name="mla_in")
    nq, = _ew(_rms_fn, [(h, (tr, QL), lambda i: (i, 0)), (p["q_norm"], (1, QL), _par)],
              [(_sds((S, QL), BF16), (tr, QL), _row)], (S // tr,), name="mla_qnorm")
    nkv, = _ew(_rms_fn, [(h, (tr, QL), lambda i: (i, 1)), (p["kv_norm"], (1, QL), _par)],
               [(_sds((S, QL), BF16), (tr, QL), _row)], (S // tr,), name="mla_kvnorm")
    qraw, = _mm(nq, p["w_uq"], "nn", [F32], name="mla_uq")
    kvraw, = _mm(nkv, p["w_ukv"], "nn", [BF16], name="mla_ukv")
    sp = _mla_specs(c)
    ins = [(a, b, m) for a, (b, m) in zip([qraw, kvraw, h, cosp, sinp], sp)]
    qp, kp = _ew(_mla_prep_fn, ins, [(_sds((S, H * QK_PAD), BF16), (ts, QK_PAD), lambda i, h: (i, h))] * 2,
                 (S // ts, H), name="mla_rope")
    o, lse, got = _flash_fwd(qp, kp, kvraw, H=H, dq=QK_PAD, dv=HEAD_DIM, qoff=0, koff=0, voff=H, causal=True,
                             scale=(QK_NOPE + QK_ROPE) ** -0.5, tq=c["t_att"], tk=c["t_att"], name="mla_attn", side=side)
    return o, dict(h=h, nq=nq, nkv=nkv, qraw=qraw, kvraw=kvraw, qp=qp, kp=kp, o=o, lse=lse), got


def _mla_bwd(sv, p, cosp, sinp, dcat, dqm, c, side=None):
    S, H, QL, ts, tr = c["S"], c["H"], c["QL"], c["t_head"], c["t_row"]
    dqp, dkp, dv, got = _flash_bwd(sv["qp"], sv["kp"], sv["kvraw"], sv["o"], sv["lse"], dcat, H=H, dq=QK_PAD,
                                   dv=HEAD_DIM, qoff=0, koff=0, voff=H, dooff=0, causal=True,
                                   scale=(QK_NOPE + QK_ROPE) ** -0.5, tq=c["t_att"], tk=c["t_att"],
                                   name="mla_attn_bwd", side=side)
    sp = _mla_specs(c)
    ins = [(a, b, m) for a, (b, m) in zip([sv["qraw"], sv["kvraw"], sv["h"], cosp, sinp], sp)]
    hd = lambda i, h: (i, h)
    dqraw, dknope, dkr = _ew_vjp(
        _mla_prep_fn, ins, [(dqp, (ts, QK_PAD), hd), (dkp, (ts, QK_PAD), hd)],
        [(_sds((S, H * QK_PAD), BF16), (ts, QK_PAD), hd, "set"), (_sds((S, H * HEAD_DIM), BF16), (ts, HEAD_DIM), hd, "set"),
         (_sds((S, LANES)), (ts, LANES), lambda i, h: (i, 0), "acc"), None, None], (S // ts, H), name="mla_rope_bwd")
    dkvraw = jnp.concatenate([dknope, dv.astype(BF16)], axis=1)
    dnq, = _mm(dqraw, p["w_uq"], "nt", [F32], name="mla_uq_dx")
    dw_uq, = _mm(sv["nq"], dqraw, "tn", [F32], name="mla_uq_dw")
    dnkv, = _mm(dkvraw, p["w_ukv"], "nt", [F32], name="mla_ukv_dx")
    dw_ukv, = _mm(sv["nkv"], dkvraw, "tn", [F32], name="mla_ukv_dw")
    dcq, dgq = _ew_vjp(_rms_fn, [(sv["h"], (tr, QL), lambda i: (i, 0)), (p["q_norm"], (1, QL), _par)],
                       [(dnq, (tr, QL), _row)],
                       [(_sds((S, QL), BF16), (tr, QL), _row, "set"), (_sds((1, QL)), (1, QL), _par, "acc_all")],
                       (S // tr,), name="mla_qnorm_bwd")
    dckv, dgkv = _ew_vjp(_rms_fn, [(sv["h"], (tr, QL), lambda i: (i, 1)), (p["kv_norm"], (1, QL), _par)],
                         [(dnkv, (tr, QL), _row)],
                         [(_sds((S, QL), BF16), (tr, QL), _row, "set"), (_sds((1, QL)), (1, QL), _par, "acc_all")],
                         (S // tr,), name="mla_kvnorm_bwd")
    pad = c["MLA_IN"] - (2 * QL + c["MEMW"] + LANES)
    dh = jnp.concatenate([dcq, dckv, dqm.astype(BF16), dkr.astype(BF16)] + ([jnp.zeros((S, pad), BF16)] if pad else []),
                         axis=1)
    grads = dict(mla_q_norm=dgq[0], mla_kv_norm=dgkv[0], mla_w_uq=_unprep_w_uq(dw_uq, c),
                 mla_w_ukv=_unprep_w_ukv(dw_ukv, c))
    return dh, grads, got


def _gdn_ins(h, qkvc, p, c):
    H, ts = c["H"], c["t_head"]
    ab_blk = (4 * c["MW"] + c["MEMW"]) // LANES
    qk_ins = [(qkvc, (ts, HEAD_DIM), lambda i, h: (i, h)), (qkvc, (ts, HEAD_DIM), lambda i, h: (i, H + h))]
    gate_ins = [(h, (ts, LANES), lambda i, h: (i, ab_blk)), (p["a_log"], (None, 1, LANES), lambda i, h: (h, 0, 0)),
                (p["dt_bias"], (None, 1, LANES), lambda i, h: (h, 0, 0))]
    return qk_ins, gate_ins


def _gdn_out_ins(o, h, p, c):
    H, ts = c["H"], c["t_head"]
    return [(o, (ts, HEAD_DIM), lambda i, h: (i, h)), (h, (ts, HEAD_DIM), lambda i, h: (i, 3 * H + h)),
            (p["o_norm"], (1, HEAD_DIM), lambda i, h: (0, 0))]


def _gdn_layer_fwd(xb, p, c, side=None):
    S, H, MW, ts = c["S"], c["H"], c["MW"], c["t_head"]
    h, = _mm(xb, p["w_in"], "nn", [F32], name="gdn_in")
    tc = _tile(3 * MW, 512, LANES)
    qkvc = _conv_fwd(h, p["conv"], C=3 * MW, ts=c["t_conv"], tc=tc, name="gdn_conv")
    qk_ins, gate_ins = _gdn_ins(h, qkvc, p, c)
    hd = lambda i, h: (i, h)
    qn, kn = _ew(_gdn_qk_fn, qk_ins, [(_sds((S, MW)), (ts, HEAD_DIM), hd)] * 2, (S // ts, H), name="gdn_qknorm")
    g3 = lambda i, h: (h, i, 0)
    g, beta = _ew(_gdn_gate_fn(H, 1), gate_ins, [(_sds((H, S, CHUNK)), (None, ts, CHUNK), g3)] * 2, (S // ts, H),
                  name="gdn_gate")
    o, states, got = _gdn_fwd(qn, kn, qkvc, g, beta, H=H, voff=2 * H, name="gdn_delta", side=side)
    mix, = _ew(_gdn_out_fn, _gdn_out_ins(o, h, p, c), [(_sds((S, MW)), (ts, HEAD_DIM), hd)], (S // ts, H),
               name="gdn_outnorm")
    return mix, dict(h=h, qkvc=qkvc, qn=qn, kn=kn, g=g, beta=beta, o=o, states=states), got


def _gdn_layer_bwd(sv, p, dcat, dqm, c, side=None):
    S, H, MW, ts = c["S"], c["H"], c["MW"], c["t_head"]
    hd = lambda i, h: (i, h)
    g3 = lambda i, h: (h, i, 0)
    h, qkvc = sv["h"], sv["qkvc"]
    do, dz, d_onorm = _ew_vjp(_gdn_out_fn, _gdn_out_ins(sv["o"], h, p, c), [(dcat, (ts, HEAD_DIM), hd)],
                              [(_sds((S, MW)), (ts, HEAD_DIM), hd, "set"), (_sds((S, MW), BF16), (ts, HEAD_DIM), hd, "set"),
                               (_sds((1, HEAD_DIM)), (1, HEAD_DIM), lambda i, h: (0, 0), "acc_all")],
                              (S // ts, H), name="gdn_outnorm_bwd")
    dqn, dkn, dv, dg, db, got = _gdn_bwd(sv["qn"], sv["kn"], qkvc, sv["g"], sv["beta"], sv["states"], do, H=H,
                                         voff=2 * H, name="gdn_delta_bwd", side=side)
    qk_ins, gate_ins = _gdn_ins(h, qkvc, p, c)
    dqc, dkc = _ew_vjp(_gdn_qk_fn, qk_ins, [(dqn, (ts, HEAD_DIM), hd), (dkn, (ts, HEAD_DIM), hd)],
                       [(_sds((S, MW)), (ts, HEAD_DIM), hd, "set")] * 2, (S // ts, H), name="gdn_qknorm_bwd")
    full3 = lambda i, h: (0, 0, 0)
    dab, dalog, ddt = _ew_vjp(
        _gdn_gate_fn(H, 1), gate_ins, [(dg, (None, ts, CHUNK), g3), (db, (None, ts, CHUNK), g3)],
        [(_sds((S, LANES), BF16), (ts, LANES), lambda i, h: (i, 0), "acc"),
         (_sds((H, 1, LANES)), (H, 1, LANES), full3, ("acc_at", 1)),
         (_sds((H, 1, LANES)), (H, 1, LANES), full3, ("acc_at", 1))], (S // ts, H), name="gdn_gate_bwd")
    dqkvc = jnp.concatenate([dqc, dkc, dv], axis=1)
    tc = _tile(3 * MW, 512, LANES)
    dxc, dconv = _conv_bwd(h, p["conv"], dqkvc, C=3 * MW, ts=c["t_conv"], tc=tc, name="gdn_conv_bwd")
    pad = c["GDN_IN"] - (4 * MW + c["MEMW"] + LANES)
    dh = jnp.concatenate([dxc.astype(BF16), dz, dqm.astype(BF16), dab] + ([jnp.zeros((S, pad), BF16)] if pad else []),
                         axis=1)
    grads = dict(gdn_conv=dconv, gdn_a_log=jnp.sum(dalog[:, 0, :], axis=-1), gdn_dt_bias=jnp.sum(ddt[:, 0, :], axis=-1),
                 gdn_o_norm=d_onorm[0])
    return dh, grads, got


def _prep_layer(W, i, c):
    p = dict(mem_w_kv=W["mem_w_kv"].astype(BF16), w_out=W["w_out"].astype(BF16),
             w1=W["mlp_w1"].astype(BF16), w2=W["mlp_w2"].astype(BF16),
             ln1_g=W["ln1_g"][None].astype(F32), ln1_b=W["ln1_b"][None].astype(F32),
             ln2_g=W["ln2_g"][None].astype(F32), ln2_b=W["ln2_b"][None].astype(F32))
    if i % 2 == 0:
        p.update(w_in=_prep_mla_w_in(W["mla_w_in"], c), q_norm=W["mla_q_norm"][None].astype(F32),
                 w_uq=_prep_w_uq(W["mla_w_uq"], c), kv_norm=W["mla_kv_norm"][None].astype(F32),
                 w_ukv=_prep_w_ukv(W["mla_w_ukv"], c))
    else:
        p.update(w_in=_prep_gdn_w_in(W["gdn_w_in"], c), conv=W["gdn_conv"].astype(F32),
                 a_log=_lane_bcast(W["gdn_a_log"]), dt_bias=_lane_bcast(W["gdn_dt_bias"]),
                 o_norm=W["gdn_o_norm"][None].astype(F32))
    return p


def _local_step(x, mem, positions, W, loss_target, c, next_weights=None, grad_sink=None):
    S, D, H, MW, ALPHA = c["S"], c["D"], c["H"], c["MW"], c["ALPHA"]
    inv_freq = 1.0 / (ROPE_THETA ** (jnp.arange(0, QK_ROPE, 2, dtype=F32) / QK_ROPE))
    ang = positions.astype(F32)[:, None] * inv_freq
    cos, sin = jnp.cos(ang), jnp.sin(ang)
    cosp = jnp.concatenate([cos, cos, jnp.ones((S, LANES - QK_ROPE), F32)], axis=1)
    sinp = jnp.concatenate([sin, sin, jnp.zeros((S, LANES - QK_ROPE), F32)], axis=1)
    memb = mem.astype(BF16)
    xf, xb = x, x.astype(BF16)
    saved, params = [], []
    w_next = W[0]
    for i in range(c["DEPTH"]):
        p = _prep_layer(w_next if next_weights is not None else W[i], i, c)
        mla = i % 2 == 0
        memkv, = _mm(memb, p["mem_w_kv"], "nn", [BF16], name="mem_kv")
        side, arrived = next_weights(i + 1) if next_weights is not None and i + 1 < c["DEPTH"] else (None, None)
        if mla:
            mix, sv, got = _mla_fwd(xb, p, cosp, sinp, c, side)
            qoff = (c["QL"] + c["KVL"]) // LANES
        else:
            mix, sv, got = _gdn_layer_fwd(xb, p, c, side)
            qoff = 4 * MW // LANES
        if side is not None:
            w_next = arrived(got)
        om, lsem = _mem_attn_fwd(sv["h"], qoff, memkv, c, "mem_attn")
        cat = jnp.concatenate([mix, om], axis=1).astype(BF16)
        z1, = _mm(cat, p["w_out"], "nn", [F32], name="w_out", extras=(xf,), epilogue=lambda acc, r: (ALPHA * r + acc,))
        x1, x1b = _layer_norm(z1, p["ln1_g"], p["ln1_b"], c, "ln1")
        u, a = _mm(x1b, p["w1"], "nn", [F32, BF16], name="mlp_up", b_major=True,
                   epilogue=lambda acc: (acc, jnp.square(jnp.maximum(acc, 0.0))))
        z2, = _mm(a, p["w2"], "nn", [F32], name="mlp_down", extras=(x1,), epilogue=lambda acc, r: (ALPHA * r + acc,))
        x2, x2b = _layer_norm(z2, p["ln2_g"], p["ln2_b"], c, "ln2")
        sv.update(xb=xb, memkv=memkv, om=om, lsem=lsem, cat=cat, z1=z1, x1b=x1b, u=u, a=a, z2=z2, qoff=qoff)
        saved.append(sv)
        params.append(p)
        xf, xb = x2, x2b

    ts = c["t_row"]
    lsum, dy = _ew(_loss_fn, [(xf, (ts, D), _row), (loss_target, (ts, D), _row)],
                   [(_sds((1, D)), (1, D), _par), (_sds((S, D)), (ts, D), _row)], (S // ts,), name="loss", acc_out=(0,))
    loss = jnp.sum(lsum)

    grads = [None] * c["DEPTH"]
    pending = None
    dx = dy
    for i in reversed(range(c["DEPTH"])):
        p, sv = params[i], saved[i]
        mla = i % 2 == 0
        G = {}
        dz2, dg, db = _layer_norm_bwd(sv["z2"], p["ln2_g"], p["ln2_b"], dx, c, "ln2_bwd")
        G["ln2_g"], G["ln2_b"] = dg[0], db[0]
        dz2b = dz2.astype(BF16)
        du, = _mm(dz2b, p["w2"], "nt", [BF16], name="mlp_down_dx", extras=(sv["u"],),
                  epilogue=lambda acc, u: (acc * (2.0 * jnp.maximum(u, 0.0)),))
        G["mlp_w2"], = _mm(sv["a"], dz2b, "tn", [F32], name="mlp_down_dw")
        G["mlp_w1"], = _mm(sv["x1b"], du, "tn", [F32], name="mlp_up_dw", out_major=True)
        dx1, = _mm(du, p["w1"], "nt", [F32], name="mlp_up_dx", extras=(dz2,), b_major=True,
                   epilogue=lambda acc, r: (ALPHA * r + acc,))
        dz1, dg, db = _layer_norm_bwd(sv["z1"], p["ln1_g"], p["ln1_b"], dx1, c, "ln1_bwd")
        G["ln1_g"], G["ln1_b"] = dg[0], db[0]
        dz1b = dz1.astype(BF16)
        dcat, = _mm(dz1b, p["w_out"], "nt", [BF16], name="w_out_dx")
        G["w_out"], = _mm(sv["cat"], dz1b, "tn", [F32], name="w_out_dw")
        dqm, dkm, dvm = _mem_attn_bwd(sv["h"], sv["qoff"], sv["memkv"], sv["om"], sv["lsem"], dcat, c, "mem_attn_bwd")
        dmemkv = jnp.concatenate([dkm, dvm], axis=1).astype(BF16)
        G["mem_w_kv"], = _mm(memb, dmemkv, "tn", [F32], name="mem_kv_dw")
        riders = ([pending] if pending is not None else []) + ([grad_sink(i, G, True) + (i,)] if grad_sink else [])
        side = _merge_plans([r[0] for r in riders]) if riders else None
        if mla:
            dh, g, got = _mla_bwd(sv, p, cosp, sinp, dcat, dqm, c, side)
            G.update(g)
            dw_in, = _mm(sv["xb"], dh, "tn", [F32], name="mla_in_dw")
            G["mla_w_in"] = _unprep_mla_w_in(dw_in, c)
            dx, = _mm(dh, p["w_in"], "nt", [F32], name="mla_in_dx", extras=(dz1,),
                      epilogue=lambda acc, r: (ALPHA * r + acc,))
        else:
            dh, g, got = _gdn_layer_bwd(sv, p, dcat, dqm, c, side)
            G.update(g)
            dw_in, = _mm(sv["xb"], dh, "tn", [F32], name="gdn_in_dw")
            G["gdn_w_in"] = _unprep_gdn_w_in(dw_in, c)
            dx, = _mm(dh, p["w_in"], "nt", [F32], name="gdn_in_dx", extras=(dz1,),
                      epilogue=lambda acc, r: (ALPHA * r + acc,))
        grads[i] = dict(G)
        for plan, done, layer in riders:
            n_out = len(plan["out_shape"])
            grads[layer].update(done(got[:n_out]))
            got = got[n_out:]
        pending = grad_sink(i, G, False) + (i,) if grad_sink else None
    if pending is not None:
        grads[pending[2]].update(pending[1](_run_plan(pending[0], "grad_scatter_last")))
    return loss, dx, grads


_HBM = pl.BlockSpec(memory_space=pltpu.HBM)
_VMEM = pl.BlockSpec(memory_space=pltpu.VMEM)


def _my_place():
    return lax.axis_index("x"), lax.axis_index("y"), lax.axis_index("c")


def _my_chip():
    return 2 * lax.axis_index("x") + lax.axis_index("y")


def _other_chips(x, y):
    return [(1 - x, y), (x, 1 - y), (1 - x, 1 - y)]


def _gather_plan(arrs, by_rows):
    n = len(arrs)
    halved = [_halved(a) for a in arrs]

    def run(ins, outs, sems, start, wait):
        send_sems, recv_sems, local_sems = sems
        x, y, c = _my_place()
        chips = _other_chips(x, y)

        def copy(i, k, chip, to):
            half = c if halved[i] else None
            src = ins[i] if half is None else ins[i].at[pl.ds(c * (arrs[i].shape[0] // 2), arrs[i].shape[0] // 2)]
            return pltpu.make_async_remote_copy(src_ref=src, dst_ref=_slab(outs[i], arrs[i].shape[0], by_rows[i], chip, half),
                                                send_sem=send_sems.at[3 * i + k], recv_sem=recv_sems.at[3 * i + k],
                                                device_id=to, device_id_type=MESH)

        mine = [pltpu.make_async_copy(ins[i], _slab(outs[i], arrs[i].shape[0], by_rows[i], 2 * x + y, None),
                                      local_sems.at[i]) for i in range(n)]
        sends = [copy(i, k, 2 * x + y, (cx, cy, c)) for i in range(n) for k, (cx, cy) in enumerate(chips)]
        if start:
            for cp in mine + sends:
                cp.start()
        if wait:
            for i in range(n):
                for k, (cx, cy) in enumerate(chips):
                    copy(i, k, 2 * cx + cy, (cx, cy, c)).wait_recv()
            for cp in sends:
                cp.wait_send()
            for cp in mine:
                cp.wait()

    shapes = [jax.ShapeDtypeStruct((N_CHIPS * a.shape[0], a.shape[1]) if r else (N_CHIPS,) + a.shape, a.dtype)
              for a, r in zip(arrs, by_rows)]
    sems = [pltpu.SemaphoreType.DMA((3 * n,)), pltpu.SemaphoreType.DMA((3 * n,)), pltpu.SemaphoreType.DMA((n,))]
    return dict(arrs=list(arrs), out_shape=shapes, sems=sems, run=run)


def _halved(a):
    return a.shape[0] % (4 * SUBLANES * (4 // a.dtype.itemsize)) == 0


def _slab(out, r, by_rows, chip, half):
    lo, n = (0, r) if half is None else (half * (r // 2), r // 2)
    return out.at[pl.ds(chip * r + lo, n)] if by_rows else out.at[chip].at[pl.ds(lo, n)]


def _gather_fill(outs, arrs, by_rows, name):
    idx = [i for i, a in enumerate(arrs) if _halved(a)]
    n = len(idx)

    def body(*refs):
        ins, bufs = refs[:n], refs[n:2 * n]
        send_sems, recv_sems = refs[2 * n:]
        x, y, c = _my_place()

        def copy(j, k, chip, half):
            r = arrs[idx[j]].shape[0]
            return pltpu.make_async_remote_copy(src_ref=_slab(ins[j], r, by_rows[idx[j]], chip, half),
                                                dst_ref=_slab(bufs[j], r, by_rows[idx[j]], chip, half),
                                                send_sem=send_sems.at[3 * j + k], recv_sem=recv_sems.at[3 * j + k],
                                                device_id=(x, y, 1 - c), device_id_type=MESH)

        chips = [2 * cx + cy for cx, cy in _other_chips(x, y)]
        sends = [copy(j, k, chip, c) for j in range(n) for k, chip in enumerate(chips)]
        for cp in sends:
            cp.start()
        for j in range(n):
            for k, chip in enumerate(chips):
                copy(j, k, chip, 1 - c).wait_recv()
        for cp in sends:
            cp.wait_send()

    filled = pl.pallas_call(
        body, name=name, in_specs=[_HBM] * n, out_specs=[_HBM] * n,
        out_shape=[jax.ShapeDtypeStruct(outs[i].shape, outs[i].dtype) for i in idx],
        input_output_aliases={j: j for j in range(n)},
        scratch_shapes=[pltpu.SemaphoreType.DMA((3 * n,)), pltpu.SemaphoreType.DMA((3 * n,))],
    )(*[outs[i] for i in idx])
    res = list(outs)
    for i, f in zip(idx, filled):
        res[i] = f
    return res


def _merge_plans(plans):
    def run(ins, outs, sems, start, wait):
        a = b = s = 0
        for p in plans:
            na, nb, ns = len(p["arrs"]), len(p["out_shape"]), len(p["sems"])
            p["run"](ins[a:a + na], outs[b:b + nb], sems[s:s + ns], start, wait)
            a, b, s = a + na, b + nb, s + ns

    return dict(arrs=sum((p["arrs"] for p in plans), []), out_shape=sum((p["out_shape"] for p in plans), []),
                sems=sum((p["sems"] for p in plans), []), run=run)


def _scatter_plan(ps):
    n = len(ps)

    def run(ins, outs, sems, start, wait):
        send_sems, recv_sems = sems
        x, y, c = _my_place()
        cps = [pltpu.make_async_remote_copy(src_ref=ins[i].at[2 * cx + cy], dst_ref=outs[i].at[k],
                                            send_sem=send_sems.at[3 * i + k], recv_sem=recv_sems.at[3 * i + k],
                                            device_id=(cx, cy, c), device_id_type=MESH)
               for i in range(n) for k, (cx, cy) in enumerate(_other_chips(x, y))]
        if start:
            for cp in cps:
                cp.start()
        if wait:
            for cp in cps:
                cp.wait()

    shapes = [jax.ShapeDtypeStruct((3,) + p.shape[1:], p.dtype) for p in ps]
    sems = [pltpu.SemaphoreType.DMA((3 * n,)), pltpu.SemaphoreType.DMA((3 * n,))]
    return dict(arrs=list(ps), out_shape=shapes, sems=sems, run=run)


def _run_plan(plan, name):
    n_in, n_out = len(plan["arrs"]), len(plan["out_shape"])

    def body(*refs):
        plan["run"](refs[:n_in], refs[n_in:n_in + n_out], refs[n_in + n_out:], True, True)

    return pl.pallas_call(body, name=name, in_specs=[_HBM] * n_in, out_specs=[_HBM] * n_out,
                          out_shape=plan["out_shape"], scratch_shapes=plan["sems"])(*plan["arrs"])


def _with_side(body, n_in, n_out, grid, side):
    if side is None:
        return body, [], [], [], [], []
    s_in, s_out, s_sem = len(side["arrs"]), len(side["out_shape"]), len(side["sems"])

    def wrapped(*refs):
        ins, s_ins = refs[:n_in], refs[n_in:n_in + s_in]
        o0 = n_in + s_in
        outs, s_outs = refs[o0:o0 + n_out], refs[o0 + n_out:o0 + n_out + s_out]
        rest = refs[o0 + n_out + s_out:]
        scratch, s_sems = rest[:len(rest) - s_sem], rest[len(rest) - s_sem:]
        ids = [pl.program_id(d) for d in range(len(grid))]
        first = functools.reduce(jnp.logical_and, [i == 0 for i in ids])
        last = functools.reduce(jnp.logical_and, [i == g - 1 for i, g in zip(ids, grid)])
        pl.when(first)(lambda: side["run"](s_ins, s_outs, s_sems, True, False))
        body(*ins, *outs, *scratch)
        pl.when(last)(lambda: side["run"](s_ins, s_outs, s_sems, False, True))

    return wrapped, [_HBM] * s_in, [_HBM] * s_out, side["out_shape"], side["sems"], side["arrs"]


def _swap_halves(gs, name):
    n = len(gs)

    def body(*refs):
        ins, outs = refs[:n], refs[n:2 * n]
        send_sems, recv_sems = refs[2 * n:]
        x, y, c = _my_place()
        cps = [pltpu.make_async_remote_copy(src_ref=ins[i].at[:, 1 - c], dst_ref=outs[i], send_sem=send_sems.at[i],
                                            recv_sem=recv_sems.at[i], device_id=(x, y, 1 - c), device_id_type=MESH)
               for i in range(n)]
        for cp in cps:
            cp.start()
        for cp in cps:
            cp.wait()

    return pl.pallas_call(
        body, name=name, in_specs=[_HBM] * n, out_specs=[_HBM] * n,
        out_shape=[jax.ShapeDtypeStruct((g.shape[0],) + g.shape[2:], g.dtype) for g in gs],
        scratch_shapes=[pltpu.SemaphoreType.DMA((n,)), pltpu.SemaphoreType.DMA((n,))],
    )(*gs)


def _join_halves(fs, name):
    n = len(fs)

    def body(*refs):
        ins, outs = refs[:n], refs[n:2 * n]
        send_sems, recv_sems = refs[2 * n:]
        x, y, c = _my_place()

        def copy(i, half):
            return pltpu.make_async_remote_copy(src_ref=ins[i].at[half], dst_ref=outs[i].at[half],
                                                send_sem=send_sems.at[i], recv_sem=recv_sems.at[i],
                                                device_id=(x, y, 1 - c), device_id_type=MESH)

        sends = [copy(i, c) for i in range(n)]
        for cp in sends:
            cp.start()
        for i in range(n):
            copy(i, 1 - c).wait_recv()
        for cp in sends:
            cp.wait_send()

    return pl.pallas_call(
        body, name=name, in_specs=[_HBM] * n, out_specs=[_HBM] * n,
        out_shape=[jax.ShapeDtypeStruct(f.shape, f.dtype) for f in fs],
        input_output_aliases={i: i for i in range(n)},
        scratch_shapes=[pltpu.SemaphoreType.DMA((n,)), pltpu.SemaphoreType.DMA((n,))],
    )(*fs)


def _row_tile(a, b):
    return _tile(a, max(SUBLANES, (1 << 19) // b // SUBLANES * SUBLANES), SUBLANES)


def _add_core(g, got, name):
    _, _, A, B = g.shape
    ta = _row_tile(A, B)
    return _ew(lambda p, q: (p + q,),
               [(g, (None, None, ta, B), lambda s, i: (s, lax.axis_index("c"), i, 0)),
                (got, (None, ta, B), lambda s, i: (s, i, 0))],
               [(_sds((N_CHIPS, A, B)), (None, ta, B), lambda s, i: (s, i, 0))], (N_CHIPS, A // ta), name=name)[0]


def _add_chips(p, got, name):
    _, A, B = p.shape
    ta = _row_tile(A, B)
    blk = (None, ta, B)
    return _ew(lambda a, b, c_, d: (((a + b) + c_) + d,),
               [(p, blk, lambda i: (_my_chip(), i, 0)), (got, blk, lambda i: (0, i, 0)),
                (got, blk, lambda i: (1, i, 0)), (got, blk, lambda i: (2, i, 0))],
               [(_sds((2, A, B)), blk, lambda i: (lax.axis_index("c"), i, 0))], (A // ta,), name=name)[0]


def _all_reduce_small(v, name):
    r = v.shape[0]
    masks = [(mx, my, mc) for mx in (0, 1) for my in (0, 1) for mc in (0, 1)][1:]

    def body(v_ref, out_ref, gath, send_sems, recv_sems):
        x, y, c = _my_place()
        me = 4 * x + 2 * y + c
        gath[me] = v_ref[...]

        def peer(m):
            return (x + m[0] - 2 * x * m[0], y + m[1] - 2 * y * m[1], c + m[2] - 2 * c * m[2])

        def copy(k, slab, to):
            return pltpu.make_async_remote_copy(src_ref=v_ref, dst_ref=gath.at[slab], send_sem=send_sems.at[k],
                                                recv_sem=recv_sems.at[k], device_id=to, device_id_type=MESH)

        sends = [copy(k, me, peer(m)) for k, m in enumerate(masks)]
        for cp in sends:
            cp.start()
        for k, m in enumerate(masks):
            px, py, pc = peer(m)
            copy(k, 4 * px + 2 * py + pc, (px, py, pc)).wait_recv()
        for cp in sends:
            cp.wait_send()
        total = gath[0]
        for d in range(1, 8):
            total = total + gath[d]
        out_ref[...] = total

    return pl.pallas_call(
        body, name=name, in_specs=[_VMEM], out_specs=_VMEM, out_shape=jax.ShapeDtypeStruct((r, LANES), F32),
        scratch_shapes=[pltpu.VMEM((8, r, LANES), F32), pltpu.SemaphoreType.DMA((7,)), pltpu.SemaphoreType.DMA((7,))],
    )(v)


def _pack_rows(arrs, dtype, row_mult):
    flat = jnp.concatenate([a.astype(dtype).reshape(-1) for a in arrs])
    n = flat.shape[0]
    rows = _round_up(-(-n // LANES), row_mult)
    return jnp.pad(flat, (0, rows * LANES - n)).reshape(rows, LANES)


def _unpack_rows(buf, shapes):
    lead = buf.shape[:-2]
    flat = buf.reshape(lead + (-1,))
    out, o = [], 0
    for s in shapes:
        n = math.prod(s)
        out.append(lax.slice_in_dim(flat, o, o + n, axis=len(lead)).reshape(lead + tuple(s)))
        o += n
    return out


WEIGHTS = ["mla_w_in", "mla_q_norm", "mla_w_uq", "mla_kv_norm", "mla_w_ukv", "gdn_w_in", "gdn_conv", "gdn_a_log",
           "gdn_dt_bias", "gdn_o_norm", "mem_w_kv", "w_out", "ln1_g", "ln1_b", "mlp_w1", "mlp_w2", "ln2_g", "ln2_b"]
SHARD_AXIS = {"mla_w_in": 1, "mla_w_uq": 2, "mla_w_ukv": 2, "gdn_w_in": 2, "gdn_conv": 2, "mem_w_kv": 1, "w_out": 1,
              "mlp_w1": 2, "mlp_w2": 1}
SMALL = [k for k in WEIGHTS if k not in SHARD_AXIS] + ["gdn_conv"]
BIG = [k for k in WEIGHTS if k not in SMALL]
MLA_KEYS = ["mla_w_in", "mla_q_norm", "mla_w_uq", "mla_kv_norm", "mla_w_ukv"]
GDN_KEYS = ["gdn_w_in", "gdn_conv", "gdn_a_log", "gdn_dt_bias", "gdn_o_norm"]
ALL_KEYS = ["mem_w_kv", "w_out", "ln1_g", "ln1_b", "mlp_w1", "mlp_w2", "ln2_g", "ln2_b"]


def _layer_keys(i):
    return (MLA_KEYS if i % 2 == 0 else GDN_KEYS) + ALL_KEYS


def _layer_slot(k, i):
    return i // 2 if k in MLA_KEYS or k in GDN_KEYS else i


def _gather_layer(w, i):
    keys = [k for k in _layer_keys(i) if k in SHARD_AXIS]
    arrs = [w[k][_layer_slot(k, i)].astype(F32 if k == "gdn_conv" else BF16) for k in keys]
    by_rows = [SHARD_AXIS[k] == 1 for k in keys]

    def arrived(outs):
        outs = _gather_fill(outs, arrs, by_rows, "gather_fill_" + ("mla" if i % 2 == 0 else "gdn"))
        full = {k: w[k][_layer_slot(k, i)] for k in _layer_keys(i) if k not in SHARD_AXIS}
        for k, o in zip(keys, outs):
            by_cols = SHARD_AXIS[k] == 2 and k != "mlp_w1"
            full[k] = jnp.concatenate([o[d] for d in range(N_CHIPS)], axis=1) if by_cols else o
        return full

    return _gather_plan(arrs, by_rows), arrived


EARLY = ["mlp_w1", "mlp_w2", "w_out", "mem_w_kv"]


def _reduce_group(i, G, early):
    kind = ("mla" if i % 2 == 0 else "gdn") + ("_early" if early else "_late")
    keys = [k for k in _layer_keys(i) if k in BIG and (k in EARLY) == early]
    canon = []
    for k in keys:
        g = G[k]
        if k == "mlp_w1":
            g = g.reshape(N_CHIPS, 2, g.shape[1] // 2, g.shape[2])
        elif SHARD_AXIS[k] == 1:
            g = g.reshape(N_CHIPS, 2, g.shape[0] // (2 * N_CHIPS), g.shape[1])
        else:
            rows, cw = g.shape[0], g.shape[1] // N_CHIPS
            g = g.reshape(rows, N_CHIPS, cw).transpose(1, 0, 2).reshape(N_CHIPS, 2, rows // 2, cw)
        canon.append(g)
    theirs = _swap_halves(canon, "grad_swap_" + kind)
    chip_sums = [_add_core(g, t, "grad_add_core") for g, t in zip(canon, theirs)]

    def done(got):
        halves = [_add_chips(p, s, "grad_add_chips") for p, s in zip(chip_sums, got)]
        joined = _join_halves(halves, "grad_join_" + kind)
        return {k: j.reshape(2 * j.shape[1], j.shape[2]) for k, j in zip(keys, joined)}

    return _scatter_plan(chip_sums), done


def _adamw(w, g, m, v, name):
    shape = w.shape
    cols = shape[-1]
    rows = math.prod(shape[:-1])
    tr = _tile(rows, max(SUBLANES, (1 << 19) // cols // SUBLANES * SUBLANES), SUBLANES)
    spec = ((tr, cols), _row)
    outs = _ew(_adamw_fn, [(a.reshape(rows, cols), *spec) for a in (w, g, m, v)], [(_sds((rows, cols)), *spec)] * 3,
               (rows // tr,), name=name)
    return [o.reshape(shape) for o in outs]


def kernel(x, mem, positions, mla_w_in, mla_q_norm, mla_w_uq, mla_kv_norm, mla_w_ukv, gdn_w_in, gdn_conv, gdn_a_log, gdn_dt_bias, gdn_o_norm, mem_w_kv, w_out, ln1_g, ln1_b, mlp_w1, mlp_w2, ln2_g, ln2_b, loss_target, m_mla_w_in, m_mla_q_norm, m_mla_w_uq, m_mla_kv_norm, m_mla_w_ukv, m_gdn_w_in, m_gdn_conv, m_gdn_a_log, m_gdn_dt_bias, m_gdn_o_norm, m_mem_w_kv, m_w_out, m_ln1_g, m_ln1_b, m_mlp_w1, m_mlp_w2, m_ln2_g, m_ln2_b, v_mla_w_in, v_mla_q_norm, v_mla_w_uq, v_mla_kv_norm, v_mla_w_ukv, v_gdn_w_in, v_gdn_conv, v_gdn_a_log, v_gdn_dt_bias, v_gdn_o_norm, v_mem_w_kv, v_w_out, v_ln1_g, v_ln1_b, v_mlp_w1, v_mlp_w2, v_ln2_g, v_ln2_b):
    w = dict(zip(WEIGHTS, (mla_w_in, mla_q_norm, mla_w_uq, mla_kv_norm, mla_w_ukv, gdn_w_in, gdn_conv, gdn_a_log,
                           gdn_dt_bias, gdn_o_norm, mem_w_kv, w_out, ln1_g, ln1_b, mlp_w1, mlp_w2, ln2_g, ln2_b)))
    m = dict(zip(WEIGHTS, (m_mla_w_in, m_mla_q_norm, m_mla_w_uq, m_mla_kv_norm, m_mla_w_ukv, m_gdn_w_in, m_gdn_conv,
                           m_gdn_a_log, m_gdn_dt_bias, m_gdn_o_norm, m_mem_w_kv, m_w_out, m_ln1_g, m_ln1_b, m_mlp_w1,
                           m_mlp_w2, m_ln2_g, m_ln2_b)))
    v = dict(zip(WEIGHTS, (v_mla_w_in, v_mla_q_norm, v_mla_w_uq, v_mla_kv_norm, v_mla_w_ukv, v_gdn_w_in, v_gdn_conv,
                           v_gdn_a_log, v_gdn_dt_bias, v_gdn_o_norm, v_mem_w_kv, v_w_out, v_ln1_g, v_ln1_b, v_mlp_w1,
                           v_mlp_w2, v_ln2_g, v_ln2_b)))
    assert x.shape[0] == 1, "one sequence per device"
    full_shapes = {k: w[k].shape for k in WEIGHTS}
    for k, ax in SHARD_AXIS.items():
        s = list(w[k].shape)
        s[ax] *= N_CHIPS
        full_shapes[k] = tuple(s)
    c = _dims(x.shape[1], x.shape[2], mem.shape[1], full_shapes)
    depth = c["DEPTH"]

    first, arrived = _gather_layer(w, 0)
    W = [arrived(_run_plan(first, "gather_first"))]
    loss_local, grad_x, G = _local_step(x[0], mem[0], positions[0], W, loss_target[0], c,
                                        next_weights=lambda i: _gather_layer(w, i), grad_sink=_reduce_group)
    loss = lax.psum(loss_local, ("x", "y", "c"))

    def stacked(k):
        return jnp.stack([G[i][k] for i in range(depth) if k in G[i]], axis=0)

    grads = {k: stacked(k) for k in BIG}
    small_shapes = [full_shapes[k] for k in SMALL]
    gsmall = _all_reduce_small(_pack_rows([stacked(k) for k in SMALL], F32, SUBLANES), "grad_all_reduce_small")
    grads.update(dict(zip(SMALL, _unpack_rows(gsmall, small_shapes))))
    conv_cols = w["gdn_conv"].shape[2]
    grads["gdn_conv"] = lax.dynamic_slice_in_dim(grads["gdn_conv"], _my_chip() * conv_cols, conv_cols, axis=2)

    delta, new_m, new_v = {}, {}, {}
    for k in BIG + ["gdn_conv"]:
        delta[k], new_m[k], new_v[k] = _adamw(w[k], grads[k], m[k], v[k], "adamw")
    small = [k for k in SMALL if k != "gdn_conv"]
    packed = [_pack_rows([d[k] for k in small], F32, SUBLANES) for d in (w, grads, m, v)]
    ds, ms, vs = _adamw(*packed, "adamw_small")
    for d, buf in ((delta, ds), (new_m, ms), (new_v, vs)):
        d.update(dict(zip(small, _unpack_rows(buf, [w[k].shape for k in small]))))

    return (loss, grad_x[None], *[grads[k] for k in WEIGHTS], *[delta[k] for k in WEIGHTS],
            *[new_m[k] for k in WEIGHTS], *[new_v[k] for k in WEIGHTS])
```

```python
import functools
import math

import jax
import jax.numpy as jnp
from jax import lax
from jax.experimental import pallas as pl
from jax.experimental.pallas import tpu as pltpu

F32 = jnp.float32
BF16 = jnp.bfloat16
MESH = pl.DeviceIdType.MESH

LANES = 128
SUBLANES = 8
VMEM_LIMIT = 56 * 1024 * 1024
N_CHIPS = 4

HEAD_DIM = 128
QK_NOPE = 128
QK_ROPE = 64
QK_PAD = 256
ROPE_THETA = 10000.0
CONV_WIDTH = 4
CHUNK = 64
LN_EPS = 1e-5
RMS_EPS = 1e-6
ADAM_LR = 0.001
ADAM_B1 = 0.9
ADAM_B2 = 0.999
ADAM_EPS = 1e-08
ADAM_WD = 0.01
ADAM_STEP = 10
HI = lax.Precision.HIGHEST


def _cparams(sem=None):
    return pltpu.CompilerParams(dimension_semantics=sem, vmem_limit_bytes=VMEM_LIMIT)


def _tile(n, cap, unit):
    best = None
    t = unit
    while t <= min(n, cap):
        if n % t == 0:
            best = t
        t += unit
    return best if best is not None else n


def _mm(a, b, mode, out_dtypes, *, name, epilogue=None, extras=(), tm_cap=1024, tn_cap=1024, tk_cap=2048,
        b_major=False, out_major=False):
    if b_major:
        b_shape = (b.shape[1], N_CHIPS * b.shape[2])
    else:
        b_shape = b.shape
    if mode == "nn":
        (M, K), (K2, N) = a.shape, b_shape
    elif mode == "nt":
        (M, K), (N, K2) = a.shape, b_shape
    else:
        (K, M), (K2, N) = a.shape, b_shape
    assert K == K2, (a.shape, b.shape, mode)
    tm = _tile(M, tm_cap, LANES if mode == "tn" else 16)
    tn = _tile(N // N_CHIPS if (out_major or (b_major and mode == "nn")) else N, tn_cap, LANES)
    tk = _tile(K // N_CHIPS if (b_major and mode == "nt") else K, tk_cap, 16 if mode == "tn" else LANES)
    nk = K // tk
    nj4, nk4 = max(N // N_CHIPS // tn, 1), max(K // N_CHIPS // tk, 1)
    if mode == "nn":
        a_spec = pl.BlockSpec((tm, tk), lambda i, j, k: (i, k))
        b_spec = pl.BlockSpec((tk, tn), lambda i, j, k: (k, j))
        if b_major:
            b_spec = pl.BlockSpec((None, tk, tn), lambda i, j, k: (j // nj4, k, j % nj4))
        dims = (((1,), (0,)), ((), ()))
    elif mode == "nt":
        a_spec = pl.BlockSpec((tm, tk), lambda i, j, k: (i, k))
        b_spec = pl.BlockSpec((tn, tk), lambda i, j, k: (j, k))
        if b_major:
            b_spec = pl.BlockSpec((None, tn, tk), lambda i, j, k: (k // nk4, j, k % nk4))
        dims = (((1,), (1,)), ((), ()))
    else:
        assert not b_major
        a_spec = pl.BlockSpec((tk, tm), lambda i, j, k: (k, i))
        b_spec = pl.BlockSpec((tk, tn), lambda i, j, k: (k, j))
        dims = (((0,), (0,)), ((), ()))
    mn_spec = pl.BlockSpec((tm, tn), lambda i, j, k: (i, j))
    o_spec, o_shape = mn_spec, (M, N)
    if out_major:
        o_spec = pl.BlockSpec((None, tm, tn), lambda i, j, k: (j // nj4, i, j % nj4))
        o_shape = (N_CHIPS, M, N // N_CHIPS)
    n_ex, n_out = len(extras), len(out_dtypes)
    for e in extras:
        assert e.shape == (M, N), (e.shape, M, N)

    def body(a_ref, b_ref, *rest):
        ex_refs, out_refs, acc = rest[:n_ex], rest[n_ex:n_ex + n_out], rest[-1]
        k = pl.program_id(2)

        @pl.when(k == 0)
        def _():
            acc[...] = jnp.zeros_like(acc)

        acc[...] += lax.dot_general(a_ref[...].astype(BF16), b_ref[...].astype(BF16), dims,
                                    preferred_element_type=F32)

        @pl.when(k == nk - 1)
        def _():
            res = (acc[...],) if epilogue is None else epilogue(acc[...], *[e[...] for e in ex_refs])
            for o_ref, r in zip(out_refs, res):
                o_ref[...] = r.astype(o_ref.dtype)

    outs = pl.pallas_call(
        body, name=name, grid=(M // tm, N // tn, nk),
        in_specs=[a_spec, b_spec] + [mn_spec] * n_ex,
        out_specs=[o_spec] * n_out,
        out_shape=[jax.ShapeDtypeStruct(o_shape, d) for d in out_dtypes],
        scratch_shapes=[pltpu.VMEM((tm, tn), F32)],
        compiler_params=_cparams(("parallel", "parallel", "arbitrary")),
    )(a, b, *extras)
    return outs


def _spec(block, imap):
    return pl.BlockSpec(block, imap)


def _ew(fn, ins, outs, grid, *, name, acc_out=()):
    n_in = len(ins)
    ng = len(grid)

    def body(*refs):
        in_refs, out_refs = refs[:n_in], refs[n_in:]
        res = fn(*[r[...] for r in in_refs])
        first = functools.reduce(jnp.logical_and, [pl.program_id(d) == 0 for d in range(ng)])
        for i, (o_ref, r) in enumerate(zip(out_refs, res)):
            if i in acc_out:
                @pl.when(first)
                def _(o_ref=o_ref):
                    o_ref[...] = jnp.zeros_like(o_ref)
                o_ref[...] += r.astype(o_ref.dtype)
            else:
                o_ref[...] = r.astype(o_ref.dtype)

    return pl.pallas_call(
        body, name=name, grid=grid,
        in_specs=[_spec(b, m) for _, b, m in ins],
        out_specs=[_spec(b, m) for _, b, m in outs],
        out_shape=[s for s, _, _ in outs],
        compiler_params=_cparams(("arbitrary",) * ng),
    )(*[a for a, _, _ in ins])


def _ew_vjp(fn, ins, cts, gouts, grid, *, name):
    n_in, n_ct = len(ins), len(cts)
    ng = len(grid)
    want = [i for i, g in enumerate(gouts) if g is not None]

    def body(*refs):
        in_refs, ct_refs, out_refs = refs[:n_in], refs[n_in:n_in + n_ct], refs[n_in + n_ct:]
        prim = [r[...] for r in in_refs]
        outs, pull = jax.vjp(fn, *prim)
        grads = pull(tuple(r[...].astype(o.dtype) for r, o in zip(ct_refs, outs)))
        first_all = functools.reduce(jnp.logical_and, [pl.program_id(d) == 0 for d in range(ng)])
        for o_ref, i in zip(out_refs, want):
            mode = gouts[i][3]
            g = grads[i]
            if mode == "set":
                o_ref[...] = g.astype(o_ref.dtype)
            elif mode == "acc":
                @pl.when(pl.program_id(ng - 1) == 0)
                def _(o_ref=o_ref):
                    o_ref[...] = jnp.zeros_like(o_ref)
                o_ref[...] += g.astype(o_ref.dtype)
            elif mode == "acc_all":
                @pl.when(first_all)
                def _(o_ref=o_ref):
                    o_ref[...] = jnp.zeros_like(o_ref)
                o_ref[...] += g.astype(o_ref.dtype)
            else:
                @pl.when(first_all)
                def _(o_ref=o_ref):
                    o_ref[...] = jnp.zeros_like(o_ref)
                idx = pl.program_id(mode[1])
                o_ref[idx] += g.astype(o_ref.dtype)

    return pl.pallas_call(
        body, name=name, grid=grid,
        in_specs=[_spec(b, m) for _, b, m in ins] + [_spec(b, m) for _, b, m in cts],
        out_specs=[_spec(gouts[i][1], gouts[i][2]) for i in want],
        out_shape=[gouts[i][0] for i in want],
        compiler_params=_cparams(("arbitrary",) * ng),
    )(*[a for a, _, _ in ins], *[a for a, _, _ in cts])


def _sds(shape, dtype=F32):
    return jax.ShapeDtypeStruct(tuple(shape), dtype)


def _ln_fn(z, g, b):
    mu = jnp.mean(z, -1, keepdims=True)
    d = z - mu
    var = jnp.mean(d * d, -1, keepdims=True)
    y = d * lax.rsqrt(var + LN_EPS) * g + b
    return y, y


def _rms_fn(x, g):
    return (x * lax.rsqrt(jnp.mean(x * x, -1, keepdims=True) + RMS_EPS) * g,)


@jax.custom_vjp
def _rot_half(x):
    lane = lax.broadcasted_iota(jnp.int32, x.shape, x.ndim - 1)
    up = pltpu.roll(x, LANES - QK_ROPE // 2, x.ndim - 1)
    dn = pltpu.roll(x, QK_ROPE // 2, x.ndim - 1)
    return jnp.where(lane < QK_ROPE // 2, -up, jnp.where(lane < QK_ROPE, dn, 0.0))


def _rot_half_fwd(x):
    return _rot_half(x), None


def _rot_half_bwd(_, ct):
    return (-_rot_half(ct),)


_rot_half.defvjp(_rot_half_fwd, _rot_half_bwd)


def _rope_blk(x, cos, sin):
    return x * cos + _rot_half(x) * sin


def _mla_prep_fn(qraw, knope, kr, cos, sin):
    qn, qr = qraw[:, :QK_NOPE], qraw[:, QK_NOPE:]
    q = jnp.concatenate([qn, _rope_blk(qr, cos, sin)], axis=1)
    k = jnp.concatenate([knope.astype(F32), _rope_blk(kr, cos, sin)], axis=1)
    return q, k


def _l2n(x):
    return x * lax.rsqrt(jnp.sum(x * x, -1, keepdims=True) + 1e-6)


def _gdn_qk_fn(qc, kc):
    return _l2n(qc) * (HEAD_DIM ** -0.5), _l2n(kc)


def _softplus(x):
    return jnp.maximum(x, 0.0) + jnp.log(1.0 + jnp.exp(-jnp.abs(x)))


def _sigmoid(x):
    return 1.0 / (1.0 + jnp.exp(-x))


def _silu(x):
    return x * _sigmoid(x)


def _gdn_gate_fn(n_heads, head_axis):
    def fn(ab, a_log, dt_bias):
        h = pl.program_id(head_axis)
        lane = lax.broadcasted_iota(jnp.int32, ab.shape, 1)
        a_in = jnp.sum(jnp.where(lane == h, ab, 0.0), -1, keepdims=True)
        b_in = jnp.sum(jnp.where(lane == h + n_heads, ab, 0.0), -1, keepdims=True)
        g = -jnp.exp(a_log[:, :CHUNK]) * _softplus(a_in + dt_bias[:, :CHUNK])
        beta = _sigmoid(b_in) + jnp.zeros_like(g)
        return g, beta
    return fn


def _gdn_out_fn(o, z, w):
    return (o * lax.rsqrt(jnp.mean(o * o, -1, keepdims=True) + RMS_EPS) * w * _silu(z),)


def _loss_fn(y, t):
    d = y - t
    return (jnp.sum(d * d, axis=0, keepdims=True) * (0.5 / y.shape[-1]), d * (1.0 / y.shape[-1]))


def _adamw_fn(w, g, m, v):
    m = ADAM_B1 * m + (1.0 - ADAM_B1) * g
    v = ADAM_B2 * v + (1.0 - ADAM_B2) * (g * g)
    m_hat = m / (1.0 - ADAM_B1 ** ADAM_STEP)
    v_hat = v / (1.0 - ADAM_B2 ** ADAM_STEP)
    delta = -ADAM_LR * (m_hat / (jnp.sqrt(v_hat) + ADAM_EPS) + ADAM_WD * w)
    return delta, m, v


def _causal_mask(shape, row0, col0):
    row = lax.broadcasted_iota(jnp.int32, shape, 0) + row0
    col = lax.broadcasted_iota(jnp.int32, shape, 1) + col0
    return col <= row


def _rows(ref, i, t):
    return ref[pl.ds(pl.multiple_of(i * t, t), t), :]


def _walk(first, n_loop, tail, products, update):
    stop = first + n_loop
    t0 = tail[0][0]

    def step(j, carry):
        nxt = products(jnp.where(j + 1 < stop, j + 1, t0))
        update(carry, j, False)
        return nxt

    carry = lax.fori_loop(first, stop, step, products(jnp.where(n_loop > 0, first, t0)))
    for n, (j, masked) in enumerate(tail):
        nxt = products(tail[n + 1][0]) if n + 1 < len(tail) else None
        update(carry, j, masked)
        carry = nxt


def _flash_fwd(q, k, v, *, H, dq, dv, qoff, koff, voff, causal, scale, tq, tk, name, side=None):
    S, Sk = q.shape[0], k.shape[0]
    nq = S // tq
    assert (tq == tk and S == Sk) or not causal

    def body(q_ref, k_ref, v_ref, o_ref, lse_ref, m_s, l_s, acc):
        qi = pl.program_id(1)
        m_s[...] = jnp.full_like(m_s, -jnp.inf)
        l_s[...] = jnp.zeros_like(l_s)
        acc[...] = jnp.zeros_like(acc)
        qb = q_ref[...].astype(BF16)

        def products(j):
            return lax.dot_general(qb, _rows(k_ref, j, tk).astype(BF16), (((1,), (1,)), ((), ())),
                                   preferred_element_type=F32)

        def update(s, j, masked):
            s = s * scale
            if masked:
                s = jnp.where(_causal_mask(s.shape, qi * tq, j * tk), s, -jnp.inf)
            m_prev = m_s[...]
            m_new = jnp.maximum(m_prev, jnp.max(s, axis=1, keepdims=True))
            alpha = jnp.exp(m_prev - m_new)
            p = jnp.exp(s - m_new[:, :1])
            l_s[...] = alpha * l_s[...] + jnp.sum(p, axis=1, keepdims=True)
            acc[...] = acc[...] * alpha[:, :1] + lax.dot_general(
                p.astype(BF16), _rows(v_ref, j, tk).astype(BF16), (((1,), (0,)), ((), ())), preferred_element_type=F32)
            m_s[...] = m_new

        if causal:
            _walk(0, qi, [(qi, True)], products, update)
        else:
            _walk(0, Sk // tk - 1, [(Sk // tk - 1, False)], products, update)
        o_ref[...] = (acc[...] / l_s[...][:, :1]).astype(o_ref.dtype)
        lse_ref[...] = m_s[...] + jnp.log(l_s[...])

    body, s_in, s_out, s_shape, s_sems, s_args = _with_side(body, 3, 2, (H, nq), side)
    o, lse, *side_outs = pl.pallas_call(
        body, name=name, grid=(H, nq),
        in_specs=[pl.BlockSpec((tq, dq), lambda h, qi: (qi, qoff + h)),
                  pl.BlockSpec((Sk, dq), lambda h, qi: (0, koff + h)),
                  pl.BlockSpec((Sk, dv), lambda h, qi: (0, voff + h))] + s_in,
        out_specs=[pl.BlockSpec((tq, dv), lambda h, qi: (qi, h)),
                   pl.BlockSpec((tq, LANES), lambda h, qi: (qi, h))] + s_out,
        out_shape=[_sds((S, H * dv)), _sds((S, H * LANES))] + s_shape,
        scratch_shapes=[pltpu.VMEM((tq, LANES), F32), pltpu.VMEM((tq, LANES), F32), pltpu.VMEM((tq, dv), F32)] + s_sems,
        compiler_params=_cparams(("arbitrary", "arbitrary")),
    )(q, k, v, *s_args)
    return o, lse, side_outs


def _flash_bwd(q, k, v, o, lse, do, *, H, dq, dv, qoff, koff, voff, dooff, causal, scale, tq, tk, name, side=None):
    S, Sk = q.shape[0], k.shape[0]
    nq, nk = S // tq, Sk // tk
    assert (tq == tk and S == Sk) or not causal
    nt = (((1,), (1,)), ((), ()))

    def body_q(q_ref, k_ref, v_ref, o_ref, do_ref, lse_ref, dq_ref, delta_ref, dq_acc):
        qi = pl.program_id(1)
        qb, dob = q_ref[...].astype(BF16), do_ref[...].astype(BF16)
        delta = jnp.sum(do_ref[...].astype(F32) * o_ref[...].astype(F32), axis=1, keepdims=True)
        delta_ref[...] = delta + jnp.zeros_like(delta_ref)
        lse1 = lse_ref[...][:, :1]
        dq_acc[...] = jnp.zeros_like(dq_acc)

        def products(j):
            return (lax.dot_general(qb, _rows(k_ref, j, tk).astype(BF16), nt, preferred_element_type=F32),
                    lax.dot_general(dob, _rows(v_ref, j, tk).astype(BF16), nt, preferred_element_type=F32))

        def update(sp, j, masked):
            s, dp = sp
            p = jnp.exp(s * scale - lse1)
            if masked:
                p = jnp.where(_causal_mask(s.shape, qi * tq, j * tk), p, 0.0)
            ds = p * (dp - delta) * scale
            dq_acc[...] += lax.dot_general(ds.astype(BF16), _rows(k_ref, j, tk).astype(BF16), (((1,), (0,)), ((), ())),
                                           preferred_element_type=F32)

        if causal:
            _walk(0, qi, [(qi, True)], products, update)
        else:
            _walk(0, nk - 1, [(nk - 1, False)], products, update)
        dq_ref[...] = dq_acc[...]

    body_q, s_in, s_out, s_shape, s_sems, s_args = _with_side(body_q, 6, 2, (H, nq), side)
    dqq, delta, *side_outs = pl.pallas_call(
        body_q, name=name + "_dq", grid=(H, nq),
        in_specs=[pl.BlockSpec((tq, dq), lambda h, qi: (qi, qoff + h)),
                  pl.BlockSpec((Sk, dq), lambda h, qi: (0, koff + h)),
                  pl.BlockSpec((Sk, dv), lambda h, qi: (0, voff + h)),
                  pl.BlockSpec((tq, dv), lambda h, qi: (qi, h)),
                  pl.BlockSpec((tq, dv), lambda h, qi: (qi, dooff + h)),
                  pl.BlockSpec((tq, LANES), lambda h, qi: (qi, h))] + s_in,
        out_specs=[pl.BlockSpec((tq, dq), lambda h, qi: (qi, h)),
                   pl.BlockSpec((tq, LANES), lambda h, qi: (qi, h))] + s_out,
        out_shape=[_sds((S, H * dq)), _sds((S, H * LANES))] + s_shape,
        scratch_shapes=[pltpu.VMEM((tq, dq), F32)] + s_sems,
        compiler_params=_cparams(("arbitrary", "arbitrary")),
    )(q, k, v, o, do, lse, *s_args)

    def as_rows(t):
        return t[:, ::LANES].T.reshape(H, nq, 1, tq)

    nn = (((1,), (0,)), ((), ()))

    def body_kv(q_ref, k_ref, v_ref, do_ref, lse_ref, delta_ref, dk_ref, dv_ref, dk_acc, dv_acc):
        kj = pl.program_id(1)
        kb, vb = k_ref[...].astype(BF16), v_ref[...].astype(BF16)
        dk_acc[...] = jnp.zeros_like(dk_acc)
        dv_acc[...] = jnp.zeros_like(dv_acc)

        def products(i):
            return (lax.dot_general(kb, _rows(q_ref, i, tq).astype(BF16), nt, preferred_element_type=F32),
                    lax.dot_general(vb, _rows(do_ref, i, tq).astype(BF16), nt, preferred_element_type=F32))

        def update(sp, i, masked):
            st, dpt = sp
            pt = jnp.exp(st * scale - lse_ref[i])
            if masked:
                key = lax.broadcasted_iota(jnp.int32, st.shape, 0) + kj * tk
                qry = lax.broadcasted_iota(jnp.int32, st.shape, 1) + i * tq
                pt = jnp.where(key <= qry, pt, 0.0)
            dst = pt * (dpt - delta_ref[i]) * scale
            dv_acc[...] += lax.dot_general(pt.astype(BF16), _rows(do_ref, i, tq).astype(BF16), nn,
                                           preferred_element_type=F32)
            dk_acc[...] += lax.dot_general(dst.astype(BF16), _rows(q_ref, i, tq).astype(BF16), nn,
                                           preferred_element_type=F32)

        if causal:
            _walk(kj + 1, nq - 1 - kj, [(kj, True)], products, update)
        else:
            _walk(0, nq - 1, [(nq - 1, False)], products, update)
        dk_ref[...] = dk_acc[...]
        dv_ref[...] = dv_acc[...]

    row_spec = pl.BlockSpec((None, nq, 1, tq), lambda h, kj: (h, 0, 0, 0))
    dk, dvv = pl.pallas_call(
        body_kv, name=name + "_dkv", grid=(H, nk),
        in_specs=[pl.BlockSpec((S, dq), lambda h, kj: (0, qoff + h)),
                  pl.BlockSpec((tk, dq), lambda h, kj: (kj, koff + h)),
                  pl.BlockSpec((tk, dv), lambda h, kj: (kj, voff + h)),
                  pl.BlockSpec((S, dv), lambda h, kj: (0, dooff + h)), row_spec, row_spec],
        out_specs=[pl.BlockSpec((tk, dq), lambda h, kj: (kj, h)), pl.BlockSpec((tk, dv), lambda h, kj: (kj, h))],
        out_shape=[_sds((Sk, H * dq)), _sds((Sk, H * dv))],
        scratch_shapes=[pltpu.VMEM((tk, dq), F32), pltpu.VMEM((tk, dv), F32)],
        compiler_params=_cparams(("parallel", "arbitrary")),
    )(q, k, v, do, as_rows(lse), as_rows(delta))
    return dqq, dk, dvv, side_outs


def _flash_bwd_causal(q, k, v, o, lse, do, *, H, dq, dv, qoff, koff, voff, dooff, scale, tq, name, side=None):
    S = q.shape[0]
    nq = S // tq
    nt = (((1,), (1,)), ((), ()))
    tn = (((0,), (0,)), ((), ()))
    nn = (((1,), (0,)), ((), ()))

    def body(q_ref, k_ref, v_ref, o_ref, do_ref, lse_ref, dq_ref, dk_ref, dv_ref, dq_acc):
        qi = pl.program_id(1)

        @pl.when(qi == 0)
        def _():
            dk_ref[...] = jnp.zeros_like(dk_ref)
            dv_ref[...] = jnp.zeros_like(dv_ref)

        qb, dob = q_ref[...].astype(BF16), do_ref[...].astype(BF16)
        delta = jnp.sum(do_ref[...].astype(F32) * o_ref[...].astype(F32), axis=1, keepdims=True)
        lse1 = lse_ref[...][:, :1]
        dq_acc[...] = jnp.zeros_like(dq_acc)

        def products(j):
            return (lax.dot_general(qb, _rows(k_ref, j, tq).astype(BF16), nt, preferred_element_type=F32),
                    lax.dot_general(dob, _rows(v_ref, j, tq).astype(BF16), nt, preferred_element_type=F32))

        def update(sp, j, masked):
            s, dp = sp
            p = jnp.exp(s * scale - lse1)
            if masked:
                p = jnp.where(_causal_mask(s.shape, qi * tq, j * tq), p, 0.0)
            ds = (p * (dp - delta) * scale).astype(BF16)
            dq_acc[...] += lax.dot_general(ds, _rows(k_ref, j, tq).astype(BF16), nn, preferred_element_type=F32)
            rows = pl.ds(pl.multiple_of(j * tq, tq), tq)
            dk_ref[rows, :] += lax.dot_general(ds, qb, tn, preferred_element_type=F32)
            dv_ref[rows, :] += lax.dot_general(p.astype(BF16), dob, tn, preferred_element_type=F32)

        _walk(0, qi, [(qi, True)], products, update)
        dq_ref[...] = dq_acc[...]

    body, s_in, s_out, s_shape, s_sems, s_args = _with_side(body, 6, 3, (H, nq), side)
    dqq, dk, dvv, *side_outs = pl.pallas_call(
        body, name=name, grid=(H, nq),
        in_specs=[pl.BlockSpec((tq, dq), lambda h, qi: (qi, qoff + h)),
                  pl.BlockSpec((S, dq), lambda h, qi: (0, koff + h)),
                  pl.BlockSpec((S, dv), lambda h, qi: (0, voff + h)),
                  pl.BlockSpec((tq, dv), lambda h, qi: (qi, h)),
                  pl.BlockSpec((tq, dv), lambda h, qi: (qi, dooff + h)),
                  pl.BlockSpec((tq, LANES), lambda h, qi: (qi, h))] + s_in,
        out_specs=[pl.BlockSpec((tq, dq), lambda h, qi: (qi, h)),
                   pl.BlockSpec((S, dq), lambda h, qi: (0, h)), pl.BlockSpec((S, dv), lambda h, qi: (0, h))] + s_out,
        out_shape=[_sds((S, H * dq)), _sds((S, H * dq)), _sds((S, H * dv))] + s_shape,
        scratch_shapes=[pltpu.VMEM((tq, dq), F32)] + s_sems,
        compiler_params=_cparams(("arbitrary", "arbitrary")),
    )(q, k, v, o, do, lse, *s_args)
    return dqq, dk, dvv, side_outs


def _shift_down(x, prev8, j):
    if j == 0:
        return x
    y = pltpu.roll(x, j, 0)
    head = pltpu.roll(prev8, j, 0)
    row = lax.broadcasted_iota(jnp.int32, x.shape, 0)
    reps = x.shape[0] // SUBLANES
    return jnp.where(row < j, jnp.tile(head, (reps, 1)), y)


def _shift_up(x, next8, j):
    if j == 0:
        return x
    n = x.shape[0]
    y = pltpu.roll(x, n - j, 0)
    tail = pltpu.roll(next8, SUBLANES - j, 0)
    row = lax.broadcasted_iota(jnp.int32, x.shape, 0)
    reps = n // SUBLANES
    return jnp.where(row >= n - j, jnp.tile(tail, (reps, 1)), y)


def _conv_pre(x_ref, p_ref, w_ref, first):
    x = x_ref[...]
    prev8 = jnp.where(first, 0.0, p_ref[...])
    w = w_ref[...]
    xs = [_shift_down(x, prev8, CONV_WIDTH - 1 - j) for j in range(CONV_WIDTH)]
    c = sum(xs[j] * w[j:j + 1, :] for j in range(CONV_WIDTH))
    return c, xs


def _conv_specs(ts, tc, C_total_blocks_off):
    rb = ts // SUBLANES
    off = C_total_blocks_off
    x_spec = pl.BlockSpec((ts, tc), lambda ci, i: (i, off + ci))
    p_spec = pl.BlockSpec((SUBLANES, tc), lambda ci, i: (jnp.maximum(i * rb - 1, 0), off + ci))
    return x_spec, p_spec


def _conv_fwd(h, w, *, C, ts, tc, name):
    S = h.shape[0]
    x_spec, p_spec = _conv_specs(ts, tc, 0)

    def body(x_ref, p_ref, w_ref, y_ref):
        c, _ = _conv_pre(x_ref, p_ref, w_ref, pl.program_id(1) == 0)
        y_ref[...] = _silu(c)

    return pl.pallas_call(
        body, name=name, grid=(C // tc, S // ts),
        in_specs=[x_spec, p_spec, pl.BlockSpec((CONV_WIDTH, tc), lambda ci, i: (0, ci))],
        out_specs=pl.BlockSpec((ts, tc), lambda ci, i: (i, ci)),
        out_shape=_sds((S, C)),
        compiler_params=_cparams(("parallel", "arbitrary")),
    )(h, h, w)


def _conv_bwd(h, w, dy, *, C, ts, tc, name):
    S = h.shape[0]
    ns = S // ts
    rb = ts // SUBLANES
    x_spec, p_spec = _conv_specs(ts, tc, 0)

    def body_a(x_ref, p_ref, w_ref, dy_ref, dc_ref, dw_ref):
        i = pl.program_id(1)
        c, xs = _conv_pre(x_ref, p_ref, w_ref, i == 0)
        sg = _sigmoid(c)
        dc = dy_ref[...] * (sg * (1.0 + c * (1.0 - sg)))
        dc_ref[...] = dc

        @pl.when(i == 0)
        def _():
            dw_ref[...] = jnp.zeros_like(dw_ref)

        dw_ref[...] += jnp.concatenate([jnp.sum(dc * xs[j], axis=0, keepdims=True) for j in range(CONV_WIDTH)], axis=0)

    dc, dw = pl.pallas_call(
        body_a, name=name + "_a", grid=(C // tc, ns),
        in_specs=[x_spec, p_spec, pl.BlockSpec((CONV_WIDTH, tc), lambda ci, i: (0, ci)),
                  pl.BlockSpec((ts, tc), lambda ci, i: (i, ci))],
        out_specs=[pl.BlockSpec((ts, tc), lambda ci, i: (i, ci)),
                   pl.BlockSpec((CONV_WIDTH, tc), lambda ci, i: (0, ci))],
        out_shape=[_sds((S, C)), _sds((CONV_WIDTH, C))],
        compiler_params=_cparams(("parallel", "arbitrary")),
    )(h, h, w, dy)

    def body_b(dc_ref, n_ref, w_ref, dx_ref):
        i = pl.program_id(1)
        dcv = dc_ref[...]
        next8 = jnp.where(i == ns - 1, 0.0, n_ref[...])
        w_ = w_ref[...]
        dx_ref[...] = sum(_shift_up(dcv, next8, CONV_WIDTH - 1 - j) * w_[j:j + 1, :] for j in range(CONV_WIDTH))

    dx = pl.pallas_call(
        body_b, name=name + "_b", grid=(C // tc, ns),
        in_specs=[pl.BlockSpec((ts, tc), lambda ci, i: (i, ci)),
                  pl.BlockSpec((SUBLANES, tc), lambda ci, i: (jnp.minimum((i + 1) * rb, ns * rb - 1), ci)),
                  pl.BlockSpec((CONV_WIDTH, tc), lambda ci, i: (0, ci))],
        out_specs=pl.BlockSpec((ts, tc), lambda ci, i: (i, ci)),
        out_shape=_sds((S, C)),
        compiler_params=_cparams(("parallel", "arbitrary")),
    )(dc, dc, w)
    return dx, dw


def _bdot(a, b, ca, cb, precision=None):
    nb = a.ndim - 2
    batch = tuple(range(nb))
    return lax.dot_general(a, b, (((nb + ca,), (nb + cb,)), (batch, batch)), precision=precision,
                           preferred_element_type=F32)


@jax.custom_vjp
def _nn(a, b):
    return _bdot(a.astype(BF16), b.astype(BF16), 1, 0)


@jax.custom_vjp
def _nt(a, b):
    return _bdot(a.astype(BF16), b.astype(BF16), 1, 1)


@jax.custom_vjp
def _tn(a, b):
    return _bdot(a.astype(BF16), b.astype(BF16), 0, 0)


_nn.defvjp(lambda a, b: (_nn(a, b), (a, b)), lambda r, g: (_nt(g, r[1]), _tn(r[0], g)))
_nt.defvjp(lambda a, b: (_nt(a, b), (a, b)), lambda r, g: (_nn(g, r[1]), _tn(g, r[0])))
_tn.defvjp(lambda a, b: (_tn(a, b), (a, b)), lambda r, g: (_nt(r[1], g), _nn(r[0], g)))


def _dot3(a, b, ca, cb):
    ah, bh = a.astype(BF16), b.astype(BF16)
    al, bl = (a - ah.astype(F32)).astype(BF16), (b - bh.astype(F32)).astype(BF16)
    return _bdot(ah, bh, ca, cb) + (_bdot(ah, bl, ca, cb) + _bdot(al, bh, ca, cb))


@jax.custom_vjp
def _nn_x3(a, b):
    return _dot3(a, b, 1, 0)


@jax.custom_vjp
def _nt_x3(a, b):
    return _dot3(a, b, 1, 1)


@jax.custom_vjp
def _tn_x3(a, b):
    return _dot3(a, b, 0, 0)


_nn_x3.defvjp(lambda a, b: (_nn_x3(a, b), (a, b)), lambda r, g: (_nt_x3(g, r[1]), _tn_x3(r[0], g)))
_nt_x3.defvjp(lambda a, b: (_nt_x3(a, b), (a, b)), lambda r, g: (_nn_x3(g, r[1]), _tn_x3(g, r[0])))
_tn_x3.defvjp(lambda a, b: (_tn_x3(a, b), (a, b)), lambda r, g: (_nt_x3(r[1], g), _nn_x3(r[0], g)))


@jax.custom_vjp
def _nn_hi(a, b):
    return _bdot(a, b, 1, 0, HI)


@jax.custom_vjp
def _nt_hi(a, b):
    return _bdot(a, b, 1, 1, HI)


@jax.custom_vjp
def _tn_hi(a, b):
    return _bdot(a, b, 0, 0, HI)


_nn_hi.defvjp(lambda a, b: (_nn_hi(a, b), (a, b)), lambda r, g: (_nt_hi(g, r[1]), _tn_hi(r[0], g)))
_nt_hi.defvjp(lambda a, b: (_nt_hi(a, b), (a, b)), lambda r, g: (_nn_hi(g, r[1]), _tn_hi(g, r[0])))
_tn_hi.defvjp(lambda a, b: (_tn_hi(a, b), (a, b)), lambda r, g: (_nt_hi(r[1], g), _nn_hi(r[0], g)))


def _gdn_chunk_fn(q, k, v, g, beta, state):
    C = CHUNK
    B = q.shape[0]
    row = lax.broadcasted_iota(jnp.int32, (B, C, C), 1)
    col = lax.broadcasted_iota(jnp.int32, (B, C, C), 2)
    tril, strict = row >= col, row > col
    ones_tril = tril.astype(F32)
    gc = _nn_hi(ones_tril, g)
    gr = _nt_hi(jnp.full((B, C, C), 1.0 / C, F32), gc)
    decay = jnp.where(tril, jnp.exp(jnp.where(tril, gc - gr, 0.0)), 0.0)
    b1 = beta[:, :, :1]
    e_gc = jnp.exp(gc[:, :, :1])
    kb = k * b1
    lmat = jnp.where(strict, _nt(kb, k) * decay, 0.0)
    a = -lmat
    t = jnp.where(row == col, 1.0, 0.0) + a
    p = a
    for _ in range(5):
        p = _nn_x3(p, p)
        t = t + _nn_x3(t, p)
    rhs = jnp.concatenate([v * b1, kb * e_gc], axis=2)
    sol = _nn_x3(t, rhs)
    u, w = sol[:, :, :HEAD_DIM], sol[:, :, HEAD_DIM:]
    a_qk = jnp.where(tril, _nt(q, k) * decay, 0.0)
    gl = gc[:, C - 1:C, :1]
    q_dec = q * e_gc
    k_dec = k * jnp.exp(gl - gc[:, :, :1])
    v_new = u - _nn(w, state)
    o = _nn(q_dec, state) + _nn(a_qk, v_new)
    new_state = state * jnp.exp(gl) + _tn(k_dec, v_new)
    return o, new_state


def _split_heads(x, B):
    return jnp.stack([x[:, j * HEAD_DIM:(j + 1) * HEAD_DIM] for j in range(B)], axis=0)


def _merge_heads(x):
    return jnp.concatenate([x[j] for j in range(x.shape[0])], axis=1)


def _gdn_group(H, voff):
    return next(b for b in (12, 6, 4, 3, 2, 1) if H % b == 0 and voff % b == 0)


def _gdn_fwd(q, k, v, g, beta, *, H, voff, name, side=None):
    S = q.shape[0]
    N = S // CHUNK
    B = _gdn_group(H, voff)
    W = B * HEAD_DIM
    qs = lambda off: pl.BlockSpec((CHUNK, W), lambda h, n: (n, off // B + h))
    gs = pl.BlockSpec((B, CHUNK, CHUNK), lambda h, n: (h, n, 0))

    def body(q_ref, k_ref, v_ref, g_ref, b_ref, o_ref, st_ref, state):
        @pl.when(pl.program_id(1) == 0)
        def _():
            state[...] = jnp.zeros_like(state)

        s0 = state[...]
        st_ref[...] = s0
        o, s1 = _gdn_chunk_fn(_split_heads(q_ref[...], B), _split_heads(k_ref[...], B), _split_heads(v_ref[...], B),
                              g_ref[...], b_ref[...], s0)
        o_ref[...] = _merge_heads(o)
        state[...] = s1

    body, s_in, s_out, s_shape, s_sems, s_args = _with_side(body, 5, 2, (H // B, N), side)
    o, states, *side_outs = pl.pallas_call(
        body, name=name, grid=(H // B, N),
        in_specs=[qs(0), qs(0), qs(voff), gs, gs] + s_in,
        out_specs=[qs(0), pl.BlockSpec((B, None, HEAD_DIM, HEAD_DIM), lambda h, n: (h, n, 0, 0))] + s_out,
        out_shape=[_sds((S, H * HEAD_DIM)), _sds((H, N, HEAD_DIM, HEAD_DIM))] + s_shape,
        scratch_shapes=[pltpu.VMEM((B, HEAD_DIM, HEAD_DIM), F32)] + s_sems,
        compiler_params=_cparams(("arbitrary", "arbitrary")),
    )(q, k, v, g, beta, *s_args)
    return o, states, side_outs


def _gdn_bwd(q, k, v, g, beta, states, do, *, H, voff, name, side=None):
    S = q.shape[0]
    N = S // CHUNK
    B = _gdn_group(H, voff)
    W = B * HEAD_DIM
    rs = lambda off: pl.BlockSpec((CHUNK, W), lambda h, n: (N - 1 - n, off // B + h))
    gs = pl.BlockSpec((B, CHUNK, CHUNK), lambda h, n: (h, N - 1 - n, 0))

    def body(q_ref, k_ref, v_ref, g_ref, b_ref, st_ref, do_ref, dq_ref, dk_ref, dv_ref, dg_ref, db_ref, dstate):
        @pl.when(pl.program_id(1) == 0)
        def _():
            dstate[...] = jnp.zeros_like(dstate)

        _, pull = jax.vjp(_gdn_chunk_fn, _split_heads(q_ref[...], B), _split_heads(k_ref[...], B),
                          _split_heads(v_ref[...], B), g_ref[...], b_ref[...], st_ref[...])
        dq, dk, dv, dg, db, ds = pull((_split_heads(do_ref[...], B), dstate[...]))
        dq_ref[...] = _merge_heads(dq)
        dk_ref[...] = _merge_heads(dk)
        dv_ref[...] = _merge_heads(dv)
        dg_ref[...] = dg
        db_ref[...] = db
        dstate[...] = ds

    body, s_in, s_out, s_shape, s_sems, s_args = _with_side(body, 7, 5, (H // B, N), side)
    dq, dk, dv, dg, db, *side_outs = pl.pallas_call(
        body, name=name, grid=(H // B, N),
        in_specs=[rs(0), rs(0), rs(voff), gs, gs,
                  pl.BlockSpec((B, None, HEAD_DIM, HEAD_DIM), lambda h, n: (h, N - 1 - n, 0, 0)), rs(0)] + s_in,
        out_specs=[rs(0), rs(0), rs(0), gs, gs] + s_out,
        out_shape=[_sds((S, H * HEAD_DIM))] * 3 + [_sds((H, S, CHUNK))] * 2 + s_shape,
        scratch_shapes=[pltpu.VMEM((B, HEAD_DIM, HEAD_DIM), F32)] + s_sems,
        compiler_params=_cparams(("arbitrary", "arbitrary")),
    )(q, k, v, g, beta, states, do, *s_args)
    return dq, dk, dv, dg, db, side_outs


def _round_up(n, m):
    return (n + m - 1) // m * m


def _dims(S, D, M, shapes):
    c = dict(S=S, D=D, M=M)
    c["H"] = shapes["gdn_a_log"][-1]
    c["QL"] = shapes["mla_q_norm"][-1]
    c["KVL"] = shapes["mla_kv_norm"][-1]
    assert c["QL"] == c["KVL"]
    c["MEMW"] = shapes["mem_w_kv"][-1] // 2
    c["HM"] = c["MEMW"] // HEAD_DIM
    c["MW"] = c["H"] * HEAD_DIM
    c["F"] = shapes["mlp_w1"][-1]
    c["DEPTH"] = shapes["ln1_g"][0]
    c["ALPHA"] = (2 * c["DEPTH"]) ** 0.25
    c["MLA_IN"] = _round_up(c["QL"] + c["KVL"] + c["MEMW"] + LANES, 2 * LANES)
    c["GDN_IN"] = _round_up(4 * c["MW"] + c["MEMW"] + LANES, 2 * LANES)
    c["t_row"] = min(256, S)
    c["t_head"] = min(2048, S)
    c["t_conv"] = min(512, S)
    c["t_att"] = min(512, S)
    return c


def _pad_cols(w, n):
    return jnp.pad(w, ((0, 0), (0, n - w.shape[1])))


def _prep_mla_w_in(w, c):
    a = c["QL"] + c["KVL"]
    w = jnp.concatenate([w[:, :a], w[:, a + QK_ROPE:a + QK_ROPE + c["MEMW"]], w[:, a:a + QK_ROPE]], axis=1)
    return _pad_cols(w, c["MLA_IN"]).astype(BF16)


def _unprep_mla_w_in(dw, c):
    a, m = c["QL"] + c["KVL"], c["MEMW"]
    return jnp.concatenate([dw[:, :a], dw[:, a + m:a + m + QK_ROPE], dw[:, a:a + m]], axis=1)


def _prep_w_uq(w, c):
    w = w.reshape(c["QL"], c["H"], QK_NOPE + QK_ROPE)
    w = jnp.pad(w, ((0, 0), (0, 0), (0, QK_PAD - QK_NOPE - QK_ROPE)))
    return w.reshape(c["QL"], c["H"] * QK_PAD).astype(BF16)


def _unprep_w_uq(dw, c):
    return dw.reshape(c["QL"], c["H"], QK_PAD)[:, :, :QK_NOPE + QK_ROPE].reshape(c["QL"], c["H"] * (QK_NOPE + QK_ROPE))


def _prep_w_ukv(w, c):
    return w.reshape(c["KVL"], c["H"], 2, HEAD_DIM).transpose(0, 2, 1, 3).reshape(c["KVL"], 2 * c["MW"]).astype(BF16)


def _unprep_w_ukv(dw, c):
    return dw.reshape(c["KVL"], 2, c["H"], HEAD_DIM).transpose(0, 2, 1, 3).reshape(c["KVL"], 2 * c["MW"])


def _prep_gdn_w_in(w, c):
    a, h2 = 4 * c["MW"], 2 * c["H"]
    w = jnp.concatenate([w[:, :a], w[:, a + h2:], w[:, a:a + h2]], axis=1)
    return _pad_cols(w, c["GDN_IN"]).astype(BF16)


def _unprep_gdn_w_in(dw, c):
    a, h2, m = 4 * c["MW"], 2 * c["H"], c["MEMW"]
    return jnp.concatenate([dw[:, :a], dw[:, a + m:a + m + h2], dw[:, a:a + m]], axis=1)


def _lane_bcast(v):
    return jnp.broadcast_to(v.astype(F32)[:, None, None], (v.shape[0], 1, LANES))


def _row(i):
    return (i, 0)


def _par(i):
    return (0, 0)


def _layer_norm(z, g, b, c, name):
    S, D, ts = c["S"], c["D"], c["t_row"]
    return _ew(_ln_fn, [(z, (ts, D), _row), (g, (1, D), _par), (b, (1, D), _par)],
               [(_sds((S, D)), (ts, D), _row), (_sds((S, D), BF16), (ts, D), _row)], (S // ts,), name=name)


def _layer_norm_bwd(z, g, b, dy, c, name):
    S, D, ts = c["S"], c["D"], c["t_row"]
    fn = lambda z, g, b: _ln_fn(z, g, b)[:1]

    def both(z, g, b):
        return fn(z, g, b)

    dz, dg, db = _ew_vjp(both, [(z, (ts, D), _row), (g, (1, D), _par), (b, (1, D), _par)], [(dy, (ts, D), _row)],
                         [(_sds((S, D)), (ts, D), _row, "set"), (_sds((1, D)), (1, D), _par, "acc_all"),
                          (_sds((1, D)), (1, D), _par, "acc_all")], (S // ts,), name=name)
    return dz, dg, db


def _mem_attn_fwd(h, qoff, memkv, c, name):
    return _flash_fwd(h, memkv, memkv, H=c["HM"], dq=HEAD_DIM, dv=HEAD_DIM, qoff=qoff, koff=0, voff=c["HM"],
                      causal=False, scale=HEAD_DIM ** -0.5, tq=c["t_att"], tk=c["M"], name=name)[:2]


def _mem_attn_bwd(h, qoff, memkv, om, lsem, dcat, c, name):
    return _flash_bwd(h, memkv, memkv, om, lsem, dcat, H=c["HM"], dq=HEAD_DIM, dv=HEAD_DIM, qoff=qoff, koff=0,
                      voff=c["HM"], dooff=c["H"], causal=False, scale=HEAD_DIM ** -0.5, tq=c["t_att"], tk=c["M"],
                      name=name)[:3]


def _mla_specs(c):
    H, ts = c["H"], c["t_head"]
    kr_blk = (c["QL"] + c["KVL"] + c["MEMW"]) // LANES
    hd = lambda i, h: (i, h)
    return [((ts, QK_PAD), hd), ((ts, HEAD_DIM), hd), ((ts, LANES), lambda i, h: (i, kr_blk)),
            ((ts, LANES), lambda i, h: (i, 0)), ((ts, LANES), lambda i, h: (i, 0))]


def _mla_fwd(xb, p, cosp, sinp, c, side=None):
    S, H, QL, ts, tr = c["S"], c["H"], c["QL"], c["t_head"], c["t_row"]
    h, = _mm(xb, p["w_in"], "nn", [F32], name="mla_in")
    nq, = _ew(_rms_fn, [(h, (tr, QL), lambda i: (i, 0)), (p["q_norm"], (1, QL), _par)],
              [(_sds((S, QL), BF16), (tr, QL), _row)], (S // tr,), name="mla_qnorm")
    nkv, = _ew(_rms_fn, [(h, (tr, QL), lambda i: (i, 1)), (p["kv_norm"], (1, QL), _par)],
               [(_sds((S, QL), BF16), (tr, QL), _row)], (S // tr,), name="mla_kvnorm")
    qraw, = _mm(nq, p["w_uq"], "nn", [F32], name="mla_uq")
    kvraw, = _mm(nkv, p["w_ukv"], "nn", [BF16], name="mla_ukv")
    sp = _mla_specs(c)
    ins = [(a, b, m) for a, (b, m) in zip([qraw, kvraw, h, cosp, sinp], sp)]
    qp, kp = _ew(_mla_prep_fn, ins, [(_sds((S, H * QK_PAD), BF16), (ts, QK_PAD), lambda i, h: (i, h))] * 2,
                 (S // ts, H), name="mla_rope")
    o, lse, got = _flash_fwd(qp, kp, kvraw, H=H, dq=QK_PAD, dv=HEAD_DIM, qoff=0, koff=0, voff=H, causal=True,
                             scale=(QK_NOPE + QK_ROPE) ** -0.5, tq=c["t_att"], tk=c["t_att"], name="mla_attn", side=side)
    return o, dict(h=h, nq=nq, nkv=nkv, qraw=qraw, kvraw=kvraw, qp=qp, kp=kp, o=o, lse=lse), got


def _mla_bwd(sv, p, cosp, sinp, dcat, dqm, c, side=None):
    S, H, QL, ts, tr = c["S"], c["H"], c["QL"], c["t_head"], c["t_row"]
    dqp, dkp, dv, got = _flash_bwd_causal(sv["qp"], sv["kp"], sv["kvraw"], sv["o"], sv["lse"], dcat, H=H, dq=QK_PAD,
                                          dv=HEAD_DIM, qoff=0, koff=0, voff=H, dooff=0,
                                          scale=(QK_NOPE + QK_ROPE) ** -0.5, tq=c["t_att"], name="mla_attn_bwd",
                                          side=side)
    sp = _mla_specs(c)
    ins = [(a, b, m) for a, (b, m) in zip([sv["qraw"], sv["kvraw"], sv["h"], cosp, sinp], sp)]
    hd = lambda i, h: (i, h)
    dqraw, dknope, dkr = _ew_vjp(
        _mla_prep_fn, ins, [(dqp, (ts, QK_PAD), hd), (dkp, (ts, QK_PAD), hd)],
        [(_sds((S, H * QK_PAD), BF16), (ts, QK_PAD), hd, "set"), (_sds((S, H * HEAD_DIM), BF16), (ts, HEAD_DIM), hd, "set"),
         (_sds((S, LANES)), (ts, LANES), lambda i, h: (i, 0), "acc"), None, None], (S // ts, H), name="mla_rope_bwd")
    dkvraw = jnp.concatenate([dknope, dv.astype(BF16)], axis=1)
    dnq, = _mm(dqraw, p["w_uq"], "nt", [F32], name="mla_uq_dx")
    dw_uq, = _mm(sv["nq"], dqraw, "tn", [F32], name="mla_uq_dw")
    dnkv, = _mm(dkvraw, p["w_ukv"], "nt", [F32], name="mla_ukv_dx")
    dw_ukv, = _mm(sv["nkv"], dkvraw, "tn", [F32], name="mla_ukv_dw")
    dcq, dgq = _ew_vjp(_rms_fn, [(sv["h"], (tr, QL), lambda i: (i, 0)), (p["q_norm"], (1, QL), _par)],
                       [(dnq, (tr, QL), _row)],
                       [(_sds((S, QL), BF16), (tr, QL), _row, "set"), (_sds((1, QL)), (1, QL), _par, "acc_all")],
                       (S // tr,), name="mla_qnorm_bwd")
    dckv, dgkv = _ew_vjp(_rms_fn, [(sv["h"], (tr, QL), lambda i: (i, 1)), (p["kv_norm"], (1, QL), _par)],
                         [(dnkv, (tr, QL), _row)],
                         [(_sds((S, QL), BF16), (tr, QL), _row, "set"), (_sds((1, QL)), (1, QL), _par, "acc_all")],
                         (S // tr,), name="mla_kvnorm_bwd")
    pad = c["MLA_IN"] - (2 * QL + c["MEMW"] + LANES)
    dh = jnp.concatenate([dcq, dckv, dqm.astype(BF16), dkr.astype(BF16)] + ([jnp.zeros((S, pad), BF16)] if pad else []),
                         axis=1)
    grads = dict(mla_q_norm=dgq[0], mla_kv_norm=dgkv[0], mla_w_uq=_unprep_w_uq(dw_uq, c),
                 mla_w_ukv=_unprep_w_ukv(dw_ukv, c))
    return dh, grads, got


def _gdn_ins(h, qkvc, p, c):
    H, ts = c["H"], c["t_head"]
    ab_blk = (4 * c["MW"] + c["MEMW"]) // LANES
    qk_ins = [(qkvc, (ts, HEAD_DIM), lambda i, h: (i, h)), (qkvc, (ts, HEAD_DIM), lambda i, h: (i, H + h))]
    gate_ins = [(h, (ts, LANES), lambda i, h: (i, ab_blk)), (p["a_log"], (None, 1, LANES), lambda i, h: (h, 0, 0)),
                (p["dt_bias"], (None, 1, LANES), lambda i, h: (h, 0, 0))]
    return qk_ins, gate_ins


def _gdn_out_ins(o, h, p, c):
    H, ts = c["H"], c["t_head"]
    return [(o, (ts, HEAD_DIM), lambda i, h: (i, h)), (h, (ts, HEAD_DIM), lambda i, h: (i, 3 * H + h)),
            (p["o_norm"], (1, HEAD_DIM), lambda i, h: (0, 0))]


def _gdn_layer_fwd(xb, p, c, side=None):
    S, H, MW, ts = c["S"], c["H"], c["MW"], c["t_head"]
    h, = _mm(xb, p["w_in"], "nn", [F32], name="gdn_in")
    tc = _tile(3 * MW, 512, LANES)
    qkvc = _conv_fwd(h, p["conv"], C=3 * MW, ts=c["t_conv"], tc=tc, name="gdn_conv")
    qk_ins, gate_ins = _gdn_ins(h, qkvc, p, c)
    hd = lambda i, h: (i, h)
    qn, kn = _ew(_gdn_qk_fn, qk_ins, [(_sds((S, MW)), (ts, HEAD_DIM), hd)] * 2, (S // ts, H), name="gdn_qknorm")
    g3 = lambda i, h: (h, i, 0)
    g, beta = _ew(_gdn_gate_fn(H, 1), gate_ins, [(_sds((H, S, CHUNK)), (None, ts, CHUNK), g3)] * 2, (S // ts, H),
                  name="gdn_gate")
    o, states, got = _gdn_fwd(qn, kn, qkvc, g, beta, H=H, voff=2 * H, name="gdn_delta", side=side)
    mix, = _ew(_gdn_out_fn, _gdn_out_ins(o, h, p, c), [(_sds((S, MW)), (ts, HEAD_DIM), hd)], (S // ts, H),
               name="gdn_outnorm")
    return mix, dict(h=h, qkvc=qkvc, qn=qn, kn=kn, g=g, beta=beta, o=o, states=states), got


def _gdn_layer_bwd(sv, p, dcat, dqm, c, side=None):
    S, H, MW, ts = c["S"], c["H"], c["MW"], c["t_head"]
    hd = lambda i, h: (i, h)
    g3 = lambda i, h: (h, i, 0)
    h, qkvc = sv["h"], sv["qkvc"]
    do, dz, d_onorm = _ew_vjp(_gdn_out_fn, _gdn_out_ins(sv["o"], h, p, c), [(dcat, (ts, HEAD_DIM), hd)],
                              [(_sds((S, MW)), (ts, HEAD_DIM), hd, "set"), (_sds((S, MW), BF16), (ts, HEAD_DIM), hd, "set"),
                               (_sds((1, HEAD_DIM)), (1, HEAD_DIM), lambda i, h: (0, 0), "acc_all")],
                              (S // ts, H), name="gdn_outnorm_bwd")
    dqn, dkn, dv, dg, db, got = _gdn_bwd(sv["qn"], sv["kn"], qkvc, sv["g"], sv["beta"], sv["states"], do, H=H,
                                         voff=2 * H, name="gdn_delta_bwd", side=side)
    qk_ins, gate_ins = _gdn_ins(h, qkvc, p, c)
    dqc, dkc = _ew_vjp(_gdn_qk_fn, qk_ins, [(dqn, (ts, HEAD_DIM), hd), (dkn, (ts, HEAD_DIM), hd)],
                       [(_sds((S, MW)), (ts, HEAD_DIM), hd, "set")] * 2, (S // ts, H), name="gdn_qknorm_bwd")
    full3 = lambda i, h: (0, 0, 0)
    dab, dalog, ddt = _ew_vjp(
        _gdn_gate_fn(H, 1), gate_ins, [(dg, (None, ts, CHUNK), g3), (db, (None, ts, CHUNK), g3)],
        [(_sds((S, LANES), BF16), (ts, LANES), lambda i, h: (i, 0), "acc"),
         (_sds((H, 1, LANES)), (H, 1, LANES), full3, ("acc_at", 1)),
         (_sds((H, 1, LANES)), (H, 1, LANES), full3, ("acc_at", 1))], (S // ts, H), name="gdn_gate_bwd")
    dqkvc = jnp.concatenate([dqc, dkc, dv], axis=1)
    tc = _tile(3 * MW, 512, LANES)
    dxc, dconv = _conv_bwd(h, p["conv"], dqkvc, C=3 * MW, ts=c["t_conv"], tc=tc, name="gdn_conv_bwd")
    pad = c["GDN_IN"] - (4 * MW + c["MEMW"] + LANES)
    dh = jnp.concatenate([dxc.astype(BF16), dz, dqm.astype(BF16), dab] + ([jnp.zeros((S, pad), BF16)] if pad else []),
                         axis=1)
    grads = dict(gdn_conv=dconv, gdn_a_log=jnp.sum(dalog[:, 0, :], axis=-1), gdn_dt_bias=jnp.sum(ddt[:, 0, :], axis=-1),
                 gdn_o_norm=d_onorm[0])
    return dh, grads, got


def _prep_layer(W, i, c):
    p = dict(mem_w_kv=W["mem_w_kv"].astype(BF16), w_out=W["w_out"].astype(BF16),
             w1=W["mlp_w1"].astype(BF16), w2=W["mlp_w2"].astype(BF16),
             ln1_g=W["ln1_g"][None].astype(F32), ln1_b=W["ln1_b"][None].astype(F32),
             ln2_g=W["ln2_g"][None].astype(F32), ln2_b=W["ln2_b"][None].astype(F32))
    if i % 2 == 0:
        p.update(w_in=_prep_mla_w_in(W["mla_w_in"], c), q_norm=W["mla_q_norm"][None].astype(F32),
                 w_uq=_prep_w_uq(W["mla_w_uq"], c), kv_norm=W["mla_kv_norm"][None].astype(F32),
                 w_ukv=_prep_w_ukv(W["mla_w_ukv"], c))
    else:
        p.update(w_in=_prep_gdn_w_in(W["gdn_w_in"], c), conv=W["gdn_conv"].astype(F32),
                 a_log=_lane_bcast(W["gdn_a_log"]), dt_bias=_lane_bcast(W["gdn_dt_bias"]),
                 o_norm=W["gdn_o_norm"][None].astype(F32))
    return p


def _local_step(x, mem, positions, W, loss_target, c, next_weights=None, grad_sink=None):
    S, D, H, MW, ALPHA = c["S"], c["D"], c["H"], c["MW"], c["ALPHA"]
    inv_freq = 1.0 / (ROPE_THETA ** (jnp.arange(0, QK_ROPE, 2, dtype=F32) / QK_ROPE))
    ang = positions.astype(F32)[:, None] * inv_freq
    cos, sin = jnp.cos(ang), jnp.sin(ang)
    cosp = jnp.concatenate([cos, cos, jnp.ones((S, LANES - QK_ROPE), F32)], axis=1)
    sinp = jnp.concatenate([sin, sin, jnp.zeros((S, LANES - QK_ROPE), F32)], axis=1)
    memb = mem.astype(BF16)
    xf, xb = x, x.astype(BF16)
    saved, params = [], []
    w_next = W[0]
    for i in range(c["DEPTH"]):
        p = _prep_layer(w_next if next_weights is not None else W[i], i, c)
        mla = i % 2 == 0
        memkv, = _mm(memb, p["mem_w_kv"], "nn", [BF16], name="mem_kv")
        side, arrived = next_weights(i + 1) if next_weights is not None and i + 1 < c["DEPTH"] else (None, None)
        if mla:
            mix, sv, got = _mla_fwd(xb, p, cosp, sinp, c, side)
            qoff = (c["QL"] + c["KVL"]) // LANES
        else:
            mix, sv, got = _gdn_layer_fwd(xb, p, c, side)
            qoff = 4 * MW // LANES
        if side is not None:
            w_next = arrived(got)
        om, lsem = _mem_attn_fwd(sv["h"], qoff, memkv, c, "mem_attn")
        cat = jnp.concatenate([mix, om], axis=1).astype(BF16)
        z1, = _mm(cat, p["w_out"], "nn", [F32], name="w_out", extras=(xf,), epilogue=lambda acc, r: (ALPHA * r + acc,))
        x1, x1b = _layer_norm(z1, p["ln1_g"], p["ln1_b"], c, "ln1")
        u, a = _mm(x1b, p["w1"], "nn", [F32, BF16], name="mlp_up", b_major=True,
                   epilogue=lambda acc: (acc, jnp.square(jnp.maximum(acc, 0.0))))
        z2, = _mm(a, p["w2"], "nn", [F32], name="mlp_down", extras=(x1,), epilogue=lambda acc, r: (ALPHA * r + acc,))
        x2, x2b = _layer_norm(z2, p["ln2_g"], p["ln2_b"], c, "ln2")
        sv.update(xb=xb, memkv=memkv, om=om, lsem=lsem, cat=cat, z1=z1, x1b=x1b, u=u, a=a, z2=z2, qoff=qoff)
        saved.append(sv)
        params.append(p)
        xf, xb = x2, x2b

    ts = c["t_row"]
    lsum, dy = _ew(_loss_fn, [(xf, (ts, D), _row), (loss_target, (ts, D), _row)],
                   [(_sds((1, D)), (1, D), _par), (_sds((S, D)), (ts, D), _row)], (S // ts,), name="loss", acc_out=(0,))
    loss = jnp.sum(lsum)

    grads = [None] * c["DEPTH"]
    pending = None
    dx = dy
    for i in reversed(range(c["DEPTH"])):
        p, sv = params[i], saved[i]
        mla = i % 2 == 0
        G = {}
        dz2, dg, db = _layer_norm_bwd(sv["z2"], p["ln2_g"], p["ln2_b"], dx, c, "ln2_bwd")
        G["ln2_g"], G["ln2_b"] = dg[0], db[0]
        dz2b = dz2.astype(BF16)
        du, = _mm(dz2b, p["w2"], "nt", [BF16], name="mlp_down_dx", extras=(sv["u"],),
                  epilogue=lambda acc, u: (acc * (2.0 * jnp.maximum(u, 0.0)),))
        G["mlp_w2"], = _mm(sv["a"], dz2b, "tn", [F32], name="mlp_down_dw")
        G["mlp_w1"], = _mm(sv["x1b"], du, "tn", [F32], name="mlp_up_dw", out_major=True)
        dx1, = _mm(du, p["w1"], "nt", [F32], name="mlp_up_dx", extras=(dz2,), b_major=True,
                   epilogue=lambda acc, r: (ALPHA * r + acc,))
        dz1, dg, db = _layer_norm_bwd(sv["z1"], p["ln1_g"], p["ln1_b"], dx1, c, "ln1_bwd")
        G["ln1_g"], G["ln1_b"] = dg[0], db[0]
        dz1b = dz1.astype(BF16)
        dcat, = _mm(dz1b, p["w_out"], "nt", [BF16], name="w_out_dx")
        G["w_out"], = _mm(sv["cat"], dz1b, "tn", [F32], name="w_out_dw")
        dqm, dkm, dvm = _mem_attn_bwd(sv["h"], sv["qoff"], sv["memkv"], sv["om"], sv["lsem"], dcat, c, "mem_attn_bwd")
        dmemkv = jnp.concatenate([dkm, dvm], axis=1).astype(BF16)
        G["mem_w_kv"], = _mm(memb, dmemkv, "tn", [F32], name="mem_kv_dw")
        riders = ([pending] if pending is not None else []) + ([grad_sink(i, G, True) + (i,)] if grad_sink else [])
        side = _merge_plans([r[0] for r in riders]) if riders else None
        if mla:
            dh, g, got = _mla_bwd(sv, p, cosp, sinp, dcat, dqm, c, side)
            G.update(g)
            dw_in, = _mm(sv["xb"], dh, "tn", [F32], name="mla_in_dw")
            G["mla_w_in"] = _unprep_mla_w_in(dw_in, c)
            dx, = _mm(dh, p["w_in"], "nt", [F32], name="mla_in_dx", extras=(dz1,),
                      epilogue=lambda acc, r: (ALPHA * r + acc,))
        else:
            dh, g, got = _gdn_layer_bwd(sv, p, dcat, dqm, c, side)
            G.update(g)
            dw_in, = _mm(sv["xb"], dh, "tn", [F32], name="gdn_in_dw")
            G["gdn_w_in"] = _unprep_gdn_w_in(dw_in, c)
            dx, = _mm(dh, p["w_in"], "nt", [F32], name="gdn_in_dx", extras=(dz1,),
                      epilogue=lambda acc, r: (ALPHA * r + acc,))
        grads[i] = dict(G)
        for plan, done, layer in riders:
            n_out = len(plan["out_shape"])
            grads[layer].update(done(got[:n_out]))
            got = got[n_out:]
        pending = grad_sink(i, G, False) + (i,) if grad_sink else None
    if pending is not None:
        grads[pending[2]].update(pending[1](_run_plan(pending[0], "grad_scatter_last")))
    return loss, dx, grads


_HBM = pl.BlockSpec(memory_space=pltpu.HBM)
_VMEM = pl.BlockSpec(memory_space=pltpu.VMEM)


def _my_place():
    return lax.axis_index("x"), lax.axis_index("y"), lax.axis_index("c")


def _my_chip():
    return 2 * lax.axis_index("x") + lax.axis_index("y")


def _other_chips(x, y):
    return [(1 - x, y), (x, 1 - y), (1 - x, 1 - y)]


def _gather_plan(arrs, by_rows):
    n = len(arrs)
    halved = [_halved(a) for a in arrs]

    def run(ins, outs, sems, start, wait):
        send_sems, recv_sems, local_sems = sems
        x, y, c = _my_place()
        chips = _other_chips(x, y)

        def copy(i, k, chip, to):
            half = c if halved[i] else None
            src = ins[i] if half is None else ins[i].at[pl.ds(c * (arrs[i].shape[0] // 2), arrs[i].shape[0] // 2)]
            return pltpu.make_async_remote_copy(src_ref=src, dst_ref=_slab(outs[i], arrs[i].shape[0], by_rows[i], chip, half),
                                                send_sem=send_sems.at[3 * i + k], recv_sem=recv_sems.at[3 * i + k],
                                                device_id=to, device_id_type=MESH)

        mine = [pltpu.make_async_copy(ins[i], _slab(outs[i], arrs[i].shape[0], by_rows[i], 2 * x + y, None),
                                      local_sems.at[i]) for i in range(n)]
        sends = [copy(i, k, 2 * x + y, (cx, cy, c)) for i in range(n) for k, (cx, cy) in enumerate(chips)]
        if start:
            for cp in mine + sends:
                cp.start()
        if wait:
            for i in range(n):
                for k, (cx, cy) in enumerate(chips):
                    copy(i, k, 2 * cx + cy, (cx, cy, c)).wait_recv()
            for cp in sends:
                cp.wait_send()
            for cp in mine:
                cp.wait()

    shapes = [jax.ShapeDtypeStruct((N_CHIPS * a.shape[0], a.shape[1]) if r else (N_CHIPS,) + a.shape, a.dtype)
              for a, r in zip(arrs, by_rows)]
    sems = [pltpu.SemaphoreType.DMA((3 * n,)), pltpu.SemaphoreType.DMA((3 * n,)), pltpu.SemaphoreType.DMA((n,))]
    return dict(arrs=list(arrs), out_shape=shapes, sems=sems, run=run)


def _halved(a):
    return a.shape[0] % (4 * SUBLANES * (4 // a.dtype.itemsize)) == 0


def _slab(out, r, by_rows, chip, half):
    lo, n = (0, r) if half is None else (half * (r // 2), r // 2)
    return out.at[pl.ds(chip * r + lo, n)] if by_rows else out.at[chip].at[pl.ds(lo, n)]


def _gather_fill(outs, arrs, by_rows, name):
    idx = [i for i, a in enumerate(arrs) if _halved(a)]
    n = len(idx)

    def body(*refs):
        ins, bufs = refs[:n], refs[n:2 * n]
        send_sems, recv_sems = refs[2 * n:]
        x, y, c = _my_place()

        def copy(j, k, chip, half):
            r = arrs[idx[j]].shape[0]
            return pltpu.make_async_remote_copy(src_ref=_slab(ins[j], r, by_rows[idx[j]], chip, half),
                                                dst_ref=_slab(bufs[j], r, by_rows[idx[j]], chip, half),
                                                send_sem=send_sems.at[3 * j + k], recv_sem=recv_sems.at[3 * j + k],
                                                device_id=(x, y, 1 - c), device_id_type=MESH)

        chips = [2 * cx + cy for cx, cy in _other_chips(x, y)]
        sends = [copy(j, k, chip, c) for j in range(n) for k, chip in enumerate(chips)]
        for cp in sends:
            cp.start()
        for j in range(n):
            for k, chip in enumerate(chips):
                copy(j, k, chip, 1 - c).wait_recv()
        for cp in sends:
            cp.wait_send()

    filled = pl.pallas_call(
        body, name=name, in_specs=[_HBM] * n, out_specs=[_HBM] * n,
        out_shape=[jax.ShapeDtypeStruct(outs[i].shape, outs[i].dtype) for i in idx],
        input_output_aliases={j: j for j in range(n)},
        scratch_shapes=[pltpu.SemaphoreType.DMA((3 * n,)), pltpu.SemaphoreType.DMA((3 * n,))],
    )(*[outs[i] for i in idx])
    res = list(outs)
    for i, f in zip(idx, filled):
        res[i] = f
    return res


def _merge_plans(plans):
    def run(ins, outs, sems, start, wait):
        a = b = s = 0
        for p in plans:
            na, nb, ns = len(p["arrs"]), len(p["out_shape"]), len(p["sems"])
            p["run"](ins[a:a + na], outs[b:b + nb], sems[s:s + ns], start, wait)
            a, b, s = a + na, b + nb, s + ns

    return dict(arrs=sum((p["arrs"] for p in plans), []), out_shape=sum((p["out_shape"] for p in plans), []),
                sems=sum((p["sems"] for p in plans), []), run=run)


def _scatter_plan(ps):
    n = len(ps)

    def run(ins, outs, sems, start, wait):
        send_sems, recv_sems = sems
        x, y, c = _my_place()
        cps = [pltpu.make_async_remote_copy(src_ref=ins[i].at[2 * cx + cy], dst_ref=outs[i].at[k],
                                            send_sem=send_sems.at[3 * i + k], recv_sem=recv_sems.at[3 * i + k],
                                            device_id=(cx, cy, c), device_id_type=MESH)
               for i in range(n) for k, (cx, cy) in enumerate(_other_chips(x, y))]
        if start:
            for cp in cps:
                cp.start()
        if wait:
            for cp in cps:
                cp.wait()

    shapes = [jax.ShapeDtypeStruct((3,) + p.shape[1:], p.dtype) for p in ps]
    sems = [pltpu.SemaphoreType.DMA((3 * n,)), pltpu.SemaphoreType.DMA((3 * n,))]
    return dict(arrs=list(ps), out_shape=shapes, sems=sems, run=run)


def _run_plan(plan, name):
    n_in, n_out = len(plan["arrs"]), len(plan["out_shape"])

    def body(*refs):
        plan["run"](refs[:n_in], refs[n_in:n_in + n_out], refs[n_in + n_out:], True, True)

    return pl.pallas_call(body, name=name, in_specs=[_HBM] * n_in, out_specs=[_HBM] * n_out,
                          out_shape=plan["out_shape"], scratch_shapes=plan["sems"])(*plan["arrs"])


def _with_side(body, n_in, n_out, grid, side):
    if side is None:
        return body, [], [], [], [], []
    s_in, s_out, s_sem = len(side["arrs"]), len(side["out_shape"]), len(side["sems"])

    def wrapped(*refs):
        ins, s_ins = refs[:n_in], refs[n_in:n_in + s_in]
        o0 = n_in + s_in
        outs, s_outs = refs[o0:o0 + n_out], refs[o0 + n_out:o0 + n_out + s_out]
        rest = refs[o0 + n_out + s_out:]
        scratch, s_sems = rest[:len(rest) - s_sem], rest[len(rest) - s_sem:]
        ids = [pl.program_id(d) for d in range(len(grid))]
        first = functools.reduce(jnp.logical_and, [i == 0 for i in ids])
        last = functools.reduce(jnp.logical_and, [i == g - 1 for i, g in zip(ids, grid)])
        pl.when(first)(lambda: side["run"](s_ins, s_outs, s_sems, True, False))
        body(*ins, *outs, *scratch)
        pl.when(last)(lambda: side["run"](s_ins, s_outs, s_sems, False, True))

    return wrapped, [_HBM] * s_in, [_HBM] * s_out, side["out_shape"], side["sems"], side["arrs"]


def _swap_halves(gs, name):
    n = len(gs)

    def body(*refs):
        ins, outs = refs[:n], refs[n:2 * n]
        send_sems, recv_sems = refs[2 * n:]
        x, y, c = _my_place()
        cps = [pltpu.make_async_remote_copy(src_ref=ins[i].at[:, 1 - c], dst_ref=outs[i], send_sem=send_sems.at[i],
                                            recv_sem=recv_sems.at[i], device_id=(x, y, 1 - c), device_id_type=MESH)
               for i in range(n)]
        for cp in cps:
            cp.start()
        for cp in cps:
            cp.wait()

    return pl.pallas_call(
        body, name=name, in_specs=[_HBM] * n, out_specs=[_HBM] * n,
        out_shape=[jax.ShapeDtypeStruct((g.shape[0],) + g.shape[2:], g.dtype) for g in gs],
        scratch_shapes=[pltpu.SemaphoreType.DMA((n,)), pltpu.SemaphoreType.DMA((n,))],
    )(*gs)


def _join_halves(fs, name):
    n = len(fs)

    def body(*refs):
        ins, outs = refs[:n], refs[n:2 * n]
        send_sems, recv_sems = refs[2 * n:]
        x, y, c = _my_place()

        def copy(i, half):
            return pltpu.make_async_remote_copy(src_ref=ins[i].at[half], dst_ref=outs[i].at[half],
                                                send_sem=send_sems.at[i], recv_sem=recv_sems.at[i],
                                                device_id=(x, y, 1 - c), device_id_type=MESH)

        sends = [copy(i, c) for i in range(n)]
        for cp in sends:
            cp.start()
        for i in range(n):
            copy(i, 1 - c).wait_recv()
        for cp in sends:
            cp.wait_send()

    return pl.pallas_call(
        body, name=name, in_specs=[_HBM] * n, out_specs=[_HBM] * n,
        out_shape=[jax.ShapeDtypeStruct(f.shape, f.dtype) for f in fs],
        input_output_aliases={i: i for i in range(n)},
        scratch_shapes=[pltpu.SemaphoreType.DMA((n,)), pltpu.SemaphoreType.DMA((n,))],
    )(*fs)


def _row_tile(a, b):
    return _tile(a, max(SUBLANES, (1 << 19) // b // SUBLANES * SUBLANES), SUBLANES)


def _add_core(g, got, name):
    _, _, A, B = g.shape
    ta = _row_tile(A, B)
    return _ew(lambda p, q: (p + q,),
               [(g, (None, None, ta, B), lambda s, i: (s, lax.axis_index("c"), i, 0)),
                (got, (None, ta, B), lambda s, i: (s, i, 0))],
               [(_sds((N_CHIPS, A, B)), (None, ta, B), lambda s, i: (s, i, 0))], (N_CHIPS, A // ta), name=name)[0]


def _add_chips(p, got, name):
    _, A, B = p.shape
    ta = _row_tile(A, B)
    blk = (None, ta, B)
    return _ew(lambda a, b, c_, d: (((a + b) + c_) + d,),
               [(p, blk, lambda i: (_my_chip(), i, 0)), (got, blk, lambda i: (0, i, 0)),
                (got, blk, lambda i: (1, i, 0)), (got, blk, lambda i: (2, i, 0))],
               [(_sds((2, A, B)), blk, lambda i: (lax.axis_index("c"), i, 0))], (A // ta,), name=name)[0]


def _all_reduce_small(v, name):
    r = v.shape[0]
    masks = [(mx, my, mc) for mx in (0, 1) for my in (0, 1) for mc in (0, 1)][1:]

    def body(v_ref, out_ref, gath, send_sems, recv_sems):
        x, y, c = _my_place()
        me = 4 * x + 2 * y + c
        gath[me] = v_ref[...]

        def peer(m):
            return (x + m[0] - 2 * x * m[0], y + m[1] - 2 * y * m[1], c + m[2] - 2 * c * m[2])

        def copy(k, slab, to):
            return pltpu.make_async_remote_copy(src_ref=v_ref, dst_ref=gath.at[slab], send_sem=send_sems.at[k],
                                                recv_sem=recv_sems.at[k], device_id=to, device_id_type=MESH)

        sends = [copy(k, me, peer(m)) for k, m in enumerate(masks)]
        for cp in sends:
            cp.start()
        for k, m in enumerate(masks):
            px, py, pc = peer(m)
            copy(k, 4 * px + 2 * py + pc, (px, py, pc)).wait_recv()
        for cp in sends:
            cp.wait_send()
        total = gath[0]
        for d in range(1, 8):
            total = total + gath[d]
        out_ref[...] = total

    return pl.pallas_call(
        body, name=name, in_specs=[_VMEM], out_specs=_VMEM, out_shape=jax.ShapeDtypeStruct((r, LANES), F32),
        scratch_shapes=[pltpu.VMEM((8, r, LANES), F32), pltpu.SemaphoreType.DMA((7,)), pltpu.SemaphoreType.DMA((7,))],
    )(v)


def _pack_rows(arrs, dtype, row_mult):
    flat = jnp.concatenate([a.astype(dtype).reshape(-1) for a in arrs])
    n = flat.shape[0]
    rows = _round_up(-(-n // LANES), row_mult)
    return jnp.pad(flat, (0, rows * LANES - n)).reshape(rows, LANES)


def _unpack_rows(buf, shapes):
    lead = buf.shape[:-2]
    flat = buf.reshape(lead + (-1,))
    out, o = [], 0
    for s in shapes:
        n = math.prod(s)
        out.append(lax.slice_in_dim(flat, o, o + n, axis=len(lead)).reshape(lead + tuple(s)))
        o += n
    return out


WEIGHTS = ["mla_w_in", "mla_q_norm", "mla_w_uq", "mla_kv_norm", "mla_w_ukv", "gdn_w_in", "gdn_conv", "gdn_a_log",
           "gdn_dt_bias", "gdn_o_norm", "mem_w_kv", "w_out", "ln1_g", "ln1_b", "mlp_w1", "mlp_w2", "ln2_g", "ln2_b"]
SHARD_AXIS = {"mla_w_in": 1, "mla_w_uq": 2, "mla_w_ukv": 2, "gdn_w_in": 2, "gdn_conv": 2, "mem_w_kv": 1, "w_out": 1,
              "mlp_w1": 2, "mlp_w2": 1}
SMALL = [k for k in WEIGHTS if k not in SHARD_AXIS] + ["gdn_conv"]
BIG = [k for k in WEIGHTS if k not in SMALL]
MLA_KEYS = ["mla_w_in", "mla_q_norm", "mla_w_uq", "mla_kv_norm", "mla_w_ukv"]
GDN_KEYS = ["gdn_w_in", "gdn_conv", "gdn_a_log", "gdn_dt_bias", "gdn_o_norm"]
ALL_KEYS = ["mem_w_kv", "w_out", "ln1_g", "ln1_b", "mlp_w1", "mlp_w2", "ln2_g", "ln2_b"]


def _layer_keys(i):
    return (MLA_KEYS if i % 2 == 0 else GDN_KEYS) + ALL_KEYS


def _layer_slot(k, i):
    return i // 2 if k in MLA_KEYS or k in GDN_KEYS else i


def _gather_layer(w, i):
    keys = [k for k in _layer_keys(i) if k in SHARD_AXIS]
    arrs = [w[k][_layer_slot(k, i)].astype(F32 if k == "gdn_conv" else BF16) for k in keys]
    by_rows = [SHARD_AXIS[k] == 1 for k in keys]

    def arrived(outs):
        outs = _gather_fill(outs, arrs, by_rows, "gather_fill_" + ("mla" if i % 2 == 0 else "gdn"))
        full = {k: w[k][_layer_slot(k, i)] for k in _layer_keys(i) if k not in SHARD_AXIS}
        for k, o in zip(keys, outs):
            by_cols = SHARD_AXIS[k] == 2 and k != "mlp_w1"
            full[k] = jnp.concatenate([o[d] for d in range(N_CHIPS)], axis=1) if by_cols else o
        return full

    return _gather_plan(arrs, by_rows), arrived


EARLY = ["mlp_w1", "mlp_w2", "w_out", "mem_w_kv"]


def _reduce_group(i, G, early):
    kind = ("mla" if i % 2 == 0 else "gdn") + ("_early" if early else "_late")
    keys = [k for k in _layer_keys(i) if k in BIG and (k in EARLY) == early]
    canon = []
    for k in keys:
        g = G[k]
        if k == "mlp_w1":
            g = g.reshape(N_CHIPS, 2, g.shape[1] // 2, g.shape[2])
        elif SHARD_AXIS[k] == 1:
            g = g.reshape(N_CHIPS, 2, g.shape[0] // (2 * N_CHIPS), g.shape[1])
        else:
            rows, cw = g.shape[0], g.shape[1] // N_CHIPS
            g = g.reshape(rows, N_CHIPS, cw).transpose(1, 0, 2).reshape(N_CHIPS, 2, rows // 2, cw)
        canon.append(g)
    theirs = _swap_halves(canon, "grad_swap_" + kind)
    chip_sums = [_add_core(g, t, "grad_add_core") for g, t in zip(canon, theirs)]

    def done(got):
        halves = [_add_chips(p, s, "grad_add_chips") for p, s in zip(chip_sums, got)]
        joined = _join_halves(halves, "grad_join_" + kind)
        return {k: j.reshape(2 * j.shape[1], j.shape[2]) for k, j in zip(keys, joined)}

    return _scatter_plan(chip_sums), done


def _adamw(w, g, m, v, name):
    shape = w.shape
    cols = shape[-1]
    rows = math.prod(shape[:-1])
    tr = _tile(rows, max(SUBLANES, (1 << 19) // cols // SUBLANES * SUBLANES), SUBLANES)
    spec = ((tr, cols), _row)
    outs = _ew(_adamw_fn, [(a.reshape(rows, cols), *spec) for a in (w, g, m, v)], [(_sds((rows, cols)), *spec)] * 3,
               (rows // tr,), name=name)
    return [o.reshape(shape) for o in outs]


def kernel(x, mem, positions, mla_w_in, mla_q_norm, mla_w_uq, mla_kv_norm, mla_w_ukv, gdn_w_in, gdn_conv, gdn_a_log, gdn_dt_bias, gdn_o_norm, mem_w_kv, w_out, ln1_g, ln1_b, mlp_w1, mlp_w2, ln2_g, ln2_b, loss_target, m_mla_w_in, m_mla_q_norm, m_mla_w_uq, m_mla_kv_norm, m_mla_w_ukv, m_gdn_w_in, m_gdn_conv, m_gdn_a_log, m_gdn_dt_bias, m_gdn_o_norm, m_mem_w_kv, m_w_out, m_ln1_g, m_ln1_b, m_mlp_w1, m_mlp_w2, m_ln2_g, m_ln2_b, v_mla_w_in, v_mla_q_norm, v_mla_w_uq, v_mla_kv_norm, v_mla_w_ukv, v_gdn_w_in, v_gdn_conv, v_gdn_a_log, v_gdn_dt_bias, v_gdn_o_norm, v_mem_w_kv, v_w_out, v_ln1_g, v_ln1_b, v_mlp_w1, v_mlp_w2, v_ln2_g, v_ln2_b):
    w = dict(zip(WEIGHTS, (mla_w_in, mla_q_norm, mla_w_uq, mla_kv_norm, mla_w_ukv, gdn_w_in, gdn_conv, gdn_a_log,
                           gdn_dt_bias, gdn_o_norm, mem_w_kv, w_out, ln1_g, ln1_b, mlp_w1, mlp_w2, ln2_g, ln2_b)))
    m = dict(zip(WEIGHTS, (m_mla_w_in, m_mla_q_norm, m_mla_w_uq, m_mla_kv_norm, m_mla_w_ukv, m_gdn_w_in, m_gdn_conv,
                           m_gdn_a_log, m_gdn_dt_bias, m_gdn_o_norm, m_mem_w_kv, m_w_out, m_ln1_g, m_ln1_b, m_mlp_w1,
                           m_mlp_w2, m_ln2_g, m_ln2_b)))
    v = dict(zip(WEIGHTS, (v_mla_w_in, v_mla_q_norm, v_mla_w_uq, v_mla_kv_norm, v_mla_w_ukv, v_gdn_w_in, v_gdn_conv,
                           v_gdn_a_log, v_gdn_dt_bias, v_gdn_o_norm, v_mem_w_kv, v_w_out, v_ln1_g, v_ln1_b, v_mlp_w1,
                           v_mlp_w2, v_ln2_g, v_ln2_b)))
    assert x.shape[0] == 1, "one sequence per device"
    full_shapes = {k: w[k].shape for k in WEIGHTS}
    for k, ax in SHARD_AXIS.items():
        s = list(w[k].shape)
        s[ax] *= N_CHIPS
        full_shapes[k] = tuple(s)
    c = _dims(x.shape[1], x.shape[2], mem.shape[1], full_shapes)
    depth = c["DEPTH"]

    first, arrived = _gather_layer(w, 0)
    W = [arrived(_run_plan(first, "gather_first"))]
    loss_local, grad_x, G = _local_step(x[0], mem[0], positions[0], W, loss_target[0], c,
                                        next_weights=lambda i: _gather_layer(w, i), grad_sink=_reduce_group)
    loss = lax.psum(loss_local, ("x", "y", "c"))

    def stacked(k):
        return jnp.stack([G[i][k] for i in range(depth) if k in G[i]], axis=0)

    grads = {k: stacked(k) for k in BIG}
    small_shapes = [full_shapes[k] for k in SMALL]
    gsmall = _all_reduce_small(_pack_rows([stacked(k) for k in SMALL], F32, SUBLANES), "grad_all_reduce_small")
    grads.update(dict(zip(SMALL, _unpack_rows(gsmall, small_shapes))))
    conv_cols = w["gdn_conv"].shape[2]
    grads["gdn_conv"] = lax.dynamic_slice_in_dim(grads["gdn_conv"], _my_chip() * conv_cols, conv_cols, axis=2)

    delta, new_m, new_v = {}, {}, {}
    for k in BIG + ["gdn_conv"]:
        delta[k], new_m[k], new_v[k] = _adamw(w[k], grads[k], m[k], v[k], "adamw")
    small = [k for k in SMALL if k != "gdn_conv"]
    packed = [_pack_rows([d[k] for k in small], F32, SUBLANES) for d in (w, grads, m, v)]
    ds, ms, vs = _adamw(*packed, "adamw_small")
    for d, buf in ((delta, ds), (new_m, ms), (new_v, vs)):
        d.update(dict(zip(small, _unpack_rows(buf, [w[k].shape for k in small]))))

    return (loss, grad_x[None], *[grads[k] for k in WEIGHTS], *[delta[k] for k in WEIGHTS],
            *[new_m[k] for k in WEIGHTS], *[new_v[k] for k in WEIGHTS])
```

```python
import functools
import math

import jax
import jax.numpy as jnp
from jax import lax
from jax.experimental import pallas as pl
from jax.experimental.pallas import tpu as pltpu

F32 = jnp.float32
BF16 = jnp.bfloat16
MESH = pl.DeviceIdType.MESH

LANES = 128
SUBLANES = 8
VMEM_LIMIT = 56 * 1024 * 1024
N_CHIPS = 4

HEAD_DIM = 128
QK_NOPE = 128
QK_ROPE = 64
QK_PAD = 256
ROPE_THETA = 10000.0
CONV_WIDTH = 4
CHUNK = 64
LN_EPS = 1e-5
RMS_EPS = 1e-6
ADAM_LR = 0.001
ADAM_B1 = 0.9
ADAM_B2 = 0.999
ADAM_EPS = 1e-08
ADAM_WD = 0.01
ADAM_STEP = 10
HI = lax.Precision.HIGHEST


def _cparams(sem=None):
    return pltpu.CompilerParams(dimension_semantics=sem, vmem_limit_bytes=VMEM_LIMIT)


def _tile(n, cap, unit):
    best = None
    t = unit
    while t <= min(n, cap):
        if n % t == 0:
            best = t
        t += unit
    return best if best is not None else n


def _mm(a, b, mode, out_dtypes, *, name, epilogue=None, extras=(), tm_cap=1024, tn_cap=1024, tk_cap=2048,
        b_major=False, out_major=False):
    if b_major:
        b_shape = (b.shape[1], N_CHIPS * b.shape[2])
    else:
        b_shape = b.shape
    if mode == "nn":
        (M, K), (K2, N) = a.shape, b_shape
    elif mode == "nt":
        (M, K), (N, K2) = a.shape, b_shape
    else:
        (K, M), (K2, N) = a.shape, b_shape
    assert K == K2, (a.shape, b.shape, mode)
    tm = _tile(M, tm_cap, LANES if mode == "tn" else 16)
    tn = _tile(N // N_CHIPS if (out_major or (b_major and mode == "nn")) else N, tn_cap, LANES)
    tk = _tile(K // N_CHIPS if (b_major and mode == "nt") else K, tk_cap, 16 if mode == "tn" else LANES)
    nk = K // tk
    nj4, nk4 = max(N // N_CHIPS // tn, 1), max(K // N_CHIPS // tk, 1)
    if mode == "nn":
        a_spec = pl.BlockSpec((tm, tk), lambda i, j, k: (i, k))
        b_spec = pl.BlockSpec((tk, tn), lambda i, j, k: (k, j))
        if b_major:
            b_spec = pl.BlockSpec((None, tk, tn), lambda i, j, k: (j // nj4, k, j % nj4))
        dims = (((1,), (0,)), ((), ()))
    elif mode == "nt":
        a_spec = pl.BlockSpec((tm, tk), lambda i, j, k: (i, k))
        b_spec = pl.BlockSpec((tn, tk), lambda i, j, k: (j, k))
        if b_major:
            b_spec = pl.BlockSpec((None, tn, tk), lambda i, j, k: (k // nk4, j, k % nk4))
        dims = (((1,), (1,)), ((), ()))
    else:
        assert not b_major
        a_spec = pl.BlockSpec((tk, tm), lambda i, j, k: (k, i))
        b_spec = pl.BlockSpec((tk, tn), lambda i, j, k: (k, j))
        dims = (((0,), (0,)), ((), ()))
    mn_spec = pl.BlockSpec((tm, tn), lambda i, j, k: (i, j))
    o_spec, o_shape = mn_spec, (M, N)
    if out_major:
        o_spec = pl.BlockSpec((None, tm, tn), lambda i, j, k: (j // nj4, i, j % nj4))
        o_shape = (N_CHIPS, M, N // N_CHIPS)
    n_ex, n_out = len(extras), len(out_dtypes)
    for e in extras:
        assert e.shape == (M, N), (e.shape, M, N)

    def body(a_ref, b_ref, *rest):
        ex_refs, out_refs, acc = rest[:n_ex], rest[n_ex:n_ex + n_out], rest[-1]
        k = pl.program_id(2)

        @pl.when(k == 0)
        def _():
            acc[...] = jnp.zeros_like(acc)

        acc[...] += lax.dot_general(a_ref[...].astype(BF16), b_ref[...].astype(BF16), dims,
                                    preferred_element_type=F32)

        @pl.when(k == nk - 1)
        def _():
            res = (acc[...],) if epilogue is None else epilogue(acc[...], *[e[...] for e in ex_refs])
            for o_ref, r in zip(out_refs, res):
                o_ref[...] = r.astype(o_ref.dtype)

    outs = pl.pallas_call(
        body, name=name, grid=(M // tm, N // tn, nk),
        in_specs=[a_spec, b_spec] + [mn_spec] * n_ex,
        out_specs=[o_spec] * n_out,
        out_shape=[jax.ShapeDtypeStruct(o_shape, d) for d in out_dtypes],
        scratch_shapes=[pltpu.VMEM((tm, tn), F32)],
        compiler_params=_cparams(("parallel", "parallel", "arbitrary")),
    )(a, b, *extras)
    return outs


def _spec(block, imap):
    return pl.BlockSpec(block, imap)


def _ew(fn, ins, outs, grid, *, name, acc_out=()):
    n_in = len(ins)
    ng = len(grid)

    def body(*refs):
        in_refs, out_refs = refs[:n_in], refs[n_in:]
        res = fn(*[r[...] for r in in_refs])
        first = functools.reduce(jnp.logical_and, [pl.program_id(d) == 0 for d in range(ng)])
        for i, (o_ref, r) in enumerate(zip(out_refs, res)):
            if i in acc_out:
                @pl.when(first)
                def _(o_ref=o_ref):
                    o_ref[...] = jnp.zeros_like(o_ref)
                o_ref[...] += r.astype(o_ref.dtype)
            else:
                o_ref[...] = r.astype(o_ref.dtype)

    return pl.pallas_call(
        body, name=name, grid=grid,
        in_specs=[_spec(b, m) for _, b, m in ins],
        out_specs=[_spec(b, m) for _, b, m in outs],
        out_shape=[s for s, _, _ in outs],
        compiler_params=_cparams(("arbitrary",) * ng),
    )(*[a for a, _, _ in ins])


def _ew_vjp(fn, ins, cts, gouts, grid, *, name):
    n_in, n_ct = len(ins), len(cts)
    ng = len(grid)
    want = [i for i, g in enumerate(gouts) if g is not None]

    def body(*refs):
        in_refs, ct_refs, out_refs = refs[:n_in], refs[n_in:n_in + n_ct], refs[n_in + n_ct:]
        prim = [r[...] for r in in_refs]
        outs, pull = jax.vjp(fn, *prim)
        grads = pull(tuple(r[...].astype(o.dtype) for r, o in zip(ct_refs, outs)))
        first_all = functools.reduce(jnp.logical_and, [pl.program_id(d) == 0 for d in range(ng)])
        for o_ref, i in zip(out_refs, want):
            mode = gouts[i][3]
            g = grads[i]
            if mode == "set":
                o_ref[...] = g.astype(o_ref.dtype)
            elif mode == "acc":
                @pl.when(pl.program_id(ng - 1) == 0)
                def _(o_ref=o_ref):
                    o_ref[...] = jnp.zeros_like(o_ref)
                o_ref[...] += g.astype(o_ref.dtype)
            elif mode == "acc_all":
                @pl.when(first_all)
                def _(o_ref=o_ref):
                    o_ref[...] = jnp.zeros_like(o_ref)
                o_ref[...] += g.astype(o_ref.dtype)
            else:
                @pl.when(first_all)
                def _(o_ref=o_ref):
                    o_ref[...] = jnp.zeros_like(o_ref)
                idx = pl.program_id(mode[1])
                o_ref[idx] += g.astype(o_ref.dtype)

    return pl.pallas_call(
        body, name=name, grid=grid,
        in_specs=[_spec(b, m) for _, b, m in ins] + [_spec(b, m) for _, b, m in cts],
        out_specs=[_spec(gouts[i][1], gouts[i][2]) for i in want],
        out_shape=[gouts[i][0] for i in want],
        compiler_params=_cparams(("arbitrary",) * ng),
    )(*[a for a, _, _ in ins], *[a for a, _, _ in cts])


def _sds(shape, dtype=F32):
    return jax.ShapeDtypeStruct(tuple(shape), dtype)


def _ln_fn(z, g, b):
    mu = jnp.mean(z, -1, keepdims=True)
    d = z - mu
    var = jnp.mean(d * d, -1, keepdims=True)
    y = d * lax.rsqrt(var + LN_EPS) * g + b
    return y, y


def _rms_fn(x, g):
    return (x * lax.rsqrt(jnp.mean(x * x, -1, keepdims=True) + RMS_EPS) * g,)


@jax.custom_vjp
def _rot_half(x):
    lane = lax.broadcasted_iota(jnp.int32, x.shape, x.ndim - 1)
    up = pltpu.roll(x, LANES - QK_ROPE // 2, x.ndim - 1)
    dn = pltpu.roll(x, QK_ROPE // 2, x.ndim - 1)
    return jnp.where(lane < QK_ROPE // 2, -up, jnp.where(lane < QK_ROPE, dn, 0.0))


def _rot_half_fwd(x):
    return _rot_half(x), None


def _rot_half_bwd(_, ct):
    return (-_rot_half(ct),)


_rot_half.defvjp(_rot_half_fwd, _rot_half_bwd)


def _rope_blk(x, cos, sin):
    return x * cos + _rot_half(x) * sin


def _mla_prep_fn(qraw, knope, kr, cos, sin):
    qn, qr = qraw[:, :QK_NOPE], qraw[:, QK_NOPE:]
    q = jnp.concatenate([qn, _rope_blk(qr, cos, sin)], axis=1)
    k = jnp.concatenate([knope.astype(F32), _rope_blk(kr, cos, sin)], axis=1)
    return q, k


def _l2n(x):
    return x * lax.rsqrt(jnp.sum(x * x, -1, keepdims=True) + 1e-6)


def _gdn_qk_fn(qc, kc):
    return _l2n(qc) * (HEAD_DIM ** -0.5), _l2n(kc)


def _softplus(x):
    return jnp.maximum(x, 0.0) + jnp.log(1.0 + jnp.exp(-jnp.abs(x)))


def _sigmoid(x):
    return 1.0 / (1.0 + jnp.exp(-x))


def _silu(x):
    return x * _sigmoid(x)


def _gdn_gate_fn(n_heads, head_axis):
    def fn(ab, a_log, dt_bias):
        h = pl.program_id(head_axis)
        lane = lax.broadcasted_iota(jnp.int32, ab.shape, 1)
        a_in = jnp.sum(jnp.where(lane == h, ab, 0.0), -1, keepdims=True)
        b_in = jnp.sum(jnp.where(lane == h + n_heads, ab, 0.0), -1, keepdims=True)
        g = -jnp.exp(a_log[:, :CHUNK]) * _softplus(a_in + dt_bias[:, :CHUNK])
        beta = _sigmoid(b_in) + jnp.zeros_like(g)
        return g, beta
    return fn


def _gdn_out_fn(o, z, w):
    return (o * lax.rsqrt(jnp.mean(o * o, -1, keepdims=True) + RMS_EPS) * w * _silu(z),)


def _loss_fn(y, t):
    d = y - t
    return (jnp.sum(d * d, axis=0, keepdims=True) * (0.5 / y.shape[-1]), d * (1.0 / y.shape[-1]))


def _adamw_fn(w, g, m, v):
    m = ADAM_B1 * m + (1.0 - ADAM_B1) * g
    v = ADAM_B2 * v + (1.0 - ADAM_B2) * (g * g)
    m_hat = m / (1.0 - ADAM_B1 ** ADAM_STEP)
    v_hat = v / (1.0 - ADAM_B2 ** ADAM_STEP)
    delta = -ADAM_LR * (m_hat / (jnp.sqrt(v_hat) + ADAM_EPS) + ADAM_WD * w)
    return delta, m, v


def _causal_mask(shape, row0, col0):
    row = lax.broadcasted_iota(jnp.int32, shape, 0) + row0
    col = lax.broadcasted_iota(jnp.int32, shape, 1) + col0
    return col <= row


def _rows(ref, i, t):
    return ref[pl.ds(pl.multiple_of(i * t, t), t), :]


def _walk(first, n_loop, tail, products, update):
    stop = first + n_loop
    t0 = tail[0][0]

    def step(j, carry):
        nxt = products(jnp.where(j + 1 < stop, j + 1, t0))
        update(carry, j, False)
        return nxt

    carry = lax.fori_loop(first, stop, step, products(jnp.where(n_loop > 0, first, t0)))
    for n, (j, masked) in enumerate(tail):
        nxt = products(tail[n + 1][0]) if n + 1 < len(tail) else None
        update(carry, j, masked)
        carry = nxt


def _flash_fwd(q, k, v, *, H, dq, dv, qoff, koff, voff, causal, scale, tq, tk, name, side=None):
    S, Sk = q.shape[0], k.shape[0]
    nq = S // tq
    assert (tq == tk and S == Sk) or not causal

    def body(q_ref, k_ref, v_ref, o_ref, lse_ref, m_s, l_s, acc):
        qi = pl.program_id(1)
        m_s[...] = jnp.full_like(m_s, -jnp.inf)
        l_s[...] = jnp.zeros_like(l_s)
        acc[...] = jnp.zeros_like(acc)
        qb = q_ref[...].astype(BF16)

        def products(j):
            return lax.dot_general(qb, _rows(k_ref, j, tk).astype(BF16), (((1,), (1,)), ((), ())),
                                   preferred_element_type=F32)

        def update(s, j, masked):
            s = s * scale
            if masked:
                s = jnp.where(_causal_mask(s.shape, qi * tq, j * tk), s, -jnp.inf)
            m_prev = m_s[...]
            m_new = jnp.maximum(m_prev, jnp.max(s, axis=1, keepdims=True))
            alpha = jnp.exp(m_prev - m_new)
            p = jnp.exp(s - m_new[:, :1])
            l_s[...] = alpha * l_s[...] + jnp.sum(p, axis=1, keepdims=True)
            acc[...] = acc[...] * alpha[:, :1] + lax.dot_general(
                p.astype(BF16), _rows(v_ref, j, tk).astype(BF16), (((1,), (0,)), ((), ())), preferred_element_type=F32)
            m_s[...] = m_new

        if causal:
            _walk(0, qi, [(qi, True)], products, update)
        else:
            _walk(0, Sk // tk - 1, [(Sk // tk - 1, False)], products, update)
        o_ref[...] = (acc[...] / l_s[...][:, :1]).astype(o_ref.dtype)
        lse_ref[...] = m_s[...] + jnp.log(l_s[...])

    body, s_in, s_out, s_shape, s_sems, s_args = _with_side(body, 3, 2, (H, nq), side)
    o, lse, *side_outs = pl.pallas_call(
        body, name=name, grid=(H, nq),
        in_specs=[pl.BlockSpec((tq, dq), lambda h, qi: (qi, qoff + h)),
                  pl.BlockSpec((Sk, dq), lambda h, qi: (0, koff + h)),
                  pl.BlockSpec((Sk, dv), lambda h, qi: (0, voff + h))] + s_in,
        out_specs=[pl.BlockSpec((tq, dv), lambda h, qi: (qi, h)),
                   pl.BlockSpec((tq, LANES), lambda h, qi: (qi, h))] + s_out,
        out_shape=[_sds((S, H * dv)), _sds((S, H * LANES))] + s_shape,
        scratch_shapes=[pltpu.VMEM((tq, LANES), F32), pltpu.VMEM((tq, LANES), F32), pltpu.VMEM((tq, dv), F32)] + s_sems,
        compiler_params=_cparams(("arbitrary", "arbitrary")),
    )(q, k, v, *s_args)
    return o, lse, side_outs


def _flash_bwd(q, k, v, o, lse, do, *, H, dq, dv, qoff, koff, voff, dooff, causal, scale, tq, tk, name, side=None):
    S, Sk = q.shape[0], k.shape[0]
    nq, nk = S // tq, Sk // tk
    assert (tq == tk and S == Sk) or not causal
    nt = (((1,), (1,)), ((), ()))

    def body_q(q_ref, k_ref, v_ref, o_ref, do_ref, lse_ref, dq_ref, delta_ref, dq_acc):
        qi = pl.program_id(1)
        qb, dob = q_ref[...].astype(BF16), do_ref[...].astype(BF16)
        delta = jnp.sum(do_ref[...].astype(F32) * o_ref[...].astype(F32), axis=1, keepdims=True)
        delta_ref[...] = delta + jnp.zeros_like(delta_ref)
        lse1 = lse_ref[...][:, :1]
        dq_acc[...] = jnp.zeros_like(dq_acc)

        def products(j):
            return (lax.dot_general(qb, _rows(k_ref, j, tk).astype(BF16), nt, preferred_element_type=F32),
                    lax.dot_general(dob, _rows(v_ref, j, tk).astype(BF16), nt, preferred_element_type=F32))

        def update(sp, j, masked):
            s, dp = sp
            p = jnp.exp(s * scale - lse1)
            if masked:
                p = jnp.where(_causal_mask(s.shape, qi * tq, j * tk), p, 0.0)
            ds = p * (dp - delta) * scale
            dq_acc[...] += lax.dot_general(ds.astype(BF16), _rows(k_ref, j, tk).astype(BF16), (((1,), (0,)), ((), ())),
                                           preferred_element_type=F32)

        if causal:
            _walk(0, qi, [(qi, True)], products, update)
        else:
            _walk(0, nk - 1, [(nk - 1, False)], products, update)
        dq_ref[...] = dq_acc[...]

    body_q, s_in, s_out, s_shape, s_sems, s_args = _with_side(body_q, 6, 2, (H, nq), side)
    dqq, delta, *side_outs = pl.pallas_call(
        body_q, name=name + "_dq", grid=(H, nq),
        in_specs=[pl.BlockSpec((tq, dq), lambda h, qi: (qi, qoff + h)),
                  pl.BlockSpec((Sk, dq), lambda h, qi: (0, koff + h)),
                  pl.BlockSpec((Sk, dv), lambda h, qi: (0, voff + h)),
                  pl.BlockSpec((tq, dv), lambda h, qi: (qi, h)),
                  pl.BlockSpec((tq, dv), lambda h, qi: (qi, dooff + h)),
                  pl.BlockSpec((tq, LANES), lambda h, qi: (qi, h))] + s_in,
        out_specs=[pl.BlockSpec((tq, dq), lambda h, qi: (qi, h)),
                   pl.BlockSpec((tq, LANES), lambda h, qi: (qi, h))] + s_out,
        out_shape=[_sds((S, H * dq)), _sds((S, H * LANES))] + s_shape,
        scratch_shapes=[pltpu.VMEM((tq, dq), F32)] + s_sems,
        compiler_params=_cparams(("arbitrary", "arbitrary")),
    )(q, k, v, o, do, lse, *s_args)

    def as_rows(t):
        return t[:, ::LANES].T.reshape(H, nq, 1, tq)

    nn = (((1,), (0,)), ((), ()))

    def body_kv(q_ref, k_ref, v_ref, do_ref, lse_ref, delta_ref, dk_ref, dv_ref, dk_acc, dv_acc):
        kj = pl.program_id(1)
        kb, vb = k_ref[...].astype(BF16), v_ref[...].astype(BF16)
        dk_acc[...] = jnp.zeros_like(dk_acc)
        dv_acc[...] = jnp.zeros_like(dv_acc)

        def products(i):
            return (lax.dot_general(kb, _rows(q_ref, i, tq).astype(BF16), nt, preferred_element_type=F32),
                    lax.dot_general(vb, _rows(do_ref, i, tq).astype(BF16), nt, preferred_element_type=F32))

        def update(sp, i, masked):
            st, dpt = sp
            pt = jnp.exp(st * scale - lse_ref[i])
            if masked:
                key = lax.broadcasted_iota(jnp.int32, st.shape, 0) + kj * tk
                qry = lax.broadcasted_iota(jnp.int32, st.shape, 1) + i * tq
                pt = jnp.where(key <= qry, pt, 0.0)
            dst = pt * (dpt - delta_ref[i]) * scale
            dv_acc[...] += lax.dot_general(pt.astype(BF16), _rows(do_ref, i, tq).astype(BF16), nn,
                                           preferred_element_type=F32)
            dk_acc[...] += lax.dot_general(dst.astype(BF16), _rows(q_ref, i, tq).astype(BF16), nn,
                                           preferred_element_type=F32)

        if causal:
            _walk(kj + 1, nq - 1 - kj, [(kj, True)], products, update)
        else:
            _walk(0, nq - 1, [(nq - 1, False)], products, update)
        dk_ref[...] = dk_acc[...]
        dv_ref[...] = dv_acc[...]

    row_spec = pl.BlockSpec((None, nq, 1, tq), lambda h, kj: (h, 0, 0, 0))
    dk, dvv = pl.pallas_call(
        body_kv, name=name + "_dkv", grid=(H, nk),
        in_specs=[pl.BlockSpec((S, dq), lambda h, kj: (0, qoff + h)),
                  pl.BlockSpec((tk, dq), lambda h, kj: (kj, koff + h)),
                  pl.BlockSpec((tk, dv), lambda h, kj: (kj, voff + h)),
                  pl.BlockSpec((S, dv), lambda h, kj: (0, dooff + h)), row_spec, row_spec],
        out_specs=[pl.BlockSpec((tk, dq), lambda h, kj: (kj, h)), pl.BlockSpec((tk, dv), lambda h, kj: (kj, h))],
        out_shape=[_sds((Sk, H * dq)), _sds((Sk, H * dv))],
        scratch_shapes=[pltpu.VMEM((tk, dq), F32), pltpu.VMEM((tk, dv), F32)],
        compiler_params=_cparams(("parallel", "arbitrary")),
    )(q, k, v, do, as_rows(lse), as_rows(delta))
    return dqq, dk, dvv, side_outs


def _flash_bwd_causal(q, k, v, o, lse, do, *, H, dq, dv, qoff, koff, voff, dooff, scale, tq, name, side=None):
    S = q.shape[0]
    nq = S // tq
    nt = (((1,), (1,)), ((), ()))
    tn = (((0,), (0,)), ((), ()))
    nn = (((1,), (0,)), ((), ()))

    def body(q_ref, k_ref, v_ref, o_ref, do_ref, lse_ref, dq_ref, dk_ref, dv_ref, dq_acc):
        qi = pl.program_id(1)

        @pl.when(qi == 0)
        def _():
            dk_ref[...] = jnp.zeros_like(dk_ref)
            dv_ref[...] = jnp.zeros_like(dv_ref)

        qb, dob = q_ref[...].astype(BF16), do_ref[...].astype(BF16)
        delta = jnp.sum(do_ref[...].astype(F32) * o_ref[...].astype(F32), axis=1, keepdims=True)
        lse1 = lse_ref[...][:, :1]
        dq_acc[...] = jnp.zeros_like(dq_acc)

        def products(j):
            return (lax.dot_general(qb, _rows(k_ref, j, tq).astype(BF16), nt, preferred_element_type=F32),
                    lax.dot_general(dob, _rows(v_ref, j, tq).astype(BF16), nt, preferred_element_type=F32))

        def update(sp, j, masked):
            s, dp = sp
            p = jnp.exp(s * scale - lse1)
            if masked:
                p = jnp.where(_causal_mask(s.shape, qi * tq, j * tq), p, 0.0)
            ds = (p * (dp - delta) * scale).astype(BF16)
            dq_acc[...] += lax.dot_general(ds, _rows(k_ref, j, tq).astype(BF16), nn, preferred_element_type=F32)
            rows = pl.ds(pl.multiple_of(j * tq, tq), tq)
            dk_ref[rows, :] += lax.dot_general(ds, qb, tn, preferred_element_type=F32)
            dv_ref[rows, :] += lax.dot_general(p.astype(BF16), dob, tn, preferred_element_type=F32)

        _walk(0, qi, [(qi, True)], products, update)
        dq_ref[...] = dq_acc[...]

    body, s_in, s_out, s_shape, s_sems, s_args = _with_side(body, 6, 3, (H, nq), side)
    dqq, dk, dvv, *side_outs = pl.pallas_call(
        body, name=name, grid=(H, nq),
        in_specs=[pl.BlockSpec((tq, dq), lambda h, qi: (qi, qoff + h)),
                  pl.BlockSpec((S, dq), lambda h, qi: (0, koff + h)),
                  pl.BlockSpec((S, dv), lambda h, qi: (0, voff + h)),
                  pl.BlockSpec((tq, dv), lambda h, qi: (qi, h)),
                  pl.BlockSpec((tq, dv), lambda h, qi: (qi, dooff + h)),
                  pl.BlockSpec((tq, LANES), lambda h, qi: (qi, h))] + s_in,
        out_specs=[pl.BlockSpec((tq, dq), lambda h, qi: (qi, h)),
                   pl.BlockSpec((S, dq), lambda h, qi: (0, h)), pl.BlockSpec((S, dv), lambda h, qi: (0, h))] + s_out,
        out_shape=[_sds((S, H * dq)), _sds((S, H * dq)), _sds((S, H * dv))] + s_shape,
        scratch_shapes=[pltpu.VMEM((tq, dq), F32)] + s_sems,
        compiler_params=_cparams(("arbitrary", "arbitrary")),
    )(q, k, v, o, do, lse, *s_args)
    return dqq, dk, dvv, side_outs


def _shift_down(x, prev8, j):
    if j == 0:
        return x
    y = pltpu.roll(x, j, 0)
    head = pltpu.roll(prev8, j, 0)
    row = lax.broadcasted_iota(jnp.int32, x.shape, 0)
    reps = x.shape[0] // SUBLANES
    return jnp.where(row < j, jnp.tile(head, (reps, 1)), y)


def _shift_up(x, next8, j):
    if j == 0:
        return x
    n = x.shape[0]
    y = pltpu.roll(x, n - j, 0)
    tail = pltpu.roll(next8, SUBLANES - j, 0)
    row = lax.broadcasted_iota(jnp.int32, x.shape, 0)
    reps = n // SUBLANES
    return jnp.where(row >= n - j, jnp.tile(tail, (reps, 1)), y)


def _conv_pre(x_ref, p_ref, w_ref, first):
    x = x_ref[...]
    prev8 = jnp.where(first, 0.0, p_ref[...])
    w = w_ref[...]
    xs = [_shift_down(x, prev8, CONV_WIDTH - 1 - j) for j in range(CONV_WIDTH)]
    c = sum(xs[j] * w[j:j + 1, :] for j in range(CONV_WIDTH))
    return c, xs


def _conv_specs(ts, tc, C_total_blocks_off):
    rb = ts // SUBLANES
    off = C_total_blocks_off
    x_spec = pl.BlockSpec((ts, tc), lambda ci, i: (i, off + ci))
    p_spec = pl.BlockSpec((SUBLANES, tc), lambda ci, i: (jnp.maximum(i * rb - 1, 0), off + ci))
    return x_spec, p_spec


def _conv_fwd(h, w, *, C, ts, tc, name):
    S = h.shape[0]
    x_spec, p_spec = _conv_specs(ts, tc, 0)

    def body(x_ref, p_ref, w_ref, y_ref):
        c, _ = _conv_pre(x_ref, p_ref, w_ref, pl.program_id(1) == 0)
        y_ref[...] = _silu(c)

    return pl.pallas_call(
        body, name=name, grid=(C // tc, S // ts),
        in_specs=[x_spec, p_spec, pl.BlockSpec((CONV_WIDTH, tc), lambda ci, i: (0, ci))],
        out_specs=pl.BlockSpec((ts, tc), lambda ci, i: (i, ci)),
        out_shape=_sds((S, C)),
        compiler_params=_cparams(("parallel", "arbitrary")),
    )(h, h, w)


def _conv_bwd(h, w, dy, *, C, ts, tc, name):
    S = h.shape[0]
    ns = S // ts
    rb = ts // SUBLANES
    x_spec, p_spec = _conv_specs(ts, tc, 0)

    def body_a(x_ref, p_ref, w_ref, dy_ref, dc_ref, dw_ref):
        i = pl.program_id(1)
        c, xs = _conv_pre(x_ref, p_ref, w_ref, i == 0)
        sg = _sigmoid(c)
        dc = dy_ref[...] * (sg * (1.0 + c * (1.0 - sg)))
        dc_ref[...] = dc

        @pl.when(i == 0)
        def _():
            dw_ref[...] = jnp.zeros_like(dw_ref)

        dw_ref[...] += jnp.concatenate([jnp.sum(dc * xs[j], axis=0, keepdims=True) for j in range(CONV_WIDTH)], axis=0)

    dc, dw = pl.pallas_call(
        body_a, name=name + "_a", grid=(C // tc, ns),
        in_specs=[x_spec, p_spec, pl.BlockSpec((CONV_WIDTH, tc), lambda ci, i: (0, ci)),
                  pl.BlockSpec((ts, tc), lambda ci, i: (i, ci))],
        out_specs=[pl.BlockSpec((ts, tc), lambda ci, i: (i, ci)),
                   pl.BlockSpec((CONV_WIDTH, tc), lambda ci, i: (0, ci))],
        out_shape=[_sds((S, C)), _sds((CONV_WIDTH, C))],
        compiler_params=_cparams(("parallel", "arbitrary")),
    )(h, h, w, dy)

    def body_b(dc_ref, n_ref, w_ref, dx_ref):
        i = pl.program_id(1)
        dcv = dc_ref[...]
        next8 = jnp.where(i == ns - 1, 0.0, n_ref[...])
        w_ = w_ref[...]
        dx_ref[...] = sum(_shift_up(dcv, next8, CONV_WIDTH - 1 - j) * w_[j:j + 1, :] for j in range(CONV_WIDTH))

    dx = pl.pallas_call(
        body_b, name=name + "_b", grid=(C // tc, ns),
        in_specs=[pl.BlockSpec((ts, tc), lambda ci, i: (i, ci)),
                  pl.BlockSpec((SUBLANES, tc), lambda ci, i: (jnp.minimum((i + 1) * rb, ns * rb - 1), ci)),
                  pl.BlockSpec((CONV_WIDTH, tc), lambda ci, i: (0, ci))],
        out_specs=pl.BlockSpec((ts, tc), lambda ci, i: (i, ci)),
        out_shape=_sds((S, C)),
        compiler_params=_cparams(("parallel", "arbitrary")),
    )(dc, dc, w)
    return dx, dw


def _bdot(a, b, ca, cb, precision=None):
    nb = a.ndim - 2
    batch = tuple(range(nb))
    return lax.dot_general(a, b, (((nb + ca,), (nb + cb,)), (batch, batch)), precision=precision,
                           preferred_element_type=F32)


@jax.custom_vjp
def _nn(a, b):
    return _bdot(a.astype(BF16), b.astype(BF16), 1, 0)


@jax.custom_vjp
def _nt(a, b):
    return _bdot(a.astype(BF16), b.astype(BF16), 1, 1)


@jax.custom_vjp
def _tn(a, b):
    return _bdot(a.astype(BF16), b.astype(BF16), 0, 0)


_nn.defvjp(lambda a, b: (_nn(a, b), (a, b)), lambda r, g: (_nt(g, r[1]), _tn(r[0], g)))
_nt.defvjp(lambda a, b: (_nt(a, b), (a, b)), lambda r, g: (_nn(g, r[1]), _tn(g, r[0])))
_tn.defvjp(lambda a, b: (_tn(a, b), (a, b)), lambda r, g: (_nt(r[1], g), _nn(r[0], g)))


def _dot3(a, b, ca, cb):
    ah, bh = a.astype(BF16), b.astype(BF16)
    al, bl = (a - ah.astype(F32)).astype(BF16), (b - bh.astype(F32)).astype(BF16)
    return _bdot(ah, bh, ca, cb) + (_bdot(ah, bl, ca, cb) + _bdot(al, bh, ca, cb))


@jax.custom_vjp
def _nn_x3(a, b):
    return _dot3(a, b, 1, 0)


@jax.custom_vjp
def _nt_x3(a, b):
    return _dot3(a, b, 1, 1)


@jax.custom_vjp
def _tn_x3(a, b):
    return _dot3(a, b, 0, 0)


_nn_x3.defvjp(lambda a, b: (_nn_x3(a, b), (a, b)), lambda r, g: (_nt_x3(g, r[1]), _tn_x3(r[0], g)))
_nt_x3.defvjp(lambda a, b: (_nt_x3(a, b), (a, b)), lambda r, g: (_nn_x3(g, r[1]), _tn_x3(g, r[0])))
_tn_x3.defvjp(lambda a, b: (_tn_x3(a, b), (a, b)), lambda r, g: (_nt_x3(r[1], g), _nn_x3(r[0], g)))


@jax.custom_vjp
def _nn_hi(a, b):
    return _bdot(a, b, 1, 0, HI)


@jax.custom_vjp
def _nt_hi(a, b):
    return _bdot(a, b, 1, 1, HI)


@jax.custom_vjp
def _tn_hi(a, b):
    return _bdot(a, b, 0, 0, HI)


_nn_hi.defvjp(lambda a, b: (_nn_hi(a, b), (a, b)), lambda r, g: (_nt_hi(g, r[1]), _tn_hi(r[0], g)))
_nt_hi.defvjp(lambda a, b: (_nt_hi(a, b), (a, b)), lambda r, g: (_nn_hi(g, r[1]), _tn_hi(g, r[0])))
_tn_hi.defvjp(lambda a, b: (_tn_hi(a, b), (a, b)), lambda r, g: (_nt_hi(r[1], g), _nn_hi(r[0], g)))


@jax.custom_vjp
def _inverse_given(lmat, t):
    return t


_inverse_given.defvjp(lambda lmat, t: (t, t),
                      lambda t, ct: (-_tn_x3(t, _nt_x3(ct, t)), jnp.zeros_like(t)))


def _gdn_chunk_fn(q, k, v, g, beta, state, t_known=None):
    C = CHUNK
    B = q.shape[0]
    row = lax.broadcasted_iota(jnp.int32, (B, C, C), 1)
    col = lax.broadcasted_iota(jnp.int32, (B, C, C), 2)
    tril, strict = row >= col, row > col
    ones_tril = tril.astype(F32)
    gc = _nn_hi(ones_tril, g)
    gr = _nt_hi(jnp.full((B, C, C), 1.0 / C, F32), gc)
    decay = jnp.where(tril, jnp.exp(jnp.where(tril, gc - gr, 0.0)), 0.0)
    b1 = beta[:, :, :1]
    e_gc = jnp.exp(gc[:, :, :1])
    kb = k * b1
    lmat = jnp.where(strict, _nt(kb, k) * decay, 0.0)
    if t_known is None:
        a = -lmat
        t = jnp.where(row == col, 1.0, 0.0) + a
        p = a
        for _ in range(5):
            p = _nn_x3(p, p)
            t = t + _nn_x3(t, p)
    else:
        t = _inverse_given(lmat, t_known)
    rhs = jnp.concatenate([v * b1, kb * e_gc], axis=2)
    sol = _nn_x3(t, rhs)
    u, w = sol[:, :, :HEAD_DIM], sol[:, :, HEAD_DIM:]
    a_qk = jnp.where(tril, _nt(q, k) * decay, 0.0)
    gl = gc[:, C - 1:C, :1]
    q_dec = q * e_gc
    k_dec = k * jnp.exp(gl - gc[:, :, :1])
    v_new = u - _nn(w, state)
    o = _nn(q_dec, state) + _nn(a_qk, v_new)
    new_state = state * jnp.exp(gl) + _tn(k_dec, v_new)
    return o, new_state, t


def _split_heads(x, B):
    return jnp.stack([x[:, j * HEAD_DIM:(j + 1) * HEAD_DIM] for j in range(B)], axis=0)


def _merge_heads(x):
    return jnp.concatenate([x[j] for j in range(x.shape[0])], axis=1)


def _gdn_group(H, voff):
    return next(b for b in (12, 6, 4, 3, 2, 1) if H % b == 0 and voff % b == 0)


def _gdn_fwd(q, k, v, g, beta, *, H, voff, name, side=None):
    S = q.shape[0]
    N = S // CHUNK
    B = _gdn_group(H, voff)
    W = B * HEAD_DIM
    qs = lambda off: pl.BlockSpec((CHUNK, W), lambda h, n: (n, off // B + h))
    gs = pl.BlockSpec((B, CHUNK, CHUNK), lambda h, n: (h, n, 0))

    def body(q_ref, k_ref, v_ref, g_ref, b_ref, o_ref, st_ref, ti_ref, state):
        @pl.when(pl.program_id(1) == 0)
        def _():
            state[...] = jnp.zeros_like(state)

        s0 = state[...]
        st_ref[...] = s0
        o, s1, t = _gdn_chunk_fn(_split_heads(q_ref[...], B), _split_heads(k_ref[...], B),
                                 _split_heads(v_ref[...], B), g_ref[...], b_ref[...], s0)
        o_ref[...] = _merge_heads(o)
        ti_ref[...] = t
        state[...] = s1

    per_chunk = lambda d: pl.BlockSpec((B, None, d, d), lambda h, n: (h, n, 0, 0))
    body, s_in, s_out, s_shape, s_sems, s_args = _with_side(body, 5, 3, (H // B, N), side)
    o, states, tinv, *side_outs = pl.pallas_call(
        body, name=name, grid=(H // B, N),
        in_specs=[qs(0), qs(0), qs(voff), gs, gs] + s_in,
        out_specs=[qs(0), per_chunk(HEAD_DIM), per_chunk(CHUNK)] + s_out,
        out_shape=[_sds((S, H * HEAD_DIM)), _sds((H, N, HEAD_DIM, HEAD_DIM)), _sds((H, N, CHUNK, CHUNK))] + s_shape,
        scratch_shapes=[pltpu.VMEM((B, HEAD_DIM, HEAD_DIM), F32)] + s_sems,
        compiler_params=_cparams(("arbitrary", "arbitrary")),
    )(q, k, v, g, beta, *s_args)
    return o, states, tinv, side_outs


def _gdn_bwd(q, k, v, g, beta, states, tinv, do, *, H, voff, name, side=None):
    S = q.shape[0]
    N = S // CHUNK
    B = _gdn_group(H, voff)
    W = B * HEAD_DIM
    rs = lambda off: pl.BlockSpec((CHUNK, W), lambda h, n: (N - 1 - n, off // B + h))
    gs = pl.BlockSpec((B, CHUNK, CHUNK), lambda h, n: (h, N - 1 - n, 0))

    def body(q_ref, k_ref, v_ref, g_ref, b_ref, st_ref, ti_ref, do_ref, dq_ref, dk_ref, dv_ref, dg_ref, db_ref, dstate):
        @pl.when(pl.program_id(1) == 0)
        def _():
            dstate[...] = jnp.zeros_like(dstate)

        t_known = ti_ref[...]
        _, pull = jax.vjp(lambda *a: _gdn_chunk_fn(*a, t_known)[:2], _split_heads(q_ref[...], B),
                          _split_heads(k_ref[...], B), _split_heads(v_ref[...], B), g_ref[...], b_ref[...], st_ref[...])
        dq, dk, dv, dg, db, ds = pull((_split_heads(do_ref[...], B), dstate[...]))
        dq_ref[...] = _merge_heads(dq)
        dk_ref[...] = _merge_heads(dk)
        dv_ref[...] = _merge_heads(dv)
        dg_ref[...] = dg
        db_ref[...] = db
        dstate[...] = ds

    per_chunk = lambda d: pl.BlockSpec((B, None, d, d), lambda h, n: (h, N - 1 - n, 0, 0))
    body, s_in, s_out, s_shape, s_sems, s_args = _with_side(body, 8, 5, (H // B, N), side)
    dq, dk, dv, dg, db, *side_outs = pl.pallas_call(
        body, name=name, grid=(H // B, N),
        in_specs=[rs(0), rs(0), rs(voff), gs, gs, per_chunk(HEAD_DIM), per_chunk(CHUNK), rs(0)] + s_in,
        out_specs=[rs(0), rs(0), rs(0), gs, gs] + s_out,
        out_shape=[_sds((S, H * HEAD_DIM))] * 3 + [_sds((H, S, CHUNK))] * 2 + s_shape,
        scratch_shapes=[pltpu.VMEM((B, HEAD_DIM, HEAD_DIM), F32)] + s_sems,
        compiler_params=_cparams(("arbitrary", "arbitrary")),
    )(q, k, v, g, beta, states, tinv, do, *s_args)
    return dq, dk, dv, dg, db, side_outs


def _round_up(n, m):
    return (n + m - 1) // m * m


def _dims(S, D, M, shapes):
    c = dict(S=S, D=D, M=M)
    c["H"] = shapes["gdn_a_log"][-1]
    c["QL"] = shapes["mla_q_norm"][-1]
    c["KVL"] = shapes["mla_kv_norm"][-1]
    assert c["QL"] == c["KVL"]
    c["MEMW"] = shapes["mem_w_kv"][-1] // 2
    c["HM"] = c["MEMW"] // HEAD_DIM
    c["MW"] = c["H"] * HEAD_DIM
    c["F"] = shapes["mlp_w1"][-1]
    c["DEPTH"] = shapes["ln1_g"][0]
    c["ALPHA"] = (2 * c["DEPTH"]) ** 0.25
    c["MLA_IN"] = _round_up(c["QL"] + c["KVL"] + c["MEMW"] + LANES, 2 * LANES)
    c["GDN_IN"] = _round_up(4 * c["MW"] + c["MEMW"] + LANES, 2 * LANES)
    c["t_row"] = min(256, S)
    c["t_head"] = min(2048, S)
    c["t_conv"] = min(512, S)
    c["t_att"] = min(512, S)
    return c


def _pad_cols(w, n):
    return jnp.pad(w, ((0, 0), (0, n - w.shape[1])))


def _prep_mla_w_in(w, c):
    a = c["QL"] + c["KVL"]
    w = jnp.concatenate([w[:, :a], w[:, a + QK_ROPE:a + QK_ROPE + c["MEMW"]], w[:, a:a + QK_ROPE]], axis=1)
    return _pad_cols(w, c["MLA_IN"]).astype(BF16)


def _unprep_mla_w_in(dw, c):
    a, m = c["QL"] + c["KVL"], c["MEMW"]
    return jnp.concatenate([dw[:, :a], dw[:, a + m:a + m + QK_ROPE], dw[:, a:a + m]], axis=1)


def _prep_w_uq(w, c):
    w = w.reshape(c["QL"], c["H"], QK_NOPE + QK_ROPE)
    w = jnp.pad(w, ((0, 0), (0, 0), (0, QK_PAD - QK_NOPE - QK_ROPE)))
    return w.reshape(c["QL"], c["H"] * QK_PAD).astype(BF16)


def _unprep_w_uq(dw, c):
    return dw.reshape(c["QL"], c["H"], QK_PAD)[:, :, :QK_NOPE + QK_ROPE].reshape(c["QL"], c["H"] * (QK_NOPE + QK_ROPE))


def _prep_w_ukv(w, c):
    return w.reshape(c["KVL"], c["H"], 2, HEAD_DIM).transpose(0, 2, 1, 3).reshape(c["KVL"], 2 * c["MW"]).astype(BF16)


def _unprep_w_ukv(dw, c):
    return dw.reshape(c["KVL"], 2, c["H"], HEAD_DIM).transpose(0, 2, 1, 3).reshape(c["KVL"], 2 * c["MW"])


def _prep_gdn_w_in(w, c):
    a, h2 = 4 * c["MW"], 2 * c["H"]
    w = jnp.concatenate([w[:, :a], w[:, a + h2:], w[:, a:a + h2]], axis=1)
    return _pad_cols(w, c["GDN_IN"]).astype(BF16)


def _unprep_gdn_w_in(dw, c):
    a, h2, m = 4 * c["MW"], 2 * c["H"], c["MEMW"]
    return jnp.concatenate([dw[:, :a], dw[:, a + m:a + m + h2], dw[:, a:a + m]], axis=1)


def _lane_bcast(v):
    return jnp.broadcast_to(v.astype(F32)[:, None, None], (v.shape[0], 1, LANES))


def _row(i):
    return (i, 0)


def _par(i):
    return (0, 0)


def _layer_norm(z, g, b, c, name):
    S, D, ts = c["S"], c["D"], c["t_row"]
    return _ew(_ln_fn, [(z, (ts, D), _row), (g, (1, D), _par), (b, (1, D), _par)],
               [(_sds((S, D)), (ts, D), _row), (_sds((S, D), BF16), (ts, D), _row)], (S // ts,), name=name)


def _layer_norm_bwd(z, g, b, dy, c, name):
    S, D, ts = c["S"], c["D"], c["t_row"]
    fn = lambda z, g, b: _ln_fn(z, g, b)[:1]

    def both(z, g, b):
        return fn(z, g, b)

    dz, dg, db = _ew_vjp(both, [(z, (ts, D), _row), (g, (1, D), _par), (b, (1, D), _par)], [(dy, (ts, D), _row)],
                         [(_sds((S, D)), (ts, D), _row, "set"), (_sds((1, D)), (1, D), _par, "acc_all"),
                          (_sds((1, D)), (1, D), _par, "acc_all")], (S // ts,), name=name)
    return dz, dg, db


def _mem_attn_fwd(h, qoff, memkv, c, name):
    return _flash_fwd(h, memkv, memkv, H=c["HM"], dq=HEAD_DIM, dv=HEAD_DIM, qoff=qoff, koff=0, voff=c["HM"],
                      causal=False, scale=HEAD_DIM ** -0.5, tq=c["t_att"], tk=c["M"], name=name)[:2]


def _mem_attn_bwd(h, qoff, memkv, om, lsem, dcat, c, name):
    return _flash_bwd(h, memkv, memkv, om, lsem, dcat, H=c["HM"], dq=HEAD_DIM, dv=HEAD_DIM, qoff=qoff, koff=0,
                      voff=c["HM"], dooff=c["H"], causal=False, scale=HEAD_DIM ** -0.5, tq=c["t_att"], tk=c["M"],
                      name=name)[:3]


def _mla_specs(c):
    H, ts = c["H"], c["t_head"]
    kr_blk = (c["QL"] + c["KVL"] + c["MEMW"]) // LANES
    hd = lambda i, h: (i, h)
    return [((ts, QK_PAD), hd), ((ts, HEAD_DIM), hd), ((ts, LANES), lambda i, h: (i, kr_blk)),
            ((ts, LANES), lambda i, h: (i, 0)), ((ts, LANES), lambda i, h: (i, 0))]


def _mla_fwd(xb, p, cosp, sinp, c, side=None):
    S, H, QL, ts, tr = c["S"], c["H"], c["QL"], c["t_head"], c["t_row"]
    h, = _mm(xb, p["w_in"], "nn", [F32], name="mla_in")
    nq, = _ew(_rms_fn, [(h, (tr, QL), lambda i: (i, 0)), (p["q_norm"], (1, QL), _par)],
              [(_sds((S, QL), BF16), (tr, QL), _row)], (S // tr,), name="mla_qnorm")
    nkv, = _ew(_rms_fn, [(h, (tr, QL), lambda i: (i, 1)), (p["kv_norm"], (1, QL), _par)],
               [(_sds((S, QL), BF16), (tr, QL), _row)], (S // tr,), name="mla_kvnorm")
    qraw, = _mm(nq, p["w_uq"], "nn", [F32], name="mla_uq")
    kvraw, = _mm(nkv, p["w_ukv"], "nn", [BF16], name="mla_ukv")
    sp = _mla_specs(c)
    ins = [(a, b, m) for a, (b, m) in zip([qraw, kvraw, h, cosp, sinp], sp)]
    qp, kp = _ew(_mla_prep_fn, ins, [(_sds((S, H * QK_PAD), BF16), (ts, QK_PAD), lambda i, h: (i, h))] * 2,
                 (S // ts, H), name="mla_rope")
    o, lse, got = _flash_fwd(qp, kp, kvraw, H=H, dq=QK_PAD, dv=HEAD_DIM, qoff=0, koff=0, voff=H, causal=True,
                             scale=(QK_NOPE + QK_ROPE) ** -0.5, tq=c["t_att"], tk=c["t_att"], name="mla_attn", side=side)
    return o, dict(h=h, nq=nq, nkv=nkv, qraw=qraw, kvraw=kvraw, qp=qp, kp=kp, o=o, lse=lse), got


def _mla_bwd(sv, p, cosp, sinp, dcat, dqm, c, side=None):
    S, H, QL, ts, tr = c["S"], c["H"], c["QL"], c["t_head"], c["t_row"]
    dqp, dkp, dv, got = _flash_bwd_causal(sv["qp"], sv["kp"], sv["kvraw"], sv["o"], sv["lse"], dcat, H=H, dq=QK_PAD,
                                          dv=HEAD_DIM, qoff=0, koff=0, voff=H, dooff=0,
                                          scale=(QK_NOPE + QK_ROPE) ** -0.5, tq=c["t_att"], name="mla_attn_bwd",
                                          side=side)
    sp = _mla_specs(c)
    ins = [(a, b, m) for a, (b, m) in zip([sv["qraw"], sv["kvraw"], sv["h"], cosp, sinp], sp)]
    hd = lambda i, h: (i, h)
    dqraw, dknope, dkr = _ew_vjp(
        _mla_prep_fn, ins, [(dqp, (ts, QK_PAD), hd), (dkp, (ts, QK_PAD), hd)],
        [(_sds((S, H * QK_PAD), BF16), (ts, QK_PAD), hd, "set"), (_sds((S, H * HEAD_DIM), BF16), (ts, HEAD_DIM), hd, "set"),
         (_sds((S, LANES)), (ts, LANES), lambda i, h: (i, 0), "acc"), None, None], (S // ts, H), name="mla_rope_bwd")
    dkvraw = jnp.concatenate([dknope, dv.astype(BF16)], axis=1)
    dnq, = _mm(dqraw, p["w_uq"], "nt", [F32], name="mla_uq_dx")
    dw_uq, = _mm(sv["nq"], dqraw, "tn", [F32], name="mla_uq_dw")
    dnkv, = _mm(dkvraw, p["w_ukv"], "nt", [F32], name="mla_ukv_dx")
    dw_ukv, = _mm(sv["nkv"], dkvraw, "tn", [F32], name="mla_ukv_dw")
    dcq, dgq = _ew_vjp(_rms_fn, [(sv["h"], (tr, QL), lambda i: (i, 0)), (p["q_norm"], (1, QL), _par)],
                       [(dnq, (tr, QL), _row)],
                       [(_sds((S, QL), BF16), (tr, QL), _row, "set"), (_sds((1, QL)), (1, QL), _par, "acc_all")],
                       (S // tr,), name="mla_qnorm_bwd")
    dckv, dgkv = _ew_vjp(_rms_fn, [(sv["h"], (tr, QL), lambda i: (i, 1)), (p["kv_norm"], (1, QL), _par)],
                         [(dnkv, (tr, QL), _row)],
                         [(_sds((S, QL), BF16), (tr, QL), _row, "set"), (_sds((1, QL)), (1, QL), _par, "acc_all")],
                         (S // tr,), name="mla_kvnorm_bwd")
    pad = c["MLA_IN"] - (2 * QL + c["MEMW"] + LANES)
    dh = jnp.concatenate([dcq, dckv, dqm.astype(BF16), dkr.astype(BF16)] + ([jnp.zeros((S, pad), BF16)] if pad else []),
                         axis=1)
    grads = dict(mla_q_norm=dgq[0], mla_kv_norm=dgkv[0], mla_w_uq=_unprep_w_uq(dw_uq, c),
                 mla_w_ukv=_unprep_w_ukv(dw_ukv, c))
    return dh, grads, got


def _gdn_ins(h, qkvc, p, c):
    H, ts = c["H"], c["t_head"]
    ab_blk = (4 * c["MW"] + c["MEMW"]) // LANES
    qk_ins = [(qkvc, (ts, HEAD_DIM), lambda i, h: (i, h)), (qkvc, (ts, HEAD_DIM), lambda i, h: (i, H + h))]
    gate_ins = [(h, (ts, LANES), lambda i, h: (i, ab_blk)), (p["a_log"], (None, 1, LANES), lambda i, h: (h, 0, 0)),
                (p["dt_bias"], (None, 1, LANES), lambda i, h: (h, 0, 0))]
    return qk_ins, gate_ins


def _gdn_out_ins(o, h, p, c):
    H, ts = c["H"], c["t_head"]
    return [(o, (ts, HEAD_DIM), lambda i, h: (i, h)), (h, (ts, HEAD_DIM), lambda i, h: (i, 3 * H + h)),
            (p["o_norm"], (1, HEAD_DIM), lambda i, h: (0, 0))]


def _gdn_layer_fwd(xb, p, c, side=None):
    S, H, MW, ts = c["S"], c["H"], c["MW"], c["t_head"]
    h, = _mm(xb, p["w_in"], "nn", [F32], name="gdn_in")
    tc = _tile(3 * MW, 512, LANES)
    qkvc = _conv_fwd(h, p["conv"], C=3 * MW, ts=c["t_conv"], tc=tc, name="gdn_conv")
    qk_ins, gate_ins = _gdn_ins(h, qkvc, p, c)
    hd = lambda i, h: (i, h)
    qn, kn = _ew(_gdn_qk_fn, qk_ins, [(_sds((S, MW)), (ts, HEAD_DIM), hd)] * 2, (S // ts, H), name="gdn_qknorm")
    g3 = lambda i, h: (h, i, 0)
    g, beta = _ew(_gdn_gate_fn(H, 1), gate_ins, [(_sds((H, S, CHUNK)), (None, ts, CHUNK), g3)] * 2, (S // ts, H),
                  name="gdn_gate")
    o, states, tinv, got = _gdn_fwd(qn, kn, qkvc, g, beta, H=H, voff=2 * H, name="gdn_delta", side=side)
    mix, = _ew(_gdn_out_fn, _gdn_out_ins(o, h, p, c), [(_sds((S, MW)), (ts, HEAD_DIM), hd)], (S // ts, H),
               name="gdn_outnorm")
    return mix, dict(h=h, qkvc=qkvc, qn=qn, kn=kn, g=g, beta=beta, o=o, states=states, tinv=tinv), got


def _gdn_layer_bwd(sv, p, dcat, dqm, c, side=None):
    S, H, MW, ts = c["S"], c["H"], c["MW"], c["t_head"]
    hd = lambda i, h: (i, h)
    g3 = lambda i, h: (h, i, 0)
    h, qkvc = sv["h"], sv["qkvc"]
    do, dz, d_onorm = _ew_vjp(_gdn_out_fn, _gdn_out_ins(sv["o"], h, p, c), [(dcat, (ts, HEAD_DIM), hd)],
                              [(_sds((S, MW)), (ts, HEAD_DIM), hd, "set"), (_sds((S, MW), BF16), (ts, HEAD_DIM), hd, "set"),
                               (_sds((1, HEAD_DIM)), (1, HEAD_DIM), lambda i, h: (0, 0), "acc_all")],
                              (S // ts, H), name="gdn_outnorm_bwd")
    dqn, dkn, dv, dg, db, got = _gdn_bwd(sv["qn"], sv["kn"], qkvc, sv["g"], sv["beta"], sv["states"], sv["tinv"], do,
                                         H=H, voff=2 * H, name="gdn_delta_bwd", side=side)
    qk_ins, gate_ins = _gdn_ins(h, qkvc, p, c)
    dqc, dkc = _ew_vjp(_gdn_qk_fn, qk_ins, [(dqn, (ts, HEAD_DIM), hd), (dkn, (ts, HEAD_DIM), hd)],
                       [(_sds((S, MW)), (ts, HEAD_DIM), hd, "set")] * 2, (S // ts, H), name="gdn_qknorm_bwd")
    full3 = lambda i, h: (0, 0, 0)
    dab, dalog, ddt = _ew_vjp(
        _gdn_gate_fn(H, 1), gate_ins, [(dg, (None, ts, CHUNK), g3), (db, (None, ts, CHUNK), g3)],
        [(_sds((S, LANES), BF16), (ts, LANES), lambda i, h: (i, 0), "acc"),
         (_sds((H, 1, LANES)), (H, 1, LANES), full3, ("acc_at", 1)),
         (_sds((H, 1, LANES)), (H, 1, LANES), full3, ("acc_at", 1))], (S // ts, H), name="gdn_gate_bwd")
    dqkvc = jnp.concatenate([dqc, dkc, dv], axis=1)
    tc = _tile(3 * MW, 512, LANES)
    dxc, dconv = _conv_bwd(h, p["conv"], dqkvc, C=3 * MW, ts=c["t_conv"], tc=tc, name="gdn_conv_bwd")
    pad = c["GDN_IN"] - (4 * MW + c["MEMW"] + LANES)
    dh = jnp.concatenate([dxc.astype(BF16), dz, dqm.astype(BF16), dab] + ([jnp.zeros((S, pad), BF16)] if pad else []),
                         axis=1)
    grads = dict(gdn_conv=dconv, gdn_a_log=jnp.sum(dalog[:, 0, :], axis=-1), gdn_dt_bias=jnp.sum(ddt[:, 0, :], axis=-1),
                 gdn_o_norm=d_onorm[0])
    return dh, grads, got


def _prep_layer(W, i, c):
    p = dict(mem_w_kv=W["mem_w_kv"].astype(BF16), w_out=W["w_out"].astype(BF16),
             w1=W["mlp_w1"].astype(BF16), w2=W["mlp_w2"].astype(BF16),
             ln1_g=W["ln1_g"][None].astype(F32), ln1_b=W["ln1_b"][None].astype(F32),
             ln2_g=W["ln2_g"][None].astype(F32), ln2_b=W["ln2_b"][None].astype(F32))
    if i % 2 == 0:
        p.update(w_in=_prep_mla_w_in(W["mla_w_in"], c), q_norm=W["mla_q_norm"][None].astype(F32),
                 w_uq=_prep_w_uq(W["mla_w_uq"], c), kv_norm=W["mla_kv_norm"][None].astype(F32),
                 w_ukv=_prep_w_ukv(W["mla_w_ukv"], c))
    else:
        p.update(w_in=_prep_gdn_w_in(W["gdn_w_in"], c), conv=W["gdn_conv"].astype(F32),
                 a_log=_lane_bcast(W["gdn_a_log"]), dt_bias=_lane_bcast(W["gdn_dt_bias"]),
                 o_norm=W["gdn_o_norm"][None].astype(F32))
    return p


def _local_step(x, mem, positions, W, loss_target, c, next_weights=None, grad_sink=None):
    S, D, H, MW, ALPHA = c["S"], c["D"], c["H"], c["MW"], c["ALPHA"]
    inv_freq = 1.0 / (ROPE_THETA ** (jnp.arange(0, QK_ROPE, 2, dtype=F32) / QK_ROPE))
    ang = positions.astype(F32)[:, None] * inv_freq
    cos, sin = jnp.cos(ang), jnp.sin(ang)
    cosp = jnp.concatenate([cos, cos, jnp.ones((S, LANES - QK_ROPE), F32)], axis=1)
    sinp = jnp.concatenate([sin, sin, jnp.zeros((S, LANES - QK_ROPE), F32)], axis=1)
    memb = mem.astype(BF16)
    xf, xb = x, x.astype(BF16)
    saved, params = [], []
    w_next = W[0]
    for i in range(c["DEPTH"]):
        p = _prep_layer(w_next if next_weights is not None else W[i], i, c)
        mla = i % 2 == 0
        memkv, = _mm(memb, p["mem_w_kv"], "nn", [BF16], name="mem_kv")
        side, arrived = next_weights(i + 1) if next_weights is not None and i + 1 < c["DEPTH"] else (None, None)
        if mla:
            mix, sv, got = _mla_fwd(xb, p, cosp, sinp, c, side)
            qoff = (c["QL"] + c["KVL"]) // LANES
        else:
            mix, sv, got = _gdn_layer_fwd(xb, p, c, side)
            qoff = 4 * MW // LANES
        if side is not None:
            w_next = arrived(got)
        om, lsem = _mem_attn_fwd(sv["h"], qoff, memkv, c, "mem_attn")
        cat = jnp.concatenate([mix, om], axis=1).astype(BF16)
        z1, = _mm(cat, p["w_out"], "nn", [F32], name="w_out", extras=(xf,), epilogue=lambda acc, r: (ALPHA * r + acc,))
        x1, x1b = _layer_norm(z1, p["ln1_g"], p["ln1_b"], c, "ln1")
        u, a = _mm(x1b, p["w1"], "nn", [F32, BF16], name="mlp_up", b_major=True,
                   epilogue=lambda acc: (acc, jnp.square(jnp.maximum(acc, 0.0))))
        z2, = _mm(a, p["w2"], "nn", [F32], name="mlp_down", extras=(x1,), epilogue=lambda acc, r: (ALPHA * r + acc,))
        x2, x2b = _layer_norm(z2, p["ln2_g"], p["ln2_b"], c, "ln2")
        sv.update(xb=xb, memkv=memkv, om=om, lsem=lsem, cat=cat, z1=z1, x1b=x1b, u=u, a=a, z2=z2, qoff=qoff)
        saved.append(sv)
        params.append(p)
        xf, xb = x2, x2b

    ts = c["t_row"]
    lsum, dy = _ew(_loss_fn, [(xf, (ts, D), _row), (loss_target, (ts, D), _row)],
                   [(_sds((1, D)), (1, D), _par), (_sds((S, D)), (ts, D), _row)], (S // ts,), name="loss", acc_out=(0,))
    loss = jnp.sum(lsum)

    grads = [None] * c["DEPTH"]
    pending = None
    dx = dy
    for i in reversed(range(c["DEPTH"])):
        p, sv = params[i], saved[i]
        mla = i % 2 == 0
        G = {}
        dz2, dg, db = _layer_norm_bwd(sv["z2"], p["ln2_g"], p["ln2_b"], dx, c, "ln2_bwd")
        G["ln2_g"], G["ln2_b"] = dg[0], db[0]
        dz2b = dz2.astype(BF16)
        du, = _mm(dz2b, p["w2"], "nt", [BF16], name="mlp_down_dx", extras=(sv["u"],),
                  epilogue=lambda acc, u: (acc * (2.0 * jnp.maximum(u, 0.0)),))
        G["mlp_w2"], = _mm(sv["a"], dz2b, "tn", [F32], name="mlp_down_dw")
        G["mlp_w1"], = _mm(sv["x1b"], du, "tn", [F32], name="mlp_up_dw", out_major=True)
        dx1, = _mm(du, p["w1"], "nt", [F32], name="mlp_up_dx", extras=(dz2,), b_major=True,
                   epilogue=lambda acc, r: (ALPHA * r + acc,))
        dz1, dg, db = _layer_norm_bwd(sv["z1"], p["ln1_g"], p["ln1_b"], dx1, c, "ln1_bwd")
        G["ln1_g"], G["ln1_b"] = dg[0], db[0]
        dz1b = dz1.astype(BF16)
        dcat, = _mm(dz1b, p["w_out"], "nt", [BF16], name="w_out_dx")
        G["w_out"], = _mm(sv["cat"], dz1b, "tn", [F32], name="w_out_dw")
        dqm, dkm, dvm = _mem_attn_bwd(sv["h"], sv["qoff"], sv["memkv"], sv["om"], sv["lsem"], dcat, c, "mem_attn_bwd")
        dmemkv = jnp.concatenate([dkm, dvm], axis=1).astype(BF16)
        G["mem_w_kv"], = _mm(memb, dmemkv, "tn", [F32], name="mem_kv_dw")
        riders = ([pending] if pending is not None else []) + ([grad_sink(i, G, True) + (i,)] if grad_sink else [])
        side = _merge_plans([r[0] for r in riders]) if riders else None
        if mla:
            dh, g, got = _mla_bwd(sv, p, cosp, sinp, dcat, dqm, c, side)
            G.update(g)
            dw_in, = _mm(sv["xb"], dh, "tn", [F32], name="mla_in_dw")
            G["mla_w_in"] = _unprep_mla_w_in(dw_in, c)
            dx, = _mm(dh, p["w_in"], "nt", [F32], name="mla_in_dx", extras=(dz1,),
                      epilogue=lambda acc, r: (ALPHA * r + acc,))
        else:
            dh, g, got = _gdn_layer_bwd(sv, p, dcat, dqm, c, side)
            G.update(g)
            dw_in, = _mm(sv["xb"], dh, "tn", [F32], name="gdn_in_dw")
            G["gdn_w_in"] = _unprep_gdn_w_in(dw_in, c)
            dx, = _mm(dh, p["w_in"], "nt", [F32], name="gdn_in_dx", extras=(dz1,),
                      epilogue=lambda acc, r: (ALPHA * r + acc,))
        grads[i] = dict(G)
        for plan, done, layer in riders:
            n_out = len(plan["out_shape"])
            grads[layer].update(done(got[:n_out]))
            got = got[n_out:]
        pending = grad_sink(i, G, False) + (i,) if grad_sink else None
    if pending is not None:
        grads[pending[2]].update(pending[1](_run_plan(pending[0], "grad_scatter_last")))
    return loss, dx, grads


_HBM = pl.BlockSpec(memory_space=pltpu.HBM)
_VMEM = pl.BlockSpec(memory_space=pltpu.VMEM)


def _my_place():
    return lax.axis_index("x"), lax.axis_index("y"), lax.axis_index("c")


def _my_chip():
    return 2 * lax.axis_index("x") + lax.axis_index("y")


def _other_chips(x, y):
    return [(1 - x, y), (x, 1 - y), (1 - x, 1 - y)]


def _gather_plan(arrs, by_rows):
    n = len(arrs)
    halved = [_halved(a) for a in arrs]

    def run(ins, outs, sems, start, wait):
        send_sems, recv_sems, local_sems = sems
        x, y, c = _my_place()
        chips = _other_chips(x, y)

        def copy(i, k, chip, to):
            half = c if halved[i] else None
            src = ins[i] if half is None else ins[i].at[pl.ds(c * (arrs[i].shape[0] // 2), arrs[i].shape[0] // 2)]
            return pltpu.make_async_remote_copy(src_ref=src, dst_ref=_slab(outs[i], arrs[i].shape[0], by_rows[i], chip, half),
                                                send_sem=send_sems.at[3 * i + k], recv_sem=recv_sems.at[3 * i + k],
                                                device_id=to, device_id_type=MESH)

        mine = [pltpu.make_async_copy(ins[i], _slab(outs[i], arrs[i].shape[0], by_rows[i], 2 * x + y, None),
                                      local_sems.at[i]) for i in range(n)]
        sends = [copy(i, k, 2 * x + y, (cx, cy, c)) for i in range(n) for k, (cx, cy) in enumerate(chips)]
        if start:
            for cp in mine + sends:
                cp.start()
        if wait:
            for i in range(n):
                for k, (cx, cy) in enumerate(chips):
                    copy(i, k, 2 * cx + cy, (cx, cy, c)).wait_recv()
            for cp in sends:
                cp.wait_send()
            for cp in mine:
                cp.wait()

    shapes = [jax.ShapeDtypeStruct((N_CHIPS * a.shape[0], a.shape[1]) if r else (N_CHIPS,) + a.shape, a.dtype)
              for a, r in zip(arrs, by_rows)]
    sems = [pltpu.SemaphoreType.DMA((3 * n,)), pltpu.SemaphoreType.DMA((3 * n,)), pltpu.SemaphoreType.DMA((n,))]
    return dict(arrs=list(arrs), out_shape=shapes, sems=sems, run=run)


def _halved(a):
    return a.shape[0] % (4 * SUBLANES * (4 // a.dtype.itemsize)) == 0


def _slab(out, r, by_rows, chip, half):
    lo, n = (0, r) if half is None else (half * (r // 2), r // 2)
    return out.at[pl.ds(chip * r + lo, n)] if by_rows else out.at[chip].at[pl.ds(lo, n)]


def _gather_fill(outs, arrs, by_rows, name):
    idx = [i for i, a in enumerate(arrs) if _halved(a)]
    n = len(idx)

    def body(*refs):
        ins, bufs = refs[:n], refs[n:2 * n]
        send_sems, recv_sems = refs[2 * n:]
        x, y, c = _my_place()

        def copy(j, k, chip, half):
            r = arrs[idx[j]].shape[0]
            return pltpu.make_async_remote_copy(src_ref=_slab(ins[j], r, by_rows[idx[j]], chip, half),
                                                dst_ref=_slab(bufs[j], r, by_rows[idx[j]], chip, half),
                                                send_sem=send_sems.at[3 * j + k], recv_sem=recv_sems.at[3 * j + k],
                                                device_id=(x, y, 1 - c), device_id_type=MESH)

        chips = [2 * cx + cy for cx, cy in _other_chips(x, y)]
        sends = [copy(j, k, chip, c) for j in range(n) for k, chip in enumerate(chips)]
        for cp in sends:
            cp.start()
        for j in range(n):
            for k, chip in enumerate(chips):
                copy(j, k, chip, 1 - c).wait_recv()
        for cp in sends:
            cp.wait_send()

    filled = pl.pallas_call(
        body, name=name, in_specs=[_HBM] * n, out_specs=[_HBM] * n,
        out_shape=[jax.ShapeDtypeStruct(outs[i].shape, outs[i].dtype) for i in idx],
        input_output_aliases={j: j for j in range(n)},
        scratch_shapes=[pltpu.SemaphoreType.DMA((3 * n,)), pltpu.SemaphoreType.DMA((3 * n,))],
    )(*[outs[i] for i in idx])
    res = list(outs)
    for i, f in zip(idx, filled):
        res[i] = f
    return res


def _merge_plans(plans):
    def run(ins, outs, sems, start, wait):
        a = b = s = 0
        for p in plans:
            na, nb, ns = len(p["arrs"]), len(p["out_shape"]), len(p["sems"])
            p["run"](ins[a:a + na], outs[b:b + nb], sems[s:s + ns], start, wait)
            a, b, s = a + na, b + nb, s + ns

    return dict(arrs=sum((p["arrs"] for p in plans), []), out_shape=sum((p["out_shape"] for p in plans), []),
                sems=sum((p["sems"] for p in plans), []), run=run)


def _scatter_plan(ps):
    n = len(ps)

    def run(ins, outs, sems, start, wait):
        send_sems, recv_sems = sems
        x, y, c = _my_place()
        cps = [pltpu.make_async_remote_copy(src_ref=ins[i].at[2 * cx + cy], dst_ref=outs[i].at[k],
                                            send_sem=send_sems.at[3 * i + k], recv_sem=recv_sems.at[3 * i + k],
                                            device_id=(cx, cy, c), device_id_type=MESH)
               for i in range(n) for k, (cx, cy) in enumerate(_other_chips(x, y))]
        if start:
            for cp in cps:
                cp.start()
        if wait:
            for cp in cps:
                cp.wait()

    shapes = [jax.ShapeDtypeStruct((3,) + p.shape[1:], p.dtype) for p in ps]
    sems = [pltpu.SemaphoreType.DMA((3 * n,)), pltpu.SemaphoreType.DMA((3 * n,))]
    return dict(arrs=list(ps), out_shape=shapes, sems=sems, run=run)


def _run_plan(plan, name):
    n_in, n_out = len(plan["arrs"]), len(plan["out_shape"])

    def body(*refs):
        plan["run"](refs[:n_in], refs[n_in:n_in + n_out], refs[n_in + n_out:], True, True)

    return pl.pallas_call(body, name=name, in_specs=[_HBM] * n_in, out_specs=[_HBM] * n_out,
                          out_shape=plan["out_shape"], scratch_shapes=plan["sems"])(*plan["arrs"])


def _with_side(body, n_in, n_out, grid, side):
    if side is None:
        return body, [], [], [], [], []
    s_in, s_out, s_sem = len(side["arrs"]), len(side["out_shape"]), len(side["sems"])

    def wrapped(*refs):
        ins, s_ins = refs[:n_in], refs[n_in:n_in + s_in]
        o0 = n_in + s_in
        outs, s_outs = refs[o0:o0 + n_out], refs[o0 + n_out:o0 + n_out + s_out]
        rest = refs[o0 + n_out + s_out:]
        scratch, s_sems = rest[:len(rest) - s_sem], rest[len(rest) - s_sem:]
        ids = [pl.program_id(d) for d in range(len(grid))]
        first = functools.reduce(jnp.logical_and, [i == 0 for i in ids])
        last = functools.reduce(jnp.logical_and, [i == g - 1 for i, g in zip(ids, grid)])
        pl.when(first)(lambda: side["run"](s_ins, s_outs, s_sems, True, False))
        body(*ins, *outs, *scratch)
        pl.when(last)(lambda: side["run"](s_ins, s_outs, s_sems, False, True))

    return wrapped, [_HBM] * s_in, [_HBM] * s_out, side["out_shape"], side["sems"], side["arrs"]


def _swap_halves(gs, name):
    n = len(gs)

    def body(*refs):
        ins, outs = refs[:n], refs[n:2 * n]
        send_sems, recv_sems = refs[2 * n:]
        x, y, c = _my_place()
        cps = [pltpu.make_async_remote_copy(src_ref=ins[i].at[:, 1 - c], dst_ref=outs[i], send_sem=send_sems.at[i],
                                            recv_sem=recv_sems.at[i], device_id=(x, y, 1 - c), device_id_type=MESH)
               for i in range(n)]
        for cp in cps:
            cp.start()
        for cp in cps:
            cp.wait()

    return pl.pallas_call(
        body, name=name, in_specs=[_HBM] * n, out_specs=[_HBM] * n,
        out_shape=[jax.ShapeDtypeStruct((g.shape[0],) + g.shape[2:], g.dtype) for g in gs],
        scratch_shapes=[pltpu.SemaphoreType.DMA((n,)), pltpu.SemaphoreType.DMA((n,))],
    )(*gs)


def _join_halves(fs, name):
    n = len(fs)

    def body(*refs):
        ins, outs = refs[:n], refs[n:2 * n]
        send_sems, recv_sems = refs[2 * n:]
        x, y, c = _my_place()

        def copy(i, half):
            return pltpu.make_async_remote_copy(src_ref=ins[i].at[half], dst_ref=outs[i].at[half],
                                                send_sem=send_sems.at[i], recv_sem=recv_sems.at[i],
                                                device_id=(x, y, 1 - c), device_id_type=MESH)

        sends = [copy(i, c) for i in range(n)]
        for cp in sends:
            cp.start()
        for i in range(n):
            copy(i, 1 - c).wait_recv()
        for cp in sends:
            cp.wait_send()

    return pl.pallas_call(
        body, name=name, in_specs=[_HBM] * n, out_specs=[_HBM] * n,
        out_shape=[jax.ShapeDtypeStruct(f.shape, f.dtype) for f in fs],
        input_output_aliases={i: i for i in range(n)},
        scratch_shapes=[pltpu.SemaphoreType.DMA((n,)), pltpu.SemaphoreType.DMA((n,))],
    )(*fs)


def _row_tile(a, b):
    return _tile(a, max(SUBLANES, (1 << 19) // b // SUBLANES * SUBLANES), SUBLANES)


def _add_core(g, got, name):
    _, _, A, B = g.shape
    ta = _row_tile(A, B)
    return _ew(lambda p, q: (p + q,),
               [(g, (None, None, ta, B), lambda s, i: (s, lax.axis_index("c"), i, 0)),
                (got, (None, ta, B), lambda s, i: (s, i, 0))],
               [(_sds((N_CHIPS, A, B)), (None, ta, B), lambda s, i: (s, i, 0))], (N_CHIPS, A // ta), name=name)[0]


def _add_chips(p, got, name):
    _, A, B = p.shape
    ta = _row_tile(A, B)
    blk = (None, ta, B)
    return _ew(lambda a, b, c_, d: (((a + b) + c_) + d,),
               [(p, blk, lambda i: (_my_chip(), i, 0)), (got, blk, lambda i: (0, i, 0)),
                (got, blk, lambda i: (1, i, 0)), (got, blk, lambda i: (2, i, 0))],
               [(_sds((2, A, B)), blk, lambda i: (lax.axis_index("c"), i, 0))], (A // ta,), name=name)[0]


def _all_reduce_small(v, name):
    r = v.shape[0]
    masks = [(mx, my, mc) for mx in (0, 1) for my in (0, 1) for mc in (0, 1)][1:]

    def body(v_ref, out_ref, gath, send_sems, recv_sems):
        x, y, c = _my_place()
        me = 4 * x + 2 * y + c
        gath[me] = v_ref[...]

        def peer(m):
            return (x + m[0] - 2 * x * m[0], y + m[1] - 2 * y * m[1], c + m[2] - 2 * c * m[2])

        def copy(k, slab, to):
            return pltpu.make_async_remote_copy(src_ref=v_ref, dst_ref=gath.at[slab], send_sem=send_sems.at[k],
                                                recv_sem=recv_sems.at[k], device_id=to, device_id_type=MESH)

        sends = [copy(k, me, peer(m)) for k, m in enumerate(masks)]
        for cp in sends:
            cp.start()
        for k, m in enumerate(masks):
            px, py, pc = peer(m)
            copy(k, 4 * px + 2 * py + pc, (px, py, pc)).wait_recv()
        for cp in sends:
            cp.wait_send()
        total = gath[0]
        for d in range(1, 8):
            total = total + gath[d]
        out_ref[...] = total

    return pl.pallas_call(
        body, name=name, in_specs=[_VMEM], out_specs=_VMEM, out_shape=jax.ShapeDtypeStruct((r, LANES), F32),
        scratch_shapes=[pltpu.VMEM((8, r, LANES), F32), pltpu.SemaphoreType.DMA((7,)), pltpu.SemaphoreType.DMA((7,))],
    )(v)


def _pack_rows(arrs, dtype, row_mult):
    flat = jnp.concatenate([a.astype(dtype).reshape(-1) for a in arrs])
    n = flat.shape[0]
    rows = _round_up(-(-n // LANES), row_mult)
    return jnp.pad(flat, (0, rows * LANES - n)).reshape(rows, LANES)


def _unpack_rows(buf, shapes):
    lead = buf.shape[:-2]
    flat = buf.reshape(lead + (-1,))
    out, o = [], 0
    for s in shapes:
        n = math.prod(s)
        out.append(lax.slice_in_dim(flat, o, o + n, axis=len(lead)).reshape(lead + tuple(s)))
        o += n
    return out


WEIGHTS = ["mla_w_in", "mla_q_norm", "mla_w_uq", "mla_kv_norm", "mla_w_ukv", "gdn_w_in", "gdn_conv", "gdn_a_log",
           "gdn_dt_bias", "gdn_o_norm", "mem_w_kv", "w_out", "ln1_g", "ln1_b", "mlp_w1", "mlp_w2", "ln2_g", "ln2_b"]
SHARD_AXIS = {"mla_w_in": 1, "mla_w_uq": 2, "mla_w_ukv": 2, "gdn_w_in": 2, "gdn_conv": 2, "mem_w_kv": 1, "w_out": 1,
              "mlp_w1": 2, "mlp_w2": 1}
SMALL = [k for k in WEIGHTS if k not in SHARD_AXIS] + ["gdn_conv"]
BIG = [k for k in WEIGHTS if k not in SMALL]
MLA_KEYS = ["mla_w_in", "mla_q_norm", "mla_w_uq", "mla_kv_norm", "mla_w_ukv"]
GDN_KEYS = ["gdn_w_in", "gdn_conv", "gdn_a_log", "gdn_dt_bias", "gdn_o_norm"]
ALL_KEYS = ["mem_w_kv", "w_out", "ln1_g", "ln1_b", "mlp_w1", "mlp_w2", "ln2_g", "ln2_b"]


def _layer_keys(i):
    return (MLA_KEYS if i % 2 == 0 else GDN_KEYS) + ALL_KEYS


def _layer_slot(k, i):
    return i // 2 if k in MLA_KEYS or k in GDN_KEYS else i


def _gather_layer(w, i):
    keys = [k for k in _layer_keys(i) if k in SHARD_AXIS]
    arrs = [w[k][_layer_slot(k, i)].astype(F32 if k == "gdn_conv" else BF16) for k in keys]
    by_rows = [SHARD_AXIS[k] == 1 for k in keys]

    def arrived(outs):
        outs = _gather_fill(outs, arrs, by_rows, "gather_fill_" + ("mla" if i % 2 == 0 else "gdn"))
        full = {k: w[k][_layer_slot(k, i)] for k in _layer_keys(i) if k not in SHARD_AXIS}
        for k, o in zip(keys, outs):
            by_cols = SHARD_AXIS[k] == 2 and k != "mlp_w1"
            full[k] = jnp.concatenate([o[d] for d in range(N_CHIPS)], axis=1) if by_cols else o
        return full

    return _gather_plan(arrs, by_rows), arrived


EARLY = ["mlp_w1", "mlp_w2", "w_out", "mem_w_kv"]


def _reduce_group(i, G, early):
    kind = ("mla" if i % 2 == 0 else "gdn") + ("_early" if early else "_late")
    keys = [k for k in _layer_keys(i) if k in BIG and (k in EARLY) == early]
    canon = []
    for k in keys:
        g = G[k]
        if k == "mlp_w1":
            g = g.reshape(N_CHIPS, 2, g.shape[1] // 2, g.shape[2])
        elif SHARD_AXIS[k] == 1:
            g = g.reshape(N_CHIPS, 2, g.shape[0] // (2 * N_CHIPS), g.shape[1])
        else:
            rows, cw = g.shape[0], g.shape[1] // N_CHIPS
            g = g.reshape(rows, N_CHIPS, cw).transpose(1, 0, 2).reshape(N_CHIPS, 2, rows // 2, cw)
        canon.append(g)
    theirs = _swap_halves(canon, "grad_swap_" + kind)
    chip_sums = [_add_core(g, t, "grad_add_core") for g, t in zip(canon, theirs)]

    def done(got):
        halves = [_add_chips(p, s, "grad_add_chips") for p, s in zip(chip_sums, got)]
        joined = _join_halves(halves, "grad_join_" + kind)
        return {k: j.reshape(2 * j.shape[1], j.shape[2]) for k, j in zip(keys, joined)}

    return _scatter_plan(chip_sums), done


def _adamw(w, g, m, v, name):
    shape = w.shape
    cols = shape[-1]
    rows = math.prod(shape[:-1])
    tr = _tile(rows, max(SUBLANES, (1 << 19) // cols // SUBLANES * SUBLANES), SUBLANES)
    spec = ((tr, cols), _row)
    outs = _ew(_adamw_fn, [(a.reshape(rows, cols), *spec) for a in (w, g, m, v)], [(_sds((rows, cols)), *spec)] * 3,
               (rows // tr,), name=name)
    return [o.reshape(shape) for o in outs]


def kernel(x, mem, positions, mla_w_in, mla_q_norm, mla_w_uq, mla_kv_norm, mla_w_ukv, gdn_w_in, gdn_conv, gdn_a_log, gdn_dt_bias, gdn_o_norm, mem_w_kv, w_out, ln1_g, ln1_b, mlp_w1, mlp_w2, ln2_g, ln2_b, loss_target, m_mla_w_in, m_mla_q_norm, m_mla_w_uq, m_mla_kv_norm, m_mla_w_ukv, m_gdn_w_in, m_gdn_conv, m_gdn_a_log, m_gdn_dt_bias, m_gdn_o_norm, m_mem_w_kv, m_w_out, m_ln1_g, m_ln1_b, m_mlp_w1, m_mlp_w2, m_ln2_g, m_ln2_b, v_mla_w_in, v_mla_q_norm, v_mla_w_uq, v_mla_kv_norm, v_mla_w_ukv, v_gdn_w_in, v_gdn_conv, v_gdn_a_log, v_gdn_dt_bias, v_gdn_o_norm, v_mem_w_kv, v_w_out, v_ln1_g, v_ln1_b, v_mlp_w1, v_mlp_w2, v_ln2_g, v_ln2_b):
    w = dict(zip(WEIGHTS, (mla_w_in, mla_q_norm, mla_w_uq, mla_kv_norm, mla_w_ukv, gdn_w_in, gdn_conv, gdn_a_log,
                           gdn_dt_bias, gdn_o_norm, mem_w_kv, w_out, ln1_g, ln1_b, mlp_w1, mlp_w2, ln2_g, ln2_b)))
    m = dict(zip(WEIGHTS, (m_mla_w_in, m_mla_q_norm, m_mla_w_uq, m_mla_kv_norm, m_mla_w_ukv, m_gdn_w_in, m_gdn_conv,
                           m_gdn_a_log, m_gdn_dt_bias, m_gdn_o_norm, m_mem_w_kv, m_w_out, m_ln1_g, m_ln1_b, m_mlp_w1,
                           m_mlp_w2, m_ln2_g, m_ln2_b)))
    v = dict(zip(WEIGHTS, (v_mla_w_in, v_mla_q_norm, v_mla_w_uq, v_mla_kv_norm, v_mla_w_ukv, v_gdn_w_in, v_gdn_conv,
                           v_gdn_a_log, v_gdn_dt_bias, v_gdn_o_norm, v_mem_w_kv, v_w_out, v_ln1_g, v_ln1_b, v_mlp_w1,
                           v_mlp_w2, v_ln2_g, v_ln2_b)))
    assert x.shape[0] == 1, "one sequence per device"
    full_shapes = {k: w[k].shape for k in WEIGHTS}
    for k, ax in SHARD_AXIS.items():
        s = list(w[k].shape)
        s[ax] *= N_CHIPS
        full_shapes[k] = tuple(s)
    c = _dims(x.shape[1], x.shape[2], mem.shape[1], full_shapes)
    depth = c["DEPTH"]

    first, arrived = _gather_layer(w, 0)
    W = [arrived(_run_plan(first, "gather_first"))]
    loss_local, grad_x, G = _local_step(x[0], mem[0], positions[0], W, loss_target[0], c,
                                        next_weights=lambda i: _gather_layer(w, i), grad_sink=_reduce_group)
    loss = lax.psum(loss_local, ("x", "y", "c"))

    def stacked(k):
        return jnp.stack([G[i][k] for i in range(depth) if k in G[i]], axis=0)

    grads = {k: stacked(k) for k in BIG}
    small_shapes = [full_shapes[k] for k in SMALL]
    gsmall = _all_reduce_small(_pack_rows([stacked(k) for k in SMALL], F32, SUBLANES), "grad_all_reduce_small")
    grads.update(dict(zip(SMALL, _unpack_rows(gsmall, small_shapes))))
    conv_cols = w["gdn_conv"].shape[2]
    grads["gdn_conv"] = lax.dynamic_slice_in_dim(grads["gdn_conv"], _my_chip() * conv_cols, conv_cols, axis=2)

    delta, new_m, new_v = {}, {}, {}
    for k in BIG + ["gdn_conv"]:
        delta[k], new_m[k], new_v[k] = _adamw(w[k], grads[k], m[k], v[k], "adamw")
    small = [k for k in SMALL if k != "gdn_conv"]
    packed = [_pack_rows([d[k] for k in small], F32, SUBLANES) for d in (w, grads, m, v)]
    ds, ms, vs = _adamw(*packed, "adamw_small")
    for d, buf in ((delta, ds), (new_m, ms), (new_v, vs)):
        d.update(dict(zip(small, _unpack_rows(buf, [w[k].shape for k in small]))))

    return (loss, grad_x[None], *[grads[k] for k in WEIGHTS], *[delta[k] for k in WEIGHTS],
            *[new_m[k] for k in WEIGHTS], *[new_v[k] for k in WEIGHTS])
```

```python
import functools
import math

import jax
import jax.numpy as jnp
from jax import lax
from jax.experimental import pallas as pl
from jax.experimental.pallas import tpu as pltpu

F32 = jnp.float32
BF16 = jnp.bfloat16
MESH = pl.DeviceIdType.MESH

LANES = 128
SUBLANES = 8
VMEM_LIMIT = 56 * 1024 * 1024
N_CHIPS = 4

HEAD_DIM = 128
QK_NOPE = 128
QK_ROPE = 64
QK_PAD = 256
ROPE_THETA = 10000.0
CONV_WIDTH = 4
CHUNK = 64
LN_EPS = 1e-5
RMS_EPS = 1e-6
ADAM_LR = 0.001
ADAM_B1 = 0.9
ADAM_B2 = 0.999
ADAM_EPS = 1e-08
ADAM_WD = 0.01
ADAM_STEP = 10
HI = lax.Precision.HIGHEST


def _cparams(sem=None):
    return pltpu.CompilerParams(dimension_semantics=sem, vmem_limit_bytes=VMEM_LIMIT)


def _tile(n, cap, unit):
    best = None
    t = unit
    while t <= min(n, cap):
        if n % t == 0:
            best = t
        t += unit
    return best if best is not None else n


def _mm(a, b, mode, out_dtypes, *, name, epilogue=None, extras=(), tm_cap=1024, tn_cap=1024, tk_cap=2048,
        b_major=False, out_major=False):
    if b_major:
        b_shape = (b.shape[1], N_CHIPS * b.shape[2])
    else:
        b_shape = b.shape
    if mode == "nn":
        (M, K), (K2, N) = a.shape, b_shape
    elif mode == "nt":
        (M, K), (N, K2) = a.shape, b_shape
    else:
        (K, M), (K2, N) = a.shape, b_shape
    assert K == K2, (a.shape, b.shape, mode)
    tm = _tile(M, tm_cap, LANES if mode == "tn" else 16)
    tn = _tile(N // N_CHIPS if (out_major or (b_major and mode == "nn")) else N, tn_cap, LANES)
    tk = _tile(K // N_CHIPS if (b_major and mode == "nt") else K, tk_cap, 16 if mode == "tn" else LANES)
    nk = K // tk
    nj4, nk4 = max(N // N_CHIPS // tn, 1), max(K // N_CHIPS // tk, 1)
    if mode == "nn":
        a_spec = pl.BlockSpec((tm, tk), lambda i, j, k: (i, k))
        b_spec = pl.BlockSpec((tk, tn), lambda i, j, k: (k, j))
        if b_major:
            b_spec = pl.BlockSpec((None, tk, tn), lambda i, j, k: (j // nj4, k, j % nj4))
        dims = (((1,), (0,)), ((), ()))
    elif mode == "nt":
        a_spec = pl.BlockSpec((tm, tk), lambda i, j, k: (i, k))
        b_spec = pl.BlockSpec((tn, tk), lambda i, j, k: (j, k))
        if b_major:
            b_spec = pl.BlockSpec((None, tn, tk), lambda i, j, k: (k // nk4, j, k % nk4))
        dims = (((1,), (1,)), ((), ()))
    else:
        assert not b_major
        a_spec = pl.BlockSpec((tk, tm), lambda i, j, k: (k, i))
        b_spec = pl.BlockSpec((tk, tn), lambda i, j, k: (k, j))
        dims = (((0,), (0,)), ((), ()))
    mn_spec = pl.BlockSpec((tm, tn), lambda i, j, k: (i, j))
    o_spec, o_shape = mn_spec, (M, N)
    if out_major:
        o_spec = pl.BlockSpec((None, tm, tn), lambda i, j, k: (j // nj4, i, j % nj4))
        o_shape = (N_CHIPS, M, N // N_CHIPS)
    n_ex, n_out = len(extras), len(out_dtypes)
    for e in extras:
        assert e.shape == (M, N), (e.shape, M, N)

    def body(a_ref, b_ref, *rest):
        ex_refs, out_refs, acc = rest[:n_ex], rest[n_ex:n_ex + n_out], rest[-1]
        k = pl.program_id(2)

        @pl.when(k == 0)
        def _():
            acc[...] = jnp.zeros_like(acc)

        acc[...] += lax.dot_general(a_ref[...].astype(BF16), b_ref[...].astype(BF16), dims,
                                    preferred_element_type=F32)

        @pl.when(k == nk - 1)
        def _():
            res = (acc[...],) if epilogue is None else epilogue(acc[...], *[e[...] for e in ex_refs])
            for o_ref, r in zip(out_refs, res):
                o_ref[...] = r.astype(o_ref.dtype)

    outs = pl.pallas_call(
        body, name=name, grid=(M // tm, N // tn, nk),
        in_specs=[a_spec, b_spec] + [mn_spec] * n_ex,
        out_specs=[o_spec] * n_out,
        out_shape=[jax.ShapeDtypeStruct(o_shape, d) for d in out_dtypes],
        scratch_shapes=[pltpu.VMEM((tm, tn), F32)],
        compiler_params=_cparams(("parallel", "parallel", "arbitrary")),
    )(a, b, *extras)
    return outs


def _spec(block, imap):
    return pl.BlockSpec(block, imap)


def _ew(fn, ins, outs, grid, *, name, acc_out=()):
    n_in = len(ins)
    ng = len(grid)

    def body(*refs):
        in_refs, out_refs = refs[:n_in], refs[n_in:]
        res = fn(*[r[...] for r in in_refs])
        first = functools.reduce(jnp.logical_and, [pl.program_id(d) == 0 for d in range(ng)])
        for i, (o_ref, r) in enumerate(zip(out_refs, res)):
            if i in acc_out:
                @pl.when(first)
                def _(o_ref=o_ref):
                    o_ref[...] = jnp.zeros_like(o_ref)
                o_ref[...] += r.astype(o_ref.dtype)
            else:
                o_ref[...] = r.astype(o_ref.dtype)

    return pl.pallas_call(
        body, name=name, grid=grid,
        in_specs=[_spec(b, m) for _, b, m in ins],
        out_specs=[_spec(b, m) for _, b, m in outs],
        out_shape=[s for s, _, _ in outs],
        compiler_params=_cparams(("arbitrary",) * ng),
    )(*[a for a, _, _ in ins])


def _ew_vjp(fn, ins, cts, gouts, grid, *, name, copies=()):
    n_in, n_ct = len(ins), len(cts)
    ng = len(grid)
    want = [i for i, g in enumerate(gouts) if g is not None]

    def body(*refs):
        in_refs, ct_refs, out_refs = refs[:n_in], refs[n_in:n_in + n_ct], refs[n_in + n_ct:]
        prim = [r[...] for r in in_refs]
        outs, pull = jax.vjp(fn, *prim)
        grads = pull(tuple(r[...].astype(o.dtype) for r, o in zip(ct_refs, outs)))
        first_all = functools.reduce(jnp.logical_and, [pl.program_id(d) == 0 for d in range(ng)])
        for o_ref, (i, _) in zip(out_refs[len(want):], copies):
            o_ref[...] = grads[i].astype(o_ref.dtype)
        for o_ref, i in zip(out_refs, want):
            mode = gouts[i][3]
            g = grads[i]
            if mode == "set":
                o_ref[...] = g.astype(o_ref.dtype)
            elif mode == "acc":
                @pl.when(pl.program_id(ng - 1) == 0)
                def _(o_ref=o_ref):
                    o_ref[...] = jnp.zeros_like(o_ref)
                o_ref[...] += g.astype(o_ref.dtype)
            elif mode == "acc_all":
                @pl.when(first_all)
                def _(o_ref=o_ref):
                    o_ref[...] = jnp.zeros_like(o_ref)
                o_ref[...] += g.astype(o_ref.dtype)
            else:
                @pl.when(first_all)
                def _(o_ref=o_ref):
                    o_ref[...] = jnp.zeros_like(o_ref)
                idx = pl.program_id(mode[1])
                o_ref[idx] += g.astype(o_ref.dtype)

    return pl.pallas_call(
        body, name=name, grid=grid,
        in_specs=[_spec(b, m) for _, b, m in ins] + [_spec(b, m) for _, b, m in cts],
        out_specs=[_spec(gouts[i][1], gouts[i][2]) for i in want] + [_spec(gouts[i][1], gouts[i][2]) for i, _ in copies],
        out_shape=[gouts[i][0] for i in want] + [s for _, s in copies],
        compiler_params=_cparams(("arbitrary",) * ng),
    )(*[a for a, _, _ in ins], *[a for a, _, _ in cts])


def _sds(shape, dtype=F32):
    return jax.ShapeDtypeStruct(tuple(shape), dtype)


def _ln_fn(z, g, b):
    mu = jnp.mean(z, -1, keepdims=True)
    d = z - mu
    var = jnp.mean(d * d, -1, keepdims=True)
    y = d * lax.rsqrt(var + LN_EPS) * g + b
    return y, y


def _rms_fn(x, g):
    return (x * lax.rsqrt(jnp.mean(x * x, -1, keepdims=True) + RMS_EPS) * g,)


@jax.custom_vjp
def _rot_half(x):
    lane = lax.broadcasted_iota(jnp.int32, x.shape, x.ndim - 1)
    up = pltpu.roll(x, LANES - QK_ROPE // 2, x.ndim - 1)
    dn = pltpu.roll(x, QK_ROPE // 2, x.ndim - 1)
    return jnp.where(lane < QK_ROPE // 2, -up, jnp.where(lane < QK_ROPE, dn, 0.0))


def _rot_half_fwd(x):
    return _rot_half(x), None


def _rot_half_bwd(_, ct):
    return (-_rot_half(ct),)


_rot_half.defvjp(_rot_half_fwd, _rot_half_bwd)


def _rope_blk(x, cos, sin):
    return x * cos + _rot_half(x) * sin


def _mla_prep_fn(qraw, knope, kr, cos, sin):
    qn, qr = qraw[:, :QK_NOPE], qraw[:, QK_NOPE:]
    q = jnp.concatenate([qn, _rope_blk(qr, cos, sin)], axis=1)
    k = jnp.concatenate([knope.astype(F32), _rope_blk(kr, cos, sin)], axis=1)
    return q, k


def _l2n(x):
    return x * lax.rsqrt(jnp.sum(x * x, -1, keepdims=True) + 1e-6)


def _gdn_qk_fn(qc, kc):
    return _l2n(qc) * (HEAD_DIM ** -0.5), _l2n(kc)


def _softplus(x):
    return jnp.maximum(x, 0.0) + jnp.log(1.0 + jnp.exp(-jnp.abs(x)))


def _sigmoid(x):
    return 1.0 / (1.0 + jnp.exp(-x))


def _silu(x):
    return x * _sigmoid(x)


def _gdn_gate_fn(n_heads, head_axis):
    def fn(ab, a_log, dt_bias):
        h = pl.program_id(head_axis)
        lane = lax.broadcasted_iota(jnp.int32, ab.shape, 1)
        a_in = jnp.sum(jnp.where(lane == h, ab, 0.0), -1, keepdims=True)
        b_in = jnp.sum(jnp.where(lane == h + n_heads, ab, 0.0), -1, keepdims=True)
        g = -jnp.exp(a_log[:, :CHUNK]) * _softplus(a_in + dt_bias[:, :CHUNK])
        beta = _sigmoid(b_in) + jnp.zeros_like(g)
        return g, beta
    return fn


def _gdn_out_fn(o, z, w):
    return (o * lax.rsqrt(jnp.mean(o * o, -1, keepdims=True) + RMS_EPS) * w * _silu(z),)


def _loss_fn(y, t):
    d = y - t
    return (jnp.sum(d * d, axis=0, keepdims=True) * (0.5 / y.shape[-1]), d * (1.0 / y.shape[-1]))


def _adamw_fn(w, g, m, v):
    m = ADAM_B1 * m + (1.0 - ADAM_B1) * g
    v = ADAM_B2 * v + (1.0 - ADAM_B2) * (g * g)
    m_hat = m / (1.0 - ADAM_B1 ** ADAM_STEP)
    v_hat = v / (1.0 - ADAM_B2 ** ADAM_STEP)
    delta = -ADAM_LR * (m_hat / (jnp.sqrt(v_hat) + ADAM_EPS) + ADAM_WD * w)
    return delta, m, v


def _causal_mask(shape, row0, col0):
    row = lax.broadcasted_iota(jnp.int32, shape, 0) + row0
    col = lax.broadcasted_iota(jnp.int32, shape, 1) + col0
    return col <= row


def _rows(ref, i, t):
    return ref[pl.ds(pl.multiple_of(i * t, t), t), :]


def _walk(first, n_loop, tail, products, update):
    stop = first + n_loop
    t0 = tail[0][0]

    def step(j, carry):
        nxt = products(jnp.where(j + 1 < stop, j + 1, t0))
        update(carry, j, False)
        return nxt

    carry = lax.fori_loop(first, stop, step, products(jnp.where(n_loop > 0, first, t0)))
    for n, (j, masked) in enumerate(tail):
        nxt = products(tail[n + 1][0]) if n + 1 < len(tail) else None
        update(carry, j, masked)
        carry = nxt


def _flash_fwd(q, k, v, *, H, dq, dv, qoff, koff, voff, causal, scale, tq, tk, name, side=None):
    S, Sk = q.shape[0], k.shape[0]
    nq = S // tq
    assert (tq == tk and S == Sk) or not causal

    def body(q_ref, k_ref, v_ref, o_ref, lse_ref, m_s, l_s, acc):
        qi = pl.program_id(1)
        m_s[...] = jnp.full_like(m_s, -jnp.inf)
        l_s[...] = jnp.zeros_like(l_s)
        acc[...] = jnp.zeros_like(acc)
        qb = q_ref[...].astype(BF16)

        def products(j):
            return lax.dot_general(qb, _rows(k_ref, j, tk).astype(BF16), (((1,), (1,)), ((), ())),
                                   preferred_element_type=F32)

        def update(s, j, masked):
            s = s * scale
            if masked:
                s = jnp.where(_causal_mask(s.shape, qi * tq, j * tk), s, -jnp.inf)
            m_prev = m_s[...]
            m_new = jnp.maximum(m_prev, jnp.max(s, axis=1, keepdims=True))
            alpha = jnp.exp(m_prev - m_new)
            p = jnp.exp(s - m_new[:, :1])
            l_s[...] = alpha * l_s[...] + jnp.sum(p, axis=1, keepdims=True)
            acc[...] = acc[...] * alpha[:, :1] + lax.dot_general(
                p.astype(BF16), _rows(v_ref, j, tk).astype(BF16), (((1,), (0,)), ((), ())), preferred_element_type=F32)
            m_s[...] = m_new

        if causal:
            _walk(0, qi, [(qi, True)], products, update)
        else:
            _walk(0, Sk // tk - 1, [(Sk // tk - 1, False)], products, update)
        o_ref[...] = (acc[...] / l_s[...][:, :1]).astype(o_ref.dtype)
        lse_ref[...] = m_s[...] + jnp.log(l_s[...])

    body, s_in, s_out, s_shape, s_sems, s_args = _with_side(body, 3, 2, (H, nq), side)
    o, lse, *side_outs = pl.pallas_call(
        body, name=name, grid=(H, nq),
        in_specs=[pl.BlockSpec((tq, dq), lambda h, qi: (qi, qoff + h)),
                  pl.BlockSpec((Sk, dq), lambda h, qi: (0, koff + h)),
                  pl.BlockSpec((Sk, dv), lambda h, qi: (0, voff + h))] + s_in,
        out_specs=[pl.BlockSpec((tq, dv), lambda h, qi: (qi, h)),
                   pl.BlockSpec((tq, LANES), lambda h, qi: (qi, h))] + s_out,
        out_shape=[_sds((S, H * dv)), _sds((S, H * LANES))] + s_shape,
        scratch_shapes=[pltpu.VMEM((tq, LANES), F32), pltpu.VMEM((tq, LANES), F32), pltpu.VMEM((tq, dv), F32)] + s_sems,
        compiler_params=_cparams(("arbitrary", "arbitrary")),
    )(q, k, v, *s_args)
    return o, lse, side_outs


def _flash_bwd(q, k, v, o, lse, do, *, H, dq, dv, qoff, koff, voff, dooff, causal, scale, tq, tk, name, side=None):
    S, Sk = q.shape[0], k.shape[0]
    nq, nk = S // tq, Sk // tk
    assert (tq == tk and S == Sk) or not causal
    nt = (((1,), (1,)), ((), ()))

    def body_q(q_ref, k_ref, v_ref, o_ref, do_ref, lse_ref, dq_ref, delta_ref, dq_acc):
        qi = pl.program_id(1)
        qb, dob = q_ref[...].astype(BF16), do_ref[...].astype(BF16)
        delta = jnp.sum(do_ref[...].astype(F32) * o_ref[...].astype(F32), axis=1, keepdims=True)
        delta_ref[...] = delta + jnp.zeros_like(delta_ref)
        lse1 = lse_ref[...][:, :1]
        dq_acc[...] = jnp.zeros_like(dq_acc)

        def products(j):
            return (lax.dot_general(qb, _rows(k_ref, j, tk).astype(BF16), nt, preferred_element_type=F32),
                    lax.dot_general(dob, _rows(v_ref, j, tk).astype(BF16), nt, preferred_element_type=F32))

        def update(sp, j, masked):
            s, dp = sp
            p = jnp.exp(s * scale - lse1)
            if masked:
                p = jnp.where(_causal_mask(s.shape, qi * tq, j * tk), p, 0.0)
            ds = p * (dp - delta) * scale
            dq_acc[...] += lax.dot_general(ds.astype(BF16), _rows(k_ref, j, tk).astype(BF16), (((1,), (0,)), ((), ())),
                                           preferred_element_type=F32)

        if causal:
            _walk(0, qi, [(qi, True)], products, update)
        else:
            _walk(0, nk - 1, [(nk - 1, False)], products, update)
        dq_ref[...] = dq_acc[...]

    body_q, s_in, s_out, s_shape, s_sems, s_args = _with_side(body_q, 6, 2, (H, nq), side)
    dqq, delta, *side_outs = pl.pallas_call(
        body_q, name=name + "_dq", grid=(H, nq),
        in_specs=[pl.BlockSpec((tq, dq), lambda h, qi: (qi, qoff + h)),
                  pl.BlockSpec((Sk, dq), lambda h, qi: (0, koff + h)),
                  pl.BlockSpec((Sk, dv), lambda h, qi: (0, voff + h)),
                  pl.BlockSpec((tq, dv), lambda h, qi: (qi, h)),
                  pl.BlockSpec((tq, dv), lambda h, qi: (qi, dooff + h)),
                  pl.BlockSpec((tq, LANES), lambda h, qi: (qi, h))] + s_in,
        out_specs=[pl.BlockSpec((tq, dq), lambda h, qi: (qi, h)),
                   pl.BlockSpec((tq, LANES), lambda h, qi: (qi, h))] + s_out,
        out_shape=[_sds((S, H * dq)), _sds((S, H * LANES))] + s_shape,
        scratch_shapes=[pltpu.VMEM((tq, dq), F32)] + s_sems,
        compiler_params=_cparams(("arbitrary", "arbitrary")),
    )(q, k, v, o, do, lse, *s_args)

    def as_rows(t):
        return t[:, ::LANES].T.reshape(H, nq, 1, tq)

    nn = (((1,), (0,)), ((), ()))

    def body_kv(q_ref, k_ref, v_ref, do_ref, lse_ref, delta_ref, dk_ref, dv_ref, dk_acc, dv_acc):
        kj = pl.program_id(1)
        kb, vb = k_ref[...].astype(BF16), v_ref[...].astype(BF16)
        dk_acc[...] = jnp.zeros_like(dk_acc)
        dv_acc[...] = jnp.zeros_like(dv_acc)

        def products(i):
            return (lax.dot_general(kb, _rows(q_ref, i, tq).astype(BF16), nt, preferred_element_type=F32),
                    lax.dot_general(vb, _rows(do_ref, i, tq).astype(BF16), nt, preferred_element_type=F32))

        def update(sp, i, masked):
            st, dpt = sp
            pt = jnp.exp(st * scale - lse_ref[i])
            if masked:
                key = lax.broadcasted_iota(jnp.int32, st.shape, 0) + kj * tk
                qry = lax.broadcasted_iota(jnp.int32, st.shape, 1) + i * tq
                pt = jnp.where(key <= qry, pt, 0.0)
            dst = pt * (dpt - delta_ref[i]) * scale
            dv_acc[...] += lax.dot_general(pt.astype(BF16), _rows(do_ref, i, tq).astype(BF16), nn,
                                           preferred_element_type=F32)
            dk_acc[...] += lax.dot_general(dst.astype(BF16), _rows(q_ref, i, tq).astype(BF16), nn,
                                           preferred_element_type=F32)

        if causal:
            _walk(kj + 1, nq - 1 - kj, [(kj, True)], products, update)
        else:
            _walk(0, nq - 1, [(nq - 1, False)], products, update)
        dk_ref[...] = dk_acc[...]
        dv_ref[...] = dv_acc[...]

    row_spec = pl.BlockSpec((None, nq, 1, tq), lambda h, kj: (h, 0, 0, 0))
    dk, dvv = pl.pallas_call(
        body_kv, name=name + "_dkv", grid=(H, nk),
        in_specs=[pl.BlockSpec((S, dq), lambda h, kj: (0, qoff + h)),
                  pl.BlockSpec((tk, dq), lambda h, kj: (kj, koff + h)),
                  pl.BlockSpec((tk, dv), lambda h, kj: (kj, voff + h)),
                  pl.BlockSpec((S, dv), lambda h, kj: (0, dooff + h)), row_spec, row_spec],
        out_specs=[pl.BlockSpec((tk, dq), lambda h, kj: (kj, h)), pl.BlockSpec((tk, dv), lambda h, kj: (kj, h))],
        out_shape=[_sds((Sk, H * dq)), _sds((Sk, H * dv))],
        scratch_shapes=[pltpu.VMEM((tk, dq), F32), pltpu.VMEM((tk, dv), F32)],
        compiler_params=_cparams(("parallel", "arbitrary")),
    )(q, k, v, do, as_rows(lse), as_rows(delta))
    return dqq, dk, dvv, side_outs


def _flash_bwd_causal(q, k, v, o, lse, do, *, H, dq, dv, qoff, koff, voff, dooff, scale, tq, name, side=None):
    S = q.shape[0]
    nq = S // tq
    nt = (((1,), (1,)), ((), ()))
    tn = (((0,), (0,)), ((), ()))
    nn = (((1,), (0,)), ((), ()))

    def body(q_ref, k_ref, v_ref, o_ref, do_ref, lse_ref, dq_ref, dk_ref, dv_ref, dq_acc):
        qi = pl.program_id(1)

        @pl.when(qi == 0)
        def _():
            dk_ref[...] = jnp.zeros_like(dk_ref)
            dv_ref[...] = jnp.zeros_like(dv_ref)

        qb, dob = q_ref[...].astype(BF16), do_ref[...].astype(BF16)
        delta = jnp.sum(do_ref[...].astype(F32) * o_ref[...].astype(F32), axis=1, keepdims=True)
        lse1 = lse_ref[...][:, :1]
        dq_acc[...] = jnp.zeros_like(dq_acc)

        def products(j):
            return (lax.dot_general(qb, _rows(k_ref, j, tq).astype(BF16), nt, preferred_element_type=F32),
                    lax.dot_general(dob, _rows(v_ref, j, tq).astype(BF16), nt, preferred_element_type=F32))

        def update(sp, j, masked):
            s, dp = sp
            p = jnp.exp(s * scale - lse1)
            if masked:
                p = jnp.where(_causal_mask(s.shape, qi * tq, j * tq), p, 0.0)
            ds = (p * (dp - delta) * scale).astype(BF16)
            dq_acc[...] += lax.dot_general(ds, _rows(k_ref, j, tq).astype(BF16), nn, preferred_element_type=F32)
            rows = pl.ds(pl.multiple_of(j * tq, tq), tq)
            dk_ref[rows, :] += lax.dot_general(ds, qb, tn, preferred_element_type=F32)
            dv_ref[rows, :] += lax.dot_general(p.astype(BF16), dob, tn, preferred_element_type=F32)

        _walk(0, qi, [(qi, True)], products, update)
        dq_ref[...] = dq_acc[...]

    body, s_in, s_out, s_shape, s_sems, s_args = _with_side(body, 6, 3, (H, nq), side)
    dqq, dk, dvv, *side_outs = pl.pallas_call(
        body, name=name, grid=(H, nq),
        in_specs=[pl.BlockSpec((tq, dq), lambda h, qi: (qi, qoff + h)),
                  pl.BlockSpec((S, dq), lambda h, qi: (0, koff + h)),
                  pl.BlockSpec((S, dv), lambda h, qi: (0, voff + h)),
                  pl.BlockSpec((tq, dv), lambda h, qi: (qi, h)),
                  pl.BlockSpec((tq, dv), lambda h, qi: (qi, dooff + h)),
                  pl.BlockSpec((tq, LANES), lambda h, qi: (qi, h))] + s_in,
        out_specs=[pl.BlockSpec((tq, dq), lambda h, qi: (qi, h)),
                   pl.BlockSpec((S, dq), lambda h, qi: (0, h)), pl.BlockSpec((S, dv), lambda h, qi: (0, h))] + s_out,
        out_shape=[_sds((S, H * dq)), _sds((S, H * dq)), _sds((S, H * dv))] + s_shape,
        scratch_shapes=[pltpu.VMEM((tq, dq), F32)] + s_sems,
        compiler_params=_cparams(("arbitrary", "arbitrary")),
    )(q, k, v, o, do, lse, *s_args)
    return dqq, dk, dvv, side_outs


def _shift_down(x, prev8, j):
    if j == 0:
        return x
    y = pltpu.roll(x, j, 0)
    head = pltpu.roll(prev8, j, 0)
    row = lax.broadcasted_iota(jnp.int32, x.shape, 0)
    reps = x.shape[0] // SUBLANES
    return jnp.where(row < j, jnp.tile(head, (reps, 1)), y)


def _shift_up(x, next8, j):
    if j == 0:
        return x
    n = x.shape[0]
    y = pltpu.roll(x, n - j, 0)
    tail = pltpu.roll(next8, SUBLANES - j, 0)
    row = lax.broadcasted_iota(jnp.int32, x.shape, 0)
    reps = n // SUBLANES
    return jnp.where(row >= n - j, jnp.tile(tail, (reps, 1)), y)


def _conv_pre(x_ref, p_ref, w_ref, first):
    x = x_ref[...]
    prev8 = jnp.where(first, 0.0, p_ref[...])
    w = w_ref[...]
    xs = [_shift_down(x, prev8, CONV_WIDTH - 1 - j) for j in range(CONV_WIDTH)]
    c = sum(xs[j] * w[j:j + 1, :] for j in range(CONV_WIDTH))
    return c, xs


def _conv_specs(ts, tc, C_total_blocks_off):
    rb = ts // SUBLANES
    off = C_total_blocks_off
    x_spec = pl.BlockSpec((ts, tc), lambda ci, i: (i, off + ci))
    p_spec = pl.BlockSpec((SUBLANES, tc), lambda ci, i: (jnp.maximum(i * rb - 1, 0), off + ci))
    return x_spec, p_spec


def _conv_fwd(h, w, *, C, ts, tc, name):
    S = h.shape[0]
    x_spec, p_spec = _conv_specs(ts, tc, 0)

    def body(x_ref, p_ref, w_ref, y_ref):
        c, _ = _conv_pre(x_ref, p_ref, w_ref, pl.program_id(1) == 0)
        y_ref[...] = _silu(c)

    return pl.pallas_call(
        body, name=name, grid=(C // tc, S // ts),
        in_specs=[x_spec, p_spec, pl.BlockSpec((CONV_WIDTH, tc), lambda ci, i: (0, ci))],
        out_specs=pl.BlockSpec((ts, tc), lambda ci, i: (i, ci)),
        out_shape=_sds((S, C)),
        compiler_params=_cparams(("parallel", "arbitrary")),
    )(h, h, w)


def _conv_bwd(h, w, dy, *, C, ts, tc, name):
    S = h.shape[0]
    ns = S // ts
    rb = ts // SUBLANES
    x_spec, p_spec = _conv_specs(ts, tc, 0)

    def body_a(x_ref, p_ref, w_ref, dy_ref, dc_ref, dw_ref):
        i = pl.program_id(1)
        c, xs = _conv_pre(x_ref, p_ref, w_ref, i == 0)
        sg = _sigmoid(c)
        dc = dy_ref[...] * (sg * (1.0 + c * (1.0 - sg)))
        dc_ref[...] = dc

        @pl.when(i == 0)
        def _():
            dw_ref[...] = jnp.zeros_like(dw_ref)

        dw_ref[...] += jnp.concatenate([jnp.sum(dc * xs[j], axis=0, keepdims=True) for j in range(CONV_WIDTH)], axis=0)

    dc, dw = pl.pallas_call(
        body_a, name=name + "_a", grid=(C // tc, ns),
        in_specs=[x_spec, p_spec, pl.BlockSpec((CONV_WIDTH, tc), lambda ci, i: (0, ci)),
                  pl.BlockSpec((ts, tc), lambda ci, i: (i, ci))],
        out_specs=[pl.BlockSpec((ts, tc), lambda ci, i: (i, ci)),
                   pl.BlockSpec((CONV_WIDTH, tc), lambda ci, i: (0, ci))],
        out_shape=[_sds((S, C)), _sds((CONV_WIDTH, C))],
        compiler_params=_cparams(("parallel", "arbitrary")),
    )(h, h, w, dy)

    def body_b(dc_ref, n_ref, w_ref, dx_ref):
        i = pl.program_id(1)
        dcv = dc_ref[...]
        next8 = jnp.where(i == ns - 1, 0.0, n_ref[...])
        w_ = w_ref[...]
        dx_ref[...] = sum(_shift_up(dcv, next8, CONV_WIDTH - 1 - j) * w_[j:j + 1, :] for j in range(CONV_WIDTH))

    dx = pl.pallas_call(
        body_b, name=name + "_b", grid=(C // tc, ns),
        in_specs=[pl.BlockSpec((ts, tc), lambda ci, i: (i, ci)),
                  pl.BlockSpec((SUBLANES, tc), lambda ci, i: (jnp.minimum((i + 1) * rb, ns * rb - 1), ci)),
                  pl.BlockSpec((CONV_WIDTH, tc), lambda ci, i: (0, ci))],
        out_specs=pl.BlockSpec((ts, tc), lambda ci, i: (i, ci)),
        out_shape=_sds((S, C)),
        compiler_params=_cparams(("parallel", "arbitrary")),
    )(dc, dc, w)
    return dx, dw


def _bdot(a, b, ca, cb, precision=None):
    nb = a.ndim - 2
    batch = tuple(range(nb))
    return lax.dot_general(a, b, (((nb + ca,), (nb + cb,)), (batch, batch)), precision=precision,
                           preferred_element_type=F32)


@jax.custom_vjp
def _nn(a, b):
    return _bdot(a.astype(BF16), b.astype(BF16), 1, 0)


@jax.custom_vjp
def _nt(a, b):
    return _bdot(a.astype(BF16), b.astype(BF16), 1, 1)


@jax.custom_vjp
def _tn(a, b):
    return _bdot(a.astype(BF16), b.astype(BF16), 0, 0)


_nn.defvjp(lambda a, b: (_nn(a, b), (a, b)), lambda r, g: (_nt(g, r[1]), _tn(r[0], g)))
_nt.defvjp(lambda a, b: (_nt(a, b), (a, b)), lambda r, g: (_nn(g, r[1]), _tn(g, r[0])))
_tn.defvjp(lambda a, b: (_tn(a, b), (a, b)), lambda r, g: (_nt(r[1], g), _nn(r[0], g)))


def _dot3(a, b, ca, cb):
    ah, bh = a.astype(BF16), b.astype(BF16)
    al, bl = (a - ah.astype(F32)).astype(BF16), (b - bh.astype(F32)).astype(BF16)
    return _bdot(ah, bh, ca, cb) + (_bdot(ah, bl, ca, cb) + _bdot(al, bh, ca, cb))


@jax.custom_vjp
def _nn_x3(a, b):
    return _dot3(a, b, 1, 0)


@jax.custom_vjp
def _nt_x3(a, b):
    return _dot3(a, b, 1, 1)


@jax.custom_vjp
def _tn_x3(a, b):
    return _dot3(a, b, 0, 0)


_nn_x3.defvjp(lambda a, b: (_nn_x3(a, b), (a, b)), lambda r, g: (_nt_x3(g, r[1]), _tn_x3(r[0], g)))
_nt_x3.defvjp(lambda a, b: (_nt_x3(a, b), (a, b)), lambda r, g: (_nn_x3(g, r[1]), _tn_x3(g, r[0])))
_tn_x3.defvjp(lambda a, b: (_tn_x3(a, b), (a, b)), lambda r, g: (_nt_x3(r[1], g), _nn_x3(r[0], g)))


@jax.custom_vjp
def _nn_hi(a, b):
    return _bdot(a, b, 1, 0, HI)


@jax.custom_vjp
def _nt_hi(a, b):
    return _bdot(a, b, 1, 1, HI)


@jax.custom_vjp
def _tn_hi(a, b):
    return _bdot(a, b, 0, 0, HI)


_nn_hi.defvjp(lambda a, b: (_nn_hi(a, b), (a, b)), lambda r, g: (_nt_hi(g, r[1]), _tn_hi(r[0], g)))
_nt_hi.defvjp(lambda a, b: (_nt_hi(a, b), (a, b)), lambda r, g: (_nn_hi(g, r[1]), _tn_hi(g, r[0])))
_tn_hi.defvjp(lambda a, b: (_tn_hi(a, b), (a, b)), lambda r, g: (_nt_hi(r[1], g), _nn_hi(r[0], g)))


@jax.custom_vjp
def _inverse_given(lmat, t):
    return t


_inverse_given.defvjp(lambda lmat, t: (t, t),
                      lambda t, ct: (-_tn_x3(t, _nt_x3(ct, t)), jnp.zeros_like(t)))


def _gdn_chunk_fn(q, k, v, g, beta, state, t_known=None):
    C = CHUNK
    B = q.shape[0]
    row = lax.broadcasted_iota(jnp.int32, (B, C, C), 1)
    col = lax.broadcasted_iota(jnp.int32, (B, C, C), 2)
    tril, strict = row >= col, row > col
    ones_tril = tril.astype(F32)
    gc = _nn_hi(ones_tril, g)
    gr = _nt_hi(jnp.full((B, C, C), 1.0 / C, F32), gc)
    decay = jnp.where(tril, jnp.exp(jnp.where(tril, gc - gr, 0.0)), 0.0)
    b1 = beta[:, :, :1]
    e_gc = jnp.exp(gc[:, :, :1])
    kb = k * b1
    lmat = jnp.where(strict, _nt(kb, k) * decay, 0.0)
    if t_known is None:
        a = -lmat
        t = jnp.where(row == col, 1.0, 0.0) + a
        p = a
        for _ in range(5):
            p = _nn_x3(p, p)
            t = t + _nn_x3(t, p)
    else:
        t = _inverse_given(lmat, t_known)
    rhs = jnp.concatenate([v * b1, kb * e_gc], axis=2)
    sol = _nn_x3(t, rhs)
    u, w = sol[:, :, :HEAD_DIM], sol[:, :, HEAD_DIM:]
    a_qk = jnp.where(tril, _nt(q, k) * decay, 0.0)
    gl = gc[:, C - 1:C, :1]
    q_dec = q * e_gc
    k_dec = k * jnp.exp(gl - gc[:, :, :1])
    v_new = u - _nn(w, state)
    o = _nn(q_dec, state) + _nn(a_qk, v_new)
    new_state = state * jnp.exp(gl) + _tn(k_dec, v_new)
    return o, new_state, t


def _split_heads(x, B):
    return jnp.stack([x[:, j * HEAD_DIM:(j + 1) * HEAD_DIM] for j in range(B)], axis=0)


def _merge_heads(x):
    return jnp.concatenate([x[j] for j in range(x.shape[0])], axis=1)


def _gdn_group(H, voff):
    return next(b for b in (12, 6, 4, 3, 2, 1) if H % b == 0 and voff % b == 0)


def _gdn_fwd(q, k, v, g, beta, *, H, voff, name, side=None):
    S = q.shape[0]
    N = S // CHUNK
    B = _gdn_group(H, voff)
    W = B * HEAD_DIM
    qs = lambda off: pl.BlockSpec((CHUNK, W), lambda h, n: (n, off // B + h))
    gs = pl.BlockSpec((B, CHUNK, CHUNK), lambda h, n: (h, n, 0))

    def body(q_ref, k_ref, v_ref, g_ref, b_ref, o_ref, st_ref, ti_ref, state):
        @pl.when(pl.program_id(1) == 0)
        def _():
            state[...] = jnp.zeros_like(state)

        s0 = state[...]
        st_ref[...] = s0
        o, s1, t = _gdn_chunk_fn(_split_heads(q_ref[...], B), _split_heads(k_ref[...], B),
                                 _split_heads(v_ref[...], B), g_ref[...], b_ref[...], s0)
        o_ref[...] = _merge_heads(o)
        ti_ref[...] = t
        state[...] = s1

    per_chunk = lambda d: pl.BlockSpec((B, None, d, d), lambda h, n: (h, n, 0, 0))
    body, s_in, s_out, s_shape, s_sems, s_args = _with_side(body, 5, 3, (H // B, N), side)
    o, states, tinv, *side_outs = pl.pallas_call(
        body, name=name, grid=(H // B, N),
        in_specs=[qs(0), qs(0), qs(voff), gs, gs] + s_in,
        out_specs=[qs(0), per_chunk(HEAD_DIM), per_chunk(CHUNK)] + s_out,
        out_shape=[_sds((S, H * HEAD_DIM)), _sds((H, N, HEAD_DIM, HEAD_DIM)), _sds((H, N, CHUNK, CHUNK))] + s_shape,
        scratch_shapes=[pltpu.VMEM((B, HEAD_DIM, HEAD_DIM), F32)] + s_sems,
        compiler_params=_cparams(("arbitrary", "arbitrary")),
    )(q, k, v, g, beta, *s_args)
    return o, states, tinv, side_outs


def _gdn_bwd(q, k, v, g, beta, states, tinv, do, *, H, voff, name, side=None):
    S = q.shape[0]
    N = S // CHUNK
    B = _gdn_group(H, voff)
    W = B * HEAD_DIM
    rs = lambda off: pl.BlockSpec((CHUNK, W), lambda h, n: (N - 1 - n, off // B + h))
    gs = pl.BlockSpec((B, CHUNK, CHUNK), lambda h, n: (h, N - 1 - n, 0))

    def body(q_ref, k_ref, v_ref, g_ref, b_ref, st_ref, ti_ref, do_ref, dq_ref, dk_ref, dv_ref, dg_ref, db_ref, dstate):
        @pl.when(pl.program_id(1) == 0)
        def _():
            dstate[...] = jnp.zeros_like(dstate)

        t_known = ti_ref[...]
        _, pull = jax.vjp(lambda *a: _gdn_chunk_fn(*a, t_known)[:2], _split_heads(q_ref[...], B),
                          _split_heads(k_ref[...], B), _split_heads(v_ref[...], B), g_ref[...], b_ref[...], st_ref[...])
        dq, dk, dv, dg, db, ds = pull((_split_heads(do_ref[...], B), dstate[...]))
        dq_ref[...] = _merge_heads(dq)
        dk_ref[...] = _merge_heads(dk)
        dv_ref[...] = _merge_heads(dv)
        dg_ref[...] = dg
        db_ref[...] = db
        dstate[...] = ds

    per_chunk = lambda d: pl.BlockSpec((B, None, d, d), lambda h, n: (h, N - 1 - n, 0, 0))
    body, s_in, s_out, s_shape, s_sems, s_args = _with_side(body, 8, 5, (H // B, N), side)
    dq, dk, dv, dg, db, *side_outs = pl.pallas_call(
        body, name=name, grid=(H // B, N),
        in_specs=[rs(0), rs(0), rs(voff), gs, gs, per_chunk(HEAD_DIM), per_chunk(CHUNK), rs(0)] + s_in,
        out_specs=[rs(0), rs(0), rs(0), gs, gs] + s_out,
        out_shape=[_sds((S, H * HEAD_DIM))] * 3 + [_sds((H, S, CHUNK))] * 2 + s_shape,
        scratch_shapes=[pltpu.VMEM((B, HEAD_DIM, HEAD_DIM), F32)] + s_sems,
        compiler_params=_cparams(("arbitrary", "arbitrary")),
    )(q, k, v, g, beta, states, tinv, do, *s_args)
    return dq, dk, dv, dg, db, side_outs


def _round_up(n, m):
    return (n + m - 1) // m * m


def _dims(S, D, M, shapes):
    c = dict(S=S, D=D, M=M)
    c["H"] = shapes["gdn_a_log"][-1]
    c["QL"] = shapes["mla_q_norm"][-1]
    c["KVL"] = shapes["mla_kv_norm"][-1]
    assert c["QL"] == c["KVL"]
    c["MEMW"] = shapes["mem_w_kv"][-1] // 2
    c["HM"] = c["MEMW"] // HEAD_DIM
    c["MW"] = c["H"] * HEAD_DIM
    c["F"] = shapes["mlp_w1"][-1]
    c["DEPTH"] = shapes["ln1_g"][0]
    c["ALPHA"] = (2 * c["DEPTH"]) ** 0.25
    c["MLA_IN"] = _round_up(c["QL"] + c["KVL"] + c["MEMW"] + LANES, 2 * LANES)
    c["GDN_IN"] = _round_up(4 * c["MW"] + c["MEMW"] + LANES, 2 * LANES)
    c["t_row"] = min(256, S)
    c["t_head"] = min(2048, S)
    c["t_conv"] = min(512, S)
    c["t_att"] = min(512, S)
    c["t_mem"] = min(2048, S)
    return c


def _pad_cols(w, n):
    return jnp.pad(w, ((0, 0), (0, n - w.shape[1])))


def _prep_mla_w_in(w, c):
    a = c["QL"] + c["KVL"]
    w = jnp.concatenate([w[:, :a], w[:, a + QK_ROPE:a + QK_ROPE + c["MEMW"]], w[:, a:a + QK_ROPE]], axis=1)
    return _pad_cols(w, c["MLA_IN"]).astype(BF16)


def _unprep_mla_w_in(dw, c):
    a, m = c["QL"] + c["KVL"], c["MEMW"]
    return jnp.concatenate([dw[:, :a], dw[:, a + m:a + m + QK_ROPE], dw[:, a:a + m]], axis=1)


def _prep_w_uq(w, c):
    w = w.reshape(c["QL"], c["H"], QK_NOPE + QK_ROPE)
    w = jnp.pad(w, ((0, 0), (0, 0), (0, QK_PAD - QK_NOPE - QK_ROPE)))
    return w.reshape(c["QL"], c["H"] * QK_PAD).astype(BF16)


def _unprep_w_uq(dw, c):
    return dw.reshape(c["QL"], c["H"], QK_PAD)[:, :, :QK_NOPE + QK_ROPE].reshape(c["QL"], c["H"] * (QK_NOPE + QK_ROPE))


def _prep_w_ukv(w, c):
    return w.reshape(c["KVL"], c["H"], 2, HEAD_DIM).transpose(0, 2, 1, 3).reshape(c["KVL"], 2 * c["MW"]).astype(BF16)


def _unprep_w_ukv(dw, c):
    return dw.reshape(c["KVL"], 2, c["H"], HEAD_DIM).transpose(0, 2, 1, 3).reshape(c["KVL"], 2 * c["MW"])


def _prep_gdn_w_in(w, c):
    a, h2 = 4 * c["MW"], 2 * c["H"]
    w = jnp.concatenate([w[:, :a], w[:, a + h2:], w[:, a:a + h2]], axis=1)
    return _pad_cols(w, c["GDN_IN"]).astype(BF16)


def _unprep_gdn_w_in(dw, c):
    a, h2, m = 4 * c["MW"], 2 * c["H"], c["MEMW"]
    return jnp.concatenate([dw[:, :a], dw[:, a + m:a + m + h2], dw[:, a:a + m]], axis=1)


def _lane_bcast(v):
    return jnp.broadcast_to(v.astype(F32)[:, None, None], (v.shape[0], 1, LANES))


def _row(i):
    return (i, 0)


def _par(i):
    return (0, 0)


def _layer_norm(z, g, b, c, name):
    S, D, ts = c["S"], c["D"], c["t_row"]
    return _ew(_ln_fn, [(z, (ts, D), _row), (g, (1, D), _par), (b, (1, D), _par)],
               [(_sds((S, D)), (ts, D), _row), (_sds((S, D), BF16), (ts, D), _row)], (S // ts,), name=name)


def _layer_norm_bwd(z, g, b, dy, c, name):
    S, D, ts = c["S"], c["D"], c["t_row"]
    fn = lambda z, g, b: _ln_fn(z, g, b)[:1]
    dz, dg, db, dzb = _ew_vjp(fn, [(z, (ts, D), _row), (g, (1, D), _par), (b, (1, D), _par)], [(dy, (ts, D), _row)],
                              [(_sds((S, D)), (ts, D), _row, "set"), (_sds((1, D)), (1, D), _par, "acc_all"),
                               (_sds((1, D)), (1, D), _par, "acc_all")], (S // ts,), name=name,
                              copies=[(0, _sds((S, D), BF16))])
    return dz, dzb, dg, db


def _mem_attn_fwd(h, qoff, memkv, c, name):
    return _flash_fwd(h, memkv, memkv, H=c["HM"], dq=HEAD_DIM, dv=HEAD_DIM, qoff=qoff, koff=0, voff=c["HM"],
                      causal=False, scale=HEAD_DIM ** -0.5, tq=c["t_mem"], tk=c["M"], name=name)[:2]


def _mem_attn_bwd(h, qoff, memkv, om, lsem, dcat, c, name):
    return _flash_bwd(h, memkv, memkv, om, lsem, dcat, H=c["HM"], dq=HEAD_DIM, dv=HEAD_DIM, qoff=qoff, koff=0,
                      voff=c["HM"], dooff=c["H"], causal=False, scale=HEAD_DIM ** -0.5, tq=c["t_mem"], tk=c["M"],
                      name=name)[:3]


def _mla_specs(c):
    H, ts = c["H"], c["t_head"]
    kr_blk = (c["QL"] + c["KVL"] + c["MEMW"]) // LANES
    hd = lambda i, h: (i, h)
    return [((ts, QK_PAD), hd), ((ts, HEAD_DIM), hd), ((ts, LANES), lambda i, h: (i, kr_blk)),
            ((ts, LANES), lambda i, h: (i, 0)), ((ts, LANES), lambda i, h: (i, 0))]


def _mla_fwd(xb, p, cosp, sinp, c, side=None):
    S, H, QL, ts, tr = c["S"], c["H"], c["QL"], c["t_head"], c["t_row"]
    h, = _mm(xb, p["w_in"], "nn", [F32], name="mla_in")
    nq, = _ew(_rms_fn, [(h, (tr, QL), lambda i: (i, 0)), (p["q_norm"], (1, QL), _par)],
              [(_sds((S, QL), BF16), (tr, QL), _row)], (S // tr,), name="mla_qnorm")
    nkv, = _ew(_rms_fn, [(h, (tr, QL), lambda i: (i, 1)), (p["kv_norm"], (1, QL), _par)],
               [(_sds((S, QL), BF16), (tr, QL), _row)], (S // tr,), name="mla_kvnorm")
    qraw, = _mm(nq, p["w_uq"], "nn", [F32], name="mla_uq")
    kvraw, = _mm(nkv, p["w_ukv"], "nn", [BF16], name="mla_ukv")
    sp = _mla_specs(c)
    ins = [(a, b, m) for a, (b, m) in zip([qraw, kvraw, h, cosp, sinp], sp)]
    qp, kp = _ew(_mla_prep_fn, ins, [(_sds((S, H * QK_PAD), BF16), (ts, QK_PAD), lambda i, h: (i, h))] * 2,
                 (S // ts, H), name="mla_rope")
    o, lse, got = _flash_fwd(qp, kp, kvraw, H=H, dq=QK_PAD, dv=HEAD_DIM, qoff=0, koff=0, voff=H, causal=True,
                             scale=(QK_NOPE + QK_ROPE) ** -0.5, tq=c["t_att"], tk=c["t_att"], name="mla_attn", side=side)
    return o, dict(h=h, nq=nq, nkv=nkv, qraw=qraw, kvraw=kvraw, qp=qp, kp=kp, o=o, lse=lse), got


def _mla_bwd(sv, p, cosp, sinp, dcat, dqm, c, side=None):
    S, H, QL, ts, tr = c["S"], c["H"], c["QL"], c["t_head"], c["t_row"]
    dqp, dkp, dv, got = _flash_bwd_causal(sv["qp"], sv["kp"], sv["kvraw"], sv["o"], sv["lse"], dcat, H=H, dq=QK_PAD,
                                          dv=HEAD_DIM, qoff=0, koff=0, voff=H, dooff=0,
                                          scale=(QK_NOPE + QK_ROPE) ** -0.5, tq=c["t_att"], name="mla_attn_bwd",
                                          side=side)
    sp = _mla_specs(c)
    ins = [(a, b, m) for a, (b, m) in zip([sv["qraw"], sv["kvraw"], sv["h"], cosp, sinp], sp)]
    hd = lambda i, h: (i, h)
    dqraw, dknope, dkr = _ew_vjp(
        _mla_prep_fn, ins, [(dqp, (ts, QK_PAD), hd), (dkp, (ts, QK_PAD), hd)],
        [(_sds((S, H * QK_PAD), BF16), (ts, QK_PAD), hd, "set"), (_sds((S, H * HEAD_DIM), BF16), (ts, HEAD_DIM), hd, "set"),
         (_sds((S, LANES)), (ts, LANES), lambda i, h: (i, 0), "acc"), None, None], (S // ts, H), name="mla_rope_bwd")
    dkvraw = jnp.concatenate([dknope, dv.astype(BF16)], axis=1)
    dnq, = _mm(dqraw, p["w_uq"], "nt", [F32], name="mla_uq_dx")
    dw_uq, = _mm(sv["nq"], dqraw, "tn", [F32], name="mla_uq_dw")
    dnkv, = _mm(dkvraw, p["w_ukv"], "nt", [F32], name="mla_ukv_dx")
    dw_ukv, = _mm(sv["nkv"], dkvraw, "tn", [F32], name="mla_ukv_dw")
    dcq, dgq = _ew_vjp(_rms_fn, [(sv["h"], (tr, QL), lambda i: (i, 0)), (p["q_norm"], (1, QL), _par)],
                       [(dnq, (tr, QL), _row)],
                       [(_sds((S, QL), BF16), (tr, QL), _row, "set"), (_sds((1, QL)), (1, QL), _par, "acc_all")],
                       (S // tr,), name="mla_qnorm_bwd")
    dckv, dgkv = _ew_vjp(_rms_fn, [(sv["h"], (tr, QL), lambda i: (i, 1)), (p["kv_norm"], (1, QL), _par)],
                         [(dnkv, (tr, QL), _row)],
                         [(_sds((S, QL), BF16), (tr, QL), _row, "set"), (_sds((1, QL)), (1, QL), _par, "acc_all")],
                         (S // tr,), name="mla_kvnorm_bwd")
    pad = c["MLA_IN"] - (2 * QL + c["MEMW"] + LANES)
    dh = jnp.concatenate([dcq, dckv, dqm.astype(BF16), dkr.astype(BF16)] + ([jnp.zeros((S, pad), BF16)] if pad else []),
                         axis=1)
    grads = dict(mla_q_norm=dgq[0], mla_kv_norm=dgkv[0], mla_w_uq=_unprep_w_uq(dw_uq, c),
                 mla_w_ukv=_unprep_w_ukv(dw_ukv, c))
    return dh, grads, got


def _gdn_ins(h, qkvc, p, c):
    H, ts = c["H"], c["t_head"]
    ab_blk = (4 * c["MW"] + c["MEMW"]) // LANES
    qk_ins = [(qkvc, (ts, HEAD_DIM), lambda i, h: (i, h)), (qkvc, (ts, HEAD_DIM), lambda i, h: (i, H + h))]
    gate_ins = [(h, (ts, LANES), lambda i, h: (i, ab_blk)), (p["a_log"], (None, 1, LANES), lambda i, h: (h, 0, 0)),
                (p["dt_bias"], (None, 1, LANES), lambda i, h: (h, 0, 0))]
    return qk_ins, gate_ins


def _gdn_out_ins(o, h, p, c):
    H, ts = c["H"], c["t_head"]
    return [(o, (ts, HEAD_DIM), lambda i, h: (i, h)), (h, (ts, HEAD_DIM), lambda i, h: (i, 3 * H + h)),
            (p["o_norm"], (1, HEAD_DIM), lambda i, h: (0, 0))]


def _gdn_layer_fwd(xb, p, c, side=None):
    S, H, MW, ts = c["S"], c["H"], c["MW"], c["t_head"]
    h, = _mm(xb, p["w_in"], "nn", [F32], name="gdn_in")
    tc = _tile(3 * MW, 512, LANES)
    qkvc = _conv_fwd(h, p["conv"], C=3 * MW, ts=c["t_conv"], tc=tc, name="gdn_conv")
    qk_ins, gate_ins = _gdn_ins(h, qkvc, p, c)
    hd = lambda i, h: (i, h)
    qn, kn = _ew(_gdn_qk_fn, qk_ins, [(_sds((S, MW)), (ts, HEAD_DIM), hd)] * 2, (S // ts, H), name="gdn_qknorm")
    g3 = lambda i, h: (h, i, 0)
    g, beta = _ew(_gdn_gate_fn(H, 1), gate_ins, [(_sds((H, S, CHUNK)), (None, ts, CHUNK), g3)] * 2, (S // ts, H),
                  name="gdn_gate")
    o, states, tinv, got = _gdn_fwd(qn, kn, qkvc, g, beta, H=H, voff=2 * H, name="gdn_delta", side=side)
    mix, = _ew(_gdn_out_fn, _gdn_out_ins(o, h, p, c), [(_sds((S, MW)), (ts, HEAD_DIM), hd)], (S // ts, H),
               name="gdn_outnorm")
    return mix, dict(h=h, qkvc=qkvc, qn=qn, kn=kn, g=g, beta=beta, o=o, states=states, tinv=tinv), got


def _gdn_layer_bwd(sv, p, dcat, dqm, c, side=None):
    S, H, MW, ts = c["S"], c["H"], c["MW"], c["t_head"]
    hd = lambda i, h: (i, h)
    g3 = lambda i, h: (h, i, 0)
    h, qkvc = sv["h"], sv["qkvc"]
    do, dz, d_onorm = _ew_vjp(_gdn_out_fn, _gdn_out_ins(sv["o"], h, p, c), [(dcat, (ts, HEAD_DIM), hd)],
                              [(_sds((S, MW)), (ts, HEAD_DIM), hd, "set"), (_sds((S, MW), BF16), (ts, HEAD_DIM), hd, "set"),
                               (_sds((1, HEAD_DIM)), (1, HEAD_DIM), lambda i, h: (0, 0), "acc_all")],
                              (S // ts, H), name="gdn_outnorm_bwd")
    dqn, dkn, dv, dg, db, got = _gdn_bwd(sv["qn"], sv["kn"], qkvc, sv["g"], sv["beta"], sv["states"], sv["tinv"], do,
                                         H=H, voff=2 * H, name="gdn_delta_bwd", side=side)
    qk_ins, gate_ins = _gdn_ins(h, qkvc, p, c)
    dqc, dkc = _ew_vjp(_gdn_qk_fn, qk_ins, [(dqn, (ts, HEAD_DIM), hd), (dkn, (ts, HEAD_DIM), hd)],
                       [(_sds((S, MW)), (ts, HEAD_DIM), hd, "set")] * 2, (S // ts, H), name="gdn_qknorm_bwd")
    full3 = lambda i, h: (0, 0, 0)
    dab, dalog, ddt = _ew_vjp(
        _gdn_gate_fn(H, 1), gate_ins, [(dg, (None, ts, CHUNK), g3), (db, (None, ts, CHUNK), g3)],
        [(_sds((S, LANES), BF16), (ts, LANES), lambda i, h: (i, 0), "acc"),
         (_sds((H, 1, LANES)), (H, 1, LANES), full3, ("acc_at", 1)),
         (_sds((H, 1, LANES)), (H, 1, LANES), full3, ("acc_at", 1))], (S // ts, H), name="gdn_gate_bwd")
    dqkvc = jnp.concatenate([dqc, dkc, dv], axis=1)
    tc = _tile(3 * MW, 512, LANES)
    dxc, dconv = _conv_bwd(h, p["conv"], dqkvc, C=3 * MW, ts=c["t_conv"], tc=tc, name="gdn_conv_bwd")
    pad = c["GDN_IN"] - (4 * MW + c["MEMW"] + LANES)
    dh = jnp.concatenate([dxc.astype(BF16), dz, dqm.astype(BF16), dab] + ([jnp.zeros((S, pad), BF16)] if pad else []),
                         axis=1)
    grads = dict(gdn_conv=dconv, gdn_a_log=jnp.sum(dalog[:, 0, :], axis=-1), gdn_dt_bias=jnp.sum(ddt[:, 0, :], axis=-1),
                 gdn_o_norm=d_onorm[0])
    return dh, grads, got


def _prep_layer(W, i, c):
    bf = lambda a: a.astype(BF16)
    row = lambda a: a[None].astype(F32)
    how = dict(mem_w_kv=("mem_w_kv", bf), w_out=("w_out", bf), mlp_w1=("w1", bf), mlp_w2=("w2", bf),
               ln1_g=("ln1_g", row), ln1_b=("ln1_b", row), ln2_g=("ln2_g", row), ln2_b=("ln2_b", row),
               mla_w_in=("w_in", lambda a: _prep_mla_w_in(a, c)), mla_q_norm=("q_norm", row),
               mla_w_uq=("w_uq", lambda a: _prep_w_uq(a, c)), mla_kv_norm=("kv_norm", row),
               mla_w_ukv=("w_ukv", lambda a: _prep_w_ukv(a, c)),
               gdn_w_in=("w_in", lambda a: _prep_gdn_w_in(a, c)), gdn_conv=("conv", lambda a: a.astype(F32)),
               gdn_a_log=("a_log", _lane_bcast), gdn_dt_bias=("dt_bias", _lane_bcast), gdn_o_norm=("o_norm", row))
    return {how[k][0]: how[k][1](a) for k, a in W.items()}


def _local_step(x, mem, positions, W, loss_target, c, next_weights=None, grad_sink=None):
    S, D, H, MW, ALPHA = c["S"], c["D"], c["H"], c["MW"], c["ALPHA"]
    inv_freq = 1.0 / (ROPE_THETA ** (jnp.arange(0, QK_ROPE, 2, dtype=F32) / QK_ROPE))
    ang = positions.astype(F32)[:, None] * inv_freq
    cos, sin = jnp.cos(ang), jnp.sin(ang)
    cosp = jnp.concatenate([cos, cos, jnp.ones((S, LANES - QK_ROPE), F32)], axis=1)
    sinp = jnp.concatenate([sin, sin, jnp.zeros((S, LANES - QK_ROPE), F32)], axis=1)
    memb = mem.astype(BF16)
    xf, xb = x, x.astype(BF16)
    saved, params = [], []
    w_next = W[0]
    for i in range(c["DEPTH"]):
        p = _prep_layer(w_next if next_weights is not None else W[i], i, c)
        mla = i % 2 == 0
        memkv, = _mm(memb, p["mem_w_kv"], "nn", [BF16], name="mem_kv")
        side, arrived = next_weights(i + 1) if next_weights is not None and i + 1 < c["DEPTH"] else (None, None)
        if mla:
            mix, sv, got = _mla_fwd(xb, p, cosp, sinp, c, side)
            qoff = (c["QL"] + c["KVL"]) // LANES
        else:
            mix, sv, got = _gdn_layer_fwd(xb, p, c, side)
            qoff = 4 * MW // LANES
        if side is not None:
            rest, w_next = arrived(got)
            p.update(_prep_layer(rest, i, c))
        om, lsem = _mem_attn_fwd(sv["h"], qoff, memkv, c, "mem_attn")
        cat = jnp.concatenate([mix, om], axis=1).astype(BF16)
        z1, = _mm(cat, p["w_out"], "nn", [F32], name="w_out", extras=(xf,), epilogue=lambda acc, r: (ALPHA * r + acc,))
        x1, x1b = _layer_norm(z1, p["ln1_g"], p["ln1_b"], c, "ln1")
        u, a = _mm(x1b, p["w1"], "nn", [F32, BF16], name="mlp_up", b_major=True,
                   epilogue=lambda acc: (acc, jnp.square(jnp.maximum(acc, 0.0))))
        z2, = _mm(a, p["w2"], "nn", [F32], name="mlp_down", extras=(x1,), epilogue=lambda acc, r: (ALPHA * r + acc,))
        x2, x2b = _layer_norm(z2, p["ln2_g"], p["ln2_b"], c, "ln2")
        sv.update(xb=xb, memkv=memkv, om=om, lsem=lsem, cat=cat, z1=z1, x1b=x1b, u=u, a=a, z2=z2, qoff=qoff)
        saved.append(sv)
        params.append(p)
        xf, xb = x2, x2b

    ts = c["t_row"]
    lsum, dy = _ew(_loss_fn, [(xf, (ts, D), _row), (loss_target, (ts, D), _row)],
                   [(_sds((1, D)), (1, D), _par), (_sds((S, D)), (ts, D), _row)], (S // ts,), name="loss", acc_out=(0,))
    loss = jnp.sum(lsum)

    grads = [None] * c["DEPTH"]
    pending = None
    dx = dy
    for i in reversed(range(c["DEPTH"])):
        p, sv = params[i], saved[i]
        mla = i % 2 == 0
        G = {}
        dz2, dz2b, dg, db = _layer_norm_bwd(sv["z2"], p["ln2_g"], p["ln2_b"], dx, c, "ln2_bwd")
        G["ln2_g"], G["ln2_b"] = dg[0], db[0]
        du, = _mm(dz2b, p["w2"], "nt", [BF16], name="mlp_down_dx", extras=(sv["u"],),
                  epilogue=lambda acc, u: (acc * (2.0 * jnp.maximum(u, 0.0)),))
        G["mlp_w2"], = _mm(sv["a"], dz2b, "tn", [F32], name="mlp_down_dw")
        G["mlp_w1"], = _mm(sv["x1b"], du, "tn", [F32], name="mlp_up_dw", out_major=True)
        dx1, = _mm(du, p["w1"], "nt", [F32], name="mlp_up_dx", extras=(dz2,), b_major=True,
                   epilogue=lambda acc, r: (ALPHA * r + acc,))
        dz1, dz1b, dg, db = _layer_norm_bwd(sv["z1"], p["ln1_g"], p["ln1_b"], dx1, c, "ln1_bwd")
        G["ln1_g"], G["ln1_b"] = dg[0], db[0]
        dcat, = _mm(dz1b, p["w_out"], "nt", [BF16], name="w_out_dx")
        G["w_out"], = _mm(sv["cat"], dz1b, "tn", [F32], name="w_out_dw")
        dqm, dkm, dvm = _mem_attn_bwd(sv["h"], sv["qoff"], sv["memkv"], sv["om"], sv["lsem"], dcat, c, "mem_attn_bwd")
        dmemkv = jnp.concatenate([dkm, dvm], axis=1).astype(BF16)
        G["mem_w_kv"], = _mm(memb, dmemkv, "tn", [F32], name="mem_kv_dw")
        riders = ([pending] if pending is not None else []) + ([grad_sink(i, G, True) + (i,)] if grad_sink else [])
        side = _merge_plans([r[0] for r in riders]) if riders else None
        if mla:
            dh, g, got = _mla_bwd(sv, p, cosp, sinp, dcat, dqm, c, side)
            G.update(g)
            dw_in, = _mm(sv["xb"], dh, "tn", [F32], name="mla_in_dw")
            G["mla_w_in"] = _unprep_mla_w_in(dw_in, c)
            dx, = _mm(dh, p["w_in"], "nt", [F32], name="mla_in_dx", extras=(dz1,),
                      epilogue=lambda acc, r: (ALPHA * r + acc,))
        else:
            dh, g, got = _gdn_layer_bwd(sv, p, dcat, dqm, c, side)
            G.update(g)
            dw_in, = _mm(sv["xb"], dh, "tn", [F32], name="gdn_in_dw")
            G["gdn_w_in"] = _unprep_gdn_w_in(dw_in, c)
            dx, = _mm(dh, p["w_in"], "nt", [F32], name="gdn_in_dx", extras=(dz1,),
                      epilogue=lambda acc, r: (ALPHA * r + acc,))
        grads[i] = dict(G)
        for plan, done, layer in riders:
            n_out = len(plan["out_shape"])
            grads[layer].update(done(got[:n_out]))
            got = got[n_out:]
        pending = grad_sink(i, G, False) + (i,) if grad_sink else None
    if pending is not None:
        grads[pending[2]].update(pending[1](_run_plan(pending[0], "grad_scatter_last")))
    return loss, dx, grads


_HBM = pl.BlockSpec(memory_space=pltpu.HBM)
_VMEM = pl.BlockSpec(memory_space=pltpu.VMEM)


def _my_place():
    return lax.axis_index("x"), lax.axis_index("y"), lax.axis_index("c")


def _my_chip():
    return 2 * lax.axis_index("x") + lax.axis_index("y")


def _other_chips(x, y):
    return [(1 - x, y), (x, 1 - y), (1 - x, 1 - y)]


def _gather_plan(arrs, by_rows):
    n = len(arrs)
    halved = [_halved(a) for a in arrs]

    def run(ins, outs, sems, start, wait):
        send_sems, recv_sems, local_sems = sems
        x, y, c = _my_place()
        chips = _other_chips(x, y)

        def copy(i, k, chip, to):
            half = c if halved[i] else None
            src = ins[i] if half is None else ins[i].at[pl.ds(c * (arrs[i].shape[0] // 2), arrs[i].shape[0] // 2)]
            return pltpu.make_async_remote_copy(src_ref=src, dst_ref=_slab(outs[i], arrs[i].shape[0], by_rows[i], chip, half),
                                                send_sem=send_sems.at[3 * i + k], recv_sem=recv_sems.at[3 * i + k],
                                                device_id=to, device_id_type=MESH)

        mine = [pltpu.make_async_copy(ins[i], _slab(outs[i], arrs[i].shape[0], by_rows[i], 2 * x + y, None),
                                      local_sems.at[i]) for i in range(n)]
        sends = [copy(i, k, 2 * x + y, (cx, cy, c)) for i in range(n) for k, (cx, cy) in enumerate(chips)]
        if start:
            for cp in mine + sends:
                cp.start()
        if wait:
            for i in range(n):
                for k, (cx, cy) in enumerate(chips):
                    copy(i, k, 2 * cx + cy, (cx, cy, c)).wait_recv()
            for cp in sends:
                cp.wait_send()
            for cp in mine:
                cp.wait()

    shapes = [jax.ShapeDtypeStruct((N_CHIPS * a.shape[0], a.shape[1]) if r else (N_CHIPS,) + a.shape, a.dtype)
              for a, r in zip(arrs, by_rows)]
    sems = [pltpu.SemaphoreType.DMA((3 * n,)), pltpu.SemaphoreType.DMA((3 * n,)), pltpu.SemaphoreType.DMA((n,))]
    return dict(arrs=list(arrs), out_shape=shapes, sems=sems, run=run)


def _halved(a):
    return a.shape[0] % (4 * SUBLANES * (4 // a.dtype.itemsize)) == 0


def _slab(out, r, by_rows, chip, half):
    lo, n = (0, r) if half is None else (half * (r // 2), r // 2)
    return out.at[pl.ds(chip * r + lo, n)] if by_rows else out.at[chip].at[pl.ds(lo, n)]


def _gather_fill(outs, arrs, by_rows, name):
    idx = [i for i, a in enumerate(arrs) if _halved(a)]
    n = len(idx)

    def body(*refs):
        ins, bufs = refs[:n], refs[n:2 * n]
        send_sems, recv_sems = refs[2 * n:]
        x, y, c = _my_place()

        def copy(j, k, chip, half):
            r = arrs[idx[j]].shape[0]
            return pltpu.make_async_remote_copy(src_ref=_slab(ins[j], r, by_rows[idx[j]], chip, half),
                                                dst_ref=_slab(bufs[j], r, by_rows[idx[j]], chip, half),
                                                send_sem=send_sems.at[3 * j + k], recv_sem=recv_sems.at[3 * j + k],
                                                device_id=(x, y, 1 - c), device_id_type=MESH)

        chips = [2 * cx + cy for cx, cy in _other_chips(x, y)]
        sends = [copy(j, k, chip, c) for j in range(n) for k, chip in enumerate(chips)]
        for cp in sends:
            cp.start()
        for j in range(n):
            for k, chip in enumerate(chips):
                copy(j, k, chip, 1 - c).wait_recv()
        for cp in sends:
            cp.wait_send()

    filled = pl.pallas_call(
        body, name=name, in_specs=[_HBM] * n, out_specs=[_HBM] * n,
        out_shape=[jax.ShapeDtypeStruct(outs[i].shape, outs[i].dtype) for i in idx],
        input_output_aliases={j: j for j in range(n)},
        scratch_shapes=[pltpu.SemaphoreType.DMA((3 * n,)), pltpu.SemaphoreType.DMA((3 * n,))],
    )(*[outs[i] for i in idx])
    res = list(outs)
    for i, f in zip(idx, filled):
        res[i] = f
    return res


def _merge_plans(plans):
    def run(ins, outs, sems, start, wait):
        a = b = s = 0
        for p in plans:
            na, nb, ns = len(p["arrs"]), len(p["out_shape"]), len(p["sems"])
            p["run"](ins[a:a + na], outs[b:b + nb], sems[s:s + ns], start, wait)
            a, b, s = a + na, b + nb, s + ns

    return dict(arrs=sum((p["arrs"] for p in plans), []), out_shape=sum((p["out_shape"] for p in plans), []),
                sems=sum((p["sems"] for p in plans), []), run=run)


def _scatter_plan(ps):
    n = len(ps)

    def run(ins, outs, sems, start, wait):
        send_sems, recv_sems = sems
        x, y, c = _my_place()
        cps = [pltpu.make_async_remote_copy(src_ref=ins[i].at[2 * cx + cy], dst_ref=outs[i].at[k],
                                            send_sem=send_sems.at[3 * i + k], recv_sem=recv_sems.at[3 * i + k],
                                            device_id=(cx, cy, c), device_id_type=MESH)
               for i in range(n) for k, (cx, cy) in enumerate(_other_chips(x, y))]
        if start:
            for cp in cps:
                cp.start()
        if wait:
            for cp in cps:
                cp.wait()

    shapes = [jax.ShapeDtypeStruct((3,) + p.shape[1:], p.dtype) for p in ps]
    sems = [pltpu.SemaphoreType.DMA((3 * n,)), pltpu.SemaphoreType.DMA((3 * n,))]
    return dict(arrs=list(ps), out_shape=shapes, sems=sems, run=run)


def _run_plan(plan, name):
    n_in, n_out = len(plan["arrs"]), len(plan["out_shape"])

    def body(*refs):
        plan["run"](refs[:n_in], refs[n_in:n_in + n_out], refs[n_in + n_out:], True, True)

    return pl.pallas_call(body, name=name, in_specs=[_HBM] * n_in, out_specs=[_HBM] * n_out,
                          out_shape=plan["out_shape"], scratch_shapes=plan["sems"])(*plan["arrs"])


def _with_side(body, n_in, n_out, grid, side):
    if side is None:
        return body, [], [], [], [], []
    s_in, s_out, s_sem = len(side["arrs"]), len(side["out_shape"]), len(side["sems"])

    def wrapped(*refs):
        ins, s_ins = refs[:n_in], refs[n_in:n_in + s_in]
        o0 = n_in + s_in
        outs, s_outs = refs[o0:o0 + n_out], refs[o0 + n_out:o0 + n_out + s_out]
        rest = refs[o0 + n_out + s_out:]
        scratch, s_sems = rest[:len(rest) - s_sem], rest[len(rest) - s_sem:]
        ids = [pl.program_id(d) for d in range(len(grid))]
        first = functools.reduce(jnp.logical_and, [i == 0 for i in ids])
        last = functools.reduce(jnp.logical_and, [i == g - 1 for i, g in zip(ids, grid)])
        pl.when(first)(lambda: side["run"](s_ins, s_outs, s_sems, True, False))
        body(*ins, *outs, *scratch)
        pl.when(last)(lambda: side["run"](s_ins, s_outs, s_sems, False, True))

    return wrapped, [_HBM] * s_in, [_HBM] * s_out, side["out_shape"], side["sems"], side["arrs"]


def _swap_halves(gs, name):
    n = len(gs)

    def body(*refs):
        ins, outs = refs[:n], refs[n:2 * n]
        send_sems, recv_sems = refs[2 * n:]
        x, y, c = _my_place()
        cps = [pltpu.make_async_remote_copy(src_ref=ins[i].at[:, 1 - c], dst_ref=outs[i], send_sem=send_sems.at[i],
                                            recv_sem=recv_sems.at[i], device_id=(x, y, 1 - c), device_id_type=MESH)
               for i in range(n)]
        for cp in cps:
            cp.start()
        for cp in cps:
            cp.wait()

    return pl.pallas_call(
        body, name=name, in_specs=[_HBM] * n, out_specs=[_HBM] * n,
        out_shape=[jax.ShapeDtypeStruct((g.shape[0],) + g.shape[2:], g.dtype) for g in gs],
        scratch_shapes=[pltpu.SemaphoreType.DMA((n,)), pltpu.SemaphoreType.DMA((n,))],
    )(*gs)


def _join_halves(fs, name):
    n = len(fs)

    def body(*refs):
        ins, outs = refs[:n], refs[n:2 * n]
        send_sems, recv_sems = refs[2 * n:]
        x, y, c = _my_place()

        def copy(i, half):
            return pltpu.make_async_remote_copy(src_ref=ins[i].at[half], dst_ref=outs[i].at[half],
                                                send_sem=send_sems.at[i], recv_sem=recv_sems.at[i],
                                                device_id=(x, y, 1 - c), device_id_type=MESH)

        sends = [copy(i, c) for i in range(n)]
        for cp in sends:
            cp.start()
        for i in range(n):
            copy(i, 1 - c).wait_recv()
        for cp in sends:
            cp.wait_send()

    return pl.pallas_call(
        body, name=name, in_specs=[_HBM] * n, out_specs=[_HBM] * n,
        out_shape=[jax.ShapeDtypeStruct(f.shape, f.dtype) for f in fs],
        input_output_aliases={i: i for i in range(n)},
        scratch_shapes=[pltpu.SemaphoreType.DMA((n,)), pltpu.SemaphoreType.DMA((n,))],
    )(*fs)


def _row_tile(a, b):
    return _tile(a, max(SUBLANES, (1 << 19) // b // SUBLANES * SUBLANES), SUBLANES)


def _add_core(g, got, name):
    _, _, A, B = g.shape
    ta = _row_tile(A, B)
    return _ew(lambda p, q: (p + q,),
               [(g, (None, None, ta, B), lambda s, i: (s, lax.axis_index("c"), i, 0)),
                (got, (None, ta, B), lambda s, i: (s, i, 0))],
               [(_sds((N_CHIPS, A, B)), (None, ta, B), lambda s, i: (s, i, 0))], (N_CHIPS, A // ta), name=name)[0]


def _add_chips(p, got, name):
    _, A, B = p.shape
    ta = _row_tile(A, B)
    blk = (None, ta, B)
    return _ew(lambda a, b, c_, d: (((a + b) + c_) + d,),
               [(p, blk, lambda i: (_my_chip(), i, 0)), (got, blk, lambda i: (0, i, 0)),
                (got, blk, lambda i: (1, i, 0)), (got, blk, lambda i: (2, i, 0))],
               [(_sds((2, A, B)), blk, lambda i: (lax.axis_index("c"), i, 0))], (A // ta,), name=name)[0]


def _all_reduce_small(v, name):
    r = v.shape[0]
    masks = [(mx, my, mc) for mx in (0, 1) for my in (0, 1) for mc in (0, 1)][1:]

    def body(v_ref, out_ref, gath, send_sems, recv_sems):
        x, y, c = _my_place()
        me = 4 * x + 2 * y + c
        gath[me] = v_ref[...]

        def peer(m):
            return (x + m[0] - 2 * x * m[0], y + m[1] - 2 * y * m[1], c + m[2] - 2 * c * m[2])

        def copy(k, slab, to):
            return pltpu.make_async_remote_copy(src_ref=v_ref, dst_ref=gath.at[slab], send_sem=send_sems.at[k],
                                                recv_sem=recv_sems.at[k], device_id=to, device_id_type=MESH)

        sends = [copy(k, me, peer(m)) for k, m in enumerate(masks)]
        for cp in sends:
            cp.start()
        for k, m in enumerate(masks):
            px, py, pc = peer(m)
            copy(k, 4 * px + 2 * py + pc, (px, py, pc)).wait_recv()
        for cp in sends:
            cp.wait_send()
        total = gath[0]
        for d in range(1, 8):
            total = total + gath[d]
        out_ref[...] = total

    return pl.pallas_call(
        body, name=name, in_specs=[_VMEM], out_specs=_VMEM, out_shape=jax.ShapeDtypeStruct((r, LANES), F32),
        scratch_shapes=[pltpu.VMEM((8, r, LANES), F32), pltpu.SemaphoreType.DMA((7,)), pltpu.SemaphoreType.DMA((7,))],
    )(v)


def _pack_rows(arrs, dtype, row_mult):
    flat = jnp.concatenate([a.astype(dtype).reshape(-1) for a in arrs])
    n = flat.shape[0]
    rows = _round_up(-(-n // LANES), row_mult)
    return jnp.pad(flat, (0, rows * LANES - n)).reshape(rows, LANES)


def _unpack_rows(buf, shapes):
    lead = buf.shape[:-2]
    flat = buf.reshape(lead + (-1,))
    out, o = [], 0
    for s in shapes:
        n = math.prod(s)
        out.append(lax.slice_in_dim(flat, o, o + n, axis=len(lead)).reshape(lead + tuple(s)))
        o += n
    return out


WEIGHTS = ["mla_w_in", "mla_q_norm", "mla_w_uq", "mla_kv_norm", "mla_w_ukv", "gdn_w_in", "gdn_conv", "gdn_a_log",
           "gdn_dt_bias", "gdn_o_norm", "mem_w_kv", "w_out", "ln1_g", "ln1_b", "mlp_w1", "mlp_w2", "ln2_g", "ln2_b"]
SHARD_AXIS = {"mla_w_in": 1, "mla_w_uq": 2, "mla_w_ukv": 2, "gdn_w_in": 2, "gdn_conv": 2, "mem_w_kv": 1, "w_out": 1,
              "mlp_w1": 2, "mlp_w2": 1}
SMALL = [k for k in WEIGHTS if k not in SHARD_AXIS] + ["gdn_conv"]
BIG = [k for k in WEIGHTS if k not in SMALL]
MLA_KEYS = ["mla_w_in", "mla_q_norm", "mla_w_uq", "mla_kv_norm", "mla_w_ukv"]
GDN_KEYS = ["gdn_w_in", "gdn_conv", "gdn_a_log", "gdn_dt_bias", "gdn_o_norm"]
ALL_KEYS = ["mem_w_kv", "w_out", "ln1_g", "ln1_b", "mlp_w1", "mlp_w2", "ln2_g", "ln2_b"]


def _layer_keys(i):
    return (MLA_KEYS if i % 2 == 0 else GDN_KEYS) + ALL_KEYS


def _layer_slot(k, i):
    return i // 2 if k in MLA_KEYS or k in GDN_KEYS else i


AFTER_MIXER = ["w_out", "mlp_w1", "mlp_w2"]


def _gather_layer(w, i, part="all"):
    mine = [k for k in _layer_keys(i) if part == "all" or (k in AFTER_MIXER) == (part == "tail")]
    keys = [k for k in mine if k in SHARD_AXIS]
    arrs = [w[k][_layer_slot(k, i)].astype(F32 if k == "gdn_conv" else BF16) for k in keys]
    by_rows = [SHARD_AXIS[k] == 1 for k in keys]

    def arrived(outs):
        outs = _gather_fill(outs, arrs, by_rows, "gather_fill_" + ("mla_" if i % 2 == 0 else "gdn_") + part)
        full = {k: w[k][_layer_slot(k, i)] for k in mine if k not in SHARD_AXIS}
        for k, o in zip(keys, outs):
            by_cols = SHARD_AXIS[k] == 2 and k != "mlp_w1"
            full[k] = jnp.concatenate([o[d] for d in range(N_CHIPS)], axis=1) if by_cols else o
        return full

    return _gather_plan(arrs, by_rows), arrived


EARLY = ["mlp_w1", "mlp_w2", "w_out", "mem_w_kv"]


def _reduce_group(i, G, early):
    kind = ("mla" if i % 2 == 0 else "gdn") + ("_early" if early else "_late")
    keys = [k for k in _layer_keys(i) if k in BIG and (k in EARLY) == early]
    canon = []
    for k in keys:
        g = G[k]
        if k == "mlp_w1":
            g = g.reshape(N_CHIPS, 2, g.shape[1] // 2, g.shape[2])
        elif SHARD_AXIS[k] == 1:
            g = g.reshape(N_CHIPS, 2, g.shape[0] // (2 * N_CHIPS), g.shape[1])
        else:
            rows, cw = g.shape[0], g.shape[1] // N_CHIPS
            g = g.reshape(rows, N_CHIPS, cw).transpose(1, 0, 2).reshape(N_CHIPS, 2, rows // 2, cw)
        canon.append(g)
    theirs = _swap_halves(canon, "grad_swap_" + kind)
    chip_sums = [_add_core(g, t, "grad_add_core") for g, t in zip(canon, theirs)]

    def done(got):
        halves = [_add_chips(p, s, "grad_add_chips") for p, s in zip(chip_sums, got)]
        joined = _join_halves(halves, "grad_join_" + kind)
        return {k: j.reshape(2 * j.shape[1], j.shape[2]) for k, j in zip(keys, joined)}

    return _scatter_plan(chip_sums), done


def _adamw(w, g, m, v, name):
    shape = w.shape
    cols = shape[-1]
    rows = math.prod(shape[:-1])
    tr = _tile(rows, max(SUBLANES, (1 << 19) // cols // SUBLANES * SUBLANES), SUBLANES)
    spec = ((tr, cols), _row)
    outs = _ew(_adamw_fn, [(a.reshape(rows, cols), *spec) for a in (w, g, m, v)], [(_sds((rows, cols)), *spec)] * 3,
               (rows // tr,), name=name)
    return [o.reshape(shape) for o in outs]


def kernel(x, mem, positions, mla_w_in, mla_q_norm, mla_w_uq, mla_kv_norm, mla_w_ukv, gdn_w_in, gdn_conv, gdn_a_log, gdn_dt_bias, gdn_o_norm, mem_w_kv, w_out, ln1_g, ln1_b, mlp_w1, mlp_w2, ln2_g, ln2_b, loss_target, m_mla_w_in, m_mla_q_norm, m_mla_w_uq, m_mla_kv_norm, m_mla_w_ukv, m_gdn_w_in, m_gdn_conv, m_gdn_a_log, m_gdn_dt_bias, m_gdn_o_norm, m_mem_w_kv, m_w_out, m_ln1_g, m_ln1_b, m_mlp_w1, m_mlp_w2, m_ln2_g, m_ln2_b, v_mla_w_in, v_mla_q_norm, v_mla_w_uq, v_mla_kv_norm, v_mla_w_ukv, v_gdn_w_in, v_gdn_conv, v_gdn_a_log, v_gdn_dt_bias, v_gdn_o_norm, v_mem_w_kv, v_w_out, v_ln1_g, v_ln1_b, v_mlp_w1, v_mlp_w2, v_ln2_g, v_ln2_b):
    w = dict(zip(WEIGHTS, (mla_w_in, mla_q_norm, mla_w_uq, mla_kv_norm, mla_w_ukv, gdn_w_in, gdn_conv, gdn_a_log,
                           gdn_dt_bias, gdn_o_norm, mem_w_kv, w_out, ln1_g, ln1_b, mlp_w1, mlp_w2, ln2_g, ln2_b)))
    m = dict(zip(WEIGHTS, (m_mla_w_in, m_mla_q_norm, m_mla_w_uq, m_mla_kv_norm, m_mla_w_ukv, m_gdn_w_in, m_gdn_conv,
                           m_gdn_a_log, m_gdn_dt_bias, m_gdn_o_norm, m_mem_w_kv, m_w_out, m_ln1_g, m_ln1_b, m_mlp_w1,
                           m_mlp_w2, m_ln2_g, m_ln2_b)))
    v = dict(zip(WEIGHTS, (v_mla_w_in, v_mla_q_norm, v_mla_w_uq, v_mla_kv_norm, v_mla_w_ukv, v_gdn_w_in, v_gdn_conv,
                           v_gdn_a_log, v_gdn_dt_bias, v_gdn_o_norm, v_mem_w_kv, v_w_out, v_ln1_g, v_ln1_b, v_mlp_w1,
                           v_mlp_w2, v_ln2_g, v_ln2_b)))
    assert x.shape[0] == 1, "one sequence per device"
    full_shapes = {k: w[k].shape for k in WEIGHTS}
    for k, ax in SHARD_AXIS.items():
        s = list(w[k].shape)
        s[ax] *= N_CHIPS
        full_shapes[k] = tuple(s)
    c = _dims(x.shape[1], x.shape[2], mem.shape[1], full_shapes)
    depth = c["DEPTH"]

    first, arrived = _gather_layer(w, 0, "head")
    W = [arrived(_run_plan(first, "gather_first"))]

    def next_weights(i):
        plan, arrived = _gather_layer(w, i)
        if i > 1:
            return plan, lambda got: ({}, arrived(got))
        tail, tail_arrived = _gather_layer(w, 0, "tail")
        n_tail = len(tail["out_shape"])
        return _merge_plans([tail, plan]), lambda got: (tail_arrived(got[:n_tail]), arrived(got[n_tail:]))

    loss_local, grad_x, G = _local_step(x[0], mem[0], positions[0], W, loss_target[0], c,
                                        next_weights=next_weights, grad_sink=_reduce_group)
    loss = lax.psum(loss_local, ("x", "y", "c"))

    def stacked(k):
        return jnp.stack([G[i][k] for i in range(depth) if k in G[i]], axis=0)

    grads = {k: stacked(k) for k in BIG}
    small_shapes = [full_shapes[k] for k in SMALL]
    gsmall = _all_reduce_small(_pack_rows([stacked(k) for k in SMALL], F32, SUBLANES), "grad_all_reduce_small")
    grads.update(dict(zip(SMALL, _unpack_rows(gsmall, small_shapes))))
    conv_cols = w["gdn_conv"].shape[2]
    grads["gdn_conv"] = lax.dynamic_slice_in_dim(grads["gdn_conv"], _my_chip() * conv_cols, conv_cols, axis=2)

    delta, new_m, new_v = {}, {}, {}
    for k in BIG + ["gdn_conv"]:
        delta[k], new_m[k], new_v[k] = _adamw(w[k], grads[k], m[k], v[k], "adamw")
    small = [k for k in SMALL if k != "gdn_conv"]
    packed = [_pack_rows([d[k] for k in small], F32, SUBLANES) for d in (w, grads, m, v)]
    ds, ms, vs = _adamw(*packed, "adamw_small")
    for d, buf in ((delta, ds), (new_m, ms), (new_v, vs)):
        d.update(dict(zip(small, _unpack_rows(buf, [w[k].shape for k in small]))))

    return (loss, grad_x[None], *[grads[k] for k in WEIGHTS], *[delta[k] for k in WEIGHTS],
            *[new_m[k] for k in WEIGHTS], *[new_v[k] for k in WEIGHTS])
```

```python
import functools
import math

import jax
import jax.numpy as jnp
from jax import lax
from jax.experimental import pallas as pl
from jax.experimental.pallas import tpu as pltpu

F32 = jnp.float32
BF16 = jnp.bfloat16
MESH = pl.DeviceIdType.MESH

LANES = 128
SUBLANES = 8
VMEM_LIMIT = 56 * 1024 * 1024
N_CHIPS = 4

HEAD_DIM = 128
QK_NOPE = 128
QK_ROPE = 64
QK_PAD = 256
ROPE_THETA = 10000.0
CONV_WIDTH = 4
CHUNK = 64
LN_EPS = 1e-5
RMS_EPS = 1e-6
ADAM_LR = 0.001
ADAM_B1 = 0.9
ADAM_B2 = 0.999
ADAM_EPS = 1e-08
ADAM_WD = 0.01
ADAM_STEP = 10
HI = lax.Precision.HIGHEST
LOG2E = 1.4426950408889634
LN2 = 0.6931471805599453
MLA_Q_SCALE = (QK_NOPE + QK_ROPE) ** -0.5 * LOG2E


def _cparams(sem=None):
    return pltpu.CompilerParams(dimension_semantics=sem, vmem_limit_bytes=VMEM_LIMIT)


def _tile(n, cap, unit):
    best = None
    t = unit
    while t <= min(n, cap):
        if n % t == 0:
            best = t
        t += unit
    return best if best is not None else n


def _mm(a, b, mode, out_dtypes, *, name, epilogue=None, extras=(), tm_cap=1024, tn_cap=1024, tk_cap=2048,
        b_major=False, out_major=False):
    if b_major:
        b_shape = (b.shape[1], N_CHIPS * b.shape[2])
    else:
        b_shape = b.shape
    if mode == "nn":
        (M, K), (K2, N) = a.shape, b_shape
    elif mode == "nt":
        (M, K), (N, K2) = a.shape, b_shape
    else:
        (K, M), (K2, N) = a.shape, b_shape
    assert K == K2, (a.shape, b.shape, mode)
    tm = _tile(M, tm_cap, LANES if mode == "tn" else 16)
    tn = _tile(N // N_CHIPS if (out_major or (b_major and mode == "nn")) else N, tn_cap, LANES)
    tk = _tile(K // N_CHIPS if (b_major and mode == "nt") else K, tk_cap, 16 if mode == "tn" else LANES)
    nk = K // tk
    nj4, nk4 = max(N // N_CHIPS // tn, 1), max(K // N_CHIPS // tk, 1)
    if mode == "nn":
        a_spec = pl.BlockSpec((tm, tk), lambda i, j, k: (i, k))
        b_spec = pl.BlockSpec((tk, tn), lambda i, j, k: (k, j))
        if b_major:
            b_spec = pl.BlockSpec((None, tk, tn), lambda i, j, k: (j // nj4, k, j % nj4))
        dims = (((1,), (0,)), ((), ()))
    elif mode == "nt":
        a_spec = pl.BlockSpec((tm, tk), lambda i, j, k: (i, k))
        b_spec = pl.BlockSpec((tn, tk), lambda i, j, k: (j, k))
        if b_major:
            b_spec = pl.BlockSpec((None, tn, tk), lambda i, j, k: (k // nk4, j, k % nk4))
        dims = (((1,), (1,)), ((), ()))
    else:
        assert not b_major
        a_spec = pl.BlockSpec((tk, tm), lambda i, j, k: (k, i))
        b_spec = pl.BlockSpec((tk, tn), lambda i, j, k: (k, j))
        dims = (((0,), (0,)), ((), ()))
    mn_spec = pl.BlockSpec((tm, tn), lambda i, j, k: (i, j))
    o_spec, o_shape = mn_spec, (M, N)
    if out_major:
        o_spec = pl.BlockSpec((None, tm, tn), lambda i, j, k: (j // nj4, i, j % nj4))
        o_shape = (N_CHIPS, M, N // N_CHIPS)
    n_ex, n_out = len(extras), len(out_dtypes)
    for e in extras:
        assert e.shape == (M, N), (e.shape, M, N)

    def body(a_ref, b_ref, *rest):
        ex_refs, out_refs, acc = rest[:n_ex], rest[n_ex:n_ex + n_out], rest[-1]
        k = pl.program_id(2)

        @pl.when(k == 0)
        def _():
            acc[...] = jnp.zeros_like(acc)

        acc[...] += lax.dot_general(a_ref[...].astype(BF16), b_ref[...].astype(BF16), dims,
                                    preferred_element_type=F32)

        @pl.when(k == nk - 1)
        def _():
            res = (acc[...],) if epilogue is None else epilogue(acc[...], *[e[...] for e in ex_refs])
            for o_ref, r in zip(out_refs, res):
                o_ref[...] = r.astype(o_ref.dtype)

    outs = pl.pallas_call(
        body, name=name, grid=(M // tm, N // tn, nk),
        in_specs=[a_spec, b_spec] + [mn_spec] * n_ex,
        out_specs=[o_spec] * n_out,
        out_shape=[jax.ShapeDtypeStruct(o_shape, d) for d in out_dtypes],
        scratch_shapes=[pltpu.VMEM((tm, tn), F32)],
        compiler_params=_cparams(("parallel", "parallel", "arbitrary")),
    )(a, b, *extras)
    return outs


def _spec(block, imap):
    return pl.BlockSpec(block, imap)


def _ew(fn, ins, outs, grid, *, name, acc_out=()):
    n_in = len(ins)
    ng = len(grid)

    def body(*refs):
        in_refs, out_refs = refs[:n_in], refs[n_in:]
        res = fn(*[r[...] for r in in_refs])
        first = functools.reduce(jnp.logical_and, [pl.program_id(d) == 0 for d in range(ng)])
        for i, (o_ref, r) in enumerate(zip(out_refs, res)):
            if i in acc_out:
                @pl.when(first)
                def _(o_ref=o_ref):
                    o_ref[...] = jnp.zeros_like(o_ref)
                o_ref[...] += r.astype(o_ref.dtype)
            else:
                o_ref[...] = r.astype(o_ref.dtype)

    return pl.pallas_call(
        body, name=name, grid=grid,
        in_specs=[_spec(b, m) for _, b, m in ins],
        out_specs=[_spec(b, m) for _, b, m in outs],
        out_shape=[s for s, _, _ in outs],
        compiler_params=_cparams(("arbitrary",) * ng),
    )(*[a for a, _, _ in ins])


def _ew_vjp(fn, ins, cts, gouts, grid, *, name, copies=()):
    n_in, n_ct = len(ins), len(cts)
    ng = len(grid)
    want = [i for i, g in enumerate(gouts) if g is not None]

    def body(*refs):
        in_refs, ct_refs, out_refs = refs[:n_in], refs[n_in:n_in + n_ct], refs[n_in + n_ct:]
        prim = [r[...] for r in in_refs]
        outs, pull = jax.vjp(fn, *prim)
        grads = pull(tuple(r[...].astype(o.dtype) for r, o in zip(ct_refs, outs)))
        first_all = functools.reduce(jnp.logical_and, [pl.program_id(d) == 0 for d in range(ng)])
        for o_ref, (i, _) in zip(out_refs[len(want):], copies):
            o_ref[...] = grads[i].astype(o_ref.dtype)
        for o_ref, i in zip(out_refs, want):
            mode = gouts[i][3]
            g = grads[i]
            if mode == "set":
                o_ref[...] = g.astype(o_ref.dtype)
            elif mode == "acc":
                @pl.when(pl.program_id(ng - 1) == 0)
                def _(o_ref=o_ref):
                    o_ref[...] = jnp.zeros_like(o_ref)
                o_ref[...] += g.astype(o_ref.dtype)
            elif mode == "acc_all":
                @pl.when(first_all)
                def _(o_ref=o_ref):
                    o_ref[...] = jnp.zeros_like(o_ref)
                o_ref[...] += g.astype(o_ref.dtype)
            else:
                @pl.when(first_all)
                def _(o_ref=o_ref):
                    o_ref[...] = jnp.zeros_like(o_ref)
                idx = pl.program_id(mode[1])
                o_ref[idx] += g.astype(o_ref.dtype)

    return pl.pallas_call(
        body, name=name, grid=grid,
        in_specs=[_spec(b, m) for _, b, m in ins] + [_spec(b, m) for _, b, m in cts],
        out_specs=[_spec(gouts[i][1], gouts[i][2]) for i in want] + [_spec(gouts[i][1], gouts[i][2]) for i, _ in copies],
        out_shape=[gouts[i][0] for i in want] + [s for _, s in copies],
        compiler_params=_cparams(("arbitrary",) * ng),
    )(*[a for a, _, _ in ins], *[a for a, _, _ in cts])


def _sds(shape, dtype=F32):
    return jax.ShapeDtypeStruct(tuple(shape), dtype)


def _ln_fn(z, g, b):
    mu = jnp.mean(z, -1, keepdims=True)
    d = z - mu
    var = jnp.mean(d * d, -1, keepdims=True)
    y = d * lax.rsqrt(var + LN_EPS) * g + b
    return y, y


def _rms_fn(x, g):
    return (x * lax.rsqrt(jnp.mean(x * x, -1, keepdims=True) + RMS_EPS) * g,)


@jax.custom_vjp
def _rot_half(x):
    lane = lax.broadcasted_iota(jnp.int32, x.shape, x.ndim - 1)
    up = pltpu.roll(x, LANES - QK_ROPE // 2, x.ndim - 1)
    dn = pltpu.roll(x, QK_ROPE // 2, x.ndim - 1)
    return jnp.where(lane < QK_ROPE // 2, -up, jnp.where(lane < QK_ROPE, dn, 0.0))


def _rot_half_fwd(x):
    return _rot_half(x), None


def _rot_half_bwd(_, ct):
    return (-_rot_half(ct),)


_rot_half.defvjp(_rot_half_fwd, _rot_half_bwd)


def _rope_blk(x, cos, sin):
    return x * cos + _rot_half(x) * sin


def _mla_prep_fn(qraw, knope, kr, cos, sin):
    qn, qr = qraw[:, :QK_NOPE], qraw[:, QK_NOPE:]
    q = jnp.concatenate([qn, _rope_blk(qr, cos, sin)], axis=1) * MLA_Q_SCALE
    k = jnp.concatenate([knope.astype(F32), _rope_blk(kr, cos, sin)], axis=1)
    return q, k


def _l2n(x):
    return x * lax.rsqrt(jnp.sum(x * x, -1, keepdims=True) + 1e-6)


def _gdn_qk_fn(qc, kc):
    return _l2n(qc) * (HEAD_DIM ** -0.5), _l2n(kc)


def _softplus(x):
    return jnp.maximum(x, 0.0) + jnp.log(1.0 + jnp.exp(-jnp.abs(x)))


def _sigmoid(x):
    return 1.0 / (1.0 + jnp.exp(-x))


def _silu(x):
    return x * _sigmoid(x)


def _gdn_gate_fn(n_heads, head_axis):
    def fn(ab, a_log, dt_bias):
        h = pl.program_id(head_axis)
        lane = lax.broadcasted_iota(jnp.int32, ab.shape, 1)
        a_in = jnp.sum(jnp.where(lane == h, ab, 0.0), -1, keepdims=True)
        b_in = jnp.sum(jnp.where(lane == h + n_heads, ab, 0.0), -1, keepdims=True)
        g = -jnp.exp(a_log[:, :CHUNK]) * _softplus(a_in + dt_bias[:, :CHUNK])
        beta = _sigmoid(b_in) + jnp.zeros_like(g)
        return g, beta
    return fn


def _gdn_out_fn(o, z, w):
    return (o * lax.rsqrt(jnp.mean(o * o, -1, keepdims=True) + RMS_EPS) * w * _silu(z),)


def _loss_fn(y, t):
    d = y - t
    return (jnp.sum(d * d, axis=0, keepdims=True) * (0.5 / y.shape[-1]), d * (1.0 / y.shape[-1]))


def _adamw_fn(w, g, m, v):
    m = ADAM_B1 * m + (1.0 - ADAM_B1) * g
    v = ADAM_B2 * v + (1.0 - ADAM_B2) * (g * g)
    m_hat = m / (1.0 - ADAM_B1 ** ADAM_STEP)
    v_hat = v / (1.0 - ADAM_B2 ** ADAM_STEP)
    delta = -ADAM_LR * (m_hat / (jnp.sqrt(v_hat) + ADAM_EPS) + ADAM_WD * w)
    return delta, m, v


def _causal_mask(shape, row0, col0):
    row = lax.broadcasted_iota(jnp.int32, shape, 0) + row0
    col = lax.broadcasted_iota(jnp.int32, shape, 1) + col0
    return col <= row


def _rows(ref, i, t):
    return ref[pl.ds(pl.multiple_of(i * t, t), t), :]


def _walk(first, n_loop, tail, products, update):
    stop = first + n_loop
    t0 = tail[0][0]

    def step(j, carry):
        nxt = products(jnp.where(j + 1 < stop, j + 1, t0))
        update(carry, j, False)
        return nxt

    carry = lax.fori_loop(first, stop, step, products(jnp.where(n_loop > 0, first, t0)))
    for n, (j, masked) in enumerate(tail):
        nxt = products(tail[n + 1][0]) if n + 1 < len(tail) else None
        update(carry, j, masked)
        carry = nxt


def _flash_fwd(q, k, v, *, H, dq, dv, qoff, koff, voff, causal, scale, tq, tk, name, side=None):
    S, Sk = q.shape[0], k.shape[0]
    nq = S // tq
    assert (tq == tk and S == Sk) or not causal

    def body(q_ref, k_ref, v_ref, o_ref, lse_ref, m_s, l_s, acc):
        qi = pl.program_id(1)
        m_s[...] = jnp.full_like(m_s, -jnp.inf)
        l_s[...] = jnp.zeros_like(l_s)
        acc[...] = jnp.zeros_like(acc)
        qb = q_ref[...].astype(BF16)

        def products(j):
            return lax.dot_general(qb, _rows(k_ref, j, tk).astype(BF16), (((1,), (1,)), ((), ())),
                                   preferred_element_type=F32)

        ex = jnp.exp if scale is not None else jnp.exp2

        def update(s, j, masked):
            if scale is not None:
                s = s * scale
            if masked:
                s = jnp.where(_causal_mask(s.shape, qi * tq, j * tk), s, -jnp.inf)
            m_prev = m_s[...]
            m_new = jnp.maximum(m_prev, jnp.max(s, axis=1, keepdims=True))
            alpha = ex(m_prev - m_new)
            p = ex(s - m_new[:, :1])
            l_s[...] = alpha * l_s[...] + jnp.sum(p, axis=1, keepdims=True)
            acc[...] = acc[...] * alpha[:, :1] + lax.dot_general(
                p.astype(BF16), _rows(v_ref, j, tk).astype(BF16), (((1,), (0,)), ((), ())), preferred_element_type=F32)
            m_s[...] = m_new

        if causal:
            _walk(0, qi, [(qi, True)], products, update)
        else:
            _walk(0, Sk // tk - 1, [(Sk // tk - 1, False)], products, update)
        o_ref[...] = (acc[...] / l_s[...][:, :1]).astype(o_ref.dtype)
        lse_ref[...] = m_s[...] + (jnp.log(l_s[...]) if scale is not None else jnp.log2(l_s[...]))

    body, s_in, s_out, s_shape, s_sems, s_args = _with_side(body, 3, 2, (H, nq), side)
    o, lse, *side_outs = pl.pallas_call(
        body, name=name, grid=(H, nq),
        in_specs=[pl.BlockSpec((tq, dq), lambda h, qi: (qi, qoff + h)),
                  pl.BlockSpec((Sk, dq), lambda h, qi: (0, koff + h)),
                  pl.BlockSpec((Sk, dv), lambda h, qi: (0, voff + h))] + s_in,
        out_specs=[pl.BlockSpec((tq, dv), lambda h, qi: (qi, h)),
                   pl.BlockSpec((tq, LANES), lambda h, qi: (qi, h))] + s_out,
        out_shape=[_sds((S, H * dv)), _sds((S, H * LANES))] + s_shape,
        scratch_shapes=[pltpu.VMEM((tq, LANES), F32), pltpu.VMEM((tq, LANES), F32), pltpu.VMEM((tq, dv), F32)] + s_sems,
        compiler_params=_cparams(("arbitrary", "arbitrary")),
    )(q, k, v, *s_args)
    return o, lse, side_outs


def _flash_bwd(q, k, v, o, lse, do, *, H, dq, dv, qoff, koff, voff, dooff, causal, scale, tq, tk, name, side=None):
    S, Sk = q.shape[0], k.shape[0]
    nq, nk = S // tq, Sk // tk
    assert (tq == tk and S == Sk) or not causal
    nt = (((1,), (1,)), ((), ()))

    def body_q(q_ref, k_ref, v_ref, o_ref, do_ref, lse_ref, dq_ref, delta_ref, dq_acc):
        qi = pl.program_id(1)
        qb, dob = q_ref[...].astype(BF16), do_ref[...].astype(BF16)
        delta = jnp.sum(do_ref[...].astype(F32) * o_ref[...].astype(F32), axis=1, keepdims=True)
        delta_ref[...] = delta + jnp.zeros_like(delta_ref)
        lse1 = lse_ref[...][:, :1]
        dq_acc[...] = jnp.zeros_like(dq_acc)

        def products(j):
            return (lax.dot_general(qb, _rows(k_ref, j, tk).astype(BF16), nt, preferred_element_type=F32),
                    lax.dot_general(dob, _rows(v_ref, j, tk).astype(BF16), nt, preferred_element_type=F32))

        def update(sp, j, masked):
            s, dp = sp
            p = jnp.exp(s * scale - lse1)
            if masked:
                p = jnp.where(_causal_mask(s.shape, qi * tq, j * tk), p, 0.0)
            ds = p * (dp - delta) * scale
            dq_acc[...] += lax.dot_general(ds.astype(BF16), _rows(k_ref, j, tk).astype(BF16), (((1,), (0,)), ((), ())),
                                           preferred_element_type=F32)

        if causal:
            _walk(0, qi, [(qi, True)], products, update)
        else:
            _walk(0, nk - 1, [(nk - 1, False)], products, update)
        dq_ref[...] = dq_acc[...]

    body_q, s_in, s_out, s_shape, s_sems, s_args = _with_side(body_q, 6, 2, (H, nq), side)
    dqq, delta, *side_outs = pl.pallas_call(
        body_q, name=name + "_dq", grid=(H, nq),
        in_specs=[pl.BlockSpec((tq, dq), lambda h, qi: (qi, qoff + h)),
                  pl.BlockSpec((Sk, dq), lambda h, qi: (0, koff + h)),
                  pl.BlockSpec((Sk, dv), lambda h, qi: (0, voff + h)),
                  pl.BlockSpec((tq, dv), lambda h, qi: (qi, h)),
                  pl.BlockSpec((tq, dv), lambda h, qi: (qi, dooff + h)),
                  pl.BlockSpec((tq, LANES), lambda h, qi: (qi, h))] + s_in,
        out_specs=[pl.BlockSpec((tq, dq), lambda h, qi: (qi, h)),
                   pl.BlockSpec((tq, LANES), lambda h, qi: (qi, h))] + s_out,
        out_shape=[_sds((S, H * dq)), _sds((S, H * LANES))] + s_shape,
        scratch_shapes=[pltpu.VMEM((tq, dq), F32)] + s_sems,
        compiler_params=_cparams(("arbitrary", "arbitrary")),
    )(q, k, v, o, do, lse, *s_args)

    def as_rows(t):
        return t[:, ::LANES].T.reshape(H, nq, 1, tq)

    nn = (((1,), (0,)), ((), ()))

    def body_kv(q_ref, k_ref, v_ref, do_ref, lse_ref, delta_ref, dk_ref, dv_ref, dk_acc, dv_acc):
        kj = pl.program_id(1)
        kb, vb = k_ref[...].astype(BF16), v_ref[...].astype(BF16)
        dk_acc[...] = jnp.zeros_like(dk_acc)
        dv_acc[...] = jnp.zeros_like(dv_acc)

        def products(i):
            return (lax.dot_general(kb, _rows(q_ref, i, tq).astype(BF16), nt, preferred_element_type=F32),
                    lax.dot_general(vb, _rows(do_ref, i, tq).astype(BF16), nt, preferred_element_type=F32))

        def update(sp, i, masked):
            st, dpt = sp
            pt = jnp.exp(st * scale - lse_ref[i])
            if masked:
                key = lax.broadcasted_iota(jnp.int32, st.shape, 0) + kj * tk
                qry = lax.broadcasted_iota(jnp.int32, st.shape, 1) + i * tq
                pt = jnp.where(key <= qry, pt, 0.0)
            dst = pt * (dpt - delta_ref[i]) * scale
            dv_acc[...] += lax.dot_general(pt.astype(BF16), _rows(do_ref, i, tq).astype(BF16), nn,
                                           preferred_element_type=F32)
            dk_acc[...] += lax.dot_general(dst.astype(BF16), _rows(q_ref, i, tq).astype(BF16), nn,
                                           preferred_element_type=F32)

        if causal:
            _walk(kj + 1, nq - 1 - kj, [(kj, True)], products, update)
        else:
            _walk(0, nq - 1, [(nq - 1, False)], products, update)
        dk_ref[...] = dk_acc[...]
        dv_ref[...] = dv_acc[...]

    row_spec = pl.BlockSpec((None, nq, 1, tq), lambda h, kj: (h, 0, 0, 0))
    dk, dvv = pl.pallas_call(
        body_kv, name=name + "_dkv", grid=(H, nk),
        in_specs=[pl.BlockSpec((S, dq), lambda h, kj: (0, qoff + h)),
                  pl.BlockSpec((tk, dq), lambda h, kj: (kj, koff + h)),
                  pl.BlockSpec((tk, dv), lambda h, kj: (kj, voff + h)),
                  pl.BlockSpec((S, dv), lambda h, kj: (0, dooff + h)), row_spec, row_spec],
        out_specs=[pl.BlockSpec((tk, dq), lambda h, kj: (kj, h)), pl.BlockSpec((tk, dv), lambda h, kj: (kj, h))],
        out_shape=[_sds((Sk, H * dq)), _sds((Sk, H * dv))],
        scratch_shapes=[pltpu.VMEM((tk, dq), F32), pltpu.VMEM((tk, dv), F32)],
        compiler_params=_cparams(("parallel", "arbitrary")),
    )(q, k, v, do, as_rows(lse), as_rows(delta))
    return dqq, dk, dvv, side_outs


def _flash_bwd_causal(q, k, v, o, lse, do, *, H, dq, dv, qoff, koff, voff, dooff, tq, name, side=None):
    S = q.shape[0]
    nq = S // tq
    nt = (((1,), (1,)), ((), ()))
    tn = (((0,), (0,)), ((), ()))
    nn = (((1,), (0,)), ((), ()))

    def body(q_ref, k_ref, v_ref, o_ref, do_ref, lse_ref, dq_ref, dk_ref, dv_ref, dq_acc):
        qi = pl.program_id(1)

        @pl.when(qi == 0)
        def _():
            dk_ref[...] = jnp.zeros_like(dk_ref)
            dv_ref[...] = jnp.zeros_like(dv_ref)

        qb, dob = q_ref[...].astype(BF16), do_ref[...].astype(BF16)
        delta = jnp.sum(do_ref[...].astype(F32) * o_ref[...].astype(F32), axis=1, keepdims=True)
        lse1 = lse_ref[...][:, :1]
        dq_acc[...] = jnp.zeros_like(dq_acc)

        def products(j):
            return (lax.dot_general(qb, _rows(k_ref, j, tq).astype(BF16), nt, preferred_element_type=F32),
                    lax.dot_general(dob, _rows(v_ref, j, tq).astype(BF16), nt, preferred_element_type=F32))

        def update(sp, j, masked):
            s, dp = sp
            p = jnp.exp2(s - lse1)
            if masked:
                p = jnp.where(_causal_mask(s.shape, qi * tq, j * tq), p, 0.0)
            ds = (p * (dp - delta)).astype(BF16)
            dq_acc[...] += lax.dot_general(ds, _rows(k_ref, j, tq).astype(BF16), nn, preferred_element_type=F32)
            rows = pl.ds(pl.multiple_of(j * tq, tq), tq)
            dk_ref[rows, :] += lax.dot_general(ds, qb, tn, preferred_element_type=F32)
            dv_ref[rows, :] += lax.dot_general(p.astype(BF16), dob, tn, preferred_element_type=F32)

        _walk(0, qi, [(qi, True)], products, update)
        dq_ref[...] = dq_acc[...] * LN2

        @pl.when(qi == nq - 1)
        def _():
            dk_ref[...] = dk_ref[...] * LN2

    body, s_in, s_out, s_shape, s_sems, s_args = _with_side(body, 6, 3, (H, nq), side)
    dqq, dk, dvv, *side_outs = pl.pallas_call(
        body, name=name, grid=(H, nq),
        in_specs=[pl.BlockSpec((tq, dq), lambda h, qi: (qi, qoff + h)),
                  pl.BlockSpec((S, dq), lambda h, qi: (0, koff + h)),
                  pl.BlockSpec((S, dv), lambda h, qi: (0, voff + h)),
                  pl.BlockSpec((tq, dv), lambda h, qi: (qi, h)),
                  pl.BlockSpec((tq, dv), lambda h, qi: (qi, dooff + h)),
                  pl.BlockSpec((tq, LANES), lambda h, qi: (qi, h))] + s_in,
        out_specs=[pl.BlockSpec((tq, dq), lambda h, qi: (qi, h)),
                   pl.BlockSpec((S, dq), lambda h, qi: (0, h)), pl.BlockSpec((S, dv), lambda h, qi: (0, h))] + s_out,
        out_shape=[_sds((S, H * dq)), _sds((S, H * dq)), _sds((S, H * dv))] + s_shape,
        scratch_shapes=[pltpu.VMEM((tq, dq), F32)] + s_sems,
        compiler_params=_cparams(("arbitrary", "arbitrary")),
    )(q, k, v, o, do, lse, *s_args)
    return dqq, dk, dvv, side_outs


def _shift_down(x, prev8, j):
    if j == 0:
        return x
    y = pltpu.roll(x, j, 0)
    head = pltpu.roll(prev8, j, 0)
    row = lax.broadcasted_iota(jnp.int32, x.shape, 0)
    reps = x.shape[0] // SUBLANES
    return jnp.where(row < j, jnp.tile(head, (reps, 1)), y)


def _shift_up(x, next8, j):
    if j == 0:
        return x
    n = x.shape[0]
    y = pltpu.roll(x, n - j, 0)
    tail = pltpu.roll(next8, SUBLANES - j, 0)
    row = lax.broadcasted_iota(jnp.int32, x.shape, 0)
    reps = n // SUBLANES
    return jnp.where(row >= n - j, jnp.tile(tail, (reps, 1)), y)


def _conv_pre(x_ref, p_ref, w_ref, first):
    x = x_ref[...]
    prev8 = jnp.where(first, 0.0, p_ref[...])
    w = w_ref[...]
    xs = [_shift_down(x, prev8, CONV_WIDTH - 1 - j) for j in range(CONV_WIDTH)]
    c = sum(xs[j] * w[j:j + 1, :] for j in range(CONV_WIDTH))
    return c, xs


def _conv_specs(ts, tc, C_total_blocks_off):
    rb = ts // SUBLANES
    off = C_total_blocks_off
    x_spec = pl.BlockSpec((ts, tc), lambda ci, i: (i, off + ci))
    p_spec = pl.BlockSpec((SUBLANES, tc), lambda ci, i: (jnp.maximum(i * rb - 1, 0), off + ci))
    return x_spec, p_spec


def _conv_fwd(h, w, *, C, ts, tc, name):
    S = h.shape[0]
    x_spec, p_spec = _conv_specs(ts, tc, 0)

    def body(x_ref, p_ref, w_ref, y_ref):
        c, _ = _conv_pre(x_ref, p_ref, w_ref, pl.program_id(1) == 0)
        y_ref[...] = _silu(c)

    return pl.pallas_call(
        body, name=name, grid=(C // tc, S // ts),
        in_specs=[x_spec, p_spec, pl.BlockSpec((CONV_WIDTH, tc), lambda ci, i: (0, ci))],
        out_specs=pl.BlockSpec((ts, tc), lambda ci, i: (i, ci)),
        out_shape=_sds((S, C)),
        compiler_params=_cparams(("parallel", "arbitrary")),
    )(h, h, w)


def _conv_bwd(h, w, dy, *, C, ts, tc, name):
    S = h.shape[0]
    ns = S // ts
    rb = ts // SUBLANES
    x_spec, p_spec = _conv_specs(ts, tc, 0)

    def body_a(x_ref, p_ref, w_ref, dy_ref, dc_ref, dw_ref):
        i = pl.program_id(1)
        c, xs = _conv_pre(x_ref, p_ref, w_ref, i == 0)
        sg = _sigmoid(c)
        dc = dy_ref[...] * (sg * (1.0 + c * (1.0 - sg)))
        dc_ref[...] = dc

        @pl.when(i == 0)
        def _():
            dw_ref[...] = jnp.zeros_like(dw_ref)

        dw_ref[...] += jnp.concatenate([jnp.sum(dc * xs[j], axis=0, keepdims=True) for j in range(CONV_WIDTH)], axis=0)

    dc, dw = pl.pallas_call(
        body_a, name=name + "_a", grid=(C // tc, ns),
        in_specs=[x_spec, p_spec, pl.BlockSpec((CONV_WIDTH, tc), lambda ci, i: (0, ci)),
                  pl.BlockSpec((ts, tc), lambda ci, i: (i, ci))],
        out_specs=[pl.BlockSpec((ts, tc), lambda ci, i: (i, ci)),
                   pl.BlockSpec((CONV_WIDTH, tc), lambda ci, i: (0, ci))],
        out_shape=[_sds((S, C)), _sds((CONV_WIDTH, C))],
        compiler_params=_cparams(("parallel", "arbitrary")),
    )(h, h, w, dy)

    def body_b(dc_ref, n_ref, w_ref, dx_ref):
        i = pl.program_id(1)
        dcv = dc_ref[...]
        next8 = jnp.where(i == ns - 1, 0.0, n_ref[...])
        w_ = w_ref[...]
        dx_ref[...] = sum(_shift_up(dcv, next8, CONV_WIDTH - 1 - j) * w_[j:j + 1, :] for j in range(CONV_WIDTH))

    dx = pl.pallas_call(
        body_b, name=name + "_b", grid=(C // tc, ns),
        in_specs=[pl.BlockSpec((ts, tc), lambda ci, i: (i, ci)),
                  pl.BlockSpec((SUBLANES, tc), lambda ci, i: (jnp.minimum((i + 1) * rb, ns * rb - 1), ci)),
                  pl.BlockSpec((CONV_WIDTH, tc), lambda ci, i: (0, ci))],
        out_specs=pl.BlockSpec((ts, tc), lambda ci, i: (i, ci)),
        out_shape=_sds((S, C)),
        compiler_params=_cparams(("parallel", "arbitrary")),
    )(dc, dc, w)
    return dx, dw


def _bdot(a, b, ca, cb, precision=None):
    nb = a.ndim - 2
    batch = tuple(range(nb))
    return lax.dot_general(a, b, (((nb + ca,), (nb + cb,)), (batch, batch)), precision=precision,
                           preferred_element_type=F32)


@jax.custom_vjp
def _nn(a, b):
    return _bdot(a.astype(BF16), b.astype(BF16), 1, 0)


@jax.custom_vjp
def _nt(a, b):
    return _bdot(a.astype(BF16), b.astype(BF16), 1, 1)


@jax.custom_vjp
def _tn(a, b):
    return _bdot(a.astype(BF16), b.astype(BF16), 0, 0)


_nn.defvjp(lambda a, b: (_nn(a, b), (a, b)), lambda r, g: (_nt(g, r[1]), _tn(r[0], g)))
_nt.defvjp(lambda a, b: (_nt(a, b), (a, b)), lambda r, g: (_nn(g, r[1]), _tn(g, r[0])))
_tn.defvjp(lambda a, b: (_tn(a, b), (a, b)), lambda r, g: (_nt(r[1], g), _nn(r[0], g)))


def _dot3(a, b, ca, cb):
    ah, bh = a.astype(BF16), b.astype(BF16)
    al, bl = (a - ah.astype(F32)).astype(BF16), (b - bh.astype(F32)).astype(BF16)
    return _bdot(ah, bh, ca, cb) + (_bdot(ah, bl, ca, cb) + _bdot(al, bh, ca, cb))


@jax.custom_vjp
def _nn_x3(a, b):
    return _dot3(a, b, 1, 0)


@jax.custom_vjp
def _nt_x3(a, b):
    return _dot3(a, b, 1, 1)


@jax.custom_vjp
def _tn_x3(a, b):
    return _dot3(a, b, 0, 0)


_nn_x3.defvjp(lambda a, b: (_nn_x3(a, b), (a, b)), lambda r, g: (_nt_x3(g, r[1]), _tn_x3(r[0], g)))
_nt_x3.defvjp(lambda a, b: (_nt_x3(a, b), (a, b)), lambda r, g: (_nn_x3(g, r[1]), _tn_x3(g, r[0])))
_tn_x3.defvjp(lambda a, b: (_tn_x3(a, b), (a, b)), lambda r, g: (_nt_x3(r[1], g), _nn_x3(r[0], g)))


@jax.custom_vjp
def _nn_hi(a, b):
    return _bdot(a, b, 1, 0, HI)


@jax.custom_vjp
def _nt_hi(a, b):
    return _bdot(a, b, 1, 1, HI)


@jax.custom_vjp
def _tn_hi(a, b):
    return _bdot(a, b, 0, 0, HI)


_nn_hi.defvjp(lambda a, b: (_nn_hi(a, b), (a, b)), lambda r, g: (_nt_hi(g, r[1]), _tn_hi(r[0], g)))
_nt_hi.defvjp(lambda a, b: (_nt_hi(a, b), (a, b)), lambda r, g: (_nn_hi(g, r[1]), _tn_hi(g, r[0])))
_tn_hi.defvjp(lambda a, b: (_tn_hi(a, b), (a, b)), lambda r, g: (_nt_hi(r[1], g), _nn_hi(r[0], g)))


@jax.custom_vjp
def _inverse_given(lmat, t):
    return t


_inverse_given.defvjp(lambda lmat, t: (t, t),
                      lambda t, ct: (-_tn_x3(t, _nt_x3(ct, t)), jnp.zeros_like(t)))


def _gdn_chunk_fn(q, k, v, g, beta, state, t_known=None):
    C = CHUNK
    B = q.shape[0]
    row = lax.broadcasted_iota(jnp.int32, (B, C, C), 1)
    col = lax.broadcasted_iota(jnp.int32, (B, C, C), 2)
    tril, strict = row >= col, row > col
    ones_tril = tril.astype(F32)
    gc = _nn_hi(ones_tril, g)
    gr = _nt_hi(jnp.full((B, C, C), 1.0 / C, F32), gc)
    decay = jnp.where(tril, jnp.exp(jnp.where(tril, gc - gr, 0.0)), 0.0)
    b1 = beta[:, :, :1]
    e_gc = jnp.exp(gc[:, :, :1])
    kb = k * b1
    lmat = jnp.where(strict, _nt(kb, k) * decay, 0.0)
    if t_known is None:
        a = -lmat
        t = jnp.where(row == col, 1.0, 0.0) + a
        p = a
        for _ in range(5):
            p = _nn_x3(p, p)
            t = t + _nn_x3(t, p)
    else:
        t = _inverse_given(lmat, t_known)
    rhs = jnp.concatenate([v * b1, kb * e_gc], axis=2)
    sol = _nn_x3(t, rhs)
    u, w = sol[:, :, :HEAD_DIM], sol[:, :, HEAD_DIM:]
    a_qk = jnp.where(tril, _nt(q, k) * decay, 0.0)
    gl = gc[:, C - 1:C, :1]
    q_dec = q * e_gc
    k_dec = k * jnp.exp(gl - gc[:, :, :1])
    v_new = u - _nn(w, state)
    o = _nn(q_dec, state) + _nn(a_qk, v_new)
    new_state = state * jnp.exp(gl) + _tn(k_dec, v_new)
    return o, new_state, t


def _split_heads(x, B):
    return jnp.stack([x[:, j * HEAD_DIM:(j + 1) * HEAD_DIM] for j in range(B)], axis=0)


def _merge_heads(x):
    return jnp.concatenate([x[j] for j in range(x.shape[0])], axis=1)


def _gdn_group(H, voff):
    return next(b for b in (12, 6, 4, 3, 2, 1) if H % b == 0 and voff % b == 0)


def _gdn_fwd(q, k, v, g, beta, *, H, voff, name, side=None):
    S = q.shape[0]
    N = S // CHUNK
    B = _gdn_group(H, voff)
    W = B * HEAD_DIM
    qs = lambda off: pl.BlockSpec((CHUNK, W), lambda h, n: (n, off // B + h))
    gs = pl.BlockSpec((B, CHUNK, CHUNK), lambda h, n: (h, n, 0))

    def body(q_ref, k_ref, v_ref, g_ref, b_ref, o_ref, st_ref, ti_ref, state):
        @pl.when(pl.program_id(1) == 0)
        def _():
            state[...] = jnp.zeros_like(state)

        s0 = state[...]
        st_ref[...] = s0
        o, s1, t = _gdn_chunk_fn(_split_heads(q_ref[...], B), _split_heads(k_ref[...], B),
                                 _split_heads(v_ref[...], B), g_ref[...], b_ref[...], s0)
        o_ref[...] = _merge_heads(o)
        ti_ref[...] = t
        state[...] = s1

    per_chunk = lambda d: pl.BlockSpec((B, None, d, d), lambda h, n: (h, n, 0, 0))
    body, s_in, s_out, s_shape, s_sems, s_args = _with_side(body, 5, 3, (H // B, N), side)
    o, states, tinv, *side_outs = pl.pallas_call(
        body, name=name, grid=(H // B, N),
        in_specs=[qs(0), qs(0), qs(voff), gs, gs] + s_in,
        out_specs=[qs(0), per_chunk(HEAD_DIM), per_chunk(CHUNK)] + s_out,
        out_shape=[_sds((S, H * HEAD_DIM)), _sds((H, N, HEAD_DIM, HEAD_DIM)), _sds((H, N, CHUNK, CHUNK))] + s_shape,
        scratch_shapes=[pltpu.VMEM((B, HEAD_DIM, HEAD_DIM), F32)] + s_sems,
        compiler_params=_cparams(("arbitrary", "arbitrary")),
    )(q, k, v, g, beta, *s_args)
    return o, states, tinv, side_outs


def _gdn_bwd(q, k, v, g, beta, states, tinv, do, *, H, voff, name, side=None):
    S = q.shape[0]
    N = S // CHUNK
    B = _gdn_group(H, voff)
    W = B * HEAD_DIM
    rs = lambda off: pl.BlockSpec((CHUNK, W), lambda h, n: (N - 1 - n, off // B + h))
    gs = pl.BlockSpec((B, CHUNK, CHUNK), lambda h, n: (h, N - 1 - n, 0))

    def body(q_ref, k_ref, v_ref, g_ref, b_ref, st_ref, ti_ref, do_ref, dq_ref, dk_ref, dv_ref, dg_ref, db_ref, dstate):
        @pl.when(pl.program_id(1) == 0)
        def _():
            dstate[...] = jnp.zeros_like(dstate)

        t_known = ti_ref[...]
        _, pull = jax.vjp(lambda *a: _gdn_chunk_fn(*a, t_known)[:2], _split_heads(q_ref[...], B),
                          _split_heads(k_ref[...], B), _split_heads(v_ref[...], B), g_ref[...], b_ref[...], st_ref[...])
        dq, dk, dv, dg, db, ds = pull((_split_heads(do_ref[...], B), dstate[...]))
        dq_ref[...] = _merge_heads(dq)
        dk_ref[...] = _merge_heads(dk)
        dv_ref[...] = _merge_heads(dv)
        dg_ref[...] = dg
        db_ref[...] = db
        dstate[...] = ds

    per_chunk = lambda d: pl.BlockSpec((B, None, d, d), lambda h, n: (h, N - 1 - n, 0, 0))
    body, s_in, s_out, s_shape, s_sems, s_args = _with_side(body, 8, 5, (H // B, N), side)
    dq, dk, dv, dg, db, *side_outs = pl.pallas_call(
        body, name=name, grid=(H // B, N),
        in_specs=[rs(0), rs(0), rs(voff), gs, gs, per_chunk(HEAD_DIM), per_chunk(CHUNK), rs(0)] + s_in,
        out_specs=[rs(0), rs(0), rs(0), gs, gs] + s_out,
        out_shape=[_sds((S, H * HEAD_DIM))] * 3 + [_sds((H, S, CHUNK))] * 2 + s_shape,
        scratch_shapes=[pltpu.VMEM((B, HEAD_DIM, HEAD_DIM), F32)] + s_sems,
        compiler_params=_cparams(("arbitrary", "arbitrary")),
    )(q, k, v, g, beta, states, tinv, do, *s_args)
    return dq, dk, dv, dg, db, side_outs


def _round_up(n, m):
    return (n + m - 1) // m * m


def _dims(S, D, M, shapes):
    c = dict(S=S, D=D, M=M)
    c["H"] = shapes["gdn_a_log"][-1]
    c["QL"] = shapes["mla_q_norm"][-1]
    c["KVL"] = shapes["mla_kv_norm"][-1]
    assert c["QL"] == c["KVL"]
    c["MEMW"] = shapes["mem_w_kv"][-1] // 2
    c["HM"] = c["MEMW"] // HEAD_DIM
    c["MW"] = c["H"] * HEAD_DIM
    c["F"] = shapes["mlp_w1"][-1]
    c["DEPTH"] = shapes["ln1_g"][0]
    c["ALPHA"] = (2 * c["DEPTH"]) ** 0.25
    c["MLA_IN"] = _round_up(c["QL"] + c["KVL"] + c["MEMW"] + LANES, 2 * LANES)
    c["GDN_IN"] = _round_up(4 * c["MW"] + c["MEMW"] + LANES, 2 * LANES)
    c["t_row"] = min(256, S)
    c["t_head"] = min(2048, S)
    c["t_conv"] = min(512, S)
    c["t_att"] = min(512, S)
    c["t_mem"] = min(2048, S)
    return c


def _pad_cols(w, n):
    return jnp.pad(w, ((0, 0), (0, n - w.shape[1])))


def _prep_mla_w_in(w, c):
    a = c["QL"] + c["KVL"]
    w = jnp.concatenate([w[:, :a], w[:, a + QK_ROPE:a + QK_ROPE + c["MEMW"]], w[:, a:a + QK_ROPE]], axis=1)
    return _pad_cols(w, c["MLA_IN"]).astype(BF16)


def _unprep_mla_w_in(dw, c):
    a, m = c["QL"] + c["KVL"], c["MEMW"]
    return jnp.concatenate([dw[:, :a], dw[:, a + m:a + m + QK_ROPE], dw[:, a:a + m]], axis=1)


def _prep_w_uq(w, c):
    w = w.reshape(c["QL"], c["H"], QK_NOPE + QK_ROPE)
    w = jnp.pad(w, ((0, 0), (0, 0), (0, QK_PAD - QK_NOPE - QK_ROPE)))
    return w.reshape(c["QL"], c["H"] * QK_PAD).astype(BF16)


def _unprep_w_uq(dw, c):
    return dw.reshape(c["QL"], c["H"], QK_PAD)[:, :, :QK_NOPE + QK_ROPE].reshape(c["QL"], c["H"] * (QK_NOPE + QK_ROPE))


def _prep_w_ukv(w, c):
    return w.reshape(c["KVL"], c["H"], 2, HEAD_DIM).transpose(0, 2, 1, 3).reshape(c["KVL"], 2 * c["MW"]).astype(BF16)


def _unprep_w_ukv(dw, c):
    return dw.reshape(c["KVL"], 2, c["H"], HEAD_DIM).transpose(0, 2, 1, 3).reshape(c["KVL"], 2 * c["MW"])


def _prep_gdn_w_in(w, c):
    a, h2 = 4 * c["MW"], 2 * c["H"]
    w = jnp.concatenate([w[:, :a], w[:, a + h2:], w[:, a:a + h2]], axis=1)
    return _pad_cols(w, c["GDN_IN"]).astype(BF16)


def _unprep_gdn_w_in(dw, c):
    a, h2, m = 4 * c["MW"], 2 * c["H"], c["MEMW"]
    return jnp.concatenate([dw[:, :a], dw[:, a + m:a + m + h2], dw[:, a:a + m]], axis=1)


def _lane_bcast(v):
    return jnp.broadcast_to(v.astype(F32)[:, None, None], (v.shape[0], 1, LANES))


def _row(i):
    return (i, 0)


def _par(i):
    return (0, 0)


def _layer_norm(z, g, b, c, name):
    S, D, ts = c["S"], c["D"], c["t_row"]
    return _ew(_ln_fn, [(z, (ts, D), _row), (g, (1, D), _par), (b, (1, D), _par)],
               [(_sds((S, D)), (ts, D), _row), (_sds((S, D), BF16), (ts, D), _row)], (S // ts,), name=name)


def _layer_norm_bwd(z, g, b, dy, c, name):
    S, D, ts = c["S"], c["D"], c["t_row"]
    fn = lambda z, g, b: _ln_fn(z, g, b)[:1]
    dz, dg, db, dzb = _ew_vjp(fn, [(z, (ts, D), _row), (g, (1, D), _par), (b, (1, D), _par)], [(dy, (ts, D), _row)],
                              [(_sds((S, D)), (ts, D), _row, "set"), (_sds((1, D)), (1, D), _par, "acc_all"),
                               (_sds((1, D)), (1, D), _par, "acc_all")], (S // ts,), name=name,
                              copies=[(0, _sds((S, D), BF16))])
    return dz, dzb, dg, db


def _mem_attn_fwd(h, qoff, memkv, c, name):
    return _flash_fwd(h, memkv, memkv, H=c["HM"], dq=HEAD_DIM, dv=HEAD_DIM, qoff=qoff, koff=0, voff=c["HM"],
                      causal=False, scale=HEAD_DIM ** -0.5, tq=c["t_mem"], tk=c["M"], name=name)[:2]


def _mem_attn_bwd(h, qoff, memkv, om, lsem, dcat, c, name):
    return _flash_bwd(h, memkv, memkv, om, lsem, dcat, H=c["HM"], dq=HEAD_DIM, dv=HEAD_DIM, qoff=qoff, koff=0,
                      voff=c["HM"], dooff=c["H"], causal=False, scale=HEAD_DIM ** -0.5, tq=c["t_mem"], tk=c["M"],
                      name=name)[:3]


def _mla_specs(c):
    H, ts = c["H"], c["t_head"]
    kr_blk = (c["QL"] + c["KVL"] + c["MEMW"]) // LANES
    hd = lambda i, h: (i, h)
    return [((ts, QK_PAD), hd), ((ts, HEAD_DIM), hd), ((ts, LANES), lambda i, h: (i, kr_blk)),
            ((ts, LANES), lambda i, h: (i, 0)), ((ts, LANES), lambda i, h: (i, 0))]


def _mla_fwd(xb, p, cosp, sinp, c, side=None):
    S, H, QL, ts, tr = c["S"], c["H"], c["QL"], c["t_head"], c["t_row"]
    h, = _mm(xb, p["w_in"], "nn", [F32], name="mla_in")
    nq, = _ew(_rms_fn, [(h, (tr, QL), lambda i: (i, 0)), (p["q_norm"], (1, QL), _par)],
              [(_sds((S, QL), BF16), (tr, QL), _row)], (S // tr,), name="mla_qnorm")
    nkv, = _ew(_rms_fn, [(h, (tr, QL), lambda i: (i, 1)), (p["kv_norm"], (1, QL), _par)],
               [(_sds((S, QL), BF16), (tr, QL), _row)], (S // tr,), name="mla_kvnorm")
    qraw, = _mm(nq, p["w_uq"], "nn", [F32], name="mla_uq")
    kvraw, = _mm(nkv, p["w_ukv"], "nn", [BF16], name="mla_ukv")
    sp = _mla_specs(c)
    ins = [(a, b, m) for a, (b, m) in zip([qraw, kvraw, h, cosp, sinp], sp)]
    qp, kp = _ew(_mla_prep_fn, ins, [(_sds((S, H * QK_PAD), BF16), (ts, QK_PAD), lambda i, h: (i, h))] * 2,
                 (S // ts, H), name="mla_rope")
    o, lse, got = _flash_fwd(qp, kp, kvraw, H=H, dq=QK_PAD, dv=HEAD_DIM, qoff=0, koff=0, voff=H, causal=True,
                             scale=None, tq=c["t_att"], tk=c["t_att"], name="mla_attn", side=side)
    return o, dict(h=h, nq=nq, nkv=nkv, qraw=qraw, kvraw=kvraw, qp=qp, kp=kp, o=o, lse=lse), got


def _mla_bwd(sv, p, cosp, sinp, dcat, dqm, c, side=None):
    S, H, QL, ts, tr = c["S"], c["H"], c["QL"], c["t_head"], c["t_row"]
    dqp, dkp, dv, got = _flash_bwd_causal(sv["qp"], sv["kp"], sv["kvraw"], sv["o"], sv["lse"], dcat, H=H, dq=QK_PAD,
                                          dv=HEAD_DIM, qoff=0, koff=0, voff=H, dooff=0,
                                          tq=c["t_att"], name="mla_attn_bwd",
                                          side=side)
    sp = _mla_specs(c)
    ins = [(a, b, m) for a, (b, m) in zip([sv["qraw"], sv["kvraw"], sv["h"], cosp, sinp], sp)]
    hd = lambda i, h: (i, h)
    dqraw, dknope, dkr = _ew_vjp(
        _mla_prep_fn, ins, [(dqp, (ts, QK_PAD), hd), (dkp, (ts, QK_PAD), hd)],
        [(_sds((S, H * QK_PAD), BF16), (ts, QK_PAD), hd, "set"), (_sds((S, H * HEAD_DIM), BF16), (ts, HEAD_DIM), hd, "set"),
         (_sds((S, LANES)), (ts, LANES), lambda i, h: (i, 0), "acc"), None, None], (S // ts, H), name="mla_rope_bwd")
    dkvraw = jnp.concatenate([dknope, dv.astype(BF16)], axis=1)
    dnq, = _mm(dqraw, p["w_uq"], "nt", [F32], name="mla_uq_dx")
    dw_uq, = _mm(sv["nq"], dqraw, "tn", [F32], name="mla_uq_dw")
    dnkv, = _mm(dkvraw, p["w_ukv"], "nt", [F32], name="mla_ukv_dx")
    dw_ukv, = _mm(sv["nkv"], dkvraw, "tn", [F32], name="mla_ukv_dw")
    dcq, dgq = _ew_vjp(_rms_fn, [(sv["h"], (tr, QL), lambda i: (i, 0)), (p["q_norm"], (1, QL), _par)],
                       [(dnq, (tr, QL), _row)],
                       [(_sds((S, QL), BF16), (tr, QL), _row, "set"), (_sds((1, QL)), (1, QL), _par, "acc_all")],
                       (S // tr,), name="mla_qnorm_bwd")
    dckv, dgkv = _ew_vjp(_rms_fn, [(sv["h"], (tr, QL), lambda i: (i, 1)), (p["kv_norm"], (1, QL), _par)],
                         [(dnkv, (tr, QL), _row)],
                         [(_sds((S, QL), BF16), (tr, QL), _row, "set"), (_sds((1, QL)), (1, QL), _par, "acc_all")],
                         (S // tr,), name="mla_kvnorm_bwd")
    pad = c["MLA_IN"] - (2 * QL + c["MEMW"] + LANES)
    dh = jnp.concatenate([dcq, dckv, dqm.astype(BF16), dkr.astype(BF16)] + ([jnp.zeros((S, pad), BF16)] if pad else []),
                         axis=1)
    grads = dict(mla_q_norm=dgq[0], mla_kv_norm=dgkv[0], mla_w_uq=_unprep_w_uq(dw_uq, c),
                 mla_w_ukv=_unprep_w_ukv(dw_ukv, c))
    return dh, grads, got


def _gdn_ins(h, qkvc, p, c):
    H, ts = c["H"], c["t_head"]
    ab_blk = (4 * c["MW"] + c["MEMW"]) // LANES
    qk_ins = [(qkvc, (ts, HEAD_DIM), lambda i, h: (i, h)), (qkvc, (ts, HEAD_DIM), lambda i, h: (i, H + h))]
    gate_ins = [(h, (ts, LANES), lambda i, h: (i, ab_blk)), (p["a_log"], (None, 1, LANES), lambda i, h: (h, 0, 0)),
                (p["dt_bias"], (None, 1, LANES), lambda i, h: (h, 0, 0))]
    return qk_ins, gate_ins


def _gdn_out_ins(o, h, p, c):
    H, ts = c["H"], c["t_head"]
    return [(o, (ts, HEAD_DIM), lambda i, h: (i, h)), (h, (ts, HEAD_DIM), lambda i, h: (i, 3 * H + h)),
            (p["o_norm"], (1, HEAD_DIM), lambda i, h: (0, 0))]


def _gdn_layer_fwd(xb, p, c, side=None):
    S, H, MW, ts = c["S"], c["H"], c["MW"], c["t_head"]
    h, = _mm(xb, p["w_in"], "nn", [F32], name="gdn_in")
    tc = _tile(3 * MW, 512, LANES)
    qkvc = _conv_fwd(h, p["conv"], C=3 * MW, ts=c["t_conv"], tc=tc, name="gdn_conv")
    qk_ins, gate_ins = _gdn_ins(h, qkvc, p, c)
    hd = lambda i, h: (i, h)
    qn, kn = _ew(_gdn_qk_fn, qk_ins, [(_sds((S, MW)), (ts, HEAD_DIM), hd)] * 2, (S // ts, H), name="gdn_qknorm")
    g3 = lambda i, h: (h, i, 0)
    g, beta = _ew(_gdn_gate_fn(H, 1), gate_ins, [(_sds((H, S, CHUNK)), (None, ts, CHUNK), g3)] * 2, (S // ts, H),
                  name="gdn_gate")
    o, states, tinv, got = _gdn_fwd(qn, kn, qkvc, g, beta, H=H, voff=2 * H, name="gdn_delta", side=side)
    mix, = _ew(_gdn_out_fn, _gdn_out_ins(o, h, p, c), [(_sds((S, MW)), (ts, HEAD_DIM), hd)], (S // ts, H),
               name="gdn_outnorm")
    return mix, dict(h=h, qkvc=qkvc, qn=qn, kn=kn, g=g, beta=beta, o=o, states=states, tinv=tinv), got


def _gdn_layer_bwd(sv, p, dcat, dqm, c, side=None):
    S, H, MW, ts = c["S"], c["H"], c["MW"], c["t_head"]
    hd = lambda i, h: (i, h)
    g3 = lambda i, h: (h, i, 0)
    h, qkvc = sv["h"], sv["qkvc"]
    do, dz, d_onorm = _ew_vjp(_gdn_out_fn, _gdn_out_ins(sv["o"], h, p, c), [(dcat, (ts, HEAD_DIM), hd)],
                              [(_sds((S, MW)), (ts, HEAD_DIM), hd, "set"), (_sds((S, MW), BF16), (ts, HEAD_DIM), hd, "set"),
                               (_sds((1, HEAD_DIM)), (1, HEAD_DIM), lambda i, h: (0, 0), "acc_all")],
                              (S // ts, H), name="gdn_outnorm_bwd")
    dqn, dkn, dv, dg, db, got = _gdn_bwd(sv["qn"], sv["kn"], qkvc, sv["g"], sv["beta"], sv["states"], sv["tinv"], do,
                                         H=H, voff=2 * H, name="gdn_delta_bwd", side=side)
    qk_ins, gate_ins = _gdn_ins(h, qkvc, p, c)
    dqc, dkc = _ew_vjp(_gdn_qk_fn, qk_ins, [(dqn, (ts, HEAD_DIM), hd), (dkn, (ts, HEAD_DIM), hd)],
                       [(_sds((S, MW)), (ts, HEAD_DIM), hd, "set")] * 2, (S // ts, H), name="gdn_qknorm_bwd")
    full3 = lambda i, h: (0, 0, 0)
    dab, dalog, ddt = _ew_vjp(
        _gdn_gate_fn(H, 1), gate_ins, [(dg, (None, ts, CHUNK), g3), (db, (None, ts, CHUNK), g3)],
        [(_sds((S, LANES), BF16), (ts, LANES), lambda i, h: (i, 0), "acc"),
         (_sds((H, 1, LANES)), (H, 1, LANES), full3, ("acc_at", 1)),
         (_sds((H, 1, LANES)), (H, 1, LANES), full3, ("acc_at", 1))], (S // ts, H), name="gdn_gate_bwd")
    dqkvc = jnp.concatenate([dqc, dkc, dv], axis=1)
    tc = _tile(3 * MW, 512, LANES)
    dxc, dconv = _conv_bwd(h, p["conv"], dqkvc, C=3 * MW, ts=c["t_conv"], tc=tc, name="gdn_conv_bwd")
    pad = c["GDN_IN"] - (4 * MW + c["MEMW"] + LANES)
    dh = jnp.concatenate([dxc.astype(BF16), dz, dqm.astype(BF16), dab] + ([jnp.zeros((S, pad), BF16)] if pad else []),
                         axis=1)
    grads = dict(gdn_conv=dconv, gdn_a_log=jnp.sum(dalog[:, 0, :], axis=-1), gdn_dt_bias=jnp.sum(ddt[:, 0, :], axis=-1),
                 gdn_o_norm=d_onorm[0])
    return dh, grads, got


def _prep_layer(W, i, c):
    bf = lambda a: a.astype(BF16)
    row = lambda a: a[None].astype(F32)
    how = dict(mem_w_kv=("mem_w_kv", bf), w_out=("w_out", bf), mlp_w1=("w1", bf), mlp_w2=("w2", bf),
               ln1_g=("ln1_g", row), ln1_b=("ln1_b", row), ln2_g=("ln2_g", row), ln2_b=("ln2_b", row),
               mla_w_in=("w_in", lambda a: _prep_mla_w_in(a, c)), mla_q_norm=("q_norm", row),
               mla_w_uq=("w_uq", lambda a: _prep_w_uq(a, c)), mla_kv_norm=("kv_norm", row),
               mla_w_ukv=("w_ukv", lambda a: _prep_w_ukv(a, c)),
               gdn_w_in=("w_in", lambda a: _prep_gdn_w_in(a, c)), gdn_conv=("conv", lambda a: a.astype(F32)),
               gdn_a_log=("a_log", _lane_bcast), gdn_dt_bias=("dt_bias", _lane_bcast), gdn_o_norm=("o_norm", row))
    return {how[k][0]: how[k][1](a) for k, a in W.items()}


def _local_step(x, mem, positions, W, loss_target, c, next_weights=None, grad_sink=None):
    S, D, H, MW, ALPHA = c["S"], c["D"], c["H"], c["MW"], c["ALPHA"]
    inv_freq = 1.0 / (ROPE_THETA ** (jnp.arange(0, QK_ROPE, 2, dtype=F32) / QK_ROPE))
    ang = positions.astype(F32)[:, None] * inv_freq
    cos, sin = jnp.cos(ang), jnp.sin(ang)
    cosp = jnp.concatenate([cos, cos, jnp.ones((S, LANES - QK_ROPE), F32)], axis=1)
    sinp = jnp.concatenate([sin, sin, jnp.zeros((S, LANES - QK_ROPE), F32)], axis=1)
    memb = mem.astype(BF16)
    xf, xb = x, x.astype(BF16)
    saved, params = [], []
    w_next = W[0]
    for i in range(c["DEPTH"]):
        p = _prep_layer(w_next if next_weights is not None else W[i], i, c)
        mla = i % 2 == 0
        memkv, = _mm(memb, p["mem_w_kv"], "nn", [BF16], name="mem_kv")
        side, arrived = next_weights(i + 1) if next_weights is not None and i + 1 < c["DEPTH"] else (None, None)
        if mla:
            mix, sv, got = _mla_fwd(xb, p, cosp, sinp, c, side)
            qoff = (c["QL"] + c["KVL"]) // LANES
        else:
            mix, sv, got = _gdn_layer_fwd(xb, p, c, side)
            qoff = 4 * MW // LANES
        if side is not None:
            rest, w_next = arrived(got)
            p.update(_prep_layer(rest, i, c))
        om, lsem = _mem_attn_fwd(sv["h"], qoff, memkv, c, "mem_attn")
        cat = jnp.concatenate([mix, om], axis=1).astype(BF16)
        z1, = _mm(cat, p["w_out"], "nn", [F32], name="w_out", extras=(xf,), epilogue=lambda acc, r: (ALPHA * r + acc,))
        x1, x1b = _layer_norm(z1, p["ln1_g"], p["ln1_b"], c, "ln1")
        u, a = _mm(x1b, p["w1"], "nn", [F32, BF16], name="mlp_up", b_major=True,
                   epilogue=lambda acc: (acc, jnp.square(jnp.maximum(acc, 0.0))))
        z2, = _mm(a, p["w2"], "nn", [F32], name="mlp_down", extras=(x1,), epilogue=lambda acc, r: (ALPHA * r + acc,))
        x2, x2b = _layer_norm(z2, p["ln2_g"], p["ln2_b"], c, "ln2")
        sv.update(xb=xb, memkv=memkv, om=om, lsem=lsem, cat=cat, z1=z1, x1b=x1b, u=u, a=a, z2=z2, qoff=qoff)
        saved.append(sv)
        params.append(p)
        xf, xb = x2, x2b

    ts = c["t_row"]
    lsum, dy = _ew(_loss_fn, [(xf, (ts, D), _row), (loss_target, (ts, D), _row)],
                   [(_sds((1, D)), (1, D), _par), (_sds((S, D)), (ts, D), _row)], (S // ts,), name="loss", acc_out=(0,))
    loss = jnp.sum(lsum)

    grads = [None] * c["DEPTH"]
    pending = None
    dx = dy
    for i in reversed(range(c["DEPTH"])):
        p, sv = params[i], saved[i]
        mla = i % 2 == 0
        G = {}
        dz2, dz2b, dg, db = _layer_norm_bwd(sv["z2"], p["ln2_g"], p["ln2_b"], dx, c, "ln2_bwd")
        G["ln2_g"], G["ln2_b"] = dg[0], db[0]
        du, = _mm(dz2b, p["w2"], "nt", [BF16], name="mlp_down_dx", extras=(sv["u"],),
                  epilogue=lambda acc, u: (acc * (2.0 * jnp.maximum(u, 0.0)),))
        G["mlp_w2"], = _mm(sv["a"], dz2b, "tn", [F32], name="mlp_down_dw")
        G["mlp_w1"], = _mm(sv["x1b"], du, "tn", [F32], name="mlp_up_dw", out_major=True)
        dx1, = _mm(du, p["w1"], "nt", [F32], name="mlp_up_dx", extras=(dz2,), b_major=True,
                   epilogue=lambda acc, r: (ALPHA * r + acc,))
        dz1, dz1b, dg, db = _layer_norm_bwd(sv["z1"], p["ln1_g"], p["ln1_b"], dx1, c, "ln1_bwd")
        G["ln1_g"], G["ln1_b"] = dg[0], db[0]
        dcat, = _mm(dz1b, p["w_out"], "nt", [BF16], name="w_out_dx")
        G["w_out"], = _mm(sv["cat"], dz1b, "tn", [F32], name="w_out_dw")
        dqm, dkm, dvm = _mem_attn_bwd(sv["h"], sv["qoff"], sv["memkv"], sv["om"], sv["lsem"], dcat, c, "mem_attn_bwd")
        dmemkv = jnp.concatenate([dkm, dvm], axis=1).astype(BF16)
        G["mem_w_kv"], = _mm(memb, dmemkv, "tn", [F32], name="mem_kv_dw")
        riders = ([pending] if pending is not None else []) + ([grad_sink(i, G, True) + (i,)] if grad_sink else [])
        side = _merge_plans([r[0] for r in riders]) if riders else None
        if mla:
            dh, g, got = _mla_bwd(sv, p, cosp, sinp, dcat, dqm, c, side)
            G.update(g)
            dw_in, = _mm(sv["xb"], dh, "tn", [F32], name="mla_in_dw")
            G["mla_w_in"] = _unprep_mla_w_in(dw_in, c)
            dx, = _mm(dh, p["w_in"], "nt", [F32], name="mla_in_dx", extras=(dz1,),
                      epilogue=lambda acc, r: (ALPHA * r + acc,))
        else:
            dh, g, got = _gdn_layer_bwd(sv, p, dcat, dqm, c, side)
            G.update(g)
            dw_in, = _mm(sv["xb"], dh, "tn", [F32], name="gdn_in_dw")
            G["gdn_w_in"] = _unprep_gdn_w_in(dw_in, c)
            dx, = _mm(dh, p["w_in"], "nt", [F32], name="gdn_in_dx", extras=(dz1,),
                      epilogue=lambda acc, r: (ALPHA * r + acc,))
        grads[i] = dict(G)
        for plan, done, layer in riders:
            n_out = len(plan["out_shape"])
            grads[layer].update(done(got[:n_out]))
            got = got[n_out:]
        pending = grad_sink(i, G, False) + (i,) if grad_sink else None
    if pending is not None:
        grads[pending[2]].update(pending[1](_run_plan(pending[0], "grad_scatter_last")))
    return loss, dx, grads


_HBM = pl.BlockSpec(memory_space=pltpu.HBM)
_VMEM = pl.BlockSpec(memory_space=pltpu.VMEM)


def _my_place():
    return lax.axis_index("x"), lax.axis_index("y"), lax.axis_index("c")


def _my_chip():
    return 2 * lax.axis_index("x") + lax.axis_index("y")


def _other_chips(x, y):
    return [(1 - x, y), (x, 1 - y), (1 - x, 1 - y)]


def _gather_plan(arrs, by_rows):
    n = len(arrs)
    halved = [_halved(a) for a in arrs]

    def run(ins, outs, sems, start, wait):
        send_sems, recv_sems, local_sems = sems
        x, y, c = _my_place()
        chips = _other_chips(x, y)

        def copy(i, k, chip, to):
            half = c if halved[i] else None
            src = ins[i] if half is None else ins[i].at[pl.ds(c * (arrs[i].shape[0] // 2), arrs[i].shape[0] // 2)]
            return pltpu.make_async_remote_copy(src_ref=src, dst_ref=_slab(outs[i], arrs[i].shape[0], by_rows[i], chip, half),
                                                send_sem=send_sems.at[3 * i + k], recv_sem=recv_sems.at[3 * i + k],
                                                device_id=to, device_id_type=MESH)

        mine = [pltpu.make_async_copy(ins[i], _slab(outs[i], arrs[i].shape[0], by_rows[i], 2 * x + y, None),
                                      local_sems.at[i]) for i in range(n)]
        sends = [copy(i, k, 2 * x + y, (cx, cy, c)) for i in range(n) for k, (cx, cy) in enumerate(chips)]
        if start:
            for cp in mine + sends:
                cp.start()
        if wait:
            for i in range(n):
                for k, (cx, cy) in enumerate(chips):
                    copy(i, k, 2 * cx + cy, (cx, cy, c)).wait_recv()
            for cp in sends:
                cp.wait_send()
            for cp in mine:
                cp.wait()

    shapes = [jax.ShapeDtypeStruct((N_CHIPS * a.shape[0], a.shape[1]) if r else (N_CHIPS,) + a.shape, a.dtype)
              for a, r in zip(arrs, by_rows)]
    sems = [pltpu.SemaphoreType.DMA((3 * n,)), pltpu.SemaphoreType.DMA((3 * n,)), pltpu.SemaphoreType.DMA((n,))]
    return dict(arrs=list(arrs), out_shape=shapes, sems=sems, run=run)


def _halved(a):
    return a.shape[0] % (4 * SUBLANES * (4 // a.dtype.itemsize)) == 0


def _slab(out, r, by_rows, chip, half):
    lo, n = (0, r) if half is None else (half * (r // 2), r // 2)
    return out.at[pl.ds(chip * r + lo, n)] if by_rows else out.at[chip].at[pl.ds(lo, n)]


def _gather_fill(outs, arrs, by_rows, name):
    idx = [i for i, a in enumerate(arrs) if _halved(a)]
    n = len(idx)

    def body(*refs):
        ins, bufs = refs[:n], refs[n:2 * n]
        send_sems, recv_sems = refs[2 * n:]
        x, y, c = _my_place()

        def copy(j, k, chip, half):
            r = arrs[idx[j]].shape[0]
            return pltpu.make_async_remote_copy(src_ref=_slab(ins[j], r, by_rows[idx[j]], chip, half),
                                                dst_ref=_slab(bufs[j], r, by_rows[idx[j]], chip, half),
                                                send_sem=send_sems.at[3 * j + k], recv_sem=recv_sems.at[3 * j + k],
                                                device_id=(x, y, 1 - c), device_id_type=MESH)

        chips = [2 * cx + cy for cx, cy in _other_chips(x, y)]
        sends = [copy(j, k, chip, c) for j in range(n) for k, chip in enumerate(chips)]
        for cp in sends:
            cp.start()
        for j in range(n):
            for k, chip in enumerate(chips):
                copy(j, k, chip, 1 - c).wait_recv()
        for cp in sends:
            cp.wait_send()

    filled = pl.pallas_call(
        body, name=name, in_specs=[_HBM] * n, out_specs=[_HBM] * n,
        out_shape=[jax.ShapeDtypeStruct(outs[i].shape, outs[i].dtype) for i in idx],
        input_output_aliases={j: j for j in range(n)},
        scratch_shapes=[pltpu.SemaphoreType.DMA((3 * n,)), pltpu.SemaphoreType.DMA((3 * n,))],
    )(*[outs[i] for i in idx])
    res = list(outs)
    for i, f in zip(idx, filled):
        res[i] = f
    return res


def _merge_plans(plans):
    def run(ins, outs, sems, start, wait):
        a = b = s = 0
        for p in plans:
            na, nb, ns = len(p["arrs"]), len(p["out_shape"]), len(p["sems"])
            p["run"](ins[a:a + na], outs[b:b + nb], sems[s:s + ns], start, wait)
            a, b, s = a + na, b + nb, s + ns

    return dict(arrs=sum((p["arrs"] for p in plans), []), out_shape=sum((p["out_shape"] for p in plans), []),
                sems=sum((p["sems"] for p in plans), []), run=run)


def _scatter_plan(ps):
    n = len(ps)

    def run(ins, outs, sems, start, wait):
        send_sems, recv_sems = sems
        x, y, c = _my_place()
        cps = [pltpu.make_async_remote_copy(src_ref=ins[i].at[2 * cx + cy], dst_ref=outs[i].at[k],
                                            send_sem=send_sems.at[3 * i + k], recv_sem=recv_sems.at[3 * i + k],
                                            device_id=(cx, cy, c), device_id_type=MESH)
               for i in range(n) for k, (cx, cy) in enumerate(_other_chips(x, y))]
        if start:
            for cp in cps:
                cp.start()
        if wait:
            for cp in cps:
                cp.wait()

    shapes = [jax.ShapeDtypeStruct((3,) + p.shape[1:], p.dtype) for p in ps]
    sems = [pltpu.SemaphoreType.DMA((3 * n,)), pltpu.SemaphoreType.DMA((3 * n,))]
    return dict(arrs=list(ps), out_shape=shapes, sems=sems, run=run)


def _run_plan(plan, name):
    n_in, n_out = len(plan["arrs"]), len(plan["out_shape"])

    def body(*refs):
        plan["run"](refs[:n_in], refs[n_in:n_in + n_out], refs[n_in + n_out:], True, True)

    return pl.pallas_call(body, name=name, in_specs=[_HBM] * n_in, out_specs=[_HBM] * n_out,
                          out_shape=plan["out_shape"], scratch_shapes=plan["sems"])(*plan["arrs"])


def _with_side(body, n_in, n_out, grid, side):
    if side is None:
        return body, [], [], [], [], []
    s_in, s_out, s_sem = len(side["arrs"]), len(side["out_shape"]), len(side["sems"])

    def wrapped(*refs):
        ins, s_ins = refs[:n_in], refs[n_in:n_in + s_in]
        o0 = n_in + s_in
        outs, s_outs = refs[o0:o0 + n_out], refs[o0 + n_out:o0 + n_out + s_out]
        rest = refs[o0 + n_out + s_out:]
        scratch, s_sems = rest[:len(rest) - s_sem], rest[len(rest) - s_sem:]
        ids = [pl.program_id(d) for d in range(len(grid))]
        first = functools.reduce(jnp.logical_and, [i == 0 for i in ids])
        last = functools.reduce(jnp.logical_and, [i == g - 1 for i, g in zip(ids, grid)])
        pl.when(first)(lambda: side["run"](s_ins, s_outs, s_sems, True, False))
        body(*ins, *outs, *scratch)
        pl.when(last)(lambda: side["run"](s_ins, s_outs, s_sems, False, True))

    return wrapped, [_HBM] * s_in, [_HBM] * s_out, side["out_shape"], side["sems"], side["arrs"]


def _swap_halves(gs, name):
    n = len(gs)

    def body(*refs):
        ins, outs = refs[:n], refs[n:2 * n]
        send_sems, recv_sems = refs[2 * n:]
        x, y, c = _my_place()
        cps = [pltpu.make_async_remote_copy(src_ref=ins[i].at[:, 1 - c], dst_ref=outs[i], send_sem=send_sems.at[i],
                                            recv_sem=recv_sems.at[i], device_id=(x, y, 1 - c), device_id_type=MESH)
               for i in range(n)]
        for cp in cps:
            cp.start()
        for cp in cps:
            cp.wait()

    return pl.pallas_call(
        body, name=name, in_specs=[_HBM] * n, out_specs=[_HBM] * n,
        out_shape=[jax.ShapeDtypeStruct((g.shape[0],) + g.shape[2:], g.dtype) for g in gs],
        scratch_shapes=[pltpu.SemaphoreType.DMA((n,)), pltpu.SemaphoreType.DMA((n,))],
    )(*gs)


def _join_halves(fs, name):
    n = len(fs)

    def body(*refs):
        ins, outs = refs[:n], refs[n:2 * n]
        send_sems, recv_sems = refs[2 * n:]
        x, y, c = _my_place()

        def copy(i, half):
            return pltpu.make_async_remote_copy(src_ref=ins[i].at[half], dst_ref=outs[i].at[half],
                                                send_sem=send_sems.at[i], recv_sem=recv_sems.at[i],
                                                device_id=(x, y, 1 - c), device_id_type=MESH)

        sends = [copy(i, c) for i in range(n)]
        for cp in sends:
            cp.start()
        for i in range(n):
            copy(i, 1 - c).wait_recv()
        for cp in sends:
            cp.wait_send()

    return pl.pallas_call(
        body, name=name, in_specs=[_HBM] * n, out_specs=[_HBM] * n,
        out_shape=[jax.ShapeDtypeStruct(f.shape, f.dtype) for f in fs],
        input_output_aliases={i: i for i in range(n)},
        scratch_shapes=[pltpu.SemaphoreType.DMA((n,)), pltpu.SemaphoreType.DMA((n,))],
    )(*fs)


def _row_tile(a, b):
    return _tile(a, max(SUBLANES, (1 << 19) // b // SUBLANES * SUBLANES), SUBLANES)


def _add_core(g, got, name):
    _, _, A, B = g.shape
    ta = _row_tile(A, B)
    return _ew(lambda p, q: (p + q,),
               [(g, (None, None, ta, B), lambda s, i: (s, lax.axis_index("c"), i, 0)),
                (got, (None, ta, B), lambda s, i: (s, i, 0))],
               [(_sds((N_CHIPS, A, B)), (None, ta, B), lambda s, i: (s, i, 0))], (N_CHIPS, A // ta), name=name)[0]


def _add_chips(p, got, name):
    _, A, B = p.shape
    ta = _row_tile(A, B)
    blk = (None, ta, B)
    return _ew(lambda a, b, c_, d: (((a + b) + c_) + d,),
               [(p, blk, lambda i: (_my_chip(), i, 0)), (got, blk, lambda i: (0, i, 0)),
                (got, blk, lambda i: (1, i, 0)), (got, blk, lambda i: (2, i, 0))],
               [(_sds((2, A, B)), blk, lambda i: (lax.axis_index("c"), i, 0))], (A // ta,), name=name)[0]


def _all_reduce_small(v, name):
    r = v.shape[0]
    masks = [(mx, my, mc) for mx in (0, 1) for my in (0, 1) for mc in (0, 1)][1:]

    def body(v_ref, out_ref, gath, send_sems, recv_sems):
        x, y, c = _my_place()
        me = 4 * x + 2 * y + c
        gath[me] = v_ref[...]

        def peer(m):
            return (x + m[0] - 2 * x * m[0], y + m[1] - 2 * y * m[1], c + m[2] - 2 * c * m[2])

        def copy(k, slab, to):
            return pltpu.make_async_remote_copy(src_ref=v_ref, dst_ref=gath.at[slab], send_sem=send_sems.at[k],
                                                recv_sem=recv_sems.at[k], device_id=to, device_id_type=MESH)

        sends = [copy(k, me, peer(m)) for k, m in enumerate(masks)]
        for cp in sends:
            cp.start()
        for k, m in enumerate(masks):
            px, py, pc = peer(m)
            copy(k, 4 * px + 2 * py + pc, (px, py, pc)).wait_recv()
        for cp in sends:
            cp.wait_send()
        total = gath[0]
        for d in range(1, 8):
            total = total + gath[d]
        out_ref[...] = total

    return pl.pallas_call(
        body, name=name, in_specs=[_VMEM], out_specs=_VMEM, out_shape=jax.ShapeDtypeStruct((r, LANES), F32),
        scratch_shapes=[pltpu.VMEM((8, r, LANES), F32), pltpu.SemaphoreType.DMA((7,)), pltpu.SemaphoreType.DMA((7,))],
    )(v)


def _pack_rows(arrs, dtype, row_mult):
    flat = jnp.concatenate([a.astype(dtype).reshape(-1) for a in arrs])
    n = flat.shape[0]
    rows = _round_up(-(-n // LANES), row_mult)
    return jnp.pad(flat, (0, rows * LANES - n)).reshape(rows, LANES)


def _unpack_rows(buf, shapes):
    lead = buf.shape[:-2]
    flat = buf.reshape(lead + (-1,))
    out, o = [], 0
    for s in shapes:
        n = math.prod(s)
        out.append(lax.slice_in_dim(flat, o, o + n, axis=len(lead)).reshape(lead + tuple(s)))
        o += n
    return out


WEIGHTS = ["mla_w_in", "mla_q_norm", "mla_w_uq", "mla_kv_norm", "mla_w_ukv", "gdn_w_in", "gdn_conv", "gdn_a_log",
           "gdn_dt_bias", "gdn_o_norm", "mem_w_kv", "w_out", "ln1_g", "ln1_b", "mlp_w1", "mlp_w2", "ln2_g", "ln2_b"]
SHARD_AXIS = {"mla_w_in": 1, "mla_w_uq": 2, "mla_w_ukv": 2, "gdn_w_in": 2, "gdn_conv": 2, "mem_w_kv": 1, "w_out": 1,
              "mlp_w1": 2, "mlp_w2": 1}
SMALL = [k for k in WEIGHTS if k not in SHARD_AXIS] + ["gdn_conv"]
BIG = [k for k in WEIGHTS if k not in SMALL]
MLA_KEYS = ["mla_w_in", "mla_q_norm", "mla_w_uq", "mla_kv_norm", "mla_w_ukv"]
GDN_KEYS = ["gdn_w_in", "gdn_conv", "gdn_a_log", "gdn_dt_bias", "gdn_o_norm"]
ALL_KEYS = ["mem_w_kv", "w_out", "ln1_g", "ln1_b", "mlp_w1", "mlp_w2", "ln2_g", "ln2_b"]


def _layer_keys(i):
    return (MLA_KEYS if i % 2 == 0 else GDN_KEYS) + ALL_KEYS


def _layer_slot(k, i):
    return i // 2 if k in MLA_KEYS or k in GDN_KEYS else i


AFTER_MIXER = ["w_out", "mlp_w1", "mlp_w2"]


def _gather_layer(w, i, part="all"):
    mine = [k for k in _layer_keys(i) if part == "all" or (k in AFTER_MIXER) == (part == "tail")]
    keys = [k for k in mine if k in SHARD_AXIS]
    arrs = [w[k][_layer_slot(k, i)].astype(F32 if k == "gdn_conv" else BF16) for k in keys]
    by_rows = [SHARD_AXIS[k] == 1 for k in keys]

    def arrived(outs):
        outs = _gather_fill(outs, arrs, by_rows, "gather_fill_" + ("mla_" if i % 2 == 0 else "gdn_") + part)
        full = {k: w[k][_layer_slot(k, i)] for k in mine if k not in SHARD_AXIS}
        for k, o in zip(keys, outs):
            by_cols = SHARD_AXIS[k] == 2 and k != "mlp_w1"
            full[k] = jnp.concatenate([o[d] for d in range(N_CHIPS)], axis=1) if by_cols else o
        return full

    return _gather_plan(arrs, by_rows), arrived


EARLY = ["mlp_w1", "mlp_w2", "w_out", "mem_w_kv"]


def _reduce_group(i, G, early):
    kind = ("mla" if i % 2 == 0 else "gdn") + ("_early" if early else "_late")
    keys = [k for k in _layer_keys(i) if k in BIG and (k in EARLY) == early]
    canon = []
    for k in keys:
        g = G[k]
        if k == "mlp_w1":
            g = g.reshape(N_CHIPS, 2, g.shape[1] // 2, g.shape[2])
        elif SHARD_AXIS[k] == 1:
            g = g.reshape(N_CHIPS, 2, g.shape[0] // (2 * N_CHIPS), g.shape[1])
        else:
            rows, cw = g.shape[0], g.shape[1] // N_CHIPS
            g = g.reshape(rows, N_CHIPS, cw).transpose(1, 0, 2).reshape(N_CHIPS, 2, rows // 2, cw)
        canon.append(g)
    theirs = _swap_halves(canon, "grad_swap_" + kind)
    chip_sums = [_add_core(g, t, "grad_add_core") for g, t in zip(canon, theirs)]

    def done(got):
        halves = [_add_chips(p, s, "grad_add_chips") for p, s in zip(chip_sums, got)]
        joined = _join_halves(halves, "grad_join_" + kind)
        return {k: j.reshape(2 * j.shape[1], j.shape[2]) for k, j in zip(keys, joined)}

    return _scatter_plan(chip_sums), done


def _adamw(w, g, m, v, name):
    shape = w.shape
    cols = shape[-1]
    rows = math.prod(shape[:-1])
    tr = _tile(rows, max(SUBLANES, (1 << 19) // cols // SUBLANES * SUBLANES), SUBLANES)
    spec = ((tr, cols), _row)
    outs = _ew(_adamw_fn, [(a.reshape(rows, cols), *spec) for a in (w, g, m, v)], [(_sds((rows, cols)), *spec)] * 3,
               (rows // tr,), name=name)
    return [o.reshape(shape) for o in outs]


def kernel(x, mem, positions, mla_w_in, mla_q_norm, mla_w_uq, mla_kv_norm, mla_w_ukv, gdn_w_in, gdn_conv, gdn_a_log, gdn_dt_bias, gdn_o_norm, mem_w_kv, w_out, ln1_g, ln1_b, mlp_w1, mlp_w2, ln2_g, ln2_b, loss_target, m_mla_w_in, m_mla_q_norm, m_mla_w_uq, m_mla_kv_norm, m_mla_w_ukv, m_gdn_w_in, m_gdn_conv, m_gdn_a_log, m_gdn_dt_bias, m_gdn_o_norm, m_mem_w_kv, m_w_out, m_ln1_g, m_ln1_b, m_mlp_w1, m_mlp_w2, m_ln2_g, m_ln2_b, v_mla_w_in, v_mla_q_norm, v_mla_w_uq, v_mla_kv_norm, v_mla_w_ukv, v_gdn_w_in, v_gdn_conv, v_gdn_a_log, v_gdn_dt_bias, v_gdn_o_norm, v_mem_w_kv, v_w_out, v_ln1_g, v_ln1_b, v_mlp_w1, v_mlp_w2, v_ln2_g, v_ln2_b):
    w = dict(zip(WEIGHTS, (mla_w_in, mla_q_norm, mla_w_uq, mla_kv_norm, mla_w_ukv, gdn_w_in, gdn_conv, gdn_a_log,
                           gdn_dt_bias, gdn_o_norm, mem_w_kv, w_out, ln1_g, ln1_b, mlp_w1, mlp_w2, ln2_g, ln2_b)))
    m = dict(zip(WEIGHTS, (m_mla_w_in, m_mla_q_norm, m_mla_w_uq, m_mla_kv_norm, m_mla_w_ukv, m_gdn_w_in, m_gdn_conv,
                           m_gdn_a_log, m_gdn_dt_bias, m_gdn_o_norm, m_mem_w_kv, m_w_out, m_ln1_g, m_ln1_b, m_mlp_w1,
                           m_mlp_w2, m_ln2_g, m_ln2_b)))
    v = dict(zip(WEIGHTS, (v_mla_w_in, v_mla_q_norm, v_mla_w_uq, v_mla_kv_norm, v_mla_w_ukv, v_gdn_w_in, v_gdn_conv,
                           v_gdn_a_log, v_gdn_dt_bias, v_gdn_o_norm, v_mem_w_kv, v_w_out, v_ln1_g, v_ln1_b, v_mlp_w1,
                           v_mlp_w2, v_ln2_g, v_ln2_b)))
    assert x.shape[0] == 1, "one sequence per device"
    full_shapes = {k: w[k].shape for k in WEIGHTS}
    for k, ax in SHARD_AXIS.items():
        s = list(w[k].shape)
        s[ax] *= N_CHIPS
        full_shapes[k] = tuple(s)
    c = _dims(x.shape[1], x.shape[2], mem.shape[1], full_shapes)
    depth = c["DEPTH"]

    first, arrived = _gather_layer(w, 0, "head")
    W = [arrived(_run_plan(first, "gather_first"))]

    def next_weights(i):
        plan, arrived = _gather_layer(w, i)
        if i > 1:
            return plan, lambda got: ({}, arrived(got))
        tail, tail_arrived = _gather_layer(w, 0, "tail")
        n_tail = len(tail["out_shape"])
        return _merge_plans([tail, plan]), lambda got: (tail_arrived(got[:n_tail]), arrived(got[n_tail:]))

    loss_local, grad_x, G = _local_step(x[0], mem[0], positions[0], W, loss_target[0], c,
                                        next_weights=next_weights, grad_sink=_reduce_group)
    loss = lax.psum(loss_local, ("x", "y", "c"))

    def stacked(k):
        return jnp.stack([G[i][k] for i in range(depth) if k in G[i]], axis=0)

    grads = {k: stacked(k) for k in BIG}
    small_shapes = [full_shapes[k] for k in SMALL]
    gsmall = _all_reduce_small(_pack_rows([stacked(k) for k in SMALL], F32, SUBLANES), "grad_all_reduce_small")
    grads.update(dict(zip(SMALL, _unpack_rows(gsmall, small_shapes))))
    conv_cols = w["gdn_conv"].shape[2]
    grads["gdn_conv"] = lax.dynamic_slice_in_dim(grads["gdn_conv"], _my_chip() * conv_cols, conv_cols, axis=2)

    delta, new_m, new_v = {}, {}, {}
    for k in BIG + ["gdn_conv"]:
        delta[k], new_m[k], new_v[k] = _adamw(w[k], grads[k], m[k], v[k], "adamw")
    small = [k for k in SMALL if k != "gdn_conv"]
    packed = [_pack_rows([d[k] for k in small], F32, SUBLANES) for d in (w, grads, m, v)]
    ds, ms, vs = _adamw(*packed, "adamw_small")
    for d, buf in ((delta, ds), (new_m, ms), (new_v, vs)):
        d.update(dict(zip(small, _unpack_rows(buf, [w[k].shape for k in small]))))

    return (loss, grad_x[None], *[grads[k] for k in WEIGHTS], *[delta[k] for k in WEIGHTS],
            *[new_m[k] for k in WEIGHTS], *[new_v[k] for k in WEIGHTS])
```

```python
import functools
import math

import jax
import jax.numpy as jnp
from jax import lax
from jax.experimental import pallas as pl
from jax.experimental.pallas import tpu as pltpu

F32 = jnp.float32
BF16 = jnp.bfloat16
MESH = pl.DeviceIdType.MESH

LANES = 128
SUBLANES = 8
VMEM_LIMIT = 56 * 1024 * 1024
N_CHIPS = 4

HEAD_DIM = 128
QK_NOPE = 128
QK_ROPE = 64
QK_PAD = 256
ROPE_THETA = 10000.0
CONV_WIDTH = 4
CHUNK = 64
LN_EPS = 1e-5
RMS_EPS = 1e-6
ADAM_LR = 0.001
ADAM_B1 = 0.9
ADAM_B2 = 0.999
ADAM_EPS = 1e-08
ADAM_WD = 0.01
ADAM_STEP = 10
HI = lax.Precision.HIGHEST
LOG2E = 1.4426950408889634
LN2 = 0.6931471805599453
MLA_Q_SCALE = (QK_NOPE + QK_ROPE) ** -0.5 * LOG2E


def _cparams(sem=None):
    return pltpu.CompilerParams(dimension_semantics=sem, vmem_limit_bytes=VMEM_LIMIT)


def _tile(n, cap, unit):
    best = None
    t = unit
    while t <= min(n, cap):
        if n % t == 0:
            best = t
        t += unit
    return best if best is not None else n


def _mm(a, b, mode, out_dtypes, *, name, epilogue=None, extras=(), tm_cap=1024, tn_cap=1024, tk_cap=2048,
        b_major=False, out_major=False):
    if b_major:
        b_shape = (b.shape[1], N_CHIPS * b.shape[2])
    else:
        b_shape = b.shape
    if mode == "nn":
        (M, K), (K2, N) = a.shape, b_shape
    elif mode == "nt":
        (M, K), (N, K2) = a.shape, b_shape
    else:
        (K, M), (K2, N) = a.shape, b_shape
    assert K == K2, (a.shape, b.shape, mode)
    tm = _tile(M, tm_cap, LANES if mode == "tn" else 16)
    tn = _tile(N // N_CHIPS if (out_major or (b_major and mode == "nn")) else N, tn_cap, LANES)
    tk = _tile(K // N_CHIPS if (b_major and mode == "nt") else K, tk_cap, 16 if mode == "tn" else LANES)
    nk = K // tk
    nj4, nk4 = max(N // N_CHIPS // tn, 1), max(K // N_CHIPS // tk, 1)
    if mode == "nn":
        a_spec = pl.BlockSpec((tm, tk), lambda i, j, k: (i, k))
        b_spec = pl.BlockSpec((tk, tn), lambda i, j, k: (k, j))
        if b_major:
            b_spec = pl.BlockSpec((None, tk, tn), lambda i, j, k: (j // nj4, k, j % nj4))
        dims = (((1,), (0,)), ((), ()))
    elif mode == "nt":
        a_spec = pl.BlockSpec((tm, tk), lambda i, j, k: (i, k))
        b_spec = pl.BlockSpec((tn, tk), lambda i, j, k: (j, k))
        if b_major:
            b_spec = pl.BlockSpec((None, tn, tk), lambda i, j, k: (k // nk4, j, k % nk4))
        dims = (((1,), (1,)), ((), ()))
    else:
        assert not b_major
        a_spec = pl.BlockSpec((tk, tm), lambda i, j, k: (k, i))
        b_spec = pl.BlockSpec((tk, tn), lambda i, j, k: (k, j))
        dims = (((0,), (0,)), ((), ()))
    mn_spec = pl.BlockSpec((tm, tn), lambda i, j, k: (i, j))
    o_spec, o_shape = mn_spec, (M, N)
    if out_major:
        o_spec = pl.BlockSpec((None, tm, tn), lambda i, j, k: (j // nj4, i, j % nj4))
        o_shape = (N_CHIPS, M, N // N_CHIPS)
    n_ex, n_out = len(extras), len(out_dtypes)
    for e in extras:
        assert e.shape == (M, N), (e.shape, M, N)

    def body(a_ref, b_ref, *rest):
        ex_refs, out_refs, acc = rest[:n_ex], rest[n_ex:n_ex + n_out], rest[-1]
        k = pl.program_id(2)

        @pl.when(k == 0)
        def _():
            acc[...] = jnp.zeros_like(acc)

        acc[...] += lax.dot_general(a_ref[...].astype(BF16), b_ref[...].astype(BF16), dims,
                                    preferred_element_type=F32)

        @pl.when(k == nk - 1)
        def _():
            res = (acc[...],) if epilogue is None else epilogue(acc[...], *[e[...] for e in ex_refs])
            for o_ref, r in zip(out_refs, res):
                o_ref[...] = r.astype(o_ref.dtype)

    outs = pl.pallas_call(
        body, name=name, grid=(M // tm, N // tn, nk),
        in_specs=[a_spec, b_spec] + [mn_spec] * n_ex,
        out_specs=[o_spec] * n_out,
        out_shape=[jax.ShapeDtypeStruct(o_shape, d) for d in out_dtypes],
        scratch_shapes=[pltpu.VMEM((tm, tn), F32)],
        compiler_params=_cparams(("parallel", "parallel", "arbitrary")),
    )(a, b, *extras)
    return outs


def _spec(block, imap):
    return pl.BlockSpec(block, imap)


def _ew(fn, ins, outs, grid, *, name, acc_out=()):
    n_in = len(ins)
    ng = len(grid)

    def body(*refs):
        in_refs, out_refs = refs[:n_in], refs[n_in:]
        res = fn(*[r[...] for r in in_refs])
        first = functools.reduce(jnp.logical_and, [pl.program_id(d) == 0 for d in range(ng)])
        for i, (o_ref, r) in enumerate(zip(out_refs, res)):
            if i in acc_out:
                @pl.when(first)
                def _(o_ref=o_ref):
                    o_ref[...] = jnp.zeros_like(o_ref)
                o_ref[...] += r.astype(o_ref.dtype)
            else:
                o_ref[...] = r.astype(o_ref.dtype)

    return pl.pallas_call(
        body, name=name, grid=grid,
        in_specs=[_spec(b, m) for _, b, m in ins],
        out_specs=[_spec(b, m) for _, b, m in outs],
        out_shape=[s for s, _, _ in outs],
        compiler_params=_cparams(("arbitrary",) * ng),
    )(*[a for a, _, _ in ins])


def _ew_vjp(fn, ins, cts, gouts, grid, *, name, copies=()):
    n_in, n_ct = len(ins), len(cts)
    ng = len(grid)
    want = [i for i, g in enumerate(gouts) if g is not None]

    def body(*refs):
        in_refs, ct_refs, out_refs = refs[:n_in], refs[n_in:n_in + n_ct], refs[n_in + n_ct:]
        prim = [r[...] for r in in_refs]
        outs, pull = jax.vjp(fn, *prim)
        grads = pull(tuple(r[...].astype(o.dtype) for r, o in zip(ct_refs, outs)))
        first_all = functools.reduce(jnp.logical_and, [pl.program_id(d) == 0 for d in range(ng)])
        for o_ref, (i, _) in zip(out_refs[len(want):], copies):
            o_ref[...] = grads[i].astype(o_ref.dtype)
        for o_ref, i in zip(out_refs, want):
            mode = gouts[i][3]
            g = grads[i]
            if mode == "set":
                o_ref[...] = g.astype(o_ref.dtype)
            elif mode == "acc":
                @pl.when(pl.program_id(ng - 1) == 0)
                def _(o_ref=o_ref):
                    o_ref[...] = jnp.zeros_like(o_ref)
                o_ref[...] += g.astype(o_ref.dtype)
            elif mode == "acc_all":
                @pl.when(first_all)
                def _(o_ref=o_ref):
                    o_ref[...] = jnp.zeros_like(o_ref)
                o_ref[...] += g.astype(o_ref.dtype)
            else:
                @pl.when(first_all)
                def _(o_ref=o_ref):
                    o_ref[...] = jnp.zeros_like(o_ref)
                idx = pl.program_id(mode[1])
                o_ref[idx] += g.astype(o_ref.dtype)

    return pl.pallas_call(
        body, name=name, grid=grid,
        in_specs=[_spec(b, m) for _, b, m in ins] + [_spec(b, m) for _, b, m in cts],
        out_specs=[_spec(gouts[i][1], gouts[i][2]) for i in want] + [_spec(gouts[i][1], gouts[i][2]) for i, _ in copies],
        out_shape=[gouts[i][0] for i in want] + [s for _, s in copies],
        compiler_params=_cparams(("arbitrary",) * ng),
    )(*[a for a, _, _ in ins], *[a for a, _, _ in cts])


def _sds(shape, dtype=F32):
    return jax.ShapeDtypeStruct(tuple(shape), dtype)


def _ln_fn(z, g, b):
    mu = jnp.mean(z, -1, keepdims=True)
    d = z - mu
    var = jnp.mean(d * d, -1, keepdims=True)
    y = d * lax.rsqrt(var + LN_EPS) * g + b
    return y, y


def _rms_fn(x, g):
    return (x * lax.rsqrt(jnp.mean(x * x, -1, keepdims=True) + RMS_EPS) * g,)


@jax.custom_vjp
def _rot_half(x):
    lane = lax.broadcasted_iota(jnp.int32, x.shape, x.ndim - 1)
    up = pltpu.roll(x, LANES - QK_ROPE // 2, x.ndim - 1)
    dn = pltpu.roll(x, QK_ROPE // 2, x.ndim - 1)
    return jnp.where(lane < QK_ROPE // 2, -up, jnp.where(lane < QK_ROPE, dn, 0.0))


def _rot_half_fwd(x):
    return _rot_half(x), None


def _rot_half_bwd(_, ct):
    return (-_rot_half(ct),)


_rot_half.defvjp(_rot_half_fwd, _rot_half_bwd)


def _rope_blk(x, cos, sin):
    return x * cos + _rot_half(x) * sin


def _mla_prep_fn(qraw, knope, kr, cos, sin):
    qn, qr = qraw[:, :QK_NOPE], qraw[:, QK_NOPE:]
    q = jnp.concatenate([qn, _rope_blk(qr, cos, sin)], axis=1) * MLA_Q_SCALE
    k = jnp.concatenate([knope.astype(F32), _rope_blk(kr, cos, sin)], axis=1)
    return q, k


def _l2n(x):
    return x * lax.rsqrt(jnp.sum(x * x, -1, keepdims=True) + 1e-6)


def _gdn_qk_fn(qc, kc):
    return _l2n(qc) * (HEAD_DIM ** -0.5), _l2n(kc)


def _softplus(x):
    return jnp.maximum(x, 0.0) + jnp.log(1.0 + jnp.exp(-jnp.abs(x)))


def _sigmoid(x):
    return 1.0 / (1.0 + jnp.exp(-x))


def _silu(x):
    return x * _sigmoid(x)


def _gdn_gate_fn(n_heads, head_axis):
    def fn(ab, a_log, dt_bias):
        h = pl.program_id(head_axis)
        lane = lax.broadcasted_iota(jnp.int32, ab.shape, 1)
        a_in = jnp.sum(jnp.where(lane == h, ab, 0.0), -1, keepdims=True)
        b_in = jnp.sum(jnp.where(lane == h + n_heads, ab, 0.0), -1, keepdims=True)
        g = -jnp.exp(a_log[:, :CHUNK]) * _softplus(a_in + dt_bias[:, :CHUNK])
        beta = _sigmoid(b_in) + jnp.zeros_like(g)
        return g, beta
    return fn


def _gdn_out_fn(o, z, w):
    return (o * lax.rsqrt(jnp.mean(o * o, -1, keepdims=True) + RMS_EPS) * w * _silu(z),)


def _loss_fn(y, t):
    d = y - t
    return (jnp.sum(d * d, axis=0, keepdims=True) * (0.5 / y.shape[-1]), d * (1.0 / y.shape[-1]))


def _adamw_fn(w, g, m, v):
    m = ADAM_B1 * m + (1.0 - ADAM_B1) * g
    v = ADAM_B2 * v + (1.0 - ADAM_B2) * (g * g)
    m_hat = m / (1.0 - ADAM_B1 ** ADAM_STEP)
    v_hat = v / (1.0 - ADAM_B2 ** ADAM_STEP)
    delta = -ADAM_LR * (m_hat / (jnp.sqrt(v_hat) + ADAM_EPS) + ADAM_WD * w)
    return delta, m, v


def _causal_mask(shape, row0, col0):
    row = lax.broadcasted_iota(jnp.int32, shape, 0) + row0
    col = lax.broadcasted_iota(jnp.int32, shape, 1) + col0
    return col <= row


def _rows(ref, i, t):
    return ref[pl.ds(pl.multiple_of(i * t, t), t), :]


def _walk(first, n_loop, tail, products, update):
    stop = first + n_loop
    t0 = tail[0][0]

    def step(j, carry):
        nxt = products(jnp.where(j + 1 < stop, j + 1, t0))
        update(carry, j, False)
        return nxt

    carry = lax.fori_loop(first, stop, step, products(jnp.where(n_loop > 0, first, t0)))
    for n, (j, masked) in enumerate(tail):
        nxt = products(tail[n + 1][0]) if n + 1 < len(tail) else None
        update(carry, j, masked)
        carry = nxt


def _flash_fwd(q, k, v, *, H, dq, dv, qoff, koff, voff, causal, scale, tq, tk, name, side=None):
    S, Sk = q.shape[0], k.shape[0]
    nq = S // tq
    assert (tq == tk and S == Sk) or not causal

    def body(q_ref, k_ref, v_ref, o_ref, lse_ref, m_s, l_s, acc):
        qi = pl.program_id(1)
        m_s[...] = jnp.full_like(m_s, -jnp.inf)
        l_s[...] = jnp.zeros_like(l_s)
        acc[...] = jnp.zeros_like(acc)
        qb = q_ref[...].astype(BF16)

        def products(j):
            return lax.dot_general(qb, _rows(k_ref, j, tk).astype(BF16), (((1,), (1,)), ((), ())),
                                   preferred_element_type=F32)

        ex = jnp.exp if scale is not None else jnp.exp2

        def update(s, j, masked):
            if scale is not None:
                s = s * scale
            if masked:
                s = jnp.where(_causal_mask(s.shape, qi * tq, j * tk), s, -jnp.inf)
            m_prev = m_s[...]
            m_new = jnp.maximum(m_prev, jnp.max(s, axis=1, keepdims=True))
            alpha = ex(m_prev - m_new)
            p = ex(s - m_new[:, :1])
            l_s[...] = alpha * l_s[...] + jnp.sum(p, axis=1, keepdims=True)
            acc[...] = acc[...] * alpha[:, :1] + lax.dot_general(
                p.astype(BF16), _rows(v_ref, j, tk).astype(BF16), (((1,), (0,)), ((), ())), preferred_element_type=F32)
            m_s[...] = m_new

        if causal:
            _walk(0, qi, [(qi, True)], products, update)
        else:
            _walk(0, Sk // tk - 1, [(Sk // tk - 1, False)], products, update)
        o_ref[...] = (acc[...] / l_s[...][:, :1]).astype(o_ref.dtype)
        lse_ref[...] = m_s[...] + (jnp.log(l_s[...]) if scale is not None else jnp.log2(l_s[...]))

    body, s_in, s_out, s_shape, s_sems, s_args = _with_side(body, 3, 2, (H, nq), side)
    o, lse, *side_outs = pl.pallas_call(
        body, name=name, grid=(H, nq),
        in_specs=[pl.BlockSpec((tq, dq), lambda h, qi: (qi, qoff + h)),
                  pl.BlockSpec((Sk, dq), lambda h, qi: (0, koff + h)),
                  pl.BlockSpec((Sk, dv), lambda h, qi: (0, voff + h))] + s_in,
        out_specs=[pl.BlockSpec((tq, dv), lambda h, qi: (qi, h)),
                   pl.BlockSpec((tq, LANES), lambda h, qi: (qi, h))] + s_out,
        out_shape=[_sds((S, H * dv)), _sds((S, H * LANES))] + s_shape,
        scratch_shapes=[pltpu.VMEM((tq, LANES), F32), pltpu.VMEM((tq, LANES), F32), pltpu.VMEM((tq, dv), F32)] + s_sems,
        compiler_params=_cparams(("arbitrary", "arbitrary")),
    )(q, k, v, *s_args)
    return o, lse, side_outs


def _flash_bwd(q, k, v, o, lse, do, *, H, dq, dv, qoff, koff, voff, dooff, causal, scale, tq, tk, name, side=None):
    S, Sk = q.shape[0], k.shape[0]
    nq, nk = S // tq, Sk // tk
    assert (tq == tk and S == Sk) or not causal
    nt = (((1,), (1,)), ((), ()))

    def body_q(q_ref, k_ref, v_ref, o_ref, do_ref, lse_ref, dq_ref, delta_ref, dq_acc):
        qi = pl.program_id(1)
        qb, dob = q_ref[...].astype(BF16), do_ref[...].astype(BF16)
        delta = jnp.sum(do_ref[...].astype(F32) * o_ref[...].astype(F32), axis=1, keepdims=True)
        delta_ref[...] = delta + jnp.zeros_like(delta_ref)
        lse1 = lse_ref[...][:, :1]
        dq_acc[...] = jnp.zeros_like(dq_acc)

        def products(j):
            return (lax.dot_general(qb, _rows(k_ref, j, tk).astype(BF16), nt, preferred_element_type=F32),
                    lax.dot_general(dob, _rows(v_ref, j, tk).astype(BF16), nt, preferred_element_type=F32))

        def update(sp, j, masked):
            s, dp = sp
            p = jnp.exp(s * scale - lse1)
            if masked:
                p = jnp.where(_causal_mask(s.shape, qi * tq, j * tk), p, 0.0)
            ds = p * (dp - delta) * scale
            dq_acc[...] += lax.dot_general(ds.astype(BF16), _rows(k_ref, j, tk).astype(BF16), (((1,), (0,)), ((), ())),
                                           preferred_element_type=F32)

        if causal:
            _walk(0, qi, [(qi, True)], products, update)
        else:
            _walk(0, nk - 1, [(nk - 1, False)], products, update)
        dq_ref[...] = dq_acc[...]

    body_q, s_in, s_out, s_shape, s_sems, s_args = _with_side(body_q, 6, 2, (H, nq), side)
    dqq, delta, *side_outs = pl.pallas_call(
        body_q, name=name + "_dq", grid=(H, nq),
        in_specs=[pl.BlockSpec((tq, dq), lambda h, qi: (qi, qoff + h)),
                  pl.BlockSpec((Sk, dq), lambda h, qi: (0, koff + h)),
                  pl.BlockSpec((Sk, dv), lambda h, qi: (0, voff + h)),
                  pl.BlockSpec((tq, dv), lambda h, qi: (qi, h)),
                  pl.BlockSpec((tq, dv), lambda h, qi: (qi, dooff + h)),
                  pl.BlockSpec((tq, LANES), lambda h, qi: (qi, h))] + s_in,
        out_specs=[pl.BlockSpec((tq, dq), lambda h, qi: (qi, h)),
                   pl.BlockSpec((tq, LANES), lambda h, qi: (qi, h))] + s_out,
        out_shape=[_sds((S, H * dq)), _sds((S, H * LANES))] + s_shape,
        scratch_shapes=[pltpu.VMEM((tq, dq), F32)] + s_sems,
        compiler_params=_cparams(("arbitrary", "arbitrary")),
    )(q, k, v, o, do, lse, *s_args)

    def as_rows(t):
        return t[:, ::LANES].T.reshape(H, nq, 1, tq)

    nn = (((1,), (0,)), ((), ()))

    def body_kv(q_ref, k_ref, v_ref, do_ref, lse_ref, delta_ref, dk_ref, dv_ref, dk_acc, dv_acc):
        kj = pl.program_id(1)
        kb, vb = k_ref[...].astype(BF16), v_ref[...].astype(BF16)
        dk_acc[...] = jnp.zeros_like(dk_acc)
        dv_acc[...] = jnp.zeros_like(dv_acc)

        def products(i):
            return (lax.dot_general(kb, _rows(q_ref, i, tq).astype(BF16), nt, preferred_element_type=F32),
                    lax.dot_general(vb, _rows(do_ref, i, tq).astype(BF16), nt, preferred_element_type=F32))

        def update(sp, i, masked):
            st, dpt = sp
            pt = jnp.exp(st * scale - lse_ref[i])
            if masked:
                key = lax.broadcasted_iota(jnp.int32, st.shape, 0) + kj * tk
                qry = lax.broadcasted_iota(jnp.int32, st.shape, 1) + i * tq
                pt = jnp.where(key <= qry, pt, 0.0)
            dst = pt * (dpt - delta_ref[i]) * scale
            dv_acc[...] += lax.dot_general(pt.astype(BF16), _rows(do_ref, i, tq).astype(BF16), nn,
                                           preferred_element_type=F32)
            dk_acc[...] += lax.dot_general(dst.astype(BF16), _rows(q_ref, i, tq).astype(BF16), nn,
                                           preferred_element_type=F32)

        if causal:
            _walk(kj + 1, nq - 1 - kj, [(kj, True)], products, update)
        else:
            _walk(0, nq - 1, [(nq - 1, False)], products, update)
        dk_ref[...] = dk_acc[...]
        dv_ref[...] = dv_acc[...]

    row_spec = pl.BlockSpec((None, nq, 1, tq), lambda h, kj: (h, 0, 0, 0))
    dk, dvv = pl.pallas_call(
        body_kv, name=name + "_dkv", grid=(H, nk),
        in_specs=[pl.BlockSpec((S, dq), lambda h, kj: (0, qoff + h)),
                  pl.BlockSpec((tk, dq), lambda h, kj: (kj, koff + h)),
                  pl.BlockSpec((tk, dv), lambda h, kj: (kj, voff + h)),
                  pl.BlockSpec((S, dv), lambda h, kj: (0, dooff + h)), row_spec, row_spec],
        out_specs=[pl.BlockSpec((tk, dq), lambda h, kj: (kj, h)), pl.BlockSpec((tk, dv), lambda h, kj: (kj, h))],
        out_shape=[_sds((Sk, H * dq)), _sds((Sk, H * dv))],
        scratch_shapes=[pltpu.VMEM((tk, dq), F32), pltpu.VMEM((tk, dv), F32)],
        compiler_params=_cparams(("parallel", "arbitrary")),
    )(q, k, v, do, as_rows(lse), as_rows(delta))
    return dqq, dk, dvv, side_outs


def _flash_bwd_causal(q, k, v, o, lse, do, *, H, dq, dv, qoff, koff, voff, dooff, tq, name, side=None):
    S = q.shape[0]
    nq = S // tq
    nt = (((1,), (1,)), ((), ()))
    tn = (((0,), (0,)), ((), ()))
    nn = (((1,), (0,)), ((), ()))

    def body(q_ref, k_ref, v_ref, o_ref, do_ref, lse_ref, dq_ref, dk_ref, dv_ref, dq_acc):
        qi = pl.program_id(1)

        @pl.when(qi == 0)
        def _():
            dk_ref[...] = jnp.zeros_like(dk_ref)
            dv_ref[...] = jnp.zeros_like(dv_ref)

        qb, dob = q_ref[...].astype(BF16), do_ref[...].astype(BF16)
        delta = jnp.sum(do_ref[...].astype(F32) * o_ref[...].astype(F32), axis=1, keepdims=True)
        lse1 = lse_ref[...][:, :1]
        dq_acc[...] = jnp.zeros_like(dq_acc)

        def products(j):
            return (lax.dot_general(qb, _rows(k_ref, j, tq).astype(BF16), nt, preferred_element_type=F32),
                    lax.dot_general(dob, _rows(v_ref, j, tq).astype(BF16), nt, preferred_element_type=F32))

        def update(sp, j, masked):
            s, dp = sp
            p = jnp.exp2(s - lse1)
            if masked:
                p = jnp.where(_causal_mask(s.shape, qi * tq, j * tq), p, 0.0)
            ds = (p * (dp - delta)).astype(BF16)
            dq_acc[...] += lax.dot_general(ds, _rows(k_ref, j, tq).astype(BF16), nn, preferred_element_type=F32)
            rows = pl.ds(pl.multiple_of(j * tq, tq), tq)
            dk_ref[rows, :] += lax.dot_general(ds, qb, tn, preferred_element_type=F32)
            dv_ref[rows, :] += lax.dot_general(p.astype(BF16), dob, tn, preferred_element_type=F32)

        _walk(0, qi, [(qi, True)], products, update)
        dq_ref[...] = dq_acc[...] * LN2

        @pl.when(qi == nq - 1)
        def _():
            dk_ref[...] = dk_ref[...] * LN2

    body, s_in, s_out, s_shape, s_sems, s_args = _with_side(body, 6, 3, (H, nq), side)
    dqq, dk, dvv, *side_outs = pl.pallas_call(
        body, name=name, grid=(H, nq),
        in_specs=[pl.BlockSpec((tq, dq), lambda h, qi: (qi, qoff + h)),
                  pl.BlockSpec((S, dq), lambda h, qi: (0, koff + h)),
                  pl.BlockSpec((S, dv), lambda h, qi: (0, voff + h)),
                  pl.BlockSpec((tq, dv), lambda h, qi: (qi, h)),
                  pl.BlockSpec((tq, dv), lambda h, qi: (qi, dooff + h)),
                  pl.BlockSpec((tq, LANES), lambda h, qi: (qi, h))] + s_in,
        out_specs=[pl.BlockSpec((tq, dq), lambda h, qi: (qi, h)),
                   pl.BlockSpec((S, dq), lambda h, qi: (0, h)), pl.BlockSpec((S, dv), lambda h, qi: (0, h))] + s_out,
        out_shape=[_sds((S, H * dq)), _sds((S, H * dq)), _sds((S, H * dv))] + s_shape,
        scratch_shapes=[pltpu.VMEM((tq, dq), F32)] + s_sems,
        compiler_params=_cparams(("arbitrary", "arbitrary")),
    )(q, k, v, o, do, lse, *s_args)
    return dqq, dk, dvv, side_outs


def _shift_down(x, prev8, j):
    if j == 0:
        return x
    y = pltpu.roll(x, j, 0)
    head = pltpu.roll(prev8, j, 0)
    row = lax.broadcasted_iota(jnp.int32, x.shape, 0)
    reps = x.shape[0] // SUBLANES
    return jnp.where(row < j, jnp.tile(head, (reps, 1)), y)


def _shift_up(x, next8, j):
    if j == 0:
        return x
    n = x.shape[0]
    y = pltpu.roll(x, n - j, 0)
    tail = pltpu.roll(next8, SUBLANES - j, 0)
    row = lax.broadcasted_iota(jnp.int32, x.shape, 0)
    reps = n // SUBLANES
    return jnp.where(row >= n - j, jnp.tile(tail, (reps, 1)), y)


def _conv_pre(x_ref, p_ref, w_ref, first):
    x = x_ref[...]
    prev8 = jnp.where(first, 0.0, p_ref[...])
    w = w_ref[...]
    xs = [_shift_down(x, prev8, CONV_WIDTH - 1 - j) for j in range(CONV_WIDTH)]
    c = sum(xs[j] * w[j:j + 1, :] for j in range(CONV_WIDTH))
    return c, xs


def _conv_specs(ts, tc, C_total_blocks_off):
    rb = ts // SUBLANES
    off = C_total_blocks_off
    x_spec = pl.BlockSpec((ts, tc), lambda ci, i: (i, off + ci))
    p_spec = pl.BlockSpec((SUBLANES, tc), lambda ci, i: (jnp.maximum(i * rb - 1, 0), off + ci))
    return x_spec, p_spec


def _conv_fwd(h, w, *, C, ts, tc, name):
    S = h.shape[0]
    x_spec, p_spec = _conv_specs(ts, tc, 0)

    def body(x_ref, p_ref, w_ref, y_ref):
        c, _ = _conv_pre(x_ref, p_ref, w_ref, pl.program_id(1) == 0)
        y_ref[...] = _silu(c)

    return pl.pallas_call(
        body, name=name, grid=(C // tc, S // ts),
        in_specs=[x_spec, p_spec, pl.BlockSpec((CONV_WIDTH, tc), lambda ci, i: (0, ci))],
        out_specs=pl.BlockSpec((ts, tc), lambda ci, i: (i, ci)),
        out_shape=_sds((S, C)),
        compiler_params=_cparams(("parallel", "arbitrary")),
    )(h, h, w)


def _conv_bwd(h, w, dy, *, C, ts, tc, name):
    S = h.shape[0]
    ns = S // ts
    rb = ts // SUBLANES
    x_spec, p_spec = _conv_specs(ts, tc, 0)

    def body_a(x_ref, p_ref, w_ref, dy_ref, dc_ref, dw_ref):
        i = pl.program_id(1)
        c, xs = _conv_pre(x_ref, p_ref, w_ref, i == 0)
        sg = _sigmoid(c)
        dc = dy_ref[...] * (sg * (1.0 + c * (1.0 - sg)))
        dc_ref[...] = dc

        @pl.when(i == 0)
        def _():
            dw_ref[...] = jnp.zeros_like(dw_ref)

        dw_ref[...] += jnp.concatenate([jnp.sum(dc * xs[j], axis=0, keepdims=True) for j in range(CONV_WIDTH)], axis=0)

    dc, dw = pl.pallas_call(
        body_a, name=name + "_a", grid=(C // tc, ns),
        in_specs=[x_spec, p_spec, pl.BlockSpec((CONV_WIDTH, tc), lambda ci, i: (0, ci)),
                  pl.BlockSpec((ts, tc), lambda ci, i: (i, ci))],
        out_specs=[pl.BlockSpec((ts, tc), lambda ci, i: (i, ci)),
                   pl.BlockSpec((CONV_WIDTH, tc), lambda ci, i: (0, ci))],
        out_shape=[_sds((S, C)), _sds((CONV_WIDTH, C))],
        compiler_params=_cparams(("parallel", "arbitrary")),
    )(h, h, w, dy)

    def body_b(dc_ref, n_ref, w_ref, dx_ref):
        i = pl.program_id(1)
        dcv = dc_ref[...]
        next8 = jnp.where(i == ns - 1, 0.0, n_ref[...])
        w_ = w_ref[...]
        dx_ref[...] = sum(_shift_up(dcv, next8, CONV_WIDTH - 1 - j) * w_[j:j + 1, :] for j in range(CONV_WIDTH))

    dx = pl.pallas_call(
        body_b, name=name + "_b", grid=(C // tc, ns),
        in_specs=[pl.BlockSpec((ts, tc), lambda ci, i: (i, ci)),
                  pl.BlockSpec((SUBLANES, tc), lambda ci, i: (jnp.minimum((i + 1) * rb, ns * rb - 1), ci)),
                  pl.BlockSpec((CONV_WIDTH, tc), lambda ci, i: (0, ci))],
        out_specs=pl.BlockSpec((ts, tc), lambda ci, i: (i, ci)),
        out_shape=_sds((S, C)),
        compiler_params=_cparams(("parallel", "arbitrary")),
    )(dc, dc, w)
    return dx, dw


def _bdot(a, b, ca, cb, precision=None):
    nb = a.ndim - 2
    batch = tuple(range(nb))
    return lax.dot_general(a, b, (((nb + ca,), (nb + cb,)), (batch, batch)), precision=precision,
                           preferred_element_type=F32)


@jax.custom_vjp
def _nn(a, b):
    return _bdot(a.astype(BF16), b.astype(BF16), 1, 0)


@jax.custom_vjp
def _nt(a, b):
    return _bdot(a.astype(BF16), b.astype(BF16), 1, 1)


@jax.custom_vjp
def _tn(a, b):
    return _bdot(a.astype(BF16), b.astype(BF16), 0, 0)


_nn.defvjp(lambda a, b: (_nn(a, b), (a, b)), lambda r, g: (_nt(g, r[1]), _tn(r[0], g)))
_nt.defvjp(lambda a, b: (_nt(a, b), (a, b)), lambda r, g: (_nn(g, r[1]), _tn(g, r[0])))
_tn.defvjp(lambda a, b: (_tn(a, b), (a, b)), lambda r, g: (_nt(r[1], g), _nn(r[0], g)))


def _dot3(a, b, ca, cb):
    ah, bh = a.astype(BF16), b.astype(BF16)
    al, bl = (a - ah.astype(F32)).astype(BF16), (b - bh.astype(F32)).astype(BF16)
    return _bdot(ah, bh, ca, cb) + (_bdot(ah, bl, ca, cb) + _bdot(al, bh, ca, cb))


@jax.custom_vjp
def _nn_x3(a, b):
    return _dot3(a, b, 1, 0)


@jax.custom_vjp
def _nt_x3(a, b):
    return _dot3(a, b, 1, 1)


@jax.custom_vjp
def _tn_x3(a, b):
    return _dot3(a, b, 0, 0)


_nn_x3.defvjp(lambda a, b: (_nn_x3(a, b), (a, b)), lambda r, g: (_nt_x3(g, r[1]), _tn_x3(r[0], g)))
_nt_x3.defvjp(lambda a, b: (_nt_x3(a, b), (a, b)), lambda r, g: (_nn_x3(g, r[1]), _tn_x3(g, r[0])))
_tn_x3.defvjp(lambda a, b: (_tn_x3(a, b), (a, b)), lambda r, g: (_nt_x3(r[1], g), _nn_x3(r[0], g)))


@jax.custom_vjp
def _nn_hi(a, b):
    return _bdot(a, b, 1, 0, HI)


@jax.custom_vjp
def _nt_hi(a, b):
    return _bdot(a, b, 1, 1, HI)


@jax.custom_vjp
def _tn_hi(a, b):
    return _bdot(a, b, 0, 0, HI)


_nn_hi.defvjp(lambda a, b: (_nn_hi(a, b), (a, b)), lambda r, g: (_nt_hi(g, r[1]), _tn_hi(r[0], g)))
_nt_hi.defvjp(lambda a, b: (_nt_hi(a, b), (a, b)), lambda r, g: (_nn_hi(g, r[1]), _tn_hi(g, r[0])))
_tn_hi.defvjp(lambda a, b: (_tn_hi(a, b), (a, b)), lambda r, g: (_nt_hi(r[1], g), _nn_hi(r[0], g)))


@jax.custom_vjp
def _inverse_given(lmat, t):
    return t


_inverse_given.defvjp(lambda lmat, t: (t, t),
                      lambda t, ct: (-_tn_x3(t, _nt_x3(ct, t)), jnp.zeros_like(t)))


def _gdn_chunk_fn(q, k, v, g, beta, state, t_known=None):
    C = CHUNK
    B = q.shape[0]
    row = lax.broadcasted_iota(jnp.int32, (B, C, C), 1)
    col = lax.broadcasted_iota(jnp.int32, (B, C, C), 2)
    tril, strict = row >= col, row > col
    ones_tril = tril.astype(F32)
    gc = _nn_hi(ones_tril, g)
    gr = _nt_hi(jnp.full((B, C, C), 1.0 / C, F32), gc)
    decay = jnp.where(tril, jnp.exp(jnp.where(tril, gc - gr, 0.0)), 0.0)
    b1 = beta[:, :, :1]
    e_gc = jnp.exp(gc[:, :, :1])
    kb = k * b1
    lmat = jnp.where(strict, _nt(kb, k) * decay, 0.0)
    if t_known is None:
        a = -lmat
        t = jnp.where(row == col, 1.0, 0.0) + a
        p = a
        for _ in range(5):
            p = _nn_x3(p, p)
            t = t + _nn_x3(t, p)
    else:
        t = _inverse_given(lmat, t_known)
    rhs = jnp.concatenate([v * b1, kb * e_gc], axis=2)
    sol = _nn_x3(t, rhs)
    u, w = sol[:, :, :HEAD_DIM], sol[:, :, HEAD_DIM:]
    a_qk = jnp.where(tril, _nt(q, k) * decay, 0.0)
    gl = gc[:, C - 1:C, :1]
    q_dec = q * e_gc
    k_dec = k * jnp.exp(gl - gc[:, :, :1])
    v_new = u - _nn(w, state)
    o = _nn(q_dec, state) + _nn(a_qk, v_new)
    new_state = state * jnp.exp(gl) + _tn(k_dec, v_new)
    return o, new_state, t


def _split_heads(x, B):
    return jnp.stack([x[:, j * HEAD_DIM:(j + 1) * HEAD_DIM] for j in range(B)], axis=0)


def _merge_heads(x):
    return jnp.concatenate([x[j] for j in range(x.shape[0])], axis=1)


def _gdn_group(H, voff):
    return next(b for b in (12, 6, 4, 3, 2, 1) if H % b == 0 and voff % b == 0)


def _gdn_fwd(q, k, v, g, beta, *, H, voff, name, side=None):
    S = q.shape[0]
    N = S // CHUNK
    B = _gdn_group(H, voff)
    W = B * HEAD_DIM
    qs = lambda off: pl.BlockSpec((CHUNK, W), lambda h, n: (n, off // B + h))
    gs = pl.BlockSpec((B, CHUNK, CHUNK), lambda h, n: (h, n, 0))

    def body(q_ref, k_ref, v_ref, g_ref, b_ref, o_ref, st_ref, ti_ref, state):
        @pl.when(pl.program_id(1) == 0)
        def _():
            state[...] = jnp.zeros_like(state)

        s0 = state[...]
        st_ref[...] = s0
        o, s1, t = _gdn_chunk_fn(_split_heads(q_ref[...], B), _split_heads(k_ref[...], B),
                                 _split_heads(v_ref[...], B), g_ref[...], b_ref[...], s0)
        o_ref[...] = _merge_heads(o)
        ti_ref[...] = t
        state[...] = s1

    per_chunk = lambda d: pl.BlockSpec((B, None, d, d), lambda h, n: (h, n, 0, 0))
    body, s_in, s_out, s_shape, s_sems, s_args = _with_side(body, 5, 3, (H // B, N), side)
    o, states, tinv, *side_outs = pl.pallas_call(
        body, name=name, grid=(H // B, N),
        in_specs=[qs(0), qs(0), qs(voff), gs, gs] + s_in,
        out_specs=[qs(0), per_chunk(HEAD_DIM), per_chunk(CHUNK)] + s_out,
        out_shape=[_sds((S, H * HEAD_DIM)), _sds((H, N, HEAD_DIM, HEAD_DIM)), _sds((H, N, CHUNK, CHUNK))] + s_shape,
        scratch_shapes=[pltpu.VMEM((B, HEAD_DIM, HEAD_DIM), F32)] + s_sems,
        compiler_params=_cparams(("arbitrary", "arbitrary")),
    )(q, k, v, g, beta, *s_args)
    return o, states, tinv, side_outs


def _gdn_bwd(q, k, v, g, beta, states, tinv, do, *, H, voff, name, side=None):
    S = q.shape[0]
    N = S // CHUNK
    B = _gdn_group(H, voff)
    W = B * HEAD_DIM
    rs = lambda off: pl.BlockSpec((CHUNK, W), lambda h, n: (N - 1 - n, off // B + h))
    gs = pl.BlockSpec((B, CHUNK, CHUNK), lambda h, n: (h, N - 1 - n, 0))

    def body(q_ref, k_ref, v_ref, g_ref, b_ref, st_ref, ti_ref, do_ref, dq_ref, dk_ref, dv_ref, dg_ref, db_ref, dstate):
        @pl.when(pl.program_id(1) == 0)
        def _():
            dstate[...] = jnp.zeros_like(dstate)

        t_known = ti_ref[...]
        _, pull = jax.vjp(lambda *a: _gdn_chunk_fn(*a, t_known)[:2], _split_heads(q_ref[...], B),
                          _split_heads(k_ref[...], B), _split_heads(v_ref[...], B), g_ref[...], b_ref[...], st_ref[...])
        dq, dk, dv, dg, db, ds = pull((_split_heads(do_ref[...], B), dstate[...]))
        dq_ref[...] = _merge_heads(dq)
        dk_ref[...] = _merge_heads(dk)
        dv_ref[...] = _merge_heads(dv)
        dg_ref[...] = dg
        db_ref[...] = db
        dstate[...] = ds

    per_chunk = lambda d: pl.BlockSpec((B, None, d, d), lambda h, n: (h, N - 1 - n, 0, 0))
    body, s_in, s_out, s_shape, s_sems, s_args = _with_side(body, 8, 5, (H // B, N), side)
    dq, dk, dv, dg, db, *side_outs = pl.pallas_call(
        body, name=name, grid=(H // B, N),
        in_specs=[rs(0), rs(0), rs(voff), gs, gs, per_chunk(HEAD_DIM), per_chunk(CHUNK), rs(0)] + s_in,
        out_specs=[rs(0), rs(0), rs(0), gs, gs] + s_out,
        out_shape=[_sds((S, H * HEAD_DIM))] * 3 + [_sds((H, S, CHUNK))] * 2 + s_shape,
        scratch_shapes=[pltpu.VMEM((B, HEAD_DIM, HEAD_DIM), F32)] + s_sems,
        compiler_params=_cparams(("arbitrary", "arbitrary")),
    )(q, k, v, g, beta, states, tinv, do, *s_args)
    return dq, dk, dv, dg, db, side_outs


def _round_up(n, m):
    return (n + m - 1) // m * m


def _dims(S, D, M, shapes):
    c = dict(S=S, D=D, M=M)
    c["H"] = shapes["gdn_a_log"][-1]
    c["QL"] = shapes["mla_q_norm"][-1]
    c["KVL"] = shapes["mla_kv_norm"][-1]
    assert c["QL"] == c["KVL"]
    c["MEMW"] = shapes["mem_w_kv"][-1] // 2
    c["HM"] = c["MEMW"] // HEAD_DIM
    c["MW"] = c["H"] * HEAD_DIM
    c["F"] = shapes["mlp_w1"][-1]
    c["DEPTH"] = shapes["ln1_g"][0]
    c["ALPHA"] = (2 * c["DEPTH"]) ** 0.25
    c["MLA_IN"] = _round_up(c["QL"] + c["KVL"] + c["MEMW"] + LANES, 2 * LANES)
    c["GDN_IN"] = _round_up(4 * c["MW"] + c["MEMW"] + LANES, 2 * LANES)
    c["t_row"] = min(256, S)
    c["t_head"] = min(2048, S)
    c["t_conv"] = min(512, S)
    c["t_att"] = min(512, S)
    c["t_mem"] = min(2048, S)
    return c


def _pad_cols(w, n):
    return jnp.pad(w, ((0, 0), (0, n - w.shape[1])))


def _prep_mla_w_in(w, c):
    a = c["QL"] + c["KVL"]
    w = jnp.concatenate([w[:, :a], w[:, a + QK_ROPE:a + QK_ROPE + c["MEMW"]], w[:, a:a + QK_ROPE]], axis=1)
    return _pad_cols(w, c["MLA_IN"]).astype(BF16)


def _unprep_mla_w_in(dw, c):
    a, m = c["QL"] + c["KVL"], c["MEMW"]
    return jnp.concatenate([dw[:, :a], dw[:, a + m:a + m + QK_ROPE], dw[:, a:a + m]], axis=1)


def _prep_w_uq(w, c):
    w = w.reshape(c["QL"], c["H"], QK_NOPE + QK_ROPE)
    w = jnp.pad(w, ((0, 0), (0, 0), (0, QK_PAD - QK_NOPE - QK_ROPE)))
    return w.reshape(c["QL"], c["H"] * QK_PAD).astype(BF16)


def _unprep_w_uq(dw, c):
    return dw.reshape(c["QL"], c["H"], QK_PAD)[:, :, :QK_NOPE + QK_ROPE].reshape(c["QL"], c["H"] * (QK_NOPE + QK_ROPE))


def _prep_w_ukv(w, c):
    return w.reshape(c["KVL"], c["H"], 2, HEAD_DIM).transpose(0, 2, 1, 3).reshape(c["KVL"], 2 * c["MW"]).astype(BF16)


def _unprep_w_ukv(dw, c):
    return dw.reshape(c["KVL"], 2, c["H"], HEAD_DIM).transpose(0, 2, 1, 3).reshape(c["KVL"], 2 * c["MW"])


def _prep_gdn_w_in(w, c):
    a, h2 = 4 * c["MW"], 2 * c["H"]
    w = jnp.concatenate([w[:, :a], w[:, a + h2:], w[:, a:a + h2]], axis=1)
    return _pad_cols(w, c["GDN_IN"]).astype(BF16)


def _unprep_gdn_w_in(dw, c):
    a, h2, m = 4 * c["MW"], 2 * c["H"], c["MEMW"]
    return jnp.concatenate([dw[:, :a], dw[:, a + m:a + m + h2], dw[:, a:a + m]], axis=1)


def _lane_bcast(v):
    return jnp.broadcast_to(v.astype(F32)[:, None, None], (v.shape[0], 1, LANES))


def _row(i):
    return (i, 0)


def _par(i):
    return (0, 0)


def _layer_norm(z, g, b, c, name):
    S, D, ts = c["S"], c["D"], c["t_row"]
    return _ew(_ln_fn, [(z, (ts, D), _row), (g, (1, D), _par), (b, (1, D), _par)],
               [(_sds((S, D)), (ts, D), _row), (_sds((S, D), BF16), (ts, D), _row)], (S // ts,), name=name)


def _layer_norm_bwd(z, g, b, dy, c, name):
    S, D, ts = c["S"], c["D"], c["t_row"]
    fn = lambda z, g, b: _ln_fn(z, g, b)[:1]
    dz, dg, db, dzb = _ew_vjp(fn, [(z, (ts, D), _row), (g, (1, D), _par), (b, (1, D), _par)], [(dy, (ts, D), _row)],
                              [(_sds((S, D)), (ts, D), _row, "set"), (_sds((1, D)), (1, D), _par, "acc_all"),
                               (_sds((1, D)), (1, D), _par, "acc_all")], (S // ts,), name=name,
                              copies=[(0, _sds((S, D), BF16))])
    return dz, dzb, dg, db


def _mem_attn_fwd(h, qoff, memkv, c, name):
    return _flash_fwd(h, memkv, memkv, H=c["HM"], dq=HEAD_DIM, dv=HEAD_DIM, qoff=qoff, koff=0, voff=c["HM"],
                      causal=False, scale=HEAD_DIM ** -0.5, tq=c["t_mem"], tk=c["M"], name=name)[:2]


def _mem_attn_bwd(h, qoff, memkv, om, lsem, dcat, c, name):
    return _flash_bwd(h, memkv, memkv, om, lsem, dcat, H=c["HM"], dq=HEAD_DIM, dv=HEAD_DIM, qoff=qoff, koff=0,
                      voff=c["HM"], dooff=c["H"], causal=False, scale=HEAD_DIM ** -0.5, tq=c["t_mem"], tk=c["M"],
                      name=name)[:3]


def _mla_specs(c):
    H, ts = c["H"], c["t_head"]
    kr_blk = (c["QL"] + c["KVL"] + c["MEMW"]) // LANES
    hd = lambda i, h: (i, h)
    return [((ts, QK_PAD), hd), ((ts, HEAD_DIM), hd), ((ts, LANES), lambda i, h: (i, kr_blk)),
            ((ts, LANES), lambda i, h: (i, 0)), ((ts, LANES), lambda i, h: (i, 0))]


def _mla_fwd(xb, p, cosp, sinp, c, side=None):
    S, H, QL, ts, tr = c["S"], c["H"], c["QL"], c["t_head"], c["t_row"]
    h, = _mm(xb, p["w_in"], "nn", [F32], name="mla_in")
    nq, = _ew(_rms_fn, [(h, (tr, QL), lambda i: (i, 0)), (p["q_norm"], (1, QL), _par)],
              [(_sds((S, QL), BF16), (tr, QL), _row)], (S // tr,), name="mla_qnorm")
    nkv, = _ew(_rms_fn, [(h, (tr, QL), lambda i: (i, 1)), (p["kv_norm"], (1, QL), _par)],
               [(_sds((S, QL), BF16), (tr, QL), _row)], (S // tr,), name="mla_kvnorm")
    qraw, = _mm(nq, p["w_uq"], "nn", [F32], name="mla_uq")
    kvraw, = _mm(nkv, p["w_ukv"], "nn", [BF16], name="mla_ukv")
    sp = _mla_specs(c)
    ins = [(a, b, m) for a, (b, m) in zip([qraw, kvraw, h, cosp, sinp], sp)]
    qp, kp = _ew(_mla_prep_fn, ins, [(_sds((S, H * QK_PAD), BF16), (ts, QK_PAD), lambda i, h: (i, h))] * 2,
                 (S // ts, H), name="mla_rope")
    o, lse, got = _flash_fwd(qp, kp, kvraw, H=H, dq=QK_PAD, dv=HEAD_DIM, qoff=0, koff=0, voff=H, causal=True,
                             scale=None, tq=c["t_att"], tk=c["t_att"], name="mla_attn", side=side)
    return o, dict(h=h, nq=nq, nkv=nkv, qraw=qraw, kvraw=kvraw, qp=qp, kp=kp, o=o, lse=lse), got


def _mla_bwd(sv, p, cosp, sinp, dcat, dqm, c, side=None):
    S, H, QL, ts, tr = c["S"], c["H"], c["QL"], c["t_head"], c["t_row"]
    dqp, dkp, dv, got = _flash_bwd_causal(sv["qp"], sv["kp"], sv["kvraw"], sv["o"], sv["lse"], dcat, H=H, dq=QK_PAD,
                                          dv=HEAD_DIM, qoff=0, koff=0, voff=H, dooff=0,
                                          tq=c["t_att"], name="mla_attn_bwd",
                                          side=side)
    sp = _mla_specs(c)
    ins = [(a, b, m) for a, (b, m) in zip([sv["qraw"], sv["kvraw"], sv["h"], cosp, sinp], sp)]
    hd = lambda i, h: (i, h)
    dqraw, dknope, dkr = _ew_vjp(
        _mla_prep_fn, ins, [(dqp, (ts, QK_PAD), hd), (dkp, (ts, QK_PAD), hd)],
        [(_sds((S, H * QK_PAD), BF16), (ts, QK_PAD), hd, "set"), (_sds((S, H * HEAD_DIM), BF16), (ts, HEAD_DIM), hd, "set"),
         (_sds((S, LANES)), (ts, LANES), lambda i, h: (i, 0), "acc"), None, None], (S // ts, H), name="mla_rope_bwd")
    dkvraw = jnp.concatenate([dknope, dv.astype(BF16)], axis=1)
    dnq, = _mm(dqraw, p["w_uq"], "nt", [F32], name="mla_uq_dx")
    dw_uq, = _mm(sv["nq"], dqraw, "tn", [F32], name="mla_uq_dw")
    dnkv, = _mm(dkvraw, p["w_ukv"], "nt", [F32], name="mla_ukv_dx")
    dw_ukv, = _mm(sv["nkv"], dkvraw, "tn", [F32], name="mla_ukv_dw")
    dcq, dgq = _ew_vjp(_rms_fn, [(sv["h"], (tr, QL), lambda i: (i, 0)), (p["q_norm"], (1, QL), _par)],
                       [(dnq, (tr, QL), _row)],
                       [(_sds((S, QL), BF16), (tr, QL), _row, "set"), (_sds((1, QL)), (1, QL), _par, "acc_all")],
                       (S // tr,), name="mla_qnorm_bwd")
    dckv, dgkv = _ew_vjp(_rms_fn, [(sv["h"], (tr, QL), lambda i: (i, 1)), (p["kv_norm"], (1, QL), _par)],
                         [(dnkv, (tr, QL), _row)],
                         [(_sds((S, QL), BF16), (tr, QL), _row, "set"), (_sds((1, QL)), (1, QL), _par, "acc_all")],
                         (S // tr,), name="mla_kvnorm_bwd")
    pad = c["MLA_IN"] - (2 * QL + c["MEMW"] + LANES)
    dh = jnp.concatenate([dcq, dckv, dqm.astype(BF16), dkr.astype(BF16)] + ([jnp.zeros((S, pad), BF16)] if pad else []),
                         axis=1)
    grads = dict(mla_q_norm=dgq[0], mla_kv_norm=dgkv[0], mla_w_uq=_unprep_w_uq(dw_uq, c),
                 mla_w_ukv=_unprep_w_ukv(dw_ukv, c))
    return dh, grads, got


def _gdn_ins(h, qkvc, p, c):
    H, ts = c["H"], c["t_head"]
    ab_blk = (4 * c["MW"] + c["MEMW"]) // LANES
    qk_ins = [(qkvc, (ts, HEAD_DIM), lambda i, h: (i, h)), (qkvc, (ts, HEAD_DIM), lambda i, h: (i, H + h))]
    gate_ins = [(h, (ts, LANES), lambda i, h: (i, ab_blk)), (p["a_log"], (None, 1, LANES), lambda i, h: (h, 0, 0)),
                (p["dt_bias"], (None, 1, LANES), lambda i, h: (h, 0, 0))]
    return qk_ins, gate_ins


def _gdn_out_ins(o, h, p, c):
    H, ts = c["H"], c["t_head"]
    return [(o, (ts, HEAD_DIM), lambda i, h: (i, h)), (h, (ts, HEAD_DIM), lambda i, h: (i, 3 * H + h)),
            (p["o_norm"], (1, HEAD_DIM), lambda i, h: (0, 0))]


def _gdn_layer_fwd(xb, p, c, side=None):
    S, H, MW, ts = c["S"], c["H"], c["MW"], c["t_head"]
    h, = _mm(xb, p["w_in"], "nn", [F32], name="gdn_in")
    tc = _tile(3 * MW, 512, LANES)
    qkvc = _conv_fwd(h, p["conv"], C=3 * MW, ts=c["t_conv"], tc=tc, name="gdn_conv")
    qk_ins, gate_ins = _gdn_ins(h, qkvc, p, c)
    hd = lambda i, h: (i, h)
    qn, kn = _ew(_gdn_qk_fn, qk_ins, [(_sds((S, MW)), (ts, HEAD_DIM), hd)] * 2, (S // ts, H), name="gdn_qknorm")
    g3 = lambda i, h: (h, i, 0)
    g, beta = _ew(_gdn_gate_fn(H, 1), gate_ins, [(_sds((H, S, CHUNK)), (None, ts, CHUNK), g3)] * 2, (S // ts, H),
                  name="gdn_gate")
    o, states, tinv, got = _gdn_fwd(qn, kn, qkvc, g, beta, H=H, voff=2 * H, name="gdn_delta", side=side)
    mix, = _ew(_gdn_out_fn, _gdn_out_ins(o, h, p, c), [(_sds((S, MW)), (ts, HEAD_DIM), hd)], (S // ts, H),
               name="gdn_outnorm")
    return mix, dict(h=h, qkvc=qkvc, qn=qn, kn=kn, g=g, beta=beta, o=o, states=states, tinv=tinv), got


def _gdn_layer_bwd(sv, p, dcat, dqm, c, side=None):
    S, H, MW, ts = c["S"], c["H"], c["MW"], c["t_head"]
    hd = lambda i, h: (i, h)
    g3 = lambda i, h: (h, i, 0)
    h, qkvc = sv["h"], sv["qkvc"]
    do, dz, d_onorm = _ew_vjp(_gdn_out_fn, _gdn_out_ins(sv["o"], h, p, c), [(dcat, (ts, HEAD_DIM), hd)],
                              [(_sds((S, MW)), (ts, HEAD_DIM), hd, "set"), (_sds((S, MW), BF16), (ts, HEAD_DIM), hd, "set"),
                               (_sds((1, HEAD_DIM)), (1, HEAD_DIM), lambda i, h: (0, 0), "acc_all")],
                              (S // ts, H), name="gdn_outnorm_bwd")
    dqn, dkn, dv, dg, db, got = _gdn_bwd(sv["qn"], sv["kn"], qkvc, sv["g"], sv["beta"], sv["states"], sv["tinv"], do,
                                         H=H, voff=2 * H, name="gdn_delta_bwd", side=side)
    qk_ins, gate_ins = _gdn_ins(h, qkvc, p, c)
    dqc, dkc = _ew_vjp(_gdn_qk_fn, qk_ins, [(dqn, (ts, HEAD_DIM), hd), (dkn, (ts, HEAD_DIM), hd)],
                       [(_sds((S, MW)), (ts, HEAD_DIM), hd, "set")] * 2, (S // ts, H), name="gdn_qknorm_bwd")
    full3 = lambda i, h: (0, 0, 0)
    dab, dalog, ddt = _ew_vjp(
        _gdn_gate_fn(H, 1), gate_ins, [(dg, (None, ts, CHUNK), g3), (db, (None, ts, CHUNK), g3)],
        [(_sds((S, LANES), BF16), (ts, LANES), lambda i, h: (i, 0), "acc"),
         (_sds((H, 1, LANES)), (H, 1, LANES), full3, ("acc_at", 1)),
         (_sds((H, 1, LANES)), (H, 1, LANES), full3, ("acc_at", 1))], (S // ts, H), name="gdn_gate_bwd")
    dqkvc = jnp.concatenate([dqc, dkc, dv], axis=1)
    tc = _tile(3 * MW, 512, LANES)
    dxc, dconv = _conv_bwd(h, p["conv"], dqkvc, C=3 * MW, ts=c["t_conv"], tc=tc, name="gdn_conv_bwd")
    pad = c["GDN_IN"] - (4 * MW + c["MEMW"] + LANES)
    dh = jnp.concatenate([dxc.astype(BF16), dz, dqm.astype(BF16), dab] + ([jnp.zeros((S, pad), BF16)] if pad else []),
                         axis=1)
    grads = dict(gdn_conv=dconv, gdn_a_log=jnp.sum(dalog[:, 0, :], axis=-1), gdn_dt_bias=jnp.sum(ddt[:, 0, :], axis=-1),
                 gdn_o_norm=d_onorm[0])
    return dh, grads, got


def _prep_layer(W, i, c):
    bf = lambda a: a.astype(BF16)
    row = lambda a: a[None].astype(F32)
    how = dict(mem_w_kv=("mem_w_kv", bf), w_out=("w_out", bf), mlp_w1=("w1", bf), mlp_w2=("w2", bf),
               ln1_g=("ln1_g", row), ln1_b=("ln1_b", row), ln2_g=("ln2_g", row), ln2_b=("ln2_b", row),
               mla_w_in=("w_in", lambda a: _prep_mla_w_in(a, c)), mla_q_norm=("q_norm", row),
               mla_w_uq=("w_uq", lambda a: _prep_w_uq(a, c)), mla_kv_norm=("kv_norm", row),
               mla_w_ukv=("w_ukv", lambda a: _prep_w_ukv(a, c)),
               gdn_w_in=("w_in", lambda a: _prep_gdn_w_in(a, c)), gdn_conv=("conv", lambda a: a.astype(F32)),
               gdn_a_log=("a_log", _lane_bcast), gdn_dt_bias=("dt_bias", _lane_bcast), gdn_o_norm=("o_norm", row))
    return {how[k][0]: how[k][1](a) for k, a in W.items()}


def _local_step(x, mem, positions, W, loss_target, c, next_weights=None, grad_sink=None):
    S, D, H, MW, ALPHA = c["S"], c["D"], c["H"], c["MW"], c["ALPHA"]
    inv_freq = 1.0 / (ROPE_THETA ** (jnp.arange(0, QK_ROPE, 2, dtype=F32) / QK_ROPE))
    ang = positions.astype(F32)[:, None] * inv_freq
    cos, sin = jnp.cos(ang), jnp.sin(ang)
    cosp = jnp.concatenate([cos, cos, jnp.ones((S, LANES - QK_ROPE), F32)], axis=1)
    sinp = jnp.concatenate([sin, sin, jnp.zeros((S, LANES - QK_ROPE), F32)], axis=1)
    memb = mem.astype(BF16)
    xf, xb = x, x.astype(BF16)
    saved, params = [], []
    w_next = W[0]
    for i in range(c["DEPTH"]):
        p = _prep_layer(w_next if next_weights is not None else W[i], i, c)
        mla = i % 2 == 0
        memkv, = _mm(memb, p["mem_w_kv"], "nn", [BF16], name="mem_kv")
        side, arrived = next_weights(i + 1) if next_weights is not None and i + 1 < c["DEPTH"] else (None, None)
        if mla:
            mix, sv, got = _mla_fwd(xb, p, cosp, sinp, c, side)
            qoff = (c["QL"] + c["KVL"]) // LANES
        else:
            mix, sv, got = _gdn_layer_fwd(xb, p, c, side)
            qoff = 4 * MW // LANES
        if side is not None:
            rest, w_next = arrived(got)
            p.update(_prep_layer(rest, i, c))
        om, lsem = _mem_attn_fwd(sv["h"], qoff, memkv, c, "mem_attn")
        cat = jnp.concatenate([mix, om], axis=1).astype(BF16)
        z1, = _mm(cat, p["w_out"], "nn", [F32], name="w_out", extras=(xf,), epilogue=lambda acc, r: (ALPHA * r + acc,))
        x1, x1b = _layer_norm(z1, p["ln1_g"], p["ln1_b"], c, "ln1")
        u, a = _mm(x1b, p["w1"], "nn", [F32, BF16], name="mlp_up", b_major=True,
                   epilogue=lambda acc: (acc, jnp.square(jnp.maximum(acc, 0.0))))
        z2, = _mm(a, p["w2"], "nn", [F32], name="mlp_down", extras=(x1,), epilogue=lambda acc, r: (ALPHA * r + acc,))
        x2, x2b = _layer_norm(z2, p["ln2_g"], p["ln2_b"], c, "ln2")
        sv.update(xb=xb, memkv=memkv, om=om, lsem=lsem, cat=cat, z1=z1, x1b=x1b, u=u, a=a, z2=z2, qoff=qoff)
        saved.append(sv)
        params.append(p)
        xf, xb = x2, x2b

    ts = c["t_row"]
    lsum, dy = _ew(_loss_fn, [(xf, (ts, D), _row), (loss_target, (ts, D), _row)],
                   [(_sds((1, D)), (1, D), _par), (_sds((S, D)), (ts, D), _row)], (S // ts,), name="loss", acc_out=(0,))
    loss = jnp.sum(lsum)

    grads = [None] * c["DEPTH"]
    pending = None
    dx = dy
    for i in reversed(range(c["DEPTH"])):
        p, sv = params[i], saved[i]
        mla = i % 2 == 0
        G = {}
        dz2, dz2b, dg, db = _layer_norm_bwd(sv["z2"], p["ln2_g"], p["ln2_b"], dx, c, "ln2_bwd")
        G["ln2_g"], G["ln2_b"] = dg[0], db[0]
        du, = _mm(dz2b, p["w2"], "nt", [BF16], name="mlp_down_dx", extras=(sv["u"],),
                  epilogue=lambda acc, u: (acc * (2.0 * jnp.maximum(u, 0.0)),))
        G["mlp_w2"], = _mm(sv["a"], dz2b, "tn", [F32], name="mlp_down_dw")
        G["mlp_w1"], = _mm(sv["x1b"], du, "tn", [F32], name="mlp_up_dw", out_major=True)
        dx1, = _mm(du, p["w1"], "nt", [F32], name="mlp_up_dx", extras=(dz2,), b_major=True,
                   epilogue=lambda acc, r: (ALPHA * r + acc,))
        dz1, dz1b, dg, db = _layer_norm_bwd(sv["z1"], p["ln1_g"], p["ln1_b"], dx1, c, "ln1_bwd")
        G["ln1_g"], G["ln1_b"] = dg[0], db[0]
        dcat, = _mm(dz1b, p["w_out"], "nt", [BF16], name="w_out_dx")
        G["w_out"], = _mm(sv["cat"], dz1b, "tn", [F32], name="w_out_dw")
        dqm, dkm, dvm = _mem_attn_bwd(sv["h"], sv["qoff"], sv["memkv"], sv["om"], sv["lsem"], dcat, c, "mem_attn_bwd")
        dmemkv = jnp.concatenate([dkm, dvm], axis=1).astype(BF16)
        G["mem_w_kv"], = _mm(memb, dmemkv, "tn", [F32], name="mem_kv_dw")
        riders = ([pending] if pending is not None else []) + ([grad_sink(i, G, True) + (i,)] if grad_sink else [])
        side = _merge_plans([r[0] for r in riders]) if riders else None
        if mla:
            dh, g, got = _mla_bwd(sv, p, cosp, sinp, dcat, dqm, c, side)
            G.update(g)
            dw_in, = _mm(sv["xb"], dh, "tn", [F32], name="mla_in_dw")
            G["mla_w_in"] = _unprep_mla_w_in(dw_in, c)
            dx, = _mm(dh, p["w_in"], "nt", [F32], name="mla_in_dx", extras=(dz1,),
                      epilogue=lambda acc, r: (ALPHA * r + acc,))
        else:
            dh, g, got = _gdn_layer_bwd(sv, p, dcat, dqm, c, side)
            G.update(g)
            dw_in, = _mm(sv["xb"], dh, "tn", [F32], name="gdn_in_dw")
            G["gdn_w_in"] = _unprep_gdn_w_in(dw_in, c)
            dx, = _mm(dh, p["w_in"], "nt", [F32], name="gdn_in_dx", extras=(dz1,),
                      epilogue=lambda acc, r: (ALPHA * r + acc,))
        grads[i] = dict(G)
        for plan, done, layer in riders:
            n_out = len(plan["out_shape"])
            grads[layer].update(done(got[:n_out]))
            got = got[n_out:]
        pending = grad_sink(i, G, False) + (i,) if grad_sink else None
    if pending is not None:
        grads[pending[2]].update(pending[1](_run_plan(pending[0], "grad_scatter_last")))
    return loss, dx, grads


_HBM = pl.BlockSpec(memory_space=pltpu.HBM)
_VMEM = pl.BlockSpec(memory_space=pltpu.VMEM)


def _my_place():
    return lax.axis_index("x"), lax.axis_index("y"), lax.axis_index("c")


def _my_chip():
    return 2 * lax.axis_index("x") + lax.axis_index("y")


def _other_chips(x, y):
    return [(1 - x, y), (x, 1 - y), (1 - x, 1 - y)]


def _gather_plan(arrs, by_rows):
    n = len(arrs)
    halved = [_halved(a) for a in arrs]

    def run(ins, outs, sems, start, wait):
        send_sems, recv_sems, local_sems = sems
        x, y, c = _my_place()
        chips = _other_chips(x, y)

        def copy(i, k, chip, to):
            half = c if halved[i] else None
            src = ins[i] if half is None else ins[i].at[pl.ds(c * (arrs[i].shape[0] // 2), arrs[i].shape[0] // 2)]
            return pltpu.make_async_remote_copy(src_ref=src, dst_ref=_slab(outs[i], arrs[i].shape[0], by_rows[i], chip, half),
                                                send_sem=send_sems.at[3 * i + k], recv_sem=recv_sems.at[3 * i + k],
                                                device_id=to, device_id_type=MESH)

        mine = [pltpu.make_async_copy(ins[i], _slab(outs[i], arrs[i].shape[0], by_rows[i], 2 * x + y, None),
                                      local_sems.at[i]) for i in range(n)]
        sends = [copy(i, k, 2 * x + y, (cx, cy, c)) for i in range(n) for k, (cx, cy) in enumerate(chips)]
        if start:
            for cp in mine + sends:
                cp.start()
        if wait:
            for i in range(n):
                for k, (cx, cy) in enumerate(chips):
                    copy(i, k, 2 * cx + cy, (cx, cy, c)).wait_recv()
            for cp in sends:
                cp.wait_send()
            for cp in mine:
                cp.wait()

    shapes = [jax.ShapeDtypeStruct((N_CHIPS * a.shape[0], a.shape[1]) if r else (N_CHIPS,) + a.shape, a.dtype)
              for a, r in zip(arrs, by_rows)]
    sems = [pltpu.SemaphoreType.DMA((3 * n,)), pltpu.SemaphoreType.DMA((3 * n,)), pltpu.SemaphoreType.DMA((n,))]
    return dict(arrs=list(arrs), out_shape=shapes, sems=sems, run=run)


def _halved(a):
    return a.shape[0] % (4 * SUBLANES * (4 // a.dtype.itemsize)) == 0


def _slab(out, r, by_rows, chip, half):
    lo, n = (0, r) if half is None else (half * (r // 2), r // 2)
    return out.at[pl.ds(chip * r + lo, n)] if by_rows else out.at[chip].at[pl.ds(lo, n)]


def _gather_fill(outs, arrs, by_rows, name):
    idx = [i for i, a in enumerate(arrs) if _halved(a)]
    n = len(idx)

    def body(*refs):
        ins, bufs = refs[:n], refs[n:2 * n]
        send_sems, recv_sems = refs[2 * n:]
        x, y, c = _my_place()

        def copy(j, k, chip, half):
            r = arrs[idx[j]].shape[0]
            return pltpu.make_async_remote_copy(src_ref=_slab(ins[j], r, by_rows[idx[j]], chip, half),
                                                dst_ref=_slab(bufs[j], r, by_rows[idx[j]], chip, half),
                                                send_sem=send_sems.at[3 * j + k], recv_sem=recv_sems.at[3 * j + k],
                                                device_id=(x, y, 1 - c), device_id_type=MESH)

        chips = [2 * cx + cy for cx, cy in _other_chips(x, y)]
        sends = [copy(j, k, chip, c) for j in range(n) for k, chip in enumerate(chips)]
        for cp in sends:
            cp.start()
        for j in range(n):
            for k, chip in enumerate(chips):
                copy(j, k, chip, 1 - c).wait_recv()
        for cp in sends:
            cp.wait_send()

    filled = pl.pallas_call(
        body, name=name, in_specs=[_HBM] * n, out_specs=[_HBM] * n,
        out_shape=[jax.ShapeDtypeStruct(outs[i].shape, outs[i].dtype) for i in idx],
        input_output_aliases={j: j for j in range(n)},
        scratch_shapes=[pltpu.SemaphoreType.DMA((3 * n,)), pltpu.SemaphoreType.DMA((3 * n,))],
    )(*[outs[i] for i in idx])
    res = list(outs)
    for i, f in zip(idx, filled):
        res[i] = f
    return res


def _merge_plans(plans):
    def run(ins, outs, sems, start, wait):
        a = b = s = 0
        for p in plans:
            na, nb, ns = len(p["arrs"]), len(p["out_shape"]), len(p["sems"])
            p["run"](ins[a:a + na], outs[b:b + nb], sems[s:s + ns], start, wait)
            a, b, s = a + na, b + nb, s + ns

    return dict(arrs=sum((p["arrs"] for p in plans), []), out_shape=sum((p["out_shape"] for p in plans), []),
                sems=sum((p["sems"] for p in plans), []), run=run)


def _scatter_plan(ps):
    n = len(ps)

    def run(ins, outs, sems, start, wait):
        send_sems, recv_sems = sems
        x, y, c = _my_place()
        cps = [pltpu.make_async_remote_copy(src_ref=ins[i].at[2 * cx + cy], dst_ref=outs[i].at[k],
                                            send_sem=send_sems.at[3 * i + k], recv_sem=recv_sems.at[3 * i + k],
                                            device_id=(cx, cy, c), device_id_type=MESH)
               for i in range(n) for k, (cx, cy) in enumerate(_other_chips(x, y))]
        if start:
            for cp in cps:
                cp.start()
        if wait:
            for cp in cps:
                cp.wait()

    shapes = [jax.ShapeDtypeStruct((3,) + p.shape[1:], p.dtype) for p in ps]
    sems = [pltpu.SemaphoreType.DMA((3 * n,)), pltpu.SemaphoreType.DMA((3 * n,))]
    return dict(arrs=list(ps), out_shape=shapes, sems=sems, run=run)


def _run_plan(plan, name):
    n_in, n_out = len(plan["arrs"]), len(plan["out_shape"])

    def body(*refs):
        plan["run"](refs[:n_in], refs[n_in:n_in + n_out], refs[n_in + n_out:], True, True)

    return pl.pallas_call(body, name=name, in_specs=[_HBM] * n_in, out_specs=[_HBM] * n_out,
                          out_shape=plan["out_shape"], scratch_shapes=plan["sems"])(*plan["arrs"])


def _with_side(body, n_in, n_out, grid, side):
    if side is None:
        return body, [], [], [], [], []
    s_in, s_out, s_sem = len(side["arrs"]), len(side["out_shape"]), len(side["sems"])

    def wrapped(*refs):
        ins, s_ins = refs[:n_in], refs[n_in:n_in + s_in]
        o0 = n_in + s_in
        outs, s_outs = refs[o0:o0 + n_out], refs[o0 + n_out:o0 + n_out + s_out]
        rest = refs[o0 + n_out + s_out:]
        scratch, s_sems = rest[:len(rest) - s_sem], rest[len(rest) - s_sem:]
        ids = [pl.program_id(d) for d in range(len(grid))]
        first = functools.reduce(jnp.logical_and, [i == 0 for i in ids])
        last = functools.reduce(jnp.logical_and, [i == g - 1 for i, g in zip(ids, grid)])
        pl.when(first)(lambda: side["run"](s_ins, s_outs, s_sems, True, False))
        body(*ins, *outs, *scratch)
        pl.when(last)(lambda: side["run"](s_ins, s_outs, s_sems, False, True))

    return wrapped, [_HBM] * s_in, [_HBM] * s_out, side["out_shape"], side["sems"], side["arrs"]


def _swap_halves(gs, name):
    n = len(gs)

    def body(*refs):
        ins, outs = refs[:n], refs[n:2 * n]
        send_sems, recv_sems = refs[2 * n:]
        x, y, c = _my_place()
        cps = [pltpu.make_async_remote_copy(src_ref=ins[i].at[:, 1 - c], dst_ref=outs[i], send_sem=send_sems.at[i],
                                            recv_sem=recv_sems.at[i], device_id=(x, y, 1 - c), device_id_type=MESH)
               for i in range(n)]
        for cp in cps:
            cp.start()
        for cp in cps:
            cp.wait()

    return pl.pallas_call(
        body, name=name, in_specs=[_HBM] * n, out_specs=[_HBM] * n,
        out_shape=[jax.ShapeDtypeStruct((g.shape[0],) + g.shape[2:], g.dtype) for g in gs],
        scratch_shapes=[pltpu.SemaphoreType.DMA((n,)), pltpu.SemaphoreType.DMA((n,))],
    )(*gs)


def _join_halves(fs, name):
    n = len(fs)

    def body(*refs):
        ins, outs = refs[:n], refs[n:2 * n]
        send_sems, recv_sems = refs[2 * n:]
        x, y, c = _my_place()

        def copy(i, half):
            return pltpu.make_async_remote_copy(src_ref=ins[i].at[half], dst_ref=outs[i].at[half],
                                                send_sem=send_sems.at[i], recv_sem=recv_sems.at[i],
                                                device_id=(x, y, 1 - c), device_id_type=MESH)

        sends = [copy(i, c) for i in range(n)]
        for cp in sends:
            cp.start()
        for i in range(n):
            copy(i, 1 - c).wait_recv()
        for cp in sends:
            cp.wait_send()

    return pl.pallas_call(
        body, name=name, in_specs=[_HBM] * n, out_specs=[_HBM] * n,
        out_shape=[jax.ShapeDtypeStruct(f.shape, f.dtype) for f in fs],
        input_output_aliases={i: i for i in range(n)},
        scratch_shapes=[pltpu.SemaphoreType.DMA((n,)), pltpu.SemaphoreType.DMA((n,))],
    )(*fs)


def _row_tile(a, b):
    return _tile(a, max(SUBLANES, (1 << 19) // b // SUBLANES * SUBLANES), SUBLANES)


def _add_core(g, got, name):
    _, _, A, B = g.shape
    ta = _row_tile(A, B)
    return _ew(lambda p, q: (p + q,),
               [(g, (None, None, ta, B), lambda s, i: (s, lax.axis_index("c"), i, 0)),
                (got, (None, ta, B), lambda s, i: (s, i, 0))],
               [(_sds((N_CHIPS, A, B)), (None, ta, B), lambda s, i: (s, i, 0))], (N_CHIPS, A // ta), name=name)[0]


def _add_chips(p, got, name):
    _, A, B = p.shape
    ta = _row_tile(A, B)
    blk = (None, ta, B)
    return _ew(lambda a, b, c_, d: (((a + b) + c_) + d,),
               [(p, blk, lambda i: (_my_chip(), i, 0)), (got, blk, lambda i: (0, i, 0)),
                (got, blk, lambda i: (1, i, 0)), (got, blk, lambda i: (2, i, 0))],
               [(_sds((2, A, B)), blk, lambda i: (lax.axis_index("c"), i, 0))], (A // ta,), name=name)[0]


def _all_reduce_small(v, name):
    r = v.shape[0]
    masks = [(mx, my, mc) for mx in (0, 1) for my in (0, 1) for mc in (0, 1)][1:]

    def body(v_ref, out_ref, gath, send_sems, recv_sems):
        x, y, c = _my_place()
        me = 4 * x + 2 * y + c
        gath[me] = v_ref[...]

        def peer(m):
            return (x + m[0] - 2 * x * m[0], y + m[1] - 2 * y * m[1], c + m[2] - 2 * c * m[2])

        def copy(k, slab, to):
            return pltpu.make_async_remote_copy(src_ref=v_ref, dst_ref=gath.at[slab], send_sem=send_sems.at[k],
                                                recv_sem=recv_sems.at[k], device_id=to, device_id_type=MESH)

        sends = [copy(k, me, peer(m)) for k, m in enumerate(masks)]
        for cp in sends:
            cp.start()
        for k, m in enumerate(masks):
            px, py, pc = peer(m)
            copy(k, 4 * px + 2 * py + pc, (px, py, pc)).wait_recv()
        for cp in sends:
            cp.wait_send()
        total = gath[0]
        for d in range(1, 8):
            total = total + gath[d]
        out_ref[...] = total

    return pl.pallas_call(
        body, name=name, in_specs=[_VMEM], out_specs=_VMEM, out_shape=jax.ShapeDtypeStruct((r, LANES), F32),
        scratch_shapes=[pltpu.VMEM((8, r, LANES), F32), pltpu.SemaphoreType.DMA((7,)), pltpu.SemaphoreType.DMA((7,))],
    )(v)


def _pack_rows(arrs, dtype, row_mult):
    flat = jnp.concatenate([a.astype(dtype).reshape(-1) for a in arrs])
    n = flat.shape[0]
    rows = _round_up(-(-n // LANES), row_mult)
    return jnp.pad(flat, (0, rows * LANES - n)).reshape(rows, LANES)


def _unpack_rows(buf, shapes):
    lead = buf.shape[:-2]
    flat = buf.reshape(lead + (-1,))
    out, o = [], 0
    for s in shapes:
        n = math.prod(s)
        out.append(lax.slice_in_dim(flat, o, o + n, axis=len(lead)).reshape(lead + tuple(s)))
        o += n
    return out


WEIGHTS = ["mla_w_in", "mla_q_norm", "mla_w_uq", "mla_kv_norm", "mla_w_ukv", "gdn_w_in", "gdn_conv", "gdn_a_log",
           "gdn_dt_bias", "gdn_o_norm", "mem_w_kv", "w_out", "ln1_g", "ln1_b", "mlp_w1", "mlp_w2", "ln2_g", "ln2_b"]
SHARD_AXIS = {"mla_w_in": 1, "mla_w_uq": 2, "mla_w_ukv": 2, "gdn_w_in": 2, "gdn_conv": 2, "mem_w_kv": 1, "w_out": 1,
              "mlp_w1": 2, "mlp_w2": 1}
SMALL = [k for k in WEIGHTS if k not in SHARD_AXIS] + ["gdn_conv"]
BIG = [k for k in WEIGHTS if k not in SMALL]
MLA_KEYS = ["mla_w_in", "mla_q_norm", "mla_w_uq", "mla_kv_norm", "mla_w_ukv"]
GDN_KEYS = ["gdn_w_in", "gdn_conv", "gdn_a_log", "gdn_dt_bias", "gdn_o_norm"]
ALL_KEYS = ["mem_w_kv", "w_out", "ln1_g", "ln1_b", "mlp_w1", "mlp_w2", "ln2_g", "ln2_b"]


def _layer_keys(i):
    return (MLA_KEYS if i % 2 == 0 else GDN_KEYS) + ALL_KEYS


def _layer_slot(k, i):
    return i // 2 if k in MLA_KEYS or k in GDN_KEYS else i


AFTER_MIXER = ["w_out", "mlp_w1", "mlp_w2"]


def _gather_layer(w, i, part="all"):
    mine = [k for k in _layer_keys(i) if part == "all" or (k in AFTER_MIXER) == (part == "tail")]
    keys = [k for k in mine if k in SHARD_AXIS]
    arrs = [w[k][_layer_slot(k, i)].astype(F32 if k == "gdn_conv" else BF16) for k in keys]
    by_rows = [SHARD_AXIS[k] == 1 for k in keys]

    def arrived(outs):
        outs = _gather_fill(outs, arrs, by_rows, "gather_fill_" + ("mla_" if i % 2 == 0 else "gdn_") + part)
        full = {k: w[k][_layer_slot(k, i)] for k in mine if k not in SHARD_AXIS}
        for k, o in zip(keys, outs):
            by_cols = SHARD_AXIS[k] == 2 and k != "mlp_w1"
            full[k] = jnp.concatenate([o[d] for d in range(N_CHIPS)], axis=1) if by_cols else o
        return full

    return _gather_plan(arrs, by_rows), arrived


EARLY = {0: ["mlp_w1", "mlp_w2", "w_out", "mem_w_kv"], 1: ["mlp_w1"]}


def _reduce_group(i, G, early):
    kind = ("mla" if i % 2 == 0 else "gdn") + ("_early" if early else "_late")
    keys = [k for k in _layer_keys(i) if k in BIG and (k in EARLY[i % 2]) == early]
    canon = []
    for k in keys:
        g = G[k]
        if k == "mlp_w1":
            g = g.reshape(N_CHIPS, 2, g.shape[1] // 2, g.shape[2])
        elif SHARD_AXIS[k] == 1:
            g = g.reshape(N_CHIPS, 2, g.shape[0] // (2 * N_CHIPS), g.shape[1])
        else:
            rows, cw = g.shape[0], g.shape[1] // N_CHIPS
            g = g.reshape(rows, N_CHIPS, cw).transpose(1, 0, 2).reshape(N_CHIPS, 2, rows // 2, cw)
        canon.append(g)
    theirs = _swap_halves(canon, "grad_swap_" + kind)
    chip_sums = [_add_core(g, t, "grad_add_core") for g, t in zip(canon, theirs)]

    def done(got):
        halves = [_add_chips(p, s, "grad_add_chips") for p, s in zip(chip_sums, got)]
        joined = _join_halves(halves, "grad_join_" + kind)
        return {k: j.reshape(2 * j.shape[1], j.shape[2]) for k, j in zip(keys, joined)}

    return _scatter_plan(chip_sums), done


def _adamw(w, g, m, v, name):
    shape = w.shape
    cols = shape[-1]
    rows = math.prod(shape[:-1])
    tr = _tile(rows, max(SUBLANES, (1 << 19) // cols // SUBLANES * SUBLANES), SUBLANES)
    spec = ((tr, cols), _row)
    outs = _ew(_adamw_fn, [(a.reshape(rows, cols), *spec) for a in (w, g, m, v)], [(_sds((rows, cols)), *spec)] * 3,
               (rows // tr,), name=name)
    return [o.reshape(shape) for o in outs]


def kernel(x, mem, positions, mla_w_in, mla_q_norm, mla_w_uq, mla_kv_norm, mla_w_ukv, gdn_w_in, gdn_conv, gdn_a_log, gdn_dt_bias, gdn_o_norm, mem_w_kv, w_out, ln1_g, ln1_b, mlp_w1, mlp_w2, ln2_g, ln2_b, loss_target, m_mla_w_in, m_mla_q_norm, m_mla_w_uq, m_mla_kv_norm, m_mla_w_ukv, m_gdn_w_in, m_gdn_conv, m_gdn_a_log, m_gdn_dt_bias, m_gdn_o_norm, m_mem_w_kv, m_w_out, m_ln1_g, m_ln1_b, m_mlp_w1, m_mlp_w2, m_ln2_g, m_ln2_b, v_mla_w_in, v_mla_q_norm, v_mla_w_uq, v_mla_kv_norm, v_mla_w_ukv, v_gdn_w_in, v_gdn_conv, v_gdn_a_log, v_gdn_dt_bias, v_gdn_o_norm, v_mem_w_kv, v_w_out, v_ln1_g, v_ln1_b, v_mlp_w1, v_mlp_w2, v_ln2_g, v_ln2_b):
    w = dict(zip(WEIGHTS, (mla_w_in, mla_q_norm, mla_w_uq, mla_kv_norm, mla_w_ukv, gdn_w_in, gdn_conv, gdn_a_log,
                           gdn_dt_bias, gdn_o_norm, mem_w_kv, w_out, ln1_g, ln1_b, mlp_w1, mlp_w2, ln2_g, ln2_b)))
    m = dict(zip(WEIGHTS, (m_mla_w_in, m_mla_q_norm, m_mla_w_uq, m_mla_kv_norm, m_mla_w_ukv, m_gdn_w_in, m_gdn_conv,
                           m_gdn_a_log, m_gdn_dt_bias, m_gdn_o_norm, m_mem_w_kv, m_w_out, m_ln1_g, m_ln1_b, m_mlp_w1,
                           m_mlp_w2, m_ln2_g, m_ln2_b)))
    v = dict(zip(WEIGHTS, (v_mla_w_in, v_mla_q_norm, v_mla_w_uq, v_mla_kv_norm, v_mla_w_ukv, v_gdn_w_in, v_gdn_conv,
                           v_gdn_a_log, v_gdn_dt_bias, v_gdn_o_norm, v_mem_w_kv, v_w_out, v_ln1_g, v_ln1_b, v_mlp_w1,
                           v_mlp_w2, v_ln2_g, v_ln2_b)))
    assert x.shape[0] == 1, "one sequence per device"
    full_shapes = {k: w[k].shape for k in WEIGHTS}
    for k, ax in SHARD_AXIS.items():
        s = list(w[k].shape)
        s[ax] *= N_CHIPS
        full_shapes[k] = tuple(s)
    c = _dims(x.shape[1], x.shape[2], mem.shape[1], full_shapes)
    depth = c["DEPTH"]

    first, arrived = _gather_layer(w, 0, "head")
    W = [arrived(_run_plan(first, "gather_first"))]

    def next_weights(i):
        plan, arrived = _gather_layer(w, i)
        if i > 1:
            return plan, lambda got: ({}, arrived(got))
        tail, tail_arrived = _gather_layer(w, 0, "tail")
        n_tail = len(tail["out_shape"])
        return _merge_plans([tail, plan]), lambda got: (tail_arrived(got[:n_tail]), arrived(got[n_tail:]))

    loss_local, grad_x, G = _local_step(x[0], mem[0], positions[0], W, loss_target[0], c,
                                        next_weights=next_weights, grad_sink=_reduce_group)
    loss = lax.psum(loss_local, ("x", "y", "c"))

    def stacked(k):
        return jnp.stack([G[i][k] for i in range(depth) if k in G[i]], axis=0)

    grads = {k: stacked(k) for k in BIG}
    small_shapes = [full_shapes[k] for k in SMALL]
    gsmall = _all_reduce_small(_pack_rows([stacked(k) for k in SMALL], F32, SUBLANES), "grad_all_reduce_small")
    grads.update(dict(zip(SMALL, _unpack_rows(gsmall, small_shapes))))
    conv_cols = w["gdn_conv"].shape[2]
    grads["gdn_conv"] = lax.dynamic_slice_in_dim(grads["gdn_conv"], _my_chip() * conv_cols, conv_cols, axis=2)

    delta, new_m, new_v = {}, {}, {}
    for k in BIG + ["gdn_conv"]:
        delta[k], new_m[k], new_v[k] = _adamw(w[k], grads[k], m[k], v[k], "adamw")
    small = [k for k in SMALL if k != "gdn_conv"]
    packed = [_pack_rows([d[k] for k in small], F32, SUBLANES) for d in (w, grads, m, v)]
    ds, ms, vs = _adamw(*packed, "adamw_small")
    for d, buf in ((delta, ds), (new_m, ms), (new_v, vs)):
        d.update(dict(zip(small, _unpack_rows(buf, [w[k].shape for k in small]))))

    return (loss, grad_x[None], *[grads[k] for k in WEIGHTS], *[delta[k] for k in WEIGHTS],
            *[new_m[k] for k in WEIGHTS], *[new_v[k] for k in WEIGHTS])
```
